```python
import jax
import jax.numpy as jnp
from jax import lax
import numpy as np

D_MODEL = 2048
BATCH = 4
SEQ = 4096
DEPTH = 4

CHUNK = 64
N_MIXERS = 3
QBLOCK = 128
RMS_EPS = 1e-6
ROPE_THETA = 10000.0

A_HEADS = 16
A_KV_HEADS = 4
A_HEAD_DIM = D_MODEL // A_HEADS
A_GROUP = A_HEADS // A_KV_HEADS
IDX_HEADS = 16
IDX_DIM = 64
TOPK_MAX = 256
A_SPLITS = (A_HEADS * A_HEAD_DIM, A_KV_HEADS * A_HEAD_DIM, A_KV_HEADS * A_HEAD_DIM,
            IDX_HEADS * IDX_DIM, IDX_DIM, IDX_HEADS)
A_IN = sum(A_SPLITS)

B_HEADS = 16
B_HEAD_DIM = D_MODEL // B_HEADS
B_SPLITS = (B_HEADS * B_HEAD_DIM, B_HEADS * B_HEAD_DIM, B_HEADS * B_HEAD_DIM, B_HEADS)
B_IN = sum(B_SPLITS)

C_HEAD_DIM = 64
C_HEADS = D_MODEL // C_HEAD_DIM
C_DECAY_LORA = max(32, int(round(1.8 * D_MODEL ** 0.5 / 32)) * 32)
C_A_LORA = max(32, int(round(1.8 * D_MODEL ** 0.5 / 32)) * 32)
C_GATE_LORA = max(32, int(round(0.6 * D_MODEL ** 0.8 / 32)) * 32)
C_GN_EPS = C_HEAD_DIM * 1e-5

D_FF = ((8 * D_MODEL // 3 + 255) // 256) * 256

N_A = (DEPTH + 2) // 3
N_B = (DEPTH + 1) // 3
N_C = DEPTH // 3

kernel_name = 'hybrid_dsa_fox_rwkv7_encoder'


def rms_norm(x, g):
    xf = x.astype(jnp.float32)
    y = xf * lax.rsqrt(jnp.mean(xf * xf, axis=-1, keepdims=True) + RMS_EPS)
    return (y * g.astype(jnp.float32)).astype(x.dtype)


def rope(x, pos):
    half = x.shape[-1] // 2
    inv = ROPE_THETA ** (-jnp.arange(half, dtype=jnp.float32) / half)
    ang = pos[:, None] * inv[None, :]
    cos = jnp.cos(ang)[:, None, :].astype(x.dtype)
    sin = jnp.sin(ang)[:, None, :].astype(x.dtype)
    x1, x2 = x[..., :half], x[..., half:]
    return jnp.concatenate([x1 * cos - x2 * sin, x2 * cos + x1 * sin], axis=-1)


def _split_cols(a, sizes):
    outs, o = [], 0
    for n in sizes:
        outs.append(a[..., o:o + n])
        o += n
    return outs


def _to_blocks(a):
    b, s = a.shape[:2]
    return jnp.moveaxis(a.reshape((b, s // QBLOCK, QBLOCK) + a.shape[2:]), 1, 0)


def _from_blocks(a):
    a = jnp.moveaxis(a, 0, 1)
    return a.reshape((a.shape[0], a.shape[1] * a.shape[2]) + a.shape[3:])


def dsa_mixer(h, w_in, w_o):
    b, s, _ = h.shape
    f32 = jnp.float32
    topk = min(TOPK_MAX, s // 4)
    q, k, v, qi, ki, wi = _split_cols(h @ w_in, A_SPLITS)
    pos = jnp.arange(s, dtype=f32)
    q = rope(q.reshape(b, s, A_HEADS, A_HEAD_DIM), pos)
    k = rope(k.reshape(b, s, A_KV_HEADS, A_HEAD_DIM), pos)
    v = v.reshape(b, s, A_KV_HEADS, A_HEAD_DIM)
    qi = rope(qi.reshape(b, s, IDX_HEADS, IDX_DIM), pos).astype(f32)
    ki = rope(ki.reshape(b, s, 1, IDX_DIM), pos)[:, :, 0].astype(f32)
    wi = wi.astype(f32) * (IDX_HEADS ** -0.5 * IDX_DIM ** -0.5)
    key_chunk = jnp.arange(s) // CHUNK
    q_pos = jnp.arange(s).reshape(s // QBLOCK, QBLOCK)
    scale = A_HEAD_DIM ** -0.5
    gather = jax.vmap(lambda t, i: t[i])

    def block(args):
        qb, qib, wib, tb = args
        q_chunk = tb // CHUNK
        rel = jax.nn.relu(jnp.einsum('bqhd,bsd->bqhs', qib, ki))
        score = jnp.einsum('bqh,bqhs->bqs', wib, rel)
        admissible = key_chunk[None, :] <= q_chunk[:, None]
        score = jnp.where(admissible[None], score, -jnp.inf)
        _, idx = lax.top_k(score, topk)
        valid = (idx // CHUNK) <= q_chunk[None, :, None]
        k_sel = gather(k, idx)
        v_sel = gather(v, idx)
        qg = qb.reshape(b, QBLOCK, A_KV_HEADS, A_GROUP, A_HEAD_DIM)
        logits = jnp.einsum('bqgrd,bqkgd->bqgrk', qg, k_sel).astype(f32) * scale
        logits = jnp.where(valid[:, :, None, None, :], logits, -jnp.inf)
        p = jax.nn.softmax(logits, axis=-1).astype(v.dtype)
        o = jnp.einsum('bqgrk,bqkgd->bqgrd', p, v_sel)
        return o.reshape(b, QBLOCK, A_HEADS * A_HEAD_DIM)

    o = lax.map(block, (_to_blocks(q), _to_blocks(qi), _to_blocks(wi), q_pos))
    return _from_blocks(o) @ w_o


def fox_mixer(h, w_in, f_bias, w_o):
    b, s, _ = h.shape
    f32 = jnp.float32
    q, k, v, fl = _split_cols(h @ w_in, B_SPLITS)
    q = q.reshape(b, s, B_HEADS, B_HEAD_DIM)
    k = k.reshape(b, s, B_HEADS, B_HEAD_DIM)
    v = v.reshape(b, s, B_HEADS, B_HEAD_DIM)
    c = jnp.cumsum(jax.nn.log_sigmoid(fl.astype(f32) + f_bias.astype(f32)), axis=1)
    c_key = jnp.transpose(c, (0, 2, 1))
    key_pos = jnp.arange(s)
    q_pos = key_pos.reshape(s // QBLOCK, QBLOCK)
    scale = B_HEAD_DIM ** -0.5

    def block(args):
        qb, cqb, tb = args
        logits = jnp.einsum('bqhd,bshd->bhqs', qb, k).astype(f32) * scale
        logits = logits + jnp.transpose(cqb, (0, 2, 1))[..., None] - c_key[:, :, None, :]
        causal = key_pos[None, :] <= tb[:, None]
        logits = jnp.where(causal[None, None], logits, -jnp.inf)
        p = jax.nn.softmax(logits, axis=-1).astype(v.dtype)
        return jnp.einsum('bhqs,bshd->bqhd', p, v).reshape(b, QBLOCK, B_HEADS * B_HEAD_DIM)

    o = lax.map(block, (_to_blocks(q), _to_blocks(c), q_pos))
    return _from_blocks(o) @ w_o


def rwkv7_mixer(h, mix, w0, w1, w2, a0, a1, a2, g1, g2, k_k, k_a, r_k, w_rkv, w_o, gn_g, gn_b):
    b, s, d = h.shape
    f32 = jnp.float32
    xx = jnp.pad(h, ((0, 0), (1, 0), (0, 0)))[:, :-1] - h
    r = ((h + xx * mix[0]) @ w_rkv[0]).astype(f32)
    k = ((h + xx * mix[1]) @ w_rkv[1]).astype(f32)
    v = ((h + xx * mix[2]) @ w_rkv[2]).astype(f32)
    xw = h + xx * mix[3]
    xa = h + xx * mix[4]
    xg = h + xx * mix[5]
    w_log = -jax.nn.softplus(-(w0 + jnp.tanh(xw @ w1) @ w2).astype(f32)) - 0.5
    decay = jnp.exp(-jnp.exp(w_log))
    a = jax.nn.sigmoid((a0 + (xa @ a1) @ a2).astype(f32))
    g = jax.nn.sigmoid(xg @ g1) @ g2
    heads = lambda t: t.reshape(b, s, C_HEADS, C_HEAD_DIM)
    kk = heads(k * k_k.astype(f32))
    kk = kk * lax.rsqrt(jnp.maximum(jnp.sum(kk * kk, axis=-1, keepdims=True), 1e-24))
    k = heads(k * (1.0 + (a - 1.0) * k_a.astype(f32)))
    r, v, a, decay = heads(r), heads(v), heads(a), heads(decay)

    def step(state, inp):
        r_t, w_t, k_t, v_t, kk_t, a_t = inp
        sa = jnp.einsum('bhvk,bhk->bhv', state, -kk_t)
        state = (state * w_t[:, :, None, :] + sa[..., None] * (kk_t * a_t)[:, :, None, :]
                 + v_t[..., None] * k_t[:, :, None, :])
        return state, jnp.einsum('bhvk,bhk->bhv', state, r_t)

    tm = lambda t: jnp.moveaxis(t, 1, 0)
    state0 = jnp.zeros((b, C_HEADS, C_HEAD_DIM, C_HEAD_DIM), f32)
    _, y = lax.scan(step, state0, (tm(r), tm(decay), tm(k), tm(v), tm(kk), tm(a)))
    y = jnp.moveaxis(y, 0, 1)
    mu = jnp.mean(y, axis=-1, keepdims=True)
    var = jnp.mean(jnp.square(y - mu), axis=-1, keepdims=True)
    y = ((y - mu) * lax.rsqrt(var + C_GN_EPS)).reshape(b, s, d) * gn_g.astype(f32) + gn_b.astype(f32)
    bonus = jnp.sum(r * k * r_k.astype(f32), axis=-1, keepdims=True) * v
    y = y + bonus.reshape(b, s, d)
    return (y.astype(h.dtype) * g) @ w_o


def swiglu(h, w_gate, w_up, w_down):
    return (jax.nn.silu(h @ w_gate) * (h @ w_up)) @ w_down


def setup_inputs(seed: int = 0) -> dict:
    key = jax.random.key(seed)
    ks = iter(jax.random.split(key, 32))
    f32 = jnp.float32
    D = D_MODEL

    def nrm(shape, scale):
        return jax.random.normal(next(ks), shape, f32) * scale

    def uni(shape, lo, hi):
        return jax.random.uniform(next(ks), shape, f32, lo, hi)

    return {
        'x': nrm((BATCH, SEQ, D), 1.0),
        'norm_mix': 1.0 + nrm((DEPTH, D), 0.02),
        'norm_ffn': 1.0 + nrm((DEPTH, D), 0.02),
        'norm_final': 1.0 + nrm((D,), 0.02),
        'ffn_gate': nrm((DEPTH, D, D_FF), D ** -0.5),
        'ffn_up': nrm((DEPTH, D, D_FF), D ** -0.5),
        'ffn_down': nrm((DEPTH, D_FF, D), D_FF ** -0.5),
        'a_w_in': nrm((N_A, D, A_IN), D ** -0.5),
        'a_w_o': nrm((N_A, D, D), D ** -0.5),
        'b_w_in': nrm((N_B, D, B_IN), D ** -0.5),
        'b_f_bias': 3.0 + nrm((N_B, B_HEADS), 0.1),
        'b_w_o': nrm((N_B, D, D), D ** -0.5),
        'c_mix': uni((N_C, 6, D), 0.0, 1.0),
        'c_w0': uni((N_C, D), -6.0, -1.0),
        'c_w1': nrm((N_C, D, C_DECAY_LORA), D ** -0.5),
        'c_w2': nrm((N_C, C_DECAY_LORA, D), 0.1 * C_DECAY_LORA ** -0.5),
        'c_a0': nrm((N_C, D), 0.1),
        'c_a1': nrm((N_C, D, C_A_LORA), D ** -0.5),
        'c_a2': nrm((N_C, C_A_LORA, D), 0.1 * C_A_LORA ** -0.5),
        'c_g1': nrm((N_C, D, C_GATE_LORA), D ** -0.5),
        'c_g2': nrm((N_C, C_GATE_LORA, D), C_GATE_LORA ** -0.5),
        'c_k_k': 0.85 + nrm((N_C, D), 0.05),
        'c_k_a': 1.0 + nrm((N_C, D), 0.05),
        'c_r_k': nrm((N_C, C_HEADS, C_HEAD_DIM), 0.1),
        'c_w_rkv': nrm((N_C, 3, D, D), D ** -0.5),
        'c_w_o': nrm((N_C, D, D), D ** -0.5),
        'c_gn_g': 1.0 + nrm((N_C, D), 0.02),
        'c_gn_b': nrm((N_C, D), 0.02),
    }


def reference(x, norm_mix, norm_ffn, norm_final, ffn_gate, ffn_up, ffn_down,
              a_w_in, a_w_o, b_w_in, b_f_bias, b_w_o,
              c_mix, c_w0, c_w1, c_w2, c_a0, c_a1, c_a2, c_g1, c_g2,
              c_k_k, c_k_a, c_r_k, c_w_rkv, c_w_o, c_gn_g, c_gn_b):
    for i in range(DEPTH):
        hn = rms_norm(x, norm_mix[i])
        kind, j = i % N_MIXERS, i // N_MIXERS
        if kind == 0:
            y = dsa_mixer(hn, a_w_in[j], a_w_o[j])
        elif kind == 1:
            y = fox_mixer(hn, b_w_in[j], b_f_bias[j], b_w_o[j])
        else:
            y = rwkv7_mixer(hn, c_mix[j], c_w0[j], c_w1[j], c_w2[j], c_a0[j], c_a1[j], c_a2[j],
                            c_g1[j], c_g2[j], c_k_k[j], c_k_a[j], c_r_k[j], c_w_rkv[j],
                            c_w_o[j], c_gn_g[j], c_gn_b[j])
        x = x + y
        x = x + swiglu(rms_norm(x, norm_ffn[i]), ffn_gate[i], ffn_up[i], ffn_down[i])
    return rms_norm(x, norm_final)
```

```python
import functools

import jax
import jax.numpy as jnp
from jax import lax
from jax.experimental import pallas as pl
from jax.experimental.pallas import tpu as pltpu

F32 = jnp.float32
BF16 = jnp.bfloat16
I32 = jnp.int32
HIGHEST = lax.Precision.HIGHEST

D_MODEL = 2048
CHUNK = 64
RMS_EPS = 1e-6
ROPE_THETA = 10000.0
A_HEADS, A_KV_HEADS, A_HEAD_DIM = 16, 4, 128
IDX_HEADS, IDX_DIM, TOPK_MAX = 16, 64, 256
B_HEADS, B_HEAD_DIM = 16, 128
C_HEAD_DIM = 64
C_GN_EPS = C_HEAD_DIM * 1e-5

LANES = 128
VMEM_LIMIT_BYTES = 56 * 1024 * 1024

NEG_BIG = -1e30
_NT = (((1,), (1,)), ((), ()))
_TN = (((0,), (0,)), ((), ()))


def _params(*sem):
    return pltpu.CompilerParams(dimension_semantics=sem, vmem_limit_bytes=VMEM_LIMIT_BYTES)


def _iota(shape, dim):
    return lax.broadcasted_iota(I32, shape, dim)


def _rms(x, g):
    return x * lax.rsqrt(jnp.mean(x * x, axis=-1, keepdims=True) + RMS_EPS) * g


def _rmsnorm_kernel(x_ref, g_ref, o_ref):
    o_ref[...] = _rms(x_ref[...], g_ref[...]).astype(o_ref.dtype)


def rmsnorm(x, g, out_dtype):
    m, d = x.shape
    bm = min(m, 512)
    return pl.pallas_call(
        _rmsnorm_kernel,
        grid=(m // bm,),
        in_specs=[pl.BlockSpec((bm, d), lambda i: (i, 0)), pl.BlockSpec((1, d), lambda i: (0, 0))],
        out_specs=pl.BlockSpec((bm, d), lambda i: (i, 0)),
        out_shape=jax.ShapeDtypeStruct((m, d), out_dtype),
        compiler_params=_params("parallel"),
    )(x, g.reshape(1, d))


def _largest_divisor(n, candidates):
    for c in candidates:
        if n % c == 0:
            return c
    return n


def _mm_kernel(x_ref, w_ref, o_ref):
    o_ref[...] = jnp.dot(x_ref[...], w_ref[...], preferred_element_type=F32).astype(o_ref.dtype)


def _mm_res_kernel(x_ref, w_ref, r_ref, o_ref):
    acc = jnp.dot(x_ref[...], w_ref[...], preferred_element_type=F32)
    o_ref[...] = (r_ref[...] + acc).astype(o_ref.dtype)


def matmul(x, w, out_dtype, residual=None):
    m, k = x.shape
    n = w.shape[1]
    bm = _largest_divisor(m, (1024, 512, 256, 128)) if k <= 2048 else _largest_divisor(m, (512, 256, 128))
    bn = _largest_divisor(n, (1024, 512, 256, 128)) if k <= 2048 else _largest_divisor(n, (512, 256, 128))
    in_specs = [pl.BlockSpec((bm, k), lambda i, j: (i, 0)), pl.BlockSpec((k, bn), lambda i, j: (0, j))]
    args = [x, w]
    body = _mm_kernel
    if residual is not None:
        in_specs.append(pl.BlockSpec((bm, bn), lambda i, j: (i, j)))
        args.append(residual)
        body = _mm_res_kernel
    return pl.pallas_call(
        body,
        grid=(m // bm, n // bn),
        in_specs=in_specs,
        out_specs=pl.BlockSpec((bm, bn), lambda i, j: (i, j)),
        out_shape=jax.ShapeDtypeStruct((m, n), out_dtype),
        compiler_params=_params("parallel", "parallel"),
    )(*args)


def _gateup_kernel(x_ref, wg_ref, wu_ref, o_ref):
    x = x_ref[...]
    g = jnp.dot(x, wg_ref[...], preferred_element_type=F32)
    u = jnp.dot(x, wu_ref[...], preferred_element_type=F32)
    o_ref[...] = (g * jax.nn.sigmoid(g) * u).astype(o_ref.dtype)


def gate_up(x, wg, wu):
    m, k = x.shape
    n = wg.shape[1]
    bm = _largest_divisor(m, (1024, 512, 256, 128))
    bn = _largest_divisor(n, (512, 256, 128))
    return pl.pallas_call(
        _gateup_kernel,
        grid=(m // bm, n // bn),
        in_specs=[pl.BlockSpec((bm, k), lambda i, j: (i, 0)),
                  pl.BlockSpec((k, bn), lambda i, j: (0, j)),
                  pl.BlockSpec((k, bn), lambda i, j: (0, j))],
        out_specs=pl.BlockSpec((bm, bn), lambda i, j: (i, j)),
        out_shape=jax.ShapeDtypeStruct((m, n), BF16),
        compiler_params=_params("parallel", "parallel"),
    )(x, wg, wu)


def _lora_kernel(x_ref, w1_ref, w2_ref, o_ref, *, act):
    t = jnp.dot(x_ref[...], w1_ref[...], preferred_element_type=F32)
    if act == "tanh":
        t = jnp.tanh(t)
    elif act == "sigmoid":
        t = jax.nn.sigmoid(t)
    o_ref[...] = jnp.dot(t.astype(BF16), w2_ref[...], preferred_element_type=F32)


def lora(x, w1, w2, act):
    m, k = x.shape
    r = w1.shape[1]
    rp = -(-r // LANES) * LANES
    if act == "sigmoid":
        assert rp == r, "sigmoid(0) != 0: the rank must not be padded"
    w1 = jnp.pad(w1, ((0, 0), (0, rp - r)))
    w2 = jnp.pad(w2, ((0, rp - r), (0, 0)))
    n = w2.shape[1]
    bm = _largest_divisor(m, (1024, 512, 256, 128))
    return pl.pallas_call(
        functools.partial(_lora_kernel, act=act),
        grid=(m // bm,),
        in_specs=[pl.BlockSpec((bm, k), lambda i: (i, 0)),
                  pl.BlockSpec((k, rp), lambda i: (0, 0)),
                  pl.BlockSpec((rp, n), lambda i: (0, 0))],
        out_specs=pl.BlockSpec((bm, n), lambda i: (i, 0)),
        out_shape=jax.ShapeDtypeStruct((m, n), F32),
        compiler_params=_params("parallel"),
    )(x, w1, w2)


def swiglu_layer(x, g_norm, w_gate, w_up, w_down):
    hn = rmsnorm(x, g_norm, BF16)
    h = gate_up(hn, w_gate.astype(BF16), w_up.astype(BF16))
    return matmul(h, w_down.astype(BF16), F32, residual=x)


def _log_sigmoid(x):
    return jnp.minimum(x, 0.0) - jnp.log1p(jnp.exp(-jnp.abs(x)))


def _fox_prep_kernel(fl_ref, bias_ref, c_ref, ct_ref, *, seq, ch):
    tri = (_iota((ch, ch), 0) >= _iota((ch, ch), 1)).astype(F32)
    carry = jnp.zeros((1, LANES), F32)
    for c in range(seq // ch):
        lf = _log_sigmoid(fl_ref[c * ch:(c + 1) * ch, :] + bias_ref[...])
        cs = jnp.dot(tri, lf, precision=HIGHEST, preferred_element_type=F32) + carry
        c_ref[c * ch:(c + 1) * ch, :] = cs
        ct_ref[:, c * ch:(c + 1) * ch] = cs.T
        carry = cs[ch - 1:ch, :]


def fox_prep(fl, bias_pad, batch, seq):
    ch = min(seq, 256)
    return pl.pallas_call(
        functools.partial(_fox_prep_kernel, seq=seq, ch=ch),
        grid=(batch,),
        in_specs=[pl.BlockSpec((seq, LANES), lambda b: (b, 0)), pl.BlockSpec((1, LANES), lambda b: (0, 0))],
        out_specs=[pl.BlockSpec((seq, LANES), lambda b: (b, 0)),
                   pl.BlockSpec((None, LANES, seq), lambda b: (b, 0, 0))],
        out_shape=[jax.ShapeDtypeStruct((batch * seq, LANES), F32),
                   jax.ShapeDtypeStruct((batch, LANES, seq), F32)],
        compiler_params=_params("parallel"),
    )(fl, bias_pad)


def _fox_attn_kernel(q_ref, k_ref, v_ref, c_ref, ck_ref, o_ref, *, bq, bk, scale):
    h = pl.program_id(1)
    qi = pl.program_id(2)
    q = q_ref[...]
    cq = jnp.sum(jnp.where(_iota((bq, LANES), 1) == h, c_ref[...], 0.0), axis=1, keepdims=True)
    q_pos = qi * bq + _iota((bq, bk), 0)
    k_off = _iota((bq, bk), 1)

    def body(j, carry):
        m, l, acc = carry
        ks = pl.multiple_of(j * bk, bk)
        k = k_ref[pl.ds(ks, bk), :]
        v = v_ref[pl.ds(ks, bk), :]
        s = lax.dot_general(q, k, _NT, preferred_element_type=F32) * scale
        s = s + (cq - ck_ref[:, pl.ds(ks, bk)])
        s = jnp.where(ks + k_off <= q_pos, s, NEG_BIG)
        m_new = jnp.maximum(m, jnp.max(s, axis=1, keepdims=True))
        alpha = jnp.exp(m - m_new)
        p = jnp.exp(s - m_new)
        l = alpha * l + jnp.sum(p, axis=1, keepdims=True)
        acc = alpha * acc + jnp.dot(p.astype(BF16), v, preferred_element_type=F32)
        return m_new, l, acc

    n_kv = (qi * bq + bq + bk - 1) // bk
    init = (jnp.full((bq, 1), NEG_BIG, F32), jnp.zeros((bq, 1), F32), jnp.zeros((bq, LANES), F32))
    _, l, acc = lax.fori_loop(0, n_kv, body, init)
    o_ref[...] = (acc / l).astype(o_ref.dtype)


def fox_attention(qkv, c, ck4, batch, seq):
    nh = B_HEADS
    bq = bk = min(seq, 512)
    nq = seq // bq
    return pl.pallas_call(
        functools.partial(_fox_attn_kernel, bq=bq, bk=bk, scale=B_HEAD_DIM ** -0.5),
        grid=(batch, nh, nq),
        in_specs=[pl.BlockSpec((bq, LANES), lambda b, h, i: (b * nq + i, h)),
                  pl.BlockSpec((seq, LANES), lambda b, h, i: (b, nh + h)),
                  pl.BlockSpec((seq, LANES), lambda b, h, i: (b, 2 * nh + h)),
                  pl.BlockSpec((bq, LANES), lambda b, h, i: (b * nq + i, 0)),
                  pl.BlockSpec((None, None, 1, seq), lambda b, h, i: (b, h, 0, 0))],
        out_specs=pl.BlockSpec((bq, LANES), lambda b, h, i: (b * nq + i, h)),
        out_shape=jax.ShapeDtypeStruct((batch * seq, nh * LANES), BF16),
        compiler_params=_params("parallel", "parallel", "parallel"),
    )(qkv, qkv, qkv, c, ck4)


def fox_layer(x, g_norm, w_in, f_bias, w_o, batch, seq):
    nh = B_HEADS
    hn = rmsnorm(x, g_norm, BF16)
    n_qkv = 3 * nh * B_HEAD_DIM
    qkv = matmul(hn, w_in[:, :n_qkv].astype(BF16), BF16)
    w_f = jnp.pad(w_in[:, n_qkv:], ((0, 0), (0, LANES - nh))).astype(BF16)
    fl = matmul(hn, w_f, F32)
    bias_pad = jnp.pad(f_bias.astype(F32), (0, LANES - nh)).reshape(1, LANES)
    c, ct = fox_prep(fl, bias_pad, batch, seq)
    ck4 = ct[:, :nh, :].reshape(batch, nh, 1, seq)
    o = fox_attention(qkv, c, ck4, batch, seq)
    return matmul(o, w_o.astype(BF16), F32, residual=x)


def _rope_tables(seq, head_dim):
    half = head_dim // 2
    inv = ROPE_THETA ** (-jnp.arange(half, dtype=F32) / half)
    ang = jnp.arange(seq, dtype=F32)[:, None] * inv[None, :]
    cos = jnp.cos(ang)
    sin = jnp.sin(ang)
    reps = LANES // head_dim
    cos_t = jnp.tile(jnp.concatenate([cos, cos], axis=1), (1, reps))
    sin_t = jnp.tile(jnp.concatenate([-sin, sin], axis=1), (1, reps))
    return cos_t, sin_t


def _rope128(x, cos, sin):
    return x * cos + pltpu.roll(x, 64, 1) * sin


def _rope64(x, cos, sin, lane):
    partner = jnp.where((lane % 64) < 32, pltpu.roll(x, 96, 1), pltpu.roll(x, 32, 1))
    return x * cos + partner * sin


def _dsa_prep_kernel(main_ref, tail_ref, c128_ref, s128_ref, c64_ref, s64_ref,
                     q_ref, k_ref, v_ref, qi_ref, ki_ref, wi_ref, *, br, wi_scale):
    c128, s128 = c128_ref[...], s128_ref[...]
    c64, s64 = c64_ref[...], s64_ref[...]
    lane = _iota((br, LANES), 1)
    tile = lambda t: main_ref[:, t * LANES:(t + 1) * LANES]
    nq, nkv, npair = A_HEADS, A_KV_HEADS, IDX_HEADS // 2
    for t in range(nq):
        q_ref[:, t * LANES:(t + 1) * LANES] = _rope128(tile(t), c128, s128).astype(BF16)
    for t in range(nkv):
        k_ref[:, t * LANES:(t + 1) * LANES] = _rope128(tile(nq + t), c128, s128).astype(BF16)
        v_ref[:, t * LANES:(t + 1) * LANES] = tile(nq + nkv + t).astype(BF16)
    for t in range(npair):
        r = _rope64(tile(nq + 2 * nkv + t), c64, s64, lane)
        qi_ref[:, (2 * t) * LANES:(2 * t + 1) * LANES] = jnp.where(lane < 64, r, 0.0).astype(BF16)
        qi_ref[:, (2 * t + 1) * LANES:(2 * t + 2) * LANES] = jnp.where(lane >= 64, r, 0.0).astype(BF16)
    tl = tail_ref[...]
    kr = _rope64(tl, c64, s64, lane)
    ki_ref[...] = jnp.where(lane < 64, kr, pltpu.roll(kr, 64, 1)).astype(BF16)
    wi_ref[...] = tl * wi_scale


def dsa_prep(main, tail, batch, seq):
    br = min(seq, 256)
    nblk = seq // br
    c128, s128 = _rope_tables(seq, A_HEAD_DIM)
    c64, s64 = _rope_tables(seq, IDX_DIM)
    m = batch * seq
    tab = pl.BlockSpec((br, LANES), lambda i: (i % nblk, 0))
    row = lambda w: pl.BlockSpec((br, w), lambda i: (i, 0))
    return pl.pallas_call(
        functools.partial(_dsa_prep_kernel, br=br, wi_scale=IDX_HEADS ** -0.5 * IDX_DIM ** -0.5),
        grid=(m // br,),
        in_specs=[row(main.shape[1]), row(LANES), tab, tab, tab, tab],
        out_specs=[row(A_HEADS * LANES), row(A_KV_HEADS * LANES), row(A_KV_HEADS * LANES),
                   row(IDX_HEADS * LANES), row(LANES), row(LANES)],
        out_shape=[jax.ShapeDtypeStruct((m, A_HEADS * LANES), BF16),
                   jax.ShapeDtypeStruct((m, A_KV_HEADS * LANES), BF16),
                   jax.ShapeDtypeStruct((m, A_KV_HEADS * LANES), BF16),
                   jax.ShapeDtypeStruct((m, IDX_HEADS * LANES), BF16),
                   jax.ShapeDtypeStruct((m, LANES), BF16),
                   jax.ShapeDtypeStruct((m, LANES), F32)],
        compiler_params=_params("parallel"),
    )(main, tail, c128, s128, c64, s64)


_MIN32 = -2 ** 31
_KEY_NEG_INF = (0xFF800000 - 2 ** 32) ^ 0x7FFFFFFF


def _order_key(s):
    b = pltpu.bitcast(s, I32)
    return jnp.where(b >= 0, b, b ^ 0x7FFFFFFF)


def _dsa_index_kernel(qi_ref, ki_ref, wi_ref, bias_ref, key_ref, *, bq, seq, kc, topk):
    qb = pl.program_id(1)
    ns = seq // LANES
    spk = kc // LANES
    lane = _iota((bq, LANES), 1)
    wi = wi_ref[...]
    wcol = [jnp.sum(jnp.where(lane == 64 + h, wi, 0.0), axis=1, keepdims=True) for h in range(IDX_HEADS)]
    q_chunk = (qb * bq + _iota((bq, kc), 0)) // CHUNK
    k_off = _iota((bq, kc), 1)

    for n in range(ns):
        key_ref[n] = jnp.full((bq, LANES), _KEY_NEG_INF, I32)

    def kc_body(j, _):
        ks = pl.multiple_of(j * kc, kc)
        kblk = ki_ref[pl.ds(ks, kc), :]
        acc = jnp.zeros((bq, kc), F32)
        for h in range(IDX_HEADS):
            s = lax.dot_general(qi_ref[:, h * LANES:(h + 1) * LANES], kblk, _NT, preferred_element_type=F32)
            acc = acc + wcol[h] * jnp.maximum(s, 0.0)
        acc = jnp.where((ks + k_off) // CHUNK <= q_chunk, acc, -jnp.inf)
        key = _order_key(acc)
        for t in range(spk):
            key_ref[j * spk + t] = key[:, t * LANES:(t + 1) * LANES]
        return 0

    n_kc = ((qb + 1) * bq + kc - 1) // kc
    lax.fori_loop(0, n_kc, kc_body, 0)

    def count(indicator):
        acc = jnp.zeros((bq, LANES), F32)
        for n in range(ns):
            acc = acc + indicator(key_ref[n], n)
        return jnp.sum(acc, axis=1, keepdims=True)

    def bit_body(i, tu):
        cand_u = tu | lax.shift_left(jnp.int32(1), 31 - i)
        cand_s = cand_u ^ _MIN32
        cnt = count(lambda k, n: jnp.where(k >= cand_s, 1.0, 0.0))
        return jnp.where(cnt >= topk, cand_u, tu)

    tu = lax.fori_loop(0, 32, bit_body, jnp.zeros((bq, 1), I32))
    ts = tu ^ _MIN32
    cnt_ge = count(lambda k, n: jnp.where(k >= ts, 1.0, 0.0))
    cnt_gt = count(lambda k, n: jnp.where(k > ts, 1.0, 0.0))
    need = topk - cnt_gt
    unbounded = ts == _KEY_NEG_INF
    excess = jnp.logical_and(cnt_ge - cnt_gt > need, jnp.logical_not(unbounded))
    jlim0 = jnp.where(unbounded, -1, seq).astype(I32)

    def tie_limit():
        def j_body(i, j0):
            cand = j0 | lax.shift_left(jnp.int32(1), (seq.bit_length() - 2) - i)
            cnt = count(lambda k, n: jnp.where(k == ts, jnp.where((n * LANES + lane) < cand, 1.0, 0.0), 0.0))
            return jnp.where(cnt < need, cand, j0)
        j0 = lax.fori_loop(0, seq.bit_length() - 1, j_body, jnp.zeros((bq, 1), I32))
        return jnp.where(excess, j0, jlim0)

    any_excess = jnp.max(jnp.where(excess, 1, 0)) > 0
    jlim = lax.cond(any_excess, tie_limit, lambda: jlim0)

    for n in range(ns):
        k = key_ref[n]
        tie_ok = jnp.where((n * LANES + lane) <= jlim, 0.0, NEG_BIG)
        b = jnp.where(k > ts, 0.0, jnp.where(k == ts, tie_ok, NEG_BIG))
        bias_ref[:, n * LANES:(n + 1) * LANES] = b.astype(BF16)


def dsa_index(qi, ki, wi, batch, seq):
    bq = min(seq, 128)
    nq = seq // bq
    kc = min(seq, 256)
    topk = min(TOPK_MAX, seq // 4)
    return pl.pallas_call(
        functools.partial(_dsa_index_kernel, bq=bq, seq=seq, kc=kc, topk=topk),
        grid=(batch, nq),
        in_specs=[pl.BlockSpec((bq, IDX_HEADS * LANES), lambda b, i: (b * nq + i, 0)),
                  pl.BlockSpec((seq, LANES), lambda b, i: (b, 0)),
                  pl.BlockSpec((bq, LANES), lambda b, i: (b * nq + i, 0))],
        out_specs=pl.BlockSpec((bq, seq), lambda b, i: (b * nq + i, 0)),
        out_shape=jax.ShapeDtypeStruct((batch * seq, seq), BF16),
        scratch_shapes=[pltpu.VMEM((seq // LANES, bq, LANES), I32)],
        compiler_params=_params("parallel", "parallel"),
    )(qi, ki, wi)


def _dsa_attn_kernel(q_ref, k_ref, v_ref, bias_ref, o_ref, *, bq, bk, group, scale):
    qi = pl.program_id(2)
    q = jnp.concatenate([q_ref[:, r * LANES:(r + 1) * LANES] for r in range(group)], axis=0)
    rows = group * bq

    def body(j, carry):
        m, l, acc = carry
        ks = pl.multiple_of(j * bk, bk)
        k = k_ref[pl.ds(ks, bk), :]
        v = v_ref[pl.ds(ks, bk), :]
        bias = bias_ref[:, pl.ds(ks, bk)].astype(F32)
        s = lax.dot_general(q, k, _NT, preferred_element_type=F32) * scale
        s = (s.reshape(group, bq, bk) + bias[None]).reshape(rows, bk)
        m_new = jnp.maximum(m, jnp.max(s, axis=1, keepdims=True))
        alpha = jnp.exp(m - m_new)
        p = jnp.exp(s - m_new)
        l = alpha * l + jnp.sum(p, axis=1, keepdims=True)
        acc = alpha * acc + jnp.dot(p.astype(BF16), v, preferred_element_type=F32)
        return m_new, l, acc

    n_kv = ((qi + 1) * bq + bk - 1) // bk
    init = (jnp.full((rows, 1), -jnp.inf, F32), jnp.zeros((rows, 1), F32), jnp.zeros((rows, LANES), F32))
    _, l, acc = lax.fori_loop(0, n_kv, body, init)
    o = (acc / l).astype(o_ref.dtype)
    for r in range(group):
        o_ref[:, r * LANES:(r + 1) * LANES] = o[r * bq:(r + 1) * bq, :]


def dsa_attention(q, k, v, bias, batch, seq):
    group = A_HEADS // A_KV_HEADS
    bq = bk = min(seq, 256)
    nq = seq // bq
    gw = group * LANES
    return pl.pallas_call(
        functools.partial(_dsa_attn_kernel, bq=bq, bk=bk, group=group, scale=A_HEAD_DIM ** -0.5),
        grid=(batch, A_KV_HEADS, nq),
        in_specs=[pl.BlockSpec((bq, gw), lambda b, g, i: (b * nq + i, g)),
                  pl.BlockSpec((seq, LANES), lambda b, g, i: (b, g)),
                  pl.BlockSpec((seq, LANES), lambda b, g, i: (b, g)),
                  pl.BlockSpec((bq, seq), lambda b, g, i: (b * nq + i, 0))],
        out_specs=pl.BlockSpec((bq, gw), lambda b, g, i: (b * nq + i, g)),
        out_shape=jax.ShapeDtypeStruct((batch * seq, A_HEADS * LANES), BF16),
        compiler_params=_params("parallel", "parallel", "parallel"),
    )(q, k, v, bias)


def dsa_layer(x, g_norm, w_in, w_o, batch, seq):
    hn = rmsnorm(x, g_norm, BF16)
    n_main = (A_HEADS + 2 * A_KV_HEADS) * A_HEAD_DIM + IDX_HEADS * IDX_DIM
    n_tail = w_in.shape[1] - n_main
    main = matmul(hn, w_in[:, :n_main].astype(BF16), F32)
    w_tail = jnp.pad(w_in[:, n_main:], ((0, 0), (0, LANES - n_tail))).astype(BF16)
    tail = matmul(hn, w_tail, F32)
    q, k, v, qi, ki, wi = dsa_prep(main, tail, batch, seq)
    bias = dsa_index(qi, ki, wi, batch, seq)
    o = dsa_attention(q, k, v, bias, batch, seq)
    return matmul(o, w_o.astype(BF16), F32, residual=x)


def _rwkv_mix_kernel(x_ref, g_ref, mix_ref, *rest, bs):
    outs, hbuf = rest[:6], rest[6]
    j = pl.program_id(1)

    @pl.when(j == 0)
    def _():
        hbuf[0:8, :] = jnp.zeros((8, D_MODEL), F32)

    hn = _rms(x_ref[...], g_ref[...])
    hbuf[8:8 + bs, :] = hn
    xx = hbuf[7:7 + bs, :] - hn
    hbuf[7:8, :] = hn[bs - 1:bs, :]
    for i in range(6):
        outs[i][...] = (hn + xx * mix_ref[i:i + 1, :]).astype(BF16)


def rwkv_mix(x, g_norm, mix, batch, seq):
    bs = min(seq, 256)
    nb = seq // bs
    m, d = x.shape
    mix8 = jnp.pad(mix, ((0, 2), (0, 0)))
    return pl.pallas_call(
        functools.partial(_rwkv_mix_kernel, bs=bs),
        grid=(batch, nb),
        in_specs=[pl.BlockSpec((bs, d), lambda b, j: (b * nb + j, 0)),
                  pl.BlockSpec((1, d), lambda b, j: (0, 0)),
                  pl.BlockSpec((8, d), lambda b, j: (0, 0))],
        out_specs=[pl.BlockSpec((bs, d), lambda b, j: (b * nb + j, 0))] * 6,
        out_shape=[jax.ShapeDtypeStruct((m, d), BF16)] * 6,
        scratch_shapes=[pltpu.VMEM((bs + 8, d), F32)],
        compiler_params=_params("parallel", "arbitrary"),
    )(x, g_norm.reshape(1, d), mix8)


def _head_sum_matrix():
    return ((_iota((LANES, LANES), 0) // C_HEAD_DIM) == (_iota((LANES, LANES), 1) // C_HEAD_DIM)).astype(F32)


def _head_sum(x, p):
    return jnp.dot(x, p, precision=HIGHEST, preferred_element_type=F32)


def _rwkv_pre_kernel(k_ref, wl_ref, al_ref, w0_ref, a0_ref, kk_ref_, ka_ref, lw_o, kk_o, km_o, b_o):
    p = _head_sum_matrix()
    for t in range(D_MODEL // LANES):
        sl = slice(t * LANES, (t + 1) * LANES)
        k = k_ref[:, sl]
        z = w0_ref[:, sl] + wl_ref[:, sl]
        w_log = _log_sigmoid(z) - 0.5
        lw_o[:, sl] = -jnp.exp(w_log)
        a = jax.nn.sigmoid(a0_ref[:, sl] + al_ref[:, sl])
        kr = k * kk_ref_[:, sl]
        kk = kr * lax.rsqrt(jnp.maximum(_head_sum(kr * kr, p), 1e-24))
        kk_o[:, sl] = kk
        km_o[:, sl] = k * (1.0 + (a - 1.0) * ka_ref[:, sl])
        b_o[:, sl] = kk * a


def rwkv_pre(k, wl, al, w0, a0, k_k, k_a):
    m, d = k.shape
    bm = _largest_divisor(m, (256, 128, 64))
    big = pl.BlockSpec((bm, d), lambda i: (i, 0))
    row = pl.BlockSpec((1, d), lambda i: (0, 0))
    r2 = lambda a: a.reshape(1, d).astype(F32)
    return pl.pallas_call(
        _rwkv_pre_kernel,
        grid=(m // bm,),
        in_specs=[big, big, big, row, row, row, row],
        out_specs=[big] * 4,
        out_shape=[jax.ShapeDtypeStruct((m, d), F32)] * 4,
        compiler_params=_params("parallel"),
    )(k, wl, al, r2(w0), r2(a0), r2(k_k), r2(k_a))


def _rwkv_rec_kernel(r_ref, lw_ref, k_ref, v_ref, kk_ref, b_ref, y_ref, st_ref, *, tc, npair, mm_dtype):
    c = pl.program_id(2)

    @pl.when(c == 0)
    def _():
        st_ref[...] = jnp.zeros(st_ref.shape, F32)

    def mm(a, b, dims=None):
        a = a.astype(mm_dtype)
        b = b.astype(mm_dtype)
        prec = HIGHEST if mm_dtype == F32 else None
        if dims is None:
            return jnp.dot(a, b, precision=prec, preferred_element_type=F32)
        return lax.dot_general(a, b, dims, precision=prec, preferred_element_type=F32)

    row = _iota((tc, tc), 0)
    col = _iota((tc, tc), 1)
    tri_incl = (row >= col).astype(F32)
    strict = row > col
    incl = row >= col
    eye = (row == col).astype(F32)
    lane = _iota((tc, LANES), 1)
    head_masks = [lane < C_HEAD_DIM, lane >= C_HEAD_DIM]
    blockdiag = (_iota((LANES, LANES), 0) // C_HEAD_DIM) == (_iota((LANES, LANES), 1) // C_HEAD_DIM)
    n_sq = max((tc - 1).bit_length() - 1, 0)

    for p in range(npair):
        sl = slice(p * LANES, (p + 1) * LANES)
        r, lw, k, v, kk, b = r_ref[:, sl], lw_ref[:, sl], k_ref[:, sl], v_ref[:, sl], kk_ref[:, sl], b_ref[:, sl]
        cum = jnp.dot(tri_incl, lw, precision=HIGHEST, preferred_element_type=F32)
        g_in = jnp.exp(cum)
        g_inv = jnp.exp(-cum)
        a_t = -kk * jnp.exp(cum - lw)
        b_t = b * g_inv
        k_t = k * g_inv
        r_t = r * g_in
        g_end = g_in[tc - 1:tc, :]
        s0 = st_ref[p]

        w_parts, u_parts, y0_parts, arb = [], [], [], []
        for hm in head_masks:
            left = jnp.concatenate([jnp.where(hm, a_t, 0.0), jnp.where(hm, r_t, 0.0)], axis=0)
            gb = mm(left, b_t, _NT)
            gk = mm(left, k_t, _NT)
            a_ab = jnp.where(strict, gb[:tc], 0.0)
            a_ak = jnp.where(strict, gk[:tc], 0.0)
            a_rb = jnp.where(incl, gb[tc:], 0.0)
            a_rk = jnp.where(incl, gk[tc:], 0.0)
            x = eye + a_ab
            pw = a_ab
            for _ in range(n_sq):
                pw = mm(pw, pw)
                x = x + mm(pw, x)
            w_parts.append(mm(x, jnp.where(hm, a_t, 0.0)))
            u_parts.append(jnp.where(hm, mm(x, mm(a_ak, v)), 0.0))
            y0_parts.append(jnp.where(hm, mm(a_rk, v), 0.0))
            arb.append(a_rb)

        w_t = w_parts[0] + w_parts[1]
        ws = mm(jnp.concatenate([w_t, r_t], axis=0), s0, _NT)
        u = ws[:tc] + u_parts[0] + u_parts[1]
        y = ws[tc:] + y0_parts[0] + y0_parts[1]
        y = y + jnp.where(head_masks[0], mm(arb[0], u), mm(arb[1], u))
        y_ref[:, sl] = y
        upd = mm(u, b_t * g_end, _TN) + mm(v, k_t * g_end, _TN)
        st_ref[p] = s0 * g_end + jnp.where(blockdiag, upd, 0.0)


def rwkv_rec(r, lw, k, v, kk, b, batch, seq, mm_dtype=F32):
    m, d = r.shape
    tc = min(seq, 64)
    nc = seq // tc
    npair = 4
    gw = npair * LANES
    blk = pl.BlockSpec((tc, gw), lambda bi, g, c: (bi * nc + c, g))
    return pl.pallas_call(
        functools.partial(_rwkv_rec_kernel, tc=tc, npair=npair, mm_dtype=mm_dtype),
        grid=(batch, d // gw, nc),
        in_specs=[blk] * 6,
        out_specs=blk,
        out_shape=jax.ShapeDtypeStruct((m, d), F32),
        scratch_shapes=[pltpu.VMEM((npair, LANES, LANES), F32)],
        compiler_params=_params("parallel", "parallel", "arbitrary"),
    )(r, lw, k, v, kk, b)


def _rwkv_post_kernel(y_ref, r_ref, k_ref, v_ref, g_ref, gng_ref, gnb_ref, rk_ref, o_ref):
    p = _head_sum_matrix()
    inv_n = 1.0 / C_HEAD_DIM
    for t in range(D_MODEL // LANES):
        sl = slice(t * LANES, (t + 1) * LANES)
        y = y_ref[:, sl]
        mu = _head_sum(y, p) * inv_n
        yc = y - mu
        var = _head_sum(yc * yc, p) * inv_n
        yn = yc * lax.rsqrt(var + C_GN_EPS) * gng_ref[:, sl] + gnb_ref[:, sl]
        bonus = _head_sum(r_ref[:, sl] * k_ref[:, sl] * rk_ref[:, sl], p) * v_ref[:, sl]
        o_ref[:, sl] = ((yn + bonus) * g_ref[:, sl]).astype(o_ref.dtype)


def rwkv_post(y, r, kmod, v, g, gn_g, gn_b, r_k):
    m, d = y.shape
    bm = _largest_divisor(m, (256, 128, 64))
    big = pl.BlockSpec((bm, d), lambda i: (i, 0))
    row = pl.BlockSpec((1, d), lambda i: (0, 0))
    r2 = lambda a: a.reshape(1, d).astype(F32)
    return pl.pallas_call(
        _rwkv_post_kernel,
        grid=(m // bm,),
        in_specs=[big] * 5 + [row] * 3,
        out_specs=big,
        out_shape=jax.ShapeDtypeStruct((m, d), BF16),
        compiler_params=_params("parallel"),
    )(y, r, kmod, v, g, r2(gn_g), r2(gn_b), r2(r_k))


def rwkv_layer(x, g_norm, mix, w0, w1, w2, a0, a1, a2, g1, g2, k_k, k_a, r_k, w_rkv, w_o, gn_g, gn_b,
               batch, seq):
    bf = lambda a: a.astype(BF16)
    xr, xk, xv, xw, xa, xg = rwkv_mix(x, g_norm, mix, batch, seq)
    r = matmul(xr, bf(w_rkv[0]), F32)
    k = matmul(xk, bf(w_rkv[1]), F32)
    v = matmul(xv, bf(w_rkv[2]), F32)
    wl = lora(xw, bf(w1), bf(w2), "tanh")
    al = lora(xa, bf(a1), bf(a2), "none")
    g = lora(xg, bf(g1), bf(g2), "sigmoid")
    lw, kk, kmod, b = rwkv_pre(k, wl, al, w0, a0, k_k, k_a)
    y = rwkv_rec(r, lw, kmod, v, kk, b, batch, seq)
    o = rwkv_post(y, r, kmod, v, g, gn_g, gn_b, r_k)
    return matmul(o, bf(w_o), F32, residual=x)


def kernel(x, norm_mix, norm_ffn, norm_final, ffn_gate, ffn_up, ffn_down, a_w_in, a_w_o, b_w_in, b_f_bias,
           b_w_o, c_mix, c_w0, c_w1, c_w2, c_a0, c_a1, c_a2, c_g1, c_g2, c_k_k, c_k_a, c_r_k, c_w_rkv,
           c_w_o, c_gn_g, c_gn_b):
    batch, seq, d = x.shape
    depth = norm_mix.shape[0]
    h = x.reshape(batch * seq, d)
    for i in range(depth):
        kind, j = i % 3, i // 3
        if kind == 0:
            h = dsa_layer(h, norm_mix[i], a_w_in[j], a_w_o[j], batch, seq)
        elif kind == 1:
            h = fox_layer(h, norm_mix[i], b_w_in[j], b_f_bias[j], b_w_o[j], batch, seq)
        else:
            h = rwkv_layer(h, norm_mix[i], c_mix[j], c_w0[j], c_w1[j], c_w2[j], c_a0[j], c_a1[j], c_a2[j],
                           c_g1[j], c_g2[j], c_k_k[j], c_k_a[j], c_r_k[j], c_w_rkv[j], c_w_o[j],
                           c_gn_g[j], c_gn_b[j], batch, seq)
        h = swiglu_layer(h, norm_ffn[i], ffn_gate[i], ffn_up[i], ffn_down[i])
    return rmsnorm(h, norm_final, x.dtype).reshape(batch, seq, d)
```

```python
import functools

import jax
import jax.numpy as jnp
from jax import lax
from jax.experimental import pallas as pl
from jax.experimental.pallas import tpu as pltpu

F32 = jnp.float32
BF16 = jnp.bfloat16
I32 = jnp.int32
HIGHEST = lax.Precision.HIGHEST

D_MODEL = 2048
CHUNK = 64
RMS_EPS = 1e-6
ROPE_THETA = 10000.0
A_HEADS, A_KV_HEADS, A_HEAD_DIM = 16, 4, 128
IDX_HEADS, IDX_DIM, TOPK_MAX = 16, 64, 256
B_HEADS, B_HEAD_DIM = 16, 128
C_HEAD_DIM = 64
C_GN_EPS = C_HEAD_DIM * 1e-5

LANES = 128
VMEM_LIMIT_BYTES = 56 * 1024 * 1024

NEG_BIG = -1e30
_NT = (((1,), (1,)), ((), ()))
_TN = (((0,), (0,)), ((), ()))


def _params(*sem):
    return pltpu.CompilerParams(dimension_semantics=sem, vmem_limit_bytes=VMEM_LIMIT_BYTES)


def _iota(shape, dim):
    return lax.broadcasted_iota(I32, shape, dim)


def _rms(x, g):
    return x * lax.rsqrt(jnp.mean(x * x, axis=-1, keepdims=True) + RMS_EPS) * g


def _rmsnorm_kernel(x_ref, g_ref, o_ref):
    o_ref[...] = _rms(x_ref[...], g_ref[...]).astype(o_ref.dtype)


def rmsnorm(x, g, out_dtype):
    m, d = x.shape
    bm = min(m, 512)
    return pl.pallas_call(
        _rmsnorm_kernel,
        name="rmsnorm",
        grid=(m // bm,),
        in_specs=[pl.BlockSpec((bm, d), lambda i: (i, 0)), pl.BlockSpec((1, d), lambda i: (0, 0))],
        out_specs=pl.BlockSpec((bm, d), lambda i: (i, 0)),
        out_shape=jax.ShapeDtypeStruct((m, d), out_dtype),
        compiler_params=_params("parallel"),
    )(x, g.reshape(1, d))


def _largest_divisor(n, candidates):
    for c in candidates:
        if n % c == 0:
            return c
    return n


def _mm_kernel(x_ref, w_ref, o_ref):
    o_ref[...] = jnp.dot(x_ref[...], w_ref[...], preferred_element_type=F32).astype(o_ref.dtype)


def _mm_res_kernel(x_ref, w_ref, r_ref, o_ref):
    acc = jnp.dot(x_ref[...], w_ref[...], preferred_element_type=F32)
    o_ref[...] = (r_ref[...] + acc).astype(o_ref.dtype)


def matmul(x, w, out_dtype, residual=None):
    m, k = x.shape
    n = w.shape[1]
    bm = _largest_divisor(m, (1024, 512, 256, 128)) if k <= 2048 else _largest_divisor(m, (512, 256, 128))
    bn = _largest_divisor(n, (1024, 512, 256, 128)) if k <= 2048 else _largest_divisor(n, (512, 256, 128))
    in_specs = [pl.BlockSpec((bm, k), lambda i, j: (i, 0)), pl.BlockSpec((k, bn), lambda i, j: (0, j))]
    args = [x, w]
    body = _mm_kernel
    if residual is not None:
        in_specs.append(pl.BlockSpec((bm, bn), lambda i, j: (i, j)))
        args.append(residual)
        body = _mm_res_kernel
    return pl.pallas_call(
        body,
        name=f"mm_{k}x{n}",
        grid=(m // bm, n // bn),
        in_specs=in_specs,
        out_specs=pl.BlockSpec((bm, bn), lambda i, j: (i, j)),
        out_shape=jax.ShapeDtypeStruct((m, n), out_dtype),
        compiler_params=_params("parallel", "parallel"),
    )(*args)


def _gateup_kernel(x_ref, wg_ref, wu_ref, o_ref):
    x = x_ref[...]
    g = jnp.dot(x, wg_ref[...], preferred_element_type=F32)
    u = jnp.dot(x, wu_ref[...], preferred_element_type=F32)
    o_ref[...] = (g * jax.nn.sigmoid(g) * u).astype(o_ref.dtype)


def gate_up(x, wg, wu):
    m, k = x.shape
    n = wg.shape[1]
    bm = _largest_divisor(m, (1024, 512, 256, 128))
    bn = _largest_divisor(n, (512, 256, 128))
    return pl.pallas_call(
        _gateup_kernel,
        name="gate_up",
        grid=(m // bm, n // bn),
        in_specs=[pl.BlockSpec((bm, k), lambda i, j: (i, 0)),
                  pl.BlockSpec((k, bn), lambda i, j: (0, j)),
                  pl.BlockSpec((k, bn), lambda i, j: (0, j))],
        out_specs=pl.BlockSpec((bm, bn), lambda i, j: (i, j)),
        out_shape=jax.ShapeDtypeStruct((m, n), BF16),
        compiler_params=_params("parallel", "parallel"),
    )(x, wg, wu)


def _lora_kernel(x_ref, w1_ref, w2_ref, o_ref, *, act):
    t = jnp.dot(x_ref[...], w1_ref[...], preferred_element_type=F32)
    if act == "tanh":
        t = jnp.tanh(t)
    elif act == "sigmoid":
        t = jax.nn.sigmoid(t)
    o_ref[...] = jnp.dot(t.astype(BF16), w2_ref[...], preferred_element_type=F32)


def lora(x, w1, w2, act):
    m, k = x.shape
    r = w1.shape[1]
    rp = -(-r // LANES) * LANES
    if act == "sigmoid":
        assert rp == r, "sigmoid(0) != 0: the rank must not be padded"
    w1 = jnp.pad(w1, ((0, 0), (0, rp - r)))
    w2 = jnp.pad(w2, ((0, rp - r), (0, 0)))
    n = w2.shape[1]
    bm = _largest_divisor(m, (1024, 512, 256, 128))
    return pl.pallas_call(
        functools.partial(_lora_kernel, act=act),
        name="lora_" + act,
        grid=(m // bm,),
        in_specs=[pl.BlockSpec((bm, k), lambda i: (i, 0)),
                  pl.BlockSpec((k, rp), lambda i: (0, 0)),
                  pl.BlockSpec((rp, n), lambda i: (0, 0))],
        out_specs=pl.BlockSpec((bm, n), lambda i: (i, 0)),
        out_shape=jax.ShapeDtypeStruct((m, n), F32),
        compiler_params=_params("parallel"),
    )(x, w1, w2)


def swiglu_layer(x, g_norm, w_gate, w_up, w_down):
    hn = rmsnorm(x, g_norm, BF16)
    h = gate_up(hn, w_gate.astype(BF16), w_up.astype(BF16))
    return matmul(h, w_down.astype(BF16), F32, residual=x)


def _log_sigmoid(x):
    return jnp.minimum(x, 0.0) - jnp.log1p(jnp.exp(-jnp.abs(x)))


def _fox_prep_kernel(fl_ref, bias_ref, c_ref, ct_ref, *, seq, ch):
    tri = (_iota((ch, ch), 0) >= _iota((ch, ch), 1)).astype(F32)
    carry = jnp.zeros((1, LANES), F32)
    for c in range(seq // ch):
        lf = _log_sigmoid(fl_ref[c * ch:(c + 1) * ch, :] + bias_ref[...])
        cs = jnp.dot(tri, lf, precision=HIGHEST, preferred_element_type=F32) + carry
        c_ref[c * ch:(c + 1) * ch, :] = cs
        ct_ref[:, c * ch:(c + 1) * ch] = cs.T
        carry = cs[ch - 1:ch, :]


def fox_prep(fl, bias_pad, batch, seq):
    ch = min(seq, 256)
    return pl.pallas_call(
        functools.partial(_fox_prep_kernel, seq=seq, ch=ch),
        name="fox_prep",
        grid=(batch,),
        in_specs=[pl.BlockSpec((seq, LANES), lambda b: (b, 0)), pl.BlockSpec((1, LANES), lambda b: (0, 0))],
        out_specs=[pl.BlockSpec((seq, LANES), lambda b: (b, 0)),
                   pl.BlockSpec((None, LANES, seq), lambda b: (b, 0, 0))],
        out_shape=[jax.ShapeDtypeStruct((batch * seq, LANES), F32),
                   jax.ShapeDtypeStruct((batch, LANES, seq), F32)],
        compiler_params=_params("parallel"),
    )(fl, bias_pad)


def _fox_attn_kernel(q_ref, k_ref, v_ref, c_ref, ck_ref, o_ref, *, bq, bk, scale):
    h = pl.program_id(1)
    qi = pl.program_id(2)
    q = q_ref[...]
    cq = jnp.sum(jnp.where(_iota((bq, LANES), 1) == h, c_ref[...], 0.0), axis=1, keepdims=True)
    q_pos = qi * bq + _iota((bq, bk), 0)
    k_off = _iota((bq, bk), 1)

    def body(j, carry):
        m, l, acc = carry
        ks = pl.multiple_of(j * bk, bk)
        k = k_ref[pl.ds(ks, bk), :]
        v = v_ref[pl.ds(ks, bk), :]
        s = lax.dot_general(q, k, _NT, preferred_element_type=F32) * scale
        s = s + (cq - ck_ref[:, pl.ds(ks, bk)])
        s = jnp.where(ks + k_off <= q_pos, s, NEG_BIG)
        m_new = jnp.maximum(m, jnp.max(s, axis=1, keepdims=True))
        alpha = jnp.exp(m - m_new)
        p = jnp.exp(s - m_new)
        l = alpha * l + jnp.sum(p, axis=1, keepdims=True)
        acc = alpha * acc + jnp.dot(p.astype(BF16), v, preferred_element_type=F32)
        return m_new, l, acc

    n_kv = (qi * bq + bq + bk - 1) // bk
    init = (jnp.full((bq, 1), NEG_BIG, F32), jnp.zeros((bq, 1), F32), jnp.zeros((bq, LANES), F32))
    _, l, acc = lax.fori_loop(0, n_kv, body, init)
    o_ref[...] = (acc / l).astype(o_ref.dtype)


def fox_attention(qkv, c, ck4, batch, seq):
    nh = B_HEADS
    bq = bk = min(seq, 512)
    nq = seq // bq
    return pl.pallas_call(
        functools.partial(_fox_attn_kernel, bq=bq, bk=bk, scale=B_HEAD_DIM ** -0.5),
        name="fox_attn",
        grid=(batch, nh, nq),
        in_specs=[pl.BlockSpec((bq, LANES), lambda b, h, i: (b * nq + i, h)),
                  pl.BlockSpec((seq, LANES), lambda b, h, i: (b, nh + h)),
                  pl.BlockSpec((seq, LANES), lambda b, h, i: (b, 2 * nh + h)),
                  pl.BlockSpec((bq, LANES), lambda b, h, i: (b * nq + i, 0)),
                  pl.BlockSpec((None, None, 1, seq), lambda b, h, i: (b, h, 0, 0))],
        out_specs=pl.BlockSpec((bq, LANES), lambda b, h, i: (b * nq + i, h)),
        out_shape=jax.ShapeDtypeStruct((batch * seq, nh * LANES), BF16),
        compiler_params=_params("parallel", "parallel", "parallel"),
    )(qkv, qkv, qkv, c, ck4)


def fox_layer(x, g_norm, w_in, f_bias, w_o, batch, seq):
    nh = B_HEADS
    hn = rmsnorm(x, g_norm, BF16)
    n_qkv = 3 * nh * B_HEAD_DIM
    qkv = matmul(hn, w_in[:, :n_qkv].astype(BF16), BF16)
    w_f = jnp.pad(w_in[:, n_qkv:], ((0, 0), (0, LANES - nh))).astype(BF16)
    fl = matmul(hn, w_f, F32)
    bias_pad = jnp.pad(f_bias.astype(F32), (0, LANES - nh)).reshape(1, LANES)
    c, ct = fox_prep(fl, bias_pad, batch, seq)
    ck4 = ct[:, :nh, :].reshape(batch, nh, 1, seq)
    o = fox_attention(qkv, c, ck4, batch, seq)
    return matmul(o, w_o.astype(BF16), F32, residual=x)


def _rope_tables(seq, head_dim):
    half = head_dim // 2
    inv = ROPE_THETA ** (-jnp.arange(half, dtype=F32) / half)
    ang = jnp.arange(seq, dtype=F32)[:, None] * inv[None, :]
    cos = jnp.cos(ang)
    sin = jnp.sin(ang)
    reps = LANES // head_dim
    cos_t = jnp.tile(jnp.concatenate([cos, cos], axis=1), (1, reps))
    sin_t = jnp.tile(jnp.concatenate([-sin, sin], axis=1), (1, reps))
    return cos_t, sin_t


def _rope128(x, cos, sin):
    return x * cos + pltpu.roll(x, 64, 1) * sin


def _rope64(x, cos, sin, lane):
    partner = jnp.where((lane % 64) < 32, pltpu.roll(x, 96, 1), pltpu.roll(x, 32, 1))
    return x * cos + partner * sin


def _dsa_prep_kernel(main_ref, tail_ref, c128_ref, s128_ref, c64_ref, s64_ref,
                     q_ref, k_ref, v_ref, qi_ref, ki_ref, wi_ref, *, br, wi_scale):
    c128, s128 = c128_ref[...], s128_ref[...]
    c64, s64 = c64_ref[...], s64_ref[...]
    lane = _iota((br, LANES), 1)
    tile = lambda t: main_ref[:, t * LANES:(t + 1) * LANES]
    nq, nkv, npair = A_HEADS, A_KV_HEADS, IDX_HEADS // 2
    for t in range(nq):
        q_ref[:, t * LANES:(t + 1) * LANES] = _rope128(tile(t), c128, s128).astype(BF16)
    for t in range(nkv):
        k_ref[:, t * LANES:(t + 1) * LANES] = _rope128(tile(nq + t), c128, s128).astype(BF16)
        v_ref[:, t * LANES:(t + 1) * LANES] = tile(nq + nkv + t).astype(BF16)
    for t in range(npair):
        r = _rope64(tile(nq + 2 * nkv + t), c64, s64, lane)
        qi_ref[:, (2 * t) * LANES:(2 * t + 1) * LANES] = jnp.where(lane < 64, r, 0.0).astype(BF16)
        qi_ref[:, (2 * t + 1) * LANES:(2 * t + 2) * LANES] = jnp.where(lane >= 64, r, 0.0).astype(BF16)
    tl = tail_ref[...]
    kr = _rope64(tl, c64, s64, lane)
    ki_ref[...] = jnp.where(lane < 64, kr, pltpu.roll(kr, 64, 1)).astype(BF16)
    wi_ref[...] = tl * wi_scale


def dsa_prep(main, tail, batch, seq):
    br = min(seq, 256)
    nblk = seq // br
    c128, s128 = _rope_tables(seq, A_HEAD_DIM)
    c64, s64 = _rope_tables(seq, IDX_DIM)
    m = batch * seq
    tab = pl.BlockSpec((br, LANES), lambda i: (i % nblk, 0))
    row = lambda w: pl.BlockSpec((br, w), lambda i: (i, 0))
    return pl.pallas_call(
        functools.partial(_dsa_prep_kernel, br=br, wi_scale=IDX_HEADS ** -0.5 * IDX_DIM ** -0.5),
        name="dsa_prep",
        grid=(m // br,),
        in_specs=[row(main.shape[1]), row(LANES), tab, tab, tab, tab],
        out_specs=[row(A_HEADS * LANES), row(A_KV_HEADS * LANES), row(A_KV_HEADS * LANES),
                   row(IDX_HEADS * LANES), row(LANES), row(LANES)],
        out_shape=[jax.ShapeDtypeStruct((m, A_HEADS * LANES), BF16),
                   jax.ShapeDtypeStruct((m, A_KV_HEADS * LANES), BF16),
                   jax.ShapeDtypeStruct((m, A_KV_HEADS * LANES), BF16),
                   jax.ShapeDtypeStruct((m, IDX_HEADS * LANES), BF16),
                   jax.ShapeDtypeStruct((m, LANES), BF16),
                   jax.ShapeDtypeStruct((m, LANES), F32)],
        compiler_params=_params("parallel"),
    )(main, tail, c128, s128, c64, s64)


_MIN32 = -2 ** 31
_KEY_NEG_INF = (0xFF800000 - 2 ** 32) ^ 0x7FFFFFFF


def _order_key(s):
    b = pltpu.bitcast(s, I32)
    return jnp.where(b >= 0, b, b ^ 0x7FFFFFFF)


def _dsa_index_kernel(qi_ref, ki_ref, wi_ref, bias_ref, key_ref, *, bq, seq, kc, topk):
    qb = pl.program_id(1)
    ns = seq // LANES
    spk = kc // LANES
    lane = _iota((bq, LANES), 1)
    wi = wi_ref[...]
    wcol = [jnp.sum(jnp.where(lane == 64 + h, wi, 0.0), axis=1, keepdims=True) for h in range(IDX_HEADS)]
    q_chunk = (qb * bq + _iota((bq, kc), 0)) // CHUNK
    k_off = _iota((bq, kc), 1)

    for n in range(ns):
        key_ref[n] = jnp.full((bq, LANES), _KEY_NEG_INF, I32)

    def kc_body(j, _):
        ks = pl.multiple_of(j * kc, kc)
        kblk = ki_ref[pl.ds(ks, kc), :]
        acc = jnp.zeros((bq, kc), F32)
        for h in range(IDX_HEADS):
            s = lax.dot_general(qi_ref[:, h * LANES:(h + 1) * LANES], kblk, _NT, preferred_element_type=F32)
            acc = acc + wcol[h] * jnp.maximum(s, 0.0)
        acc = jnp.where((ks + k_off) // CHUNK <= q_chunk, acc, -jnp.inf)
        key = _order_key(acc)
        for t in range(spk):
            key_ref[j * spk + t] = key[:, t * LANES:(t + 1) * LANES]
        return 0

    n_kc = ((qb + 1) * bq + kc - 1) // kc
    lax.fori_loop(0, n_kc, kc_body, 0)

    def count(indicator):
        acc = jnp.zeros((bq, LANES), F32)
        for n in range(ns):
            acc = acc + indicator(key_ref[n], n)
        return jnp.sum(acc, axis=1, keepdims=True)

    def bit_body(i, tu):
        cand_u = tu | lax.shift_left(jnp.int32(1), 31 - i)
        cand_s = cand_u ^ _MIN32
        cnt = count(lambda k, n: jnp.where(k >= cand_s, 1.0, 0.0))
        return jnp.where(cnt >= topk, cand_u, tu)

    tu = lax.fori_loop(0, 32, bit_body, jnp.zeros((bq, 1), I32))
    ts = tu ^ _MIN32
    cnt_ge = count(lambda k, n: jnp.where(k >= ts, 1.0, 0.0))
    cnt_gt = count(lambda k, n: jnp.where(k > ts, 1.0, 0.0))
    need = topk - cnt_gt
    unbounded = ts == _KEY_NEG_INF
    excess = jnp.logical_and(cnt_ge - cnt_gt > need, jnp.logical_not(unbounded))
    jlim0 = jnp.where(unbounded, -1, seq).astype(I32)

    def tie_limit():
        def j_body(i, j0):
            cand = j0 | lax.shift_left(jnp.int32(1), (seq.bit_length() - 2) - i)
            cnt = count(lambda k, n: jnp.where(k == ts, jnp.where((n * LANES + lane) < cand, 1.0, 0.0), 0.0))
            return jnp.where(cnt < need, cand, j0)
        j0 = lax.fori_loop(0, seq.bit_length() - 1, j_body, jnp.zeros((bq, 1), I32))
        return jnp.where(excess, j0, jlim0)

    any_excess = jnp.max(jnp.where(excess, 1, 0)) > 0
    jlim = lax.cond(any_excess, tie_limit, lambda: jlim0)

    for n in range(ns):
        k = key_ref[n]
        tie_ok = jnp.where((n * LANES + lane) <= jlim, 0.0, NEG_BIG)
        b = jnp.where(k > ts, 0.0, jnp.where(k == ts, tie_ok, NEG_BIG))
        bias_ref[:, n * LANES:(n + 1) * LANES] = b.astype(BF16)


def dsa_index(qi, ki, wi, batch, seq):
    bq = min(seq, 128)
    nq = seq // bq
    kc = min(seq, 256)
    topk = min(TOPK_MAX, seq // 4)
    return pl.pallas_call(
        functools.partial(_dsa_index_kernel, bq=bq, seq=seq, kc=kc, topk=topk),
        name="dsa_index",
        grid=(batch, nq),
        in_specs=[pl.BlockSpec((bq, IDX_HEADS * LANES), lambda b, i: (b * nq + i, 0)),
                  pl.BlockSpec((seq, LANES), lambda b, i: (b, 0)),
                  pl.BlockSpec((bq, LANES), lambda b, i: (b * nq + i, 0))],
        out_specs=pl.BlockSpec((bq, seq), lambda b, i: (b * nq + i, 0)),
        out_shape=jax.ShapeDtypeStruct((batch * seq, seq), BF16),
        scratch_shapes=[pltpu.VMEM((seq // LANES, bq, LANES), I32)],
        compiler_params=_params("parallel", "parallel"),
    )(qi, ki, wi)


def _dsa_attn_kernel(q_ref, k_ref, v_ref, bias_ref, o_ref, *, bq, bk, group, scale):
    qi = pl.program_id(2)
    q = jnp.concatenate([q_ref[:, r * LANES:(r + 1) * LANES] for r in range(group)], axis=0)
    rows = group * bq

    def body(j, carry):
        m, l, acc = carry
        ks = pl.multiple_of(j * bk, bk)
        k = k_ref[pl.ds(ks, bk), :]
        v = v_ref[pl.ds(ks, bk), :]
        bias = bias_ref[:, pl.ds(ks, bk)].astype(F32)
        s = lax.dot_general(q, k, _NT, preferred_element_type=F32) * scale
        s = (s.reshape(group, bq, bk) + bias[None]).reshape(rows, bk)
        m_new = jnp.maximum(m, jnp.max(s, axis=1, keepdims=True))
        alpha = jnp.exp(m - m_new)
        p = jnp.exp(s - m_new)
        l = alpha * l + jnp.sum(p, axis=1, keepdims=True)
        acc = alpha * acc + jnp.dot(p.astype(BF16), v, preferred_element_type=F32)
        return m_new, l, acc

    n_kv = ((qi + 1) * bq + bk - 1) // bk
    init = (jnp.full((rows, 1), -jnp.inf, F32), jnp.zeros((rows, 1), F32), jnp.zeros((rows, LANES), F32))
    _, l, acc = lax.fori_loop(0, n_kv, body, init)
    o = (acc / l).astype(o_ref.dtype)
    for r in range(group):
        o_ref[:, r * LANES:(r + 1) * LANES] = o[r * bq:(r + 1) * bq, :]


def dsa_attention(q, k, v, bias, batch, seq):
    group = A_HEADS // A_KV_HEADS
    bq = bk = min(seq, 256)
    nq = seq // bq
    gw = group * LANES
    return pl.pallas_call(
        functools.partial(_dsa_attn_kernel, bq=bq, bk=bk, group=group, scale=A_HEAD_DIM ** -0.5),
        name="dsa_attn",
        grid=(batch, A_KV_HEADS, nq),
        in_specs=[pl.BlockSpec((bq, gw), lambda b, g, i: (b * nq + i, g)),
                  pl.BlockSpec((seq, LANES), lambda b, g, i: (b, g)),
                  pl.BlockSpec((seq, LANES), lambda b, g, i: (b, g)),
                  pl.BlockSpec((bq, seq), lambda b, g, i: (b * nq + i, 0))],
        out_specs=pl.BlockSpec((bq, gw), lambda b, g, i: (b * nq + i, g)),
        out_shape=jax.ShapeDtypeStruct((batch * seq, A_HEADS * LANES), BF16),
        compiler_params=_params("parallel", "parallel", "parallel"),
    )(q, k, v, bias)


def dsa_layer(x, g_norm, w_in, w_o, batch, seq):
    hn = rmsnorm(x, g_norm, BF16)
    n_main = (A_HEADS + 2 * A_KV_HEADS) * A_HEAD_DIM + IDX_HEADS * IDX_DIM
    n_tail = w_in.shape[1] - n_main
    main = matmul(hn, w_in[:, :n_main].astype(BF16), F32)
    w_tail = jnp.pad(w_in[:, n_main:], ((0, 0), (0, LANES - n_tail))).astype(BF16)
    tail = matmul(hn, w_tail, F32)
    q, k, v, qi, ki, wi = dsa_prep(main, tail, batch, seq)
    bias = dsa_index(qi, ki, wi, batch, seq)
    o = dsa_attention(q, k, v, bias, batch, seq)
    return matmul(o, w_o.astype(BF16), F32, residual=x)


def _rwkv_mix_kernel(x_ref, g_ref, mix_ref, *rest, bs):
    outs, hbuf = rest[:6], rest[6]
    j = pl.program_id(1)

    @pl.when(j == 0)
    def _():
        hbuf[0:8, :] = jnp.zeros((8, D_MODEL), F32)

    hn = _rms(x_ref[...], g_ref[...])
    hbuf[8:8 + bs, :] = hn
    xx = hbuf[7:7 + bs, :] - hn
    hbuf[7:8, :] = hn[bs - 1:bs, :]
    for i in range(6):
        outs[i][...] = (hn + xx * mix_ref[i:i + 1, :]).astype(BF16)


def rwkv_mix(x, g_norm, mix, batch, seq):
    bs = min(seq, 256)
    nb = seq // bs
    m, d = x.shape
    mix8 = jnp.pad(mix, ((0, 2), (0, 0)))
    return pl.pallas_call(
        functools.partial(_rwkv_mix_kernel, bs=bs),
        name="rwkv_mix",
        grid=(batch, nb),
        in_specs=[pl.BlockSpec((bs, d), lambda b, j: (b * nb + j, 0)),
                  pl.BlockSpec((1, d), lambda b, j: (0, 0)),
                  pl.BlockSpec((8, d), lambda b, j: (0, 0))],
        out_specs=[pl.BlockSpec((bs, d), lambda b, j: (b * nb + j, 0))] * 6,
        out_shape=[jax.ShapeDtypeStruct((m, d), BF16)] * 6,
        scratch_shapes=[pltpu.VMEM((bs + 8, d), F32)],
        compiler_params=_params("parallel", "arbitrary"),
    )(x, g_norm.reshape(1, d), mix8)


def _head_sum_matrix():
    return ((_iota((LANES, LANES), 0) // C_HEAD_DIM) == (_iota((LANES, LANES), 1) // C_HEAD_DIM)).astype(F32)


def _head_sum(x, p):
    return jnp.dot(x, p, precision=HIGHEST, preferred_element_type=F32)


def _rwkv_pre_kernel(k_ref, wl_ref, al_ref, w0_ref, a0_ref, kk_ref_, ka_ref, lw_o, kk_o, km_o, b_o):
    p = _head_sum_matrix()
    for t in range(D_MODEL // LANES):
        sl = slice(t * LANES, (t + 1) * LANES)
        k = k_ref[:, sl]
        z = w0_ref[:, sl] + wl_ref[:, sl]
        w_log = _log_sigmoid(z) - 0.5
        lw_o[:, sl] = -jnp.exp(w_log)
        a = jax.nn.sigmoid(a0_ref[:, sl] + al_ref[:, sl])
        kr = k * kk_ref_[:, sl]
        kk = kr * lax.rsqrt(jnp.maximum(_head_sum(kr * kr, p), 1e-24))
        kk_o[:, sl] = kk
        km_o[:, sl] = k * (1.0 + (a - 1.0) * ka_ref[:, sl])
        b_o[:, sl] = kk * a


def rwkv_pre(k, wl, al, w0, a0, k_k, k_a):
    m, d = k.shape
    bm = _largest_divisor(m, (256, 128, 64))
    big = pl.BlockSpec((bm, d), lambda i: (i, 0))
    row = pl.BlockSpec((1, d), lambda i: (0, 0))
    r2 = lambda a: a.reshape(1, d).astype(F32)
    return pl.pallas_call(
        _rwkv_pre_kernel,
        name="rwkv_pre",
        grid=(m // bm,),
        in_specs=[big, big, big, row, row, row, row],
        out_specs=[big] * 4,
        out_shape=[jax.ShapeDtypeStruct((m, d), F32)] * 4,
        compiler_params=_params("parallel"),
    )(k, wl, al, r2(w0), r2(a0), r2(k_k), r2(k_a))


def _rwkv_rec_kernel(r_ref, lw_ref, k_ref, v_ref, kk_ref, b_ref, y_ref, st_ref, *, tc, npair, mm_dtype):
    c = pl.program_id(2)

    @pl.when(c == 0)
    def _():
        st_ref[...] = jnp.zeros(st_ref.shape, F32)

    def mm(a, b, dims=None):
        a = a.astype(mm_dtype)
        b = b.astype(mm_dtype)
        prec = HIGHEST if mm_dtype == F32 else None
        if dims is None:
            return jnp.dot(a, b, precision=prec, preferred_element_type=F32)
        return lax.dot_general(a, b, dims, precision=prec, preferred_element_type=F32)

    row = _iota((tc, tc), 0)
    col = _iota((tc, tc), 1)
    tri_incl = (row >= col).astype(F32)
    strict = row > col
    incl = row >= col
    eye = (row == col).astype(F32)
    lane = _iota((tc, LANES), 1)
    head_masks = [lane < C_HEAD_DIM, lane >= C_HEAD_DIM]
    blockdiag = (_iota((LANES, LANES), 0) // C_HEAD_DIM) == (_iota((LANES, LANES), 1) // C_HEAD_DIM)
    n_sq = max((tc - 1).bit_length() - 1, 0)

    pairs = range(npair)
    chains = [(p, h) for p in pairs for h in range(2)]
    tile = lambda a, p: a[:, p * LANES:(p + 1) * LANES]
    cast = lambda a: a.astype(mm_dtype)

    r, lw, k, v, kk, b = (ref[...] for ref in (r_ref, lw_ref, k_ref, v_ref, kk_ref, b_ref))
    cum = jnp.dot(tri_incl, lw, precision=HIGHEST, preferred_element_type=F32)
    g_in = jnp.exp(cum)
    g_inv = jnp.exp(-cum)
    a_t = -kk * jnp.exp(cum - lw)
    b_t = b * g_inv
    k_t = k * g_inv
    r_t = r * g_in
    g_end = g_in[tc - 1:tc, :]
    b_c, k_c, v_c, r_c = cast(b_t), cast(k_t), cast(v), cast(r_t)
    bg_x, kg_x = cast((b_t * g_end).T), cast((k_t * g_end).T)
    g_col = g_in.T[:, tc - 1:tc]
    rows = lambda a, p: a[p * LANES:(p + 1) * LANES, :]

    a_m = {(p, h): cast(jnp.where(head_masks[h], tile(a_t, p), 0.0)) for p, h in chains}
    left = {(p, h): jnp.concatenate([a_m[p, h], cast(jnp.where(head_masks[h], tile(r_t, p), 0.0))], axis=0)
            for p, h in chains}
    gb = {c_: mm(left[c_], tile(b_c, c_[0]), _NT) for c_ in chains}
    gk = {c_: mm(left[c_], tile(k_c, c_[0]), _NT) for c_ in chains}
    a_ab = {c_: jnp.where(strict, gb[c_][:tc], 0.0) for c_ in chains}
    a_ak = {c_: cast(jnp.where(strict, gk[c_][:tc], 0.0)) for c_ in chains}
    a_rb = {c_: cast(jnp.where(incl, gb[c_][tc:], 0.0)) for c_ in chains}
    a_rk = {c_: cast(jnp.where(incl, gk[c_][tc:], 0.0)) for c_ in chains}
    akv = {c_: cast(mm(a_ak[c_], tile(v_c, c_[0]))) for c_ in chains}
    y0 = {c_: mm(a_rk[c_], tile(v_c, c_[0])) for c_ in chains}
    x = {c_: eye + a_ab[c_] for c_ in chains}
    pw = {c_: cast(a_ab[c_]) for c_ in chains}
    for _ in range(n_sq):
        pw = {c_: cast(mm(pw[c_], pw[c_])) for c_ in chains}
        x = {c_: x[c_] + mm(pw[c_], cast(x[c_])) for c_ in chains}
    x = {c_: cast(x[c_]) for c_ in chains}
    w_h = {c_: mm(x[c_], a_m[c_]) for c_ in chains}
    u_h = {c_: mm(x[c_], akv[c_]) for c_ in chains}

    s0 = {p: st_ref[p] for p in pairs}
    ws = {p: mm(jnp.concatenate([cast(w_h[p, 0] + w_h[p, 1]), tile(r_c, p)], axis=0), cast(s0[p]))
          for p in pairs}
    u = {p: ws[p][:tc] + jnp.where(head_masks[0], u_h[p, 0], u_h[p, 1]) for p in pairs}
    u_c = {p: cast(u[p]) for p in pairs}
    yb = {c_: mm(a_rb[c_], u_c[c_[0]]) for c_ in chains}
    upd = {p: mm(rows(bg_x, p), u_c[p]) + mm(rows(kg_x, p), tile(v_c, p)) for p in pairs}
    for p in pairs:
        y = ws[p][tc:] + jnp.where(head_masks[0], y0[p, 0] + yb[p, 0], y0[p, 1] + yb[p, 1])
        y_ref[:, p * LANES:(p + 1) * LANES] = y
        st_ref[p] = s0[p] * rows(g_col, p) + jnp.where(blockdiag, upd[p], 0.0)


def rwkv_rec(r, lw, k, v, kk, b, batch, seq, mm_dtype=BF16):
    m, d = r.shape
    tc = min(seq, 64)
    nc = seq // tc
    npair = 16
    gw = npair * LANES
    blk = pl.BlockSpec((tc, gw), lambda bi, g, c: (bi * nc + c, g))
    return pl.pallas_call(
        functools.partial(_rwkv_rec_kernel, tc=tc, npair=npair, mm_dtype=mm_dtype),
        name="rwkv_rec",
        grid=(batch, d // gw, nc),
        in_specs=[blk] * 6,
        out_specs=blk,
        out_shape=jax.ShapeDtypeStruct((m, d), F32),
        scratch_shapes=[pltpu.VMEM((npair, LANES, LANES), F32)],
        compiler_params=_params("parallel", "parallel", "arbitrary"),
    )(r, lw, k, v, kk, b)


def _rwkv_post_kernel(y_ref, r_ref, k_ref, v_ref, g_ref, gng_ref, gnb_ref, rk_ref, o_ref):
    p = _head_sum_matrix()
    inv_n = 1.0 / C_HEAD_DIM
    for t in range(D_MODEL // LANES):
        sl = slice(t * LANES, (t + 1) * LANES)
        y = y_ref[:, sl]
        mu = _head_sum(y, p) * inv_n
        yc = y - mu
        var = _head_sum(yc * yc, p) * inv_n
        yn = yc * lax.rsqrt(var + C_GN_EPS) * gng_ref[:, sl] + gnb_ref[:, sl]
        bonus = _head_sum(r_ref[:, sl] * k_ref[:, sl] * rk_ref[:, sl], p) * v_ref[:, sl]
        o_ref[:, sl] = ((yn + bonus) * g_ref[:, sl]).astype(o_ref.dtype)


def rwkv_post(y, r, kmod, v, g, gn_g, gn_b, r_k):
    m, d = y.shape
    bm = _largest_divisor(m, (256, 128, 64))
    big = pl.BlockSpec((bm, d), lambda i: (i, 0))
    row = pl.BlockSpec((1, d), lambda i: (0, 0))
    r2 = lambda a: a.reshape(1, d).astype(F32)
    return pl.pallas_call(
        _rwkv_post_kernel,
        name="rwkv_post",
        grid=(m // bm,),
        in_specs=[big] * 5 + [row] * 3,
        out_specs=big,
        out_shape=jax.ShapeDtypeStruct((m, d), BF16),
        compiler_params=_params("parallel"),
    )(y, r, kmod, v, g, r2(gn_g), r2(gn_b), r2(r_k))


def rwkv_layer(x, g_norm, mix, w0, w1, w2, a0, a1, a2, g1, g2, k_k, k_a, r_k, w_rkv, w_o, gn_g, gn_b,
               batch, seq):
    bf = lambda a: a.astype(BF16)
    xr, xk, xv, xw, xa, xg = rwkv_mix(x, g_norm, mix, batch, seq)
    r = matmul(xr, bf(w_rkv[0]), F32)
    k = matmul(xk, bf(w_rkv[1]), F32)
    v = matmul(xv, bf(w_rkv[2]), F32)
    wl = lora(xw, bf(w1), bf(w2), "tanh")
    al = lora(xa, bf(a1), bf(a2), "none")
    g = lora(xg, bf(g1), bf(g2), "sigmoid")
    lw, kk, kmod, b = rwkv_pre(k, wl, al, w0, a0, k_k, k_a)
    y = rwkv_rec(r, lw, kmod, v, kk, b, batch, seq)
    o = rwkv_post(y, r, kmod, v, g, gn_g, gn_b, r_k)
    return matmul(o, bf(w_o), F32, residual=x)


def kernel(x, norm_mix, norm_ffn, norm_final, ffn_gate, ffn_up, ffn_down, a_w_in, a_w_o, b_w_in, b_f_bias,
           b_w_o, c_mix, c_w0, c_w1, c_w2, c_a0, c_a1, c_a2, c_g1, c_g2, c_k_k, c_k_a, c_r_k, c_w_rkv,
           c_w_o, c_gn_g, c_gn_b):
    batch, seq, d = x.shape
    depth = norm_mix.shape[0]
    h = x.reshape(batch * seq, d)
    for i in range(depth):
        kind, j = i % 3, i // 3
        if kind == 0:
            h = dsa_layer(h, norm_mix[i], a_w_in[j], a_w_o[j], batch, seq)
        elif kind == 1:
            h = fox_layer(h, norm_mix[i], b_w_in[j], b_f_bias[j], b_w_o[j], batch, seq)
        else:
            h = rwkv_layer(h, norm_mix[i], c_mix[j], c_w0[j], c_w1[j], c_w2[j], c_a0[j], c_a1[j], c_a2[j],
                           c_g1[j], c_g2[j], c_k_k[j], c_k_a[j], c_r_k[j], c_w_rkv[j], c_w_o[j],
                           c_gn_g[j], c_gn_b[j], batch, seq)
        h = swiglu_layer(h, norm_ffn[i], ffn_gate[i], ffn_up[i], ffn_down[i])
    return rmsnorm(h, norm_final, x.dtype).reshape(batch, seq, d)
```

```python
import functools

import jax
import jax.numpy as jnp
from jax import lax
from jax.experimental import pallas as pl
from jax.experimental.pallas import tpu as pltpu

F32 = jnp.float32
BF16 = jnp.bfloat16
I32 = jnp.int32
HIGHEST = lax.Precision.HIGHEST

D_MODEL = 2048
CHUNK = 64
RMS_EPS = 1e-6
ROPE_THETA = 10000.0
A_HEADS, A_KV_HEADS, A_HEAD_DIM = 16, 4, 128
IDX_HEADS, IDX_DIM, TOPK_MAX = 16, 64, 256
B_HEADS, B_HEAD_DIM = 16, 128
C_HEAD_DIM = 64
C_GN_EPS = C_HEAD_DIM * 1e-5

LANES = 128
VMEM_LIMIT_BYTES = 56 * 1024 * 1024

NEG_BIG = -1e30
LOG2E = 1.4426950408889634
_NT = (((1,), (1,)), ((), ()))
_TN = (((0,), (0,)), ((), ()))


def _params(*sem):
    return pltpu.CompilerParams(dimension_semantics=sem, vmem_limit_bytes=VMEM_LIMIT_BYTES)


def _iota(shape, dim):
    return lax.broadcasted_iota(I32, shape, dim)


def _rms(x, g):
    return x * lax.rsqrt(jnp.mean(x * x, axis=-1, keepdims=True) + RMS_EPS) * g


def _rmsnorm_kernel(x_ref, g_ref, o_ref):
    o_ref[...] = _rms(x_ref[...], g_ref[...]).astype(o_ref.dtype)


def rmsnorm(x, g, out_dtype):
    m, d = x.shape
    bm = min(m, 512)
    return pl.pallas_call(
        _rmsnorm_kernel,
        name="rmsnorm",
        grid=(m // bm,),
        in_specs=[pl.BlockSpec((bm, d), lambda i: (i, 0)), pl.BlockSpec((1, d), lambda i: (0, 0))],
        out_specs=pl.BlockSpec((bm, d), lambda i: (i, 0)),
        out_shape=jax.ShapeDtypeStruct((m, d), out_dtype),
        compiler_params=_params("parallel"),
    )(x, g.reshape(1, d))


def _largest_divisor(n, candidates):
    for c in candidates:
        if n % c == 0:
            return c
    return n


def _mm_kernel(x_ref, w_ref, o_ref):
    o_ref[...] = jnp.dot(x_ref[...], w_ref[...], preferred_element_type=F32).astype(o_ref.dtype)


def _mm_res_kernel(x_ref, w_ref, r_ref, o_ref):
    acc = jnp.dot(x_ref[...], w_ref[...], preferred_element_type=F32)
    o_ref[...] = (r_ref[...] + acc).astype(o_ref.dtype)


def matmul(x, w, out_dtype, residual=None):
    m, k = x.shape
    n = w.shape[1]
    bm = _largest_divisor(m, (1024, 512, 256, 128)) if k <= 2048 else _largest_divisor(m, (512, 256, 128))
    bn = _largest_divisor(n, (1024, 512, 256, 128)) if k <= 2048 else _largest_divisor(n, (512, 256, 128))
    in_specs = [pl.BlockSpec((bm, k), lambda i, j: (i, 0)), pl.BlockSpec((k, bn), lambda i, j: (0, j))]
    args = [x, w]
    body = _mm_kernel
    if residual is not None:
        in_specs.append(pl.BlockSpec((bm, bn), lambda i, j: (i, j)))
        args.append(residual)
        body = _mm_res_kernel
    return pl.pallas_call(
        body,
        name=f"mm_{k}x{n}",
        grid=(m // bm, n // bn),
        in_specs=in_specs,
        out_specs=pl.BlockSpec((bm, bn), lambda i, j: (i, j)),
        out_shape=jax.ShapeDtypeStruct((m, n), out_dtype),
        compiler_params=_params("parallel", "parallel"),
    )(*args)


def _gateup_kernel(x_ref, wg_ref, wu_ref, o_ref):
    x = x_ref[...]
    g = jnp.dot(x, wg_ref[...], preferred_element_type=F32)
    u = jnp.dot(x, wu_ref[...], preferred_element_type=F32)
    o_ref[...] = (g * jax.nn.sigmoid(g) * u).astype(o_ref.dtype)


def gate_up(x, wg, wu):
    m, k = x.shape
    n = wg.shape[1]
    bm = _largest_divisor(m, (1024, 512, 256, 128))
    bn = _largest_divisor(n, (512, 256, 128))
    return pl.pallas_call(
        _gateup_kernel,
        name="gate_up",
        grid=(m // bm, n // bn),
        in_specs=[pl.BlockSpec((bm, k), lambda i, j: (i, 0)),
                  pl.BlockSpec((k, bn), lambda i, j: (0, j)),
                  pl.BlockSpec((k, bn), lambda i, j: (0, j))],
        out_specs=pl.BlockSpec((bm, bn), lambda i, j: (i, j)),
        out_shape=jax.ShapeDtypeStruct((m, n), BF16),
        compiler_params=_params("parallel", "parallel"),
    )(x, wg, wu)


def _lora_kernel(x_ref, w1_ref, w2_ref, o_ref, *, act):
    t = jnp.dot(x_ref[...], w1_ref[...], preferred_element_type=F32)
    if act == "tanh":
        t = jnp.tanh(t)
    elif act == "sigmoid":
        t = jax.nn.sigmoid(t)
    o_ref[...] = jnp.dot(t.astype(BF16), w2_ref[...], preferred_element_type=F32)


def lora(x, w1, w2, act):
    m, k = x.shape
    r = w1.shape[1]
    rp = -(-r // LANES) * LANES
    if act == "sigmoid":
        assert rp == r, "sigmoid(0) != 0: the rank must not be padded"
    w1 = jnp.pad(w1, ((0, 0), (0, rp - r)))
    w2 = jnp.pad(w2, ((0, rp - r), (0, 0)))
    n = w2.shape[1]
    bm = _largest_divisor(m, (1024, 512, 256, 128))
    return pl.pallas_call(
        functools.partial(_lora_kernel, act=act),
        name="lora_" + act,
        grid=(m // bm,),
        in_specs=[pl.BlockSpec((bm, k), lambda i: (i, 0)),
                  pl.BlockSpec((k, rp), lambda i: (0, 0)),
                  pl.BlockSpec((rp, n), lambda i: (0, 0))],
        out_specs=pl.BlockSpec((bm, n), lambda i: (i, 0)),
        out_shape=jax.ShapeDtypeStruct((m, n), F32),
        compiler_params=_params("parallel"),
    )(x, w1, w2)


def swiglu_layer(x, g_norm, w_gate, w_up, w_down):
    hn = rmsnorm(x, g_norm, BF16)
    h = gate_up(hn, w_gate.astype(BF16), w_up.astype(BF16))
    return matmul(h, w_down.astype(BF16), F32, residual=x)


def _log_sigmoid(x):
    return jnp.minimum(x, 0.0) - jnp.log1p(jnp.exp(-jnp.abs(x)))


def _fox_prep_kernel(fl_ref, bias_ref, c_ref, ct_ref, *, seq, ch):
    tri = (_iota((ch, ch), 0) >= _iota((ch, ch), 1)).astype(F32)
    carry = jnp.zeros((1, LANES), F32)
    for c in range(seq // ch):
        lf = _log_sigmoid(fl_ref[c * ch:(c + 1) * ch, :] + bias_ref[...])
        cs = jnp.dot(tri, lf, precision=HIGHEST, preferred_element_type=F32) + carry
        c2 = cs * LOG2E
        c_ref[c * ch:(c + 1) * ch, :] = c2
        ct_ref[:, c * ch:(c + 1) * ch] = c2.T
        carry = cs[ch - 1:ch, :]


def fox_prep(fl, bias_pad, batch, seq):
    ch = min(seq, 256)
    return pl.pallas_call(
        functools.partial(_fox_prep_kernel, seq=seq, ch=ch),
        name="fox_prep",
        grid=(batch,),
        in_specs=[pl.BlockSpec((seq, LANES), lambda b: (b, 0)), pl.BlockSpec((1, LANES), lambda b: (0, 0))],
        out_specs=[pl.BlockSpec((seq, LANES), lambda b: (b, 0)),
                   pl.BlockSpec((None, LANES, seq), lambda b: (b, 0, 0))],
        out_shape=[jax.ShapeDtypeStruct((batch * seq, LANES), F32),
                   jax.ShapeDtypeStruct((batch, LANES, seq), F32)],
        compiler_params=_params("parallel"),
    )(fl, bias_pad)


def _fox_attn_kernel(q_ref, k_ref, v_ref, c_ref, ck_ref, o_ref, *, bq):
    h = pl.program_id(1)
    qi = pl.program_id(2)
    q = q_ref[...]
    cq = jnp.sum(jnp.where(_iota((bq, LANES), 1) == h, c_ref[...], 0.0), axis=1, keepdims=True)

    def step(j, carry, diagonal):
        m, l, acc = carry
        ks = pl.multiple_of(j * bq, bq)
        k = k_ref[pl.ds(ks, bq), :]
        v = v_ref[pl.ds(ks, bq), :]
        z = lax.dot_general(q, k, _NT, preferred_element_type=F32) - ck_ref[:, pl.ds(ks, bq)]
        if diagonal:
            z = jnp.where(_iota((bq, bq), 1) <= _iota((bq, bq), 0), z, NEG_BIG)
        m_new = jnp.maximum(m, jnp.max(z, axis=1, keepdims=True) + cq)
        shift = m_new - cq
        alpha = jnp.exp2(m - m_new)
        p = jnp.exp2(z - shift)
        l = alpha * l + jnp.sum(p, axis=1, keepdims=True)
        acc = alpha * acc + jnp.dot(p.astype(BF16), v, preferred_element_type=F32)
        return m_new, l, acc

    init = (jnp.full((bq, 1), NEG_BIG, F32), jnp.zeros((bq, 1), F32), jnp.zeros((bq, LANES), F32))
    carry = lax.fori_loop(0, qi, functools.partial(step, diagonal=False), init)
    _, l, acc = step(qi, carry, diagonal=True)
    o_ref[...] = (acc / l).astype(o_ref.dtype)


def fox_attention(qkv, c, ck4, batch, seq):
    nh = B_HEADS
    bq = min(seq, 512)
    nq = seq // bq
    return pl.pallas_call(
        functools.partial(_fox_attn_kernel, bq=bq),
        name="fox_attn",
        grid=(batch, nh, nq),
        in_specs=[pl.BlockSpec((bq, LANES), lambda b, h, i: (b * nq + i, h)),
                  pl.BlockSpec((seq, LANES), lambda b, h, i: (b, nh + h)),
                  pl.BlockSpec((seq, LANES), lambda b, h, i: (b, 2 * nh + h)),
                  pl.BlockSpec((bq, LANES), lambda b, h, i: (b * nq + i, 0)),
                  pl.BlockSpec((None, None, 1, seq), lambda b, h, i: (b, h, 0, 0))],
        out_specs=pl.BlockSpec((bq, LANES), lambda b, h, i: (b * nq + i, h)),
        out_shape=jax.ShapeDtypeStruct((batch * seq, nh * LANES), BF16),
        compiler_params=_params("parallel", "parallel", "parallel"),
    )(qkv, qkv, qkv, c, ck4)


def fox_layer(x, g_norm, w_in, f_bias, w_o, batch, seq):
    nh = B_HEADS
    hn = rmsnorm(x, g_norm, BF16)
    n_q, n_qkv = nh * B_HEAD_DIM, 3 * nh * B_HEAD_DIM
    col_scale = jnp.where(jnp.arange(n_qkv) < n_q, B_HEAD_DIM ** -0.5 * LOG2E, 1.0).astype(F32)
    qkv = matmul(hn, (w_in[:, :n_qkv] * col_scale).astype(BF16), BF16)
    w_f = jnp.pad(w_in[:, n_qkv:], ((0, 0), (0, LANES - nh))).astype(BF16)
    fl = matmul(hn, w_f, F32)
    bias_pad = jnp.pad(f_bias.astype(F32), (0, LANES - nh)).reshape(1, LANES)
    c, ct = fox_prep(fl, bias_pad, batch, seq)
    ck4 = ct[:, :nh, :].reshape(batch, nh, 1, seq)
    o = fox_attention(qkv, c, ck4, batch, seq)
    return matmul(o, w_o.astype(BF16), F32, residual=x)


def _rope_tables(seq, head_dim):
    half = head_dim // 2
    inv = ROPE_THETA ** (-jnp.arange(half, dtype=F32) / half)
    ang = jnp.arange(seq, dtype=F32)[:, None] * inv[None, :]
    cos = jnp.cos(ang)
    sin = jnp.sin(ang)
    reps = LANES // head_dim
    cos_t = jnp.tile(jnp.concatenate([cos, cos], axis=1), (1, reps))
    sin_t = jnp.tile(jnp.concatenate([-sin, sin], axis=1), (1, reps))
    return cos_t, sin_t


def _rope128(x, cos, sin):
    return x * cos + pltpu.roll(x, 64, 1) * sin


def _rope64(x, cos, sin, lane):
    partner = jnp.where((lane % 64) < 32, pltpu.roll(x, 96, 1), pltpu.roll(x, 32, 1))
    return x * cos + partner * sin


def _dsa_prep_kernel(main_ref, tail_ref, c128_ref, s128_ref, c64_ref, s64_ref,
                     q_ref, k_ref, v_ref, qi_ref, ki_ref, wi_ref, *, br, q_scale, wi_scale):
    c128, s128 = c128_ref[...], s128_ref[...]
    c64, s64 = c64_ref[...], s64_ref[...]
    lane = _iota((br, LANES), 1)
    tile = lambda t: main_ref[:, t * LANES:(t + 1) * LANES]
    nq, nkv, npair = A_HEADS, A_KV_HEADS, IDX_HEADS // 2
    cq128, sq128 = c128 * q_scale, s128 * q_scale
    for t in range(nq):
        q_ref[:, t * LANES:(t + 1) * LANES] = _rope128(tile(t), cq128, sq128).astype(BF16)
    for t in range(nkv):
        k_ref[:, t * LANES:(t + 1) * LANES] = _rope128(tile(nq + t), c128, s128).astype(BF16)
        v_ref[:, t * LANES:(t + 1) * LANES] = tile(nq + nkv + t).astype(BF16)
    for t in range(npair):
        r = _rope64(tile(nq + 2 * nkv + t), c64, s64, lane)
        qi_ref[:, (2 * t) * LANES:(2 * t + 1) * LANES] = jnp.where(lane < 64, r, 0.0).astype(BF16)
        qi_ref[:, (2 * t + 1) * LANES:(2 * t + 2) * LANES] = jnp.where(lane >= 64, r, 0.0).astype(BF16)
    tl = tail_ref[...]
    kr = _rope64(tl, c64, s64, lane)
    ki_ref[...] = jnp.where(lane < 64, kr, pltpu.roll(kr, 64, 1)).astype(BF16)
    wi_ref[...] = tl * wi_scale


def dsa_prep(main, tail, batch, seq):
    br = min(seq, 256)
    nblk = seq // br
    c128, s128 = _rope_tables(seq, A_HEAD_DIM)
    c64, s64 = _rope_tables(seq, IDX_DIM)
    m = batch * seq
    tab = pl.BlockSpec((br, LANES), lambda i: (i % nblk, 0))
    row = lambda w: pl.BlockSpec((br, w), lambda i: (i, 0))
    return pl.pallas_call(
        functools.partial(_dsa_prep_kernel, br=br, q_scale=A_HEAD_DIM ** -0.5 * LOG2E,
                          wi_scale=IDX_HEADS ** -0.5 * IDX_DIM ** -0.5),
        name="dsa_prep",
        grid=(m // br,),
        in_specs=[row(main.shape[1]), row(LANES), tab, tab, tab, tab],
        out_specs=[row(A_HEADS * LANES), row(A_KV_HEADS * LANES), row(A_KV_HEADS * LANES),
                   row(IDX_HEADS * LANES), row(LANES), row(LANES)],
        out_shape=[jax.ShapeDtypeStruct((m, A_HEADS * LANES), BF16),
                   jax.ShapeDtypeStruct((m, A_KV_HEADS * LANES), BF16),
                   jax.ShapeDtypeStruct((m, A_KV_HEADS * LANES), BF16),
                   jax.ShapeDtypeStruct((m, IDX_HEADS * LANES), BF16),
                   jax.ShapeDtypeStruct((m, LANES), BF16),
                   jax.ShapeDtypeStruct((m, LANES), F32)],
        compiler_params=_params("parallel"),
    )(main, tail, c128, s128, c64, s64)


_MIN32 = -2 ** 31
_KEY_NEG_INF = (0xFF800000 - 2 ** 32) ^ 0x7FFFFFFF


def _order_key(s):
    b = pltpu.bitcast(s, I32)
    return jnp.where(b >= 0, b, b ^ 0x7FFFFFFF)


def _dsa_index_kernel(qi_ref, ki_ref, wi_ref, bias_ref, key_ref, *, bq, seq, kc, topk):
    qb = pl.program_id(1)
    spk = kc // LANES
    n_kc = ((qb + 1) * bq + kc - 1) // kc
    lane = _iota((bq, LANES), 1)
    wi = wi_ref[...]
    wcol = [jnp.sum(jnp.where(lane == 64 + h, wi, 0.0), axis=1, keepdims=True) for h in range(IDX_HEADS)]
    q_chunk = (qb * bq + _iota((bq, kc), 0)) // CHUNK
    k_off = _iota((bq, kc), 1)

    def kc_body(j, _):
        ks = pl.multiple_of(j * kc, kc)
        kblk = ki_ref[pl.ds(ks, kc), :]
        acc = jnp.zeros((bq, kc), F32)
        for h in range(IDX_HEADS):
            s = lax.dot_general(qi_ref[:, h * LANES:(h + 1) * LANES], kblk, _NT, preferred_element_type=F32)
            acc = acc + wcol[h] * jnp.maximum(s, 0.0)
        acc = jnp.where((ks + k_off) // CHUNK <= q_chunk, acc, -jnp.inf)
        key = _order_key(acc)
        for t in range(spk):
            key_ref[j * spk + t] = key[:, t * LANES:(t + 1) * LANES]
        return 0

    lax.fori_loop(0, n_kc, kc_body, 0)

    def count(indicator):
        def body(j, acc):
            for t in range(spk):
                n = j * spk + t
                acc = acc + indicator(key_ref[n], n)
            return acc
        acc = lax.fori_loop(0, n_kc, body, jnp.zeros((bq, LANES), F32))
        return jnp.sum(acc, axis=1, keepdims=True)

    n_adm = ((qb * bq + _iota((bq, 1), 0)) // CHUNK + 1) * CHUNK
    take_all = n_adm <= topk
    settled = lambda cnt_t: jnp.logical_or(take_all, cnt_t == topk)
    pending = lambda cnt_t: jnp.max(jnp.where(settled(cnt_t), 0, 1))

    def bit_cond(c):
        return jnp.logical_and(c[0] < 32, c[3] > 0)

    def bit_body(c):
        i, tu, cnt_t, _ = c
        cand_u = tu | lax.shift_left(jnp.int32(1), 31 - i)
        cand_s = jnp.broadcast_to(cand_u ^ _MIN32, (bq, LANES))
        cnt = count(lambda k, n: jnp.where(k >= cand_s, 1.0, 0.0))
        ok = cnt >= topk
        cnt_t = jnp.where(ok, cnt, cnt_t)
        return i + 1, jnp.where(ok, cand_u, tu), cnt_t, pending(cnt_t)

    cnt0 = jnp.zeros((bq, 1), F32) + (n_kc * kc).astype(F32)
    _, tu, cnt_ge, n_pending = lax.while_loop(
        bit_cond, bit_body, (jnp.int32(0), jnp.zeros((bq, 1), I32), cnt0, pending(cnt0)))
    ts = tu ^ _MIN32
    thr = jnp.where(take_all, _KEY_NEG_INF + 1, ts)

    def no_ties():
        return thr - 1, jnp.zeros((bq, 1), I32), jnp.full((bq, 1), -1, I32)

    def with_ties():
        tied = jnp.logical_not(settled(cnt_ge))
        cnt_gt = count(lambda k, n: jnp.where(k > ts, 1.0, 0.0))
        need = topk - cnt_gt

        def j_body(i, j0):
            cand = j0 | lax.shift_left(jnp.int32(1), (seq.bit_length() - 2) - i)
            cnt = count(lambda k, n: jnp.where(k == ts, jnp.where((n * LANES + lane) < cand, 1.0, 0.0), 0.0))
            return jnp.where(cnt < need, cand, j0)
        j0 = lax.fori_loop(0, seq.bit_length() - 1, j_body, jnp.zeros((bq, 1), I32))
        return jnp.where(tied, ts, thr - 1), jnp.where(tied, ts, 0), jnp.where(tied, j0, -1)

    gt_thr, eq_val, jlim = lax.cond(n_pending > 0, with_ties, no_ties)

    bias_ref[...] = jnp.full((bq, seq), NEG_BIG, BF16)

    def out_body(j, _):
        for t in range(spk):
            n = j * spk + t
            k = key_ref[n]
            tie_ok = jnp.where((n * LANES + lane) <= jlim, 0.0, NEG_BIG)
            b = jnp.where(k > gt_thr, 0.0, jnp.where(k == eq_val, tie_ok, NEG_BIG))
            bias_ref[:, pl.ds(pl.multiple_of(n * LANES, LANES), LANES)] = b.astype(BF16)
        return 0

    lax.fori_loop(0, n_kc, out_body, 0)


def dsa_index(qi, ki, wi, batch, seq):
    bq = min(seq, 256)
    nq = seq // bq
    kc = min(seq, 256)
    topk = min(TOPK_MAX, seq // 4)
    return pl.pallas_call(
        functools.partial(_dsa_index_kernel, bq=bq, seq=seq, kc=kc, topk=topk),
        name="dsa_index",
        grid=(batch, nq),
        in_specs=[pl.BlockSpec((bq, IDX_HEADS * LANES), lambda b, i: (b * nq + i, 0)),
                  pl.BlockSpec((seq, LANES), lambda b, i: (b, 0)),
                  pl.BlockSpec((bq, LANES), lambda b, i: (b * nq + i, 0))],
        out_specs=pl.BlockSpec((bq, seq), lambda b, i: (b * nq + i, 0)),
        out_shape=jax.ShapeDtypeStruct((batch * seq, seq), BF16),
        scratch_shapes=[pltpu.VMEM((seq // LANES, bq, LANES), I32)],
        compiler_params=_params("parallel", "parallel"),
    )(qi, ki, wi)


def _dsa_attn_kernel(q_ref, k_ref, v_ref, bias_ref, o_ref, *, bq, bk, group):
    qi = pl.program_id(2)
    q = jnp.concatenate([q_ref[:, r * LANES:(r + 1) * LANES] for r in range(group)], axis=0)
    rows = group * bq

    def body(j, carry):
        m, l, acc = carry
        ks = pl.multiple_of(j * bk, bk)
        k = k_ref[pl.ds(ks, bk), :]
        v = v_ref[pl.ds(ks, bk), :]
        bias = bias_ref[:, pl.ds(ks, bk)].astype(F32)
        s = lax.dot_general(q, k, _NT, preferred_element_type=F32)
        s = (s.reshape(group, bq, bk) + bias[None]).reshape(rows, bk)
        m_new = jnp.maximum(m, jnp.max(s, axis=1, keepdims=True))
        alpha = jnp.exp2(m - m_new)
        p = jnp.exp2(s - m_new)
        l = alpha * l + jnp.sum(p, axis=1, keepdims=True)
        acc = alpha * acc + jnp.dot(p.astype(BF16), v, preferred_element_type=F32)
        return m_new, l, acc

    n_kv = ((qi + 1) * bq + bk - 1) // bk
    init = (jnp.full((rows, 1), -jnp.inf, F32), jnp.zeros((rows, 1), F32), jnp.zeros((rows, LANES), F32))
    _, l, acc = lax.fori_loop(0, n_kv, body, init)
    o = (acc / l).astype(o_ref.dtype)
    for r in range(group):
        o_ref[:, r * LANES:(r + 1) * LANES] = o[r * bq:(r + 1) * bq, :]


def dsa_attention(q, k, v, bias, batch, seq):
    group = A_HEADS // A_KV_HEADS
    bq = bk = min(seq, 256)
    nq = seq // bq
    gw = group * LANES
    return pl.pallas_call(
        functools.partial(_dsa_attn_kernel, bq=bq, bk=bk, group=group),
        name="dsa_attn",
        grid=(batch, A_KV_HEADS, nq),
        in_specs=[pl.BlockSpec((bq, gw), lambda b, g, i: (b * nq + i, g)),
                  pl.BlockSpec((seq, LANES), lambda b, g, i: (b, g)),
                  pl.BlockSpec((seq, LANES), lambda b, g, i: (b, g)),
                  pl.BlockSpec((bq, seq), lambda b, g, i: (b * nq + i, 0))],
        out_specs=pl.BlockSpec((bq, gw), lambda b, g, i: (b * nq + i, g)),
        out_shape=jax.ShapeDtypeStruct((batch * seq, A_HEADS * LANES), BF16),
        compiler_params=_params("parallel", "parallel", "parallel"),
    )(q, k, v, bias)


def dsa_layer(x, g_norm, w_in, w_o, batch, seq):
    hn = rmsnorm(x, g_norm, BF16)
    n_main = (A_HEADS + 2 * A_KV_HEADS) * A_HEAD_DIM + IDX_HEADS * IDX_DIM
    n_tail = w_in.shape[1] - n_main
    main = matmul(hn, w_in[:, :n_main].astype(BF16), F32)
    w_tail = jnp.pad(w_in[:, n_main:], ((0, 0), (0, LANES - n_tail))).astype(BF16)
    tail = matmul(hn, w_tail, F32)
    q, k, v, qi, ki, wi = dsa_prep(main, tail, batch, seq)
    bias = dsa_index(qi, ki, wi, batch, seq)
    o = dsa_attention(q, k, v, bias, batch, seq)
    return matmul(o, w_o.astype(BF16), F32, residual=x)


def _rwkv_mix_kernel(x_ref, g_ref, mix_ref, *rest, bs):
    outs, hbuf = rest[:6], rest[6]
    j = pl.program_id(1)

    @pl.when(j == 0)
    def _():
        hbuf[0:8, :] = jnp.zeros((8, D_MODEL), F32)

    hn = _rms(x_ref[...], g_ref[...])
    hbuf[8:8 + bs, :] = hn
    xx = hbuf[7:7 + bs, :] - hn
    hbuf[7:8, :] = hn[bs - 1:bs, :]
    for i in range(6):
        outs[i][...] = (hn + xx * mix_ref[i:i + 1, :]).astype(BF16)


def rwkv_mix(x, g_norm, mix, batch, seq):
    bs = min(seq, 256)
    nb = seq // bs
    m, d = x.shape
    mix8 = jnp.pad(mix, ((0, 2), (0, 0)))
    return pl.pallas_call(
        functools.partial(_rwkv_mix_kernel, bs=bs),
        name="rwkv_mix",
        grid=(batch, nb),
        in_specs=[pl.BlockSpec((bs, d), lambda b, j: (b * nb + j, 0)),
                  pl.BlockSpec((1, d), lambda b, j: (0, 0)),
                  pl.BlockSpec((8, d), lambda b, j: (0, 0))],
        out_specs=[pl.BlockSpec((bs, d), lambda b, j: (b * nb + j, 0))] * 6,
        out_shape=[jax.ShapeDtypeStruct((m, d), BF16)] * 6,
        scratch_shapes=[pltpu.VMEM((bs + 8, d), F32)],
        compiler_params=_params("parallel", "arbitrary"),
    )(x, g_norm.reshape(1, d), mix8)


def _head_sum_matrix():
    return ((_iota((LANES, LANES), 0) // C_HEAD_DIM) == (_iota((LANES, LANES), 1) // C_HEAD_DIM)).astype(F32)


def _head_sum(x, p):
    return jnp.dot(x, p, precision=HIGHEST, preferred_element_type=F32)


def _rwkv_pre_kernel(k_ref, wl_ref, al_ref, w0_ref, a0_ref, kk_ref_, ka_ref, lw_o, kk_o, km_o, b_o):
    p = _head_sum_matrix()
    for t in range(D_MODEL // LANES):
        sl = slice(t * LANES, (t + 1) * LANES)
        k = k_ref[:, sl]
        z = w0_ref[:, sl] + wl_ref[:, sl]
        w_log = _log_sigmoid(z) - 0.5
        lw_o[:, sl] = -jnp.exp(w_log)
        a = jax.nn.sigmoid(a0_ref[:, sl] + al_ref[:, sl])
        kr = k * kk_ref_[:, sl]
        kk = kr * lax.rsqrt(jnp.maximum(_head_sum(kr * kr, p), 1e-24))
        kk_o[:, sl] = kk
        km_o[:, sl] = k * (1.0 + (a - 1.0) * ka_ref[:, sl])
        b_o[:, sl] = kk * a


def rwkv_pre(k, wl, al, w0, a0, k_k, k_a):
    m, d = k.shape
    bm = _largest_divisor(m, (256, 128, 64))
    big = pl.BlockSpec((bm, d), lambda i: (i, 0))
    row = pl.BlockSpec((1, d), lambda i: (0, 0))
    r2 = lambda a: a.reshape(1, d).astype(F32)
    return pl.pallas_call(
        _rwkv_pre_kernel,
        name="rwkv_pre",
        grid=(m // bm,),
        in_specs=[big, big, big, row, row, row, row],
        out_specs=[big] * 4,
        out_shape=[jax.ShapeDtypeStruct((m, d), F32)] * 4,
        compiler_params=_params("parallel"),
    )(k, wl, al, r2(w0), r2(a0), r2(k_k), r2(k_a))


def _rwkv_rec_kernel(r_ref, lw_ref, k_ref, v_ref, kk_ref, b_ref, y_ref, st_ref, *, tc, npair, mm_dtype):
    c = pl.program_id(2)

    @pl.when(c == 0)
    def _():
        st_ref[...] = jnp.zeros(st_ref.shape, F32)

    def mm(a, b, dims=None):
        a = a.astype(mm_dtype)
        b = b.astype(mm_dtype)
        prec = HIGHEST if mm_dtype == F32 else None
        if dims is None:
            return jnp.dot(a, b, precision=prec, preferred_element_type=F32)
        return lax.dot_general(a, b, dims, precision=prec, preferred_element_type=F32)

    row = _iota((tc, tc), 0)
    col = _iota((tc, tc), 1)
    tri_incl = (row >= col).astype(F32)
    strict = row > col
    incl = row >= col
    eye = (row == col).astype(F32)
    lane = _iota((tc, LANES), 1)
    head_masks = [lane < C_HEAD_DIM, lane >= C_HEAD_DIM]
    blockdiag = (_iota((LANES, LANES), 0) // C_HEAD_DIM) == (_iota((LANES, LANES), 1) // C_HEAD_DIM)
    n_sq = max((tc - 1).bit_length() - 1, 0)

    pairs = range(npair)
    chains = [(p, h) for p in pairs for h in range(2)]
    tile = lambda a, p: a[:, p * LANES:(p + 1) * LANES]
    cast = lambda a: a.astype(mm_dtype)

    r, lw, k, v, kk, b = (ref[...] for ref in (r_ref, lw_ref, k_ref, v_ref, kk_ref, b_ref))
    cum = jnp.dot(tri_incl, lw, precision=HIGHEST, preferred_element_type=F32)
    g_in = jnp.exp(cum)
    g_inv = jnp.exp(-cum)
    a_t = -kk * jnp.exp(cum - lw)
    b_t = b * g_inv
    k_t = k * g_inv
    r_t = r * g_in
    g_end = g_in[tc - 1:tc, :]
    b_c, k_c, v_c, r_c = cast(b_t), cast(k_t), cast(v), cast(r_t)
    bg_x, kg_x = cast((b_t * g_end).T), cast((k_t * g_end).T)
    g_col = g_in.T[:, tc - 1:tc]
    rows = lambda a, p: a[p * LANES:(p + 1) * LANES, :]

    a_m = {(p, h): cast(jnp.where(head_masks[h], tile(a_t, p), 0.0)) for p, h in chains}
    left = {(p, h): jnp.concatenate([a_m[p, h], cast(jnp.where(head_masks[h], tile(r_t, p), 0.0))], axis=0)
            for p, h in chains}
    gb = {c_: mm(left[c_], tile(b_c, c_[0]), _NT) for c_ in chains}
    gk = {c_: mm(left[c_], tile(k_c, c_[0]), _NT) for c_ in chains}
    a_ab = {c_: jnp.where(strict, gb[c_][:tc], 0.0) for c_ in chains}
    a_ak = {c_: cast(jnp.where(strict, gk[c_][:tc], 0.0)) for c_ in chains}
    a_rb = {c_: cast(jnp.where(incl, gb[c_][tc:], 0.0)) for c_ in chains}
    a_rk = {c_: cast(jnp.where(incl, gk[c_][tc:], 0.0)) for c_ in chains}
    akv = {c_: cast(mm(a_ak[c_], tile(v_c, c_[0]))) for c_ in chains}
    y0 = {c_: mm(a_rk[c_], tile(v_c, c_[0])) for c_ in chains}
    x = {c_: eye + a_ab[c_] for c_ in chains}
    pw = {c_: cast(a_ab[c_]) for c_ in chains}
    for _ in range(n_sq):
        pw = {c_: cast(mm(pw[c_], pw[c_])) for c_ in chains}
        x = {c_: x[c_] + mm(pw[c_], cast(x[c_])) for c_ in chains}
    x = {c_: cast(x[c_]) for c_ in chains}
    w_h = {c_: mm(x[c_], a_m[c_]) for c_ in chains}
    u_h = {c_: mm(x[c_], akv[c_]) for c_ in chains}

    s0 = {p: st_ref[p] for p in pairs}
    ws = {p: mm(jnp.concatenate([cast(w_h[p, 0] + w_h[p, 1]), tile(r_c, p)], axis=0), cast(s0[p]))
          for p in pairs}
    u = {p: ws[p][:tc] + jnp.where(head_masks[0], u_h[p, 0], u_h[p, 1]) for p in pairs}
    u_c = {p: cast(u[p]) for p in pairs}
    yb = {c_: mm(a_rb[c_], u_c[c_[0]]) for c_ in chains}
    upd = {p: mm(rows(bg_x, p), u_c[p]) + mm(rows(kg_x, p), tile(v_c, p)) for p in pairs}
    for p in pairs:
        y = ws[p][tc:] + jnp.where(head_masks[0], y0[p, 0] + yb[p, 0], y0[p, 1] + yb[p, 1])
        y_ref[:, p * LANES:(p + 1) * LANES] = y
        st_ref[p] = s0[p] * rows(g_col, p) + jnp.where(blockdiag, upd[p], 0.0)


def rwkv_rec(r, lw, k, v, kk, b, batch, seq, mm_dtype=BF16):
    m, d = r.shape
    tc = min(seq, 64)
    nc = seq // tc
    npair = 16
    gw = npair * LANES
    blk = pl.BlockSpec((tc, gw), lambda bi, g, c: (bi * nc + c, g))
    return pl.pallas_call(
        functools.partial(_rwkv_rec_kernel, tc=tc, npair=npair, mm_dtype=mm_dtype),
        name="rwkv_rec",
        grid=(batch, d // gw, nc),
        in_specs=[blk] * 6,
        out_specs=blk,
        out_shape=jax.ShapeDtypeStruct((m, d), F32),
        scratch_shapes=[pltpu.VMEM((npair, LANES, LANES), F32)],
        compiler_params=_params("parallel", "parallel", "arbitrary"),
    )(r, lw, k, v, kk, b)


def _rwkv_post_kernel(y_ref, r_ref, k_ref, v_ref, g_ref, gng_ref, gnb_ref, rk_ref, o_ref):
    p = _head_sum_matrix()
    inv_n = 1.0 / C_HEAD_DIM
    for t in range(D_MODEL // LANES):
        sl = slice(t * LANES, (t + 1) * LANES)
        y = y_ref[:, sl]
        mu = _head_sum(y, p) * inv_n
        yc = y - mu
        var = _head_sum(yc * yc, p) * inv_n
        yn = yc * lax.rsqrt(var + C_GN_EPS) * gng_ref[:, sl] + gnb_ref[:, sl]
        bonus = _head_sum(r_ref[:, sl] * k_ref[:, sl] * rk_ref[:, sl], p) * v_ref[:, sl]
        o_ref[:, sl] = ((yn + bonus) * g_ref[:, sl]).astype(o_ref.dtype)


def rwkv_post(y, r, kmod, v, g, gn_g, gn_b, r_k):
    m, d = y.shape
    bm = _largest_divisor(m, (256, 128, 64))
    big = pl.BlockSpec((bm, d), lambda i: (i, 0))
    row = pl.BlockSpec((1, d), lambda i: (0, 0))
    r2 = lambda a: a.reshape(1, d).astype(F32)
    return pl.pallas_call(
        _rwkv_post_kernel,
        name="rwkv_post",
        grid=(m // bm,),
        in_specs=[big] * 5 + [row] * 3,
        out_specs=big,
        out_shape=jax.ShapeDtypeStruct((m, d), BF16),
        compiler_params=_params("parallel"),
    )(y, r, kmod, v, g, r2(gn_g), r2(gn_b), r2(r_k))


def rwkv_layer(x, g_norm, mix, w0, w1, w2, a0, a1, a2, g1, g2, k_k, k_a, r_k, w_rkv, w_o, gn_g, gn_b,
               batch, seq):
    bf = lambda a: a.astype(BF16)
    xr, xk, xv, xw, xa, xg = rwkv_mix(x, g_norm, mix, batch, seq)
    r = matmul(xr, bf(w_rkv[0]), F32)
    k = matmul(xk, bf(w_rkv[1]), F32)
    v = matmul(xv, bf(w_rkv[2]), F32)
    wl = lora(xw, bf(w1), bf(w2), "tanh")
    al = lora(xa, bf(a1), bf(a2), "none")
    g = lora(xg, bf(g1), bf(g2), "sigmoid")
    lw, kk, kmod, b = rwkv_pre(k, wl, al, w0, a0, k_k, k_a)
    y = rwkv_rec(r, lw, kmod, v, kk, b, batch, seq)
    o = rwkv_post(y, r, kmod, v, g, gn_g, gn_b, r_k)
    return matmul(o, bf(w_o), F32, residual=x)


def kernel(x, norm_mix, norm_ffn, norm_final, ffn_gate, ffn_up, ffn_down, a_w_in, a_w_o, b_w_in, b_f_bias,
           b_w_o, c_mix, c_w0, c_w1, c_w2, c_a0, c_a1, c_a2, c_g1, c_g2, c_k_k, c_k_a, c_r_k, c_w_rkv,
           c_w_o, c_gn_g, c_gn_b):
    batch, seq, d = x.shape
    depth = norm_mix.shape[0]
    h = x.reshape(batch * seq, d)
    for i in range(depth):
        kind, j = i % 3, i // 3
        if kind == 0:
            h = dsa_layer(h, norm_mix[i], a_w_in[j], a_w_o[j], batch, seq)
        elif kind == 1:
            h = fox_layer(h, norm_mix[i], b_w_in[j], b_f_bias[j], b_w_o[j], batch, seq)
        else:
            h = rwkv_layer(h, norm_mix[i], c_mix[j], c_w0[j], c_w1[j], c_w2[j], c_a0[j], c_a1[j], c_a2[j],
                           c_g1[j], c_g2[j], c_k_k[j], c_k_a[j], c_r_k[j], c_w_rkv[j], c_w_o[j],
                           c_gn_g[j], c_gn_b[j], batch, seq)
        h = swiglu_layer(h, norm_ffn[i], ffn_gate[i], ffn_up[i], ffn_down[i])
    return rmsnorm(h, norm_final, x.dtype).reshape(batch, seq, d)
```

```python
import functools

import jax
import jax.numpy as jnp
from jax import lax
from jax.experimental import pallas as pl
from jax.experimental.pallas import tpu as pltpu

F32 = jnp.float32
BF16 = jnp.bfloat16
I32 = jnp.int32
HIGHEST = lax.Precision.HIGHEST

D_MODEL = 2048
CHUNK = 64
RMS_EPS = 1e-6
ROPE_THETA = 10000.0
A_HEADS, A_KV_HEADS, A_HEAD_DIM = 16, 4, 128
IDX_HEADS, IDX_DIM, TOPK_MAX = 16, 64, 256
B_HEADS, B_HEAD_DIM = 16, 128
C_HEAD_DIM = 64
C_GN_EPS = C_HEAD_DIM * 1e-5

LANES = 128
VMEM_LIMIT_BYTES = 56 * 1024 * 1024

NEG_BIG = -1e30
LOG2E = 1.4426950408889634
_NT = (((1,), (1,)), ((), ()))
_TN = (((0,), (0,)), ((), ()))


def _params(*sem):
    return pltpu.CompilerParams(dimension_semantics=sem, vmem_limit_bytes=VMEM_LIMIT_BYTES)


def _iota(shape, dim):
    return lax.broadcasted_iota(I32, shape, dim)


def _rms(x, g):
    return x * lax.rsqrt(jnp.mean(x * x, axis=-1, keepdims=True) + RMS_EPS) * g


def _rmsnorm_kernel(x_ref, g_ref, o_ref):
    o_ref[...] = _rms(x_ref[...], g_ref[...]).astype(o_ref.dtype)


def rmsnorm(x, g, out_dtype):
    m, d = x.shape
    bm = min(m, 512)
    return pl.pallas_call(
        _rmsnorm_kernel,
        name="rmsnorm",
        grid=(m // bm,),
        in_specs=[pl.BlockSpec((bm, d), lambda i: (i, 0)), pl.BlockSpec((1, d), lambda i: (0, 0))],
        out_specs=pl.BlockSpec((bm, d), lambda i: (i, 0)),
        out_shape=jax.ShapeDtypeStruct((m, d), out_dtype),
        compiler_params=_params("parallel"),
    )(x, g.reshape(1, d))


def _largest_divisor(n, candidates):
    for c in candidates:
        if n % c == 0:
            return c
    return n


def _mm_kernel(x_ref, w_ref, o_ref):
    o_ref[...] = jnp.dot(x_ref[...], w_ref[...], preferred_element_type=F32).astype(o_ref.dtype)


def _mm_res_kernel(x_ref, w_ref, r_ref, o_ref):
    acc = jnp.dot(x_ref[...], w_ref[...], preferred_element_type=F32)
    o_ref[...] = (r_ref[...] + acc).astype(o_ref.dtype)


def matmul(x, w, out_dtype, residual=None):
    m, k = x.shape
    n = w.shape[1]
    bm = _largest_divisor(m, (1024, 512, 256, 128)) if k <= 2048 else _largest_divisor(m, (512, 256, 128))
    bn = _largest_divisor(n, (1024, 512, 256, 128)) if k <= 2048 else _largest_divisor(n, (512, 256, 128))
    in_specs = [pl.BlockSpec((bm, k), lambda i, j: (i, 0)), pl.BlockSpec((k, bn), lambda i, j: (0, j))]
    args = [x, w]
    body = _mm_kernel
    if residual is not None:
        in_specs.append(pl.BlockSpec((bm, bn), lambda i, j: (i, j)))
        args.append(residual)
        body = _mm_res_kernel
    return pl.pallas_call(
        body,
        name=f"mm_{k}x{n}",
        grid=(m // bm, n // bn),
        in_specs=in_specs,
        out_specs=pl.BlockSpec((bm, bn), lambda i, j: (i, j)),
        out_shape=jax.ShapeDtypeStruct((m, n), out_dtype),
        compiler_params=_params("parallel", "parallel"),
    )(*args)


def _gateup_kernel(x_ref, wg_ref, wu_ref, o_ref):
    x = x_ref[...]
    g = jnp.dot(x, wg_ref[...], preferred_element_type=F32)
    u = jnp.dot(x, wu_ref[...], preferred_element_type=F32)
    o_ref[...] = (g * jax.nn.sigmoid(g) * u).astype(o_ref.dtype)


def gate_up(x, wg, wu):
    m, k = x.shape
    n = wg.shape[1]
    bm = _largest_divisor(m, (1024, 512, 256, 128))
    bn = _largest_divisor(n, (512, 256, 128))
    return pl.pallas_call(
        _gateup_kernel,
        name="gate_up",
        grid=(m // bm, n // bn),
        in_specs=[pl.BlockSpec((bm, k), lambda i, j: (i, 0)),
                  pl.BlockSpec((k, bn), lambda i, j: (0, j)),
                  pl.BlockSpec((k, bn), lambda i, j: (0, j))],
        out_specs=pl.BlockSpec((bm, bn), lambda i, j: (i, j)),
        out_shape=jax.ShapeDtypeStruct((m, n), BF16),
        compiler_params=_params("parallel", "parallel"),
    )(x, wg, wu)


def _lora_kernel(x_ref, w1_ref, w2_ref, o_ref, *, act):
    t = jnp.dot(x_ref[...], w1_ref[...], preferred_element_type=F32)
    if act == "tanh":
        t = jnp.tanh(t)
    elif act == "sigmoid":
        t = jax.nn.sigmoid(t)
    o_ref[...] = jnp.dot(t.astype(BF16), w2_ref[...], preferred_element_type=F32)


def lora(x, w1, w2, act):
    m, k = x.shape
    r = w1.shape[1]
    rp = -(-r // LANES) * LANES
    if act == "sigmoid":
        assert rp == r, "sigmoid(0) != 0: the rank must not be padded"
    w1 = jnp.pad(w1, ((0, 0), (0, rp - r)))
    w2 = jnp.pad(w2, ((0, rp - r), (0, 0)))
    n = w2.shape[1]
    bm = _largest_divisor(m, (1024, 512, 256, 128))
    return pl.pallas_call(
        functools.partial(_lora_kernel, act=act),
        name="lora_" + act,
        grid=(m // bm,),
        in_specs=[pl.BlockSpec((bm, k), lambda i: (i, 0)),
                  pl.BlockSpec((k, rp), lambda i: (0, 0)),
                  pl.BlockSpec((rp, n), lambda i: (0, 0))],
        out_specs=pl.BlockSpec((bm, n), lambda i: (i, 0)),
        out_shape=jax.ShapeDtypeStruct((m, n), F32),
        compiler_params=_params("parallel"),
    )(x, w1, w2)


def swiglu_layer(x, g_norm, w_gate, w_up, w_down):
    hn = rmsnorm(x, g_norm, BF16)
    h = gate_up(hn, w_gate.astype(BF16), w_up.astype(BF16))
    return matmul(h, w_down.astype(BF16), F32, residual=x)


def _log_sigmoid(x):
    return jnp.minimum(x, 0.0) - jnp.log1p(jnp.exp(-jnp.abs(x)))


def _fox_prep_kernel(fl_ref, bias_ref, c_ref, ct_ref, *, seq, ch):
    tri = (_iota((ch, ch), 0) >= _iota((ch, ch), 1)).astype(F32)
    carry = jnp.zeros((1, LANES), F32)
    for c in range(seq // ch):
        lf = _log_sigmoid(fl_ref[c * ch:(c + 1) * ch, :] + bias_ref[...])
        cs = jnp.dot(tri, lf, precision=HIGHEST, preferred_element_type=F32) + carry
        c2 = cs * LOG2E
        c_ref[c * ch:(c + 1) * ch, :] = c2
        ct_ref[:, c * ch:(c + 1) * ch] = c2.T
        carry = cs[ch - 1:ch, :]


def fox_prep(fl, bias_pad, batch, seq):
    ch = min(seq, 256)
    return pl.pallas_call(
        functools.partial(_fox_prep_kernel, seq=seq, ch=ch),
        name="fox_prep",
        grid=(batch,),
        in_specs=[pl.BlockSpec((seq, LANES), lambda b: (b, 0)), pl.BlockSpec((1, LANES), lambda b: (0, 0))],
        out_specs=[pl.BlockSpec((seq, LANES), lambda b: (b, 0)),
                   pl.BlockSpec((None, LANES, seq), lambda b: (b, 0, 0))],
        out_shape=[jax.ShapeDtypeStruct((batch * seq, LANES), F32),
                   jax.ShapeDtypeStruct((batch, LANES, seq), F32)],
        compiler_params=_params("parallel"),
    )(fl, bias_pad)


def _softmax_strip(z, r0, m_in_ref, m_out_ref, p_ref, row_term=None):
    rows, bk = z.shape
    tiles = [z[:, t * LANES:(t + 1) * LANES] for t in range(bk // LANES)]
    zmax = functools.reduce(jnp.maximum, tiles)
    zmax = jnp.broadcast_to(jnp.max(zmax, axis=1, keepdims=True), (rows, LANES))
    if row_term is not None:
        zmax = zmax + row_term
    m_new = jnp.maximum(m_in_ref[r0:r0 + rows, :], zmax)
    m_out_ref[r0:r0 + rows, :] = m_new
    shift = m_new if row_term is None else m_new - row_term
    for t, zt in enumerate(tiles):
        p_ref[r0:r0 + rows, t * LANES:(t + 1) * LANES] = jnp.exp2(zt - shift).astype(BF16)


def _accumulate(acc_ref, m_in_ref, m_out_ref, p_ref, v, row_lo=0):
    alpha = jnp.exp2(m_in_ref[row_lo:, :] - m_out_ref[row_lo:, :])
    v1 = jnp.concatenate([v, jnp.ones(v.shape, v.dtype)], axis=1)
    pv = jnp.dot(p_ref[row_lo:, :], v1, preferred_element_type=F32)
    for t in range(2):
        sl = slice(t * LANES, (t + 1) * LANES)
        acc_ref[row_lo:, sl] = alpha * acc_ref[row_lo:, sl] + pv[:, sl]


def _fox_attn_kernel(q_ref, k_ref, v_ref, c_ref, ck_ref, o_ref, s0_ref, s1_ref, p_ref, m0_ref, m1_ref, acc_ref,
                     *, bq, bk, strip):
    h = pl.program_id(1)
    qi = pl.program_id(2)
    q = q_ref[...]
    cq = jnp.sum(jnp.where(_iota((bq, LANES), 1) == h, c_ref[...], 0.0), axis=1, keepdims=True)
    cq = jnp.broadcast_to(cq, (bq, LANES))
    m0_ref[...] = jnp.full((bq, LANES), NEG_BIG, F32)
    acc_ref[...] = jnp.zeros((bq, 2 * LANES), F32)
    even, odd = (s0_ref, m0_ref, m1_ref), (s1_ref, m1_ref, m0_ref)

    def scores(t, bufs, row_lo=0):
        ks = pl.multiple_of(t * bk, bk)
        bufs[0][row_lo:, :] = lax.dot_general(q[row_lo:], k_ref[pl.ds(ks, bk), :], _NT,
                                              preferred_element_type=F32)

    def chunk(t, bufs, row_lo=0, diag_off=None):
        s_ref, m_in_ref, m_out_ref = bufs
        ks = pl.multiple_of(t * bk, bk)
        ck = ck_ref[:, pl.ds(ks, bk)]
        for r0 in range(row_lo, bq, strip):
            z = s_ref[r0:r0 + strip, :] - ck
            if diag_off is not None and diag_off + bk - 1 > r0:
                visible = diag_off + _iota((strip, bk), 1) <= r0 + _iota((strip, bk), 0)
                z = jnp.where(visible, z, NEG_BIG)
            _softmax_strip(z, r0, m_in_ref, m_out_ref, p_ref, row_term=cq[r0:r0 + strip])
        _accumulate(acc_ref, m_in_ref, m_out_ref, p_ref, v_ref[pl.ds(ks, bk), :], row_lo)

    assert bq == 2 * bk
    scores(0, even)

    def pair(jp, _):
        scores(2 * jp + 1, odd)
        chunk(2 * jp, even)
        scores(2 * jp + 2, even)
        chunk(2 * jp + 1, odd)
        return 0

    lax.fori_loop(0, qi, pair, 0)
    scores(2 * qi + 1, odd, row_lo=bk)
    chunk(2 * qi, even, diag_off=0)
    chunk(2 * qi + 1, odd, row_lo=bk, diag_off=bk)
    o_ref[...] = (acc_ref[:, :LANES] / acc_ref[:, LANES:]).astype(o_ref.dtype)


def fox_attention(qkv, c, ck4, batch, seq):
    nh = B_HEADS
    bq = min(seq, 512)
    bk = min(seq, 256)
    nq = seq // bq
    return pl.pallas_call(
        functools.partial(_fox_attn_kernel, bq=bq, bk=bk, strip=min(bq, 128)),
        name="fox_attn",
        grid=(batch, nh, nq),
        in_specs=[pl.BlockSpec((bq, LANES), lambda b, h, i: (b * nq + i, h)),
                  pl.BlockSpec((seq, LANES), lambda b, h, i: (b, nh + h)),
                  pl.BlockSpec((seq, LANES), lambda b, h, i: (b, 2 * nh + h)),
                  pl.BlockSpec((bq, LANES), lambda b, h, i: (b * nq + i, 0)),
                  pl.BlockSpec((None, None, 1, seq), lambda b, h, i: (b, h, 0, 0))],
        out_specs=pl.BlockSpec((bq, LANES), lambda b, h, i: (b * nq + i, h)),
        out_shape=jax.ShapeDtypeStruct((batch * seq, nh * LANES), BF16),
        scratch_shapes=[pltpu.VMEM((bq, bk), F32), pltpu.VMEM((bq, bk), F32), pltpu.VMEM((bq, bk), BF16),
                        pltpu.VMEM((bq, LANES), F32), pltpu.VMEM((bq, LANES), F32),
                        pltpu.VMEM((bq, 2 * LANES), F32)],
        compiler_params=_params("parallel", "parallel", "parallel"),
    )(qkv, qkv, qkv, c, ck4)


def fox_layer(x, g_norm, w_in, f_bias, w_o, batch, seq):
    nh = B_HEADS
    hn = rmsnorm(x, g_norm, BF16)
    n_q, n_qkv = nh * B_HEAD_DIM, 3 * nh * B_HEAD_DIM
    col_scale = jnp.where(jnp.arange(n_qkv) < n_q, B_HEAD_DIM ** -0.5 * LOG2E, 1.0).astype(F32)
    qkv = matmul(hn, (w_in[:, :n_qkv] * col_scale).astype(BF16), BF16)
    w_f = jnp.pad(w_in[:, n_qkv:], ((0, 0), (0, LANES - nh))).astype(BF16)
    fl = matmul(hn, w_f, F32)
    bias_pad = jnp.pad(f_bias.astype(F32), (0, LANES - nh)).reshape(1, LANES)
    c, ct = fox_prep(fl, bias_pad, batch, seq)
    ck4 = ct[:, :nh, :].reshape(batch, nh, 1, seq)
    o = fox_attention(qkv, c, ck4, batch, seq)
    return matmul(o, w_o.astype(BF16), F32, residual=x)


def _rope_tables(seq, head_dim):
    half = head_dim // 2
    inv = ROPE_THETA ** (-jnp.arange(half, dtype=F32) / half)
    ang = jnp.arange(seq, dtype=F32)[:, None] * inv[None, :]
    cos = jnp.cos(ang)
    sin = jnp.sin(ang)
    reps = LANES // head_dim
    cos_t = jnp.tile(jnp.concatenate([cos, cos], axis=1), (1, reps))
    sin_t = jnp.tile(jnp.concatenate([-sin, sin], axis=1), (1, reps))
    return cos_t, sin_t


def _rope128(x, cos, sin):
    return x * cos + pltpu.roll(x, 64, 1) * sin


def _rope64(x, cos, sin, lane):
    partner = jnp.where((lane % 64) < 32, pltpu.roll(x, 96, 1), pltpu.roll(x, 32, 1))
    return x * cos + partner * sin


def _dsa_prep_kernel(main_ref, tail_ref, c128_ref, s128_ref, c64_ref, s64_ref,
                     q_ref, k_ref, v_ref, qi_ref, ki_ref, wi_ref, *, br, q_scale, wi_scale):
    c128, s128 = c128_ref[...], s128_ref[...]
    c64, s64 = c64_ref[...], s64_ref[...]
    lane = _iota((br, LANES), 1)
    tile = lambda t: main_ref[:, t * LANES:(t + 1) * LANES]
    nq, nkv, npair = A_HEADS, A_KV_HEADS, IDX_HEADS // 2
    cq128, sq128 = c128 * q_scale, s128 * q_scale
    for t in range(nq):
        q_ref[:, t * LANES:(t + 1) * LANES] = _rope128(tile(t), cq128, sq128).astype(BF16)
    for t in range(nkv):
        k_ref[:, t * LANES:(t + 1) * LANES] = _rope128(tile(nq + t), c128, s128).astype(BF16)
        v_ref[:, t * LANES:(t + 1) * LANES] = tile(nq + nkv + t).astype(BF16)
    for t in range(npair):
        r = _rope64(tile(nq + 2 * nkv + t), c64, s64, lane)
        qi_ref[:, (2 * t) * LANES:(2 * t + 1) * LANES] = jnp.where(lane < 64, r, 0.0).astype(BF16)
        qi_ref[:, (2 * t + 1) * LANES:(2 * t + 2) * LANES] = jnp.where(lane >= 64, r, 0.0).astype(BF16)
    tl = tail_ref[...]
    kr = _rope64(tl, c64, s64, lane)
    ki_ref[...] = jnp.where(lane < 64, kr, pltpu.roll(kr, 64, 1)).astype(BF16)
    wi_ref[...] = tl * wi_scale


def dsa_prep(main, tail, batch, seq):
    br = min(seq, 256)
    nblk = seq // br
    c128, s128 = _rope_tables(seq, A_HEAD_DIM)
    c64, s64 = _rope_tables(seq, IDX_DIM)
    m = batch * seq
    tab = pl.BlockSpec((br, LANES), lambda i: (i % nblk, 0))
    row = lambda w: pl.BlockSpec((br, w), lambda i: (i, 0))
    return pl.pallas_call(
        functools.partial(_dsa_prep_kernel, br=br, q_scale=A_HEAD_DIM ** -0.5 * LOG2E,
                          wi_scale=IDX_HEADS ** -0.5 * IDX_DIM ** -0.5),
        name="dsa_prep",
        grid=(m // br,),
        in_specs=[row(main.shape[1]), row(LANES), tab, tab, tab, tab],
        out_specs=[row(A_HEADS * LANES), row(A_KV_HEADS * LANES), row(A_KV_HEADS * LANES),
                   row(IDX_HEADS * LANES), row(LANES), row(LANES)],
        out_shape=[jax.ShapeDtypeStruct((m, A_HEADS * LANES), BF16),
                   jax.ShapeDtypeStruct((m, A_KV_HEADS * LANES), BF16),
                   jax.ShapeDtypeStruct((m, A_KV_HEADS * LANES), BF16),
                   jax.ShapeDtypeStruct((m, IDX_HEADS * LANES), BF16),
                   jax.ShapeDtypeStruct((m, LANES), BF16),
                   jax.ShapeDtypeStruct((m, LANES), F32)],
        compiler_params=_params("parallel"),
    )(main, tail, c128, s128, c64, s64)


_MIN32 = -2 ** 31
_KEY_NEG_INF = (0xFF800000 - 2 ** 32) ^ 0x7FFFFFFF


def _order_key(s):
    b = pltpu.bitcast(s, I32)
    return jnp.where(b >= 0, b, b ^ 0x7FFFFFFF)


def _dsa_index_kernel(qi_ref, ki_ref, wi_ref, bias_ref, key_ref, *, bq, seq, kc, topk):
    qb = pl.program_id(1)
    spk = kc // LANES
    n_kc = ((qb + 1) * bq + kc - 1) // kc
    lane = _iota((bq, LANES), 1)
    wi = wi_ref[...]
    wcol = [jnp.sum(jnp.where(lane == 64 + h, wi, 0.0), axis=1, keepdims=True) for h in range(IDX_HEADS)]
    q_chunk = (qb * bq + _iota((bq, kc), 0)) // CHUNK
    k_off = _iota((bq, kc), 1)

    def kc_body(j, _):
        ks = pl.multiple_of(j * kc, kc)
        kblk = ki_ref[pl.ds(ks, kc), :]
        acc = jnp.zeros((bq, kc), F32)
        for h in range(IDX_HEADS):
            s = lax.dot_general(qi_ref[:, h * LANES:(h + 1) * LANES], kblk, _NT, preferred_element_type=F32)
            acc = acc + wcol[h] * jnp.maximum(s, 0.0)
        acc = jnp.where((ks + k_off) // CHUNK <= q_chunk, acc, -jnp.inf)
        key = _order_key(acc)
        for t in range(spk):
            key_ref[j * spk + t] = key[:, t * LANES:(t + 1) * LANES]
        return 0

    lax.fori_loop(0, n_kc, kc_body, 0)

    def count(indicator):
        def body(j, acc):
            for t in range(spk):
                n = j * spk + t
                acc = acc + indicator(key_ref[n], n)
            return acc
        acc = lax.fori_loop(0, n_kc, body, jnp.zeros((bq, LANES), F32))
        return jnp.sum(acc, axis=1, keepdims=True)

    n_adm = ((qb * bq + _iota((bq, 1), 0)) // CHUNK + 1) * CHUNK
    take_all = n_adm <= topk
    settled = lambda cnt_t: jnp.logical_or(take_all, cnt_t == topk)
    pending = lambda cnt_t: jnp.max(jnp.where(settled(cnt_t), 0, 1))

    def bit_cond(c):
        return jnp.logical_and(c[0] < 32, c[3] > 0)

    def bit_body(c):
        i, tu, cnt_t, _ = c
        cand_u = tu | lax.shift_left(jnp.int32(1), 31 - i)
        cand_s = jnp.broadcast_to(cand_u ^ _MIN32, (bq, LANES))
        cnt = count(lambda k, n: jnp.where(k >= cand_s, 1.0, 0.0))
        ok = cnt >= topk
        cnt_t = jnp.where(ok, cnt, cnt_t)
        return i + 1, jnp.where(ok, cand_u, tu), cnt_t, pending(cnt_t)

    cnt0 = jnp.zeros((bq, 1), F32) + (n_kc * kc).astype(F32)
    _, tu, cnt_ge, n_pending = lax.while_loop(
        bit_cond, bit_body, (jnp.int32(0), jnp.zeros((bq, 1), I32), cnt0, pending(cnt0)))
    ts = tu ^ _MIN32
    thr = jnp.where(take_all, _KEY_NEG_INF + 1, ts)

    def no_ties():
        return thr - 1, jnp.zeros((bq, 1), I32), jnp.full((bq, 1), -1, I32)

    def with_ties():
        tied = jnp.logical_not(settled(cnt_ge))
        cnt_gt = count(lambda k, n: jnp.where(k > ts, 1.0, 0.0))
        need = topk - cnt_gt

        def j_body(i, j0):
            cand = j0 | lax.shift_left(jnp.int32(1), (seq.bit_length() - 2) - i)
            cnt = count(lambda k, n: jnp.where(k == ts, jnp.where((n * LANES + lane) < cand, 1.0, 0.0), 0.0))
            return jnp.where(cnt < need, cand, j0)
        j0 = lax.fori_loop(0, seq.bit_length() - 1, j_body, jnp.zeros((bq, 1), I32))
        return jnp.where(tied, ts, thr - 1), jnp.where(tied, ts, 0), jnp.where(tied, j0, -1)

    gt_thr, eq_val, jlim = lax.cond(n_pending > 0, with_ties, no_ties)

    bias_ref[...] = jnp.full((bq, seq), NEG_BIG, BF16)

    def out_body(j, _):
        for t in range(spk):
            n = j * spk + t
            k = key_ref[n]
            tie_ok = jnp.where((n * LANES + lane) <= jlim, 0.0, NEG_BIG)
            b = jnp.where(k > gt_thr, 0.0, jnp.where(k == eq_val, tie_ok, NEG_BIG))
            bias_ref[:, pl.ds(pl.multiple_of(n * LANES, LANES), LANES)] = b.astype(BF16)
        return 0

    lax.fori_loop(0, n_kc, out_body, 0)


def dsa_index(qi, ki, wi, batch, seq):
    bq = min(seq, 256)
    nq = seq // bq
    kc = min(seq, 256)
    topk = min(TOPK_MAX, seq // 4)
    return pl.pallas_call(
        functools.partial(_dsa_index_kernel, bq=bq, seq=seq, kc=kc, topk=topk),
        name="dsa_index",
        grid=(batch, nq),
        in_specs=[pl.BlockSpec((bq, IDX_HEADS * LANES), lambda b, i: (b * nq + i, 0)),
                  pl.BlockSpec((seq, LANES), lambda b, i: (b, 0)),
                  pl.BlockSpec((bq, LANES), lambda b, i: (b * nq + i, 0))],
        out_specs=pl.BlockSpec((bq, seq), lambda b, i: (b * nq + i, 0)),
        out_shape=jax.ShapeDtypeStruct((batch * seq, seq), BF16),
        scratch_shapes=[pltpu.VMEM((seq // LANES, bq, LANES), I32)],
        compiler_params=_params("parallel", "parallel"),
    )(qi, ki, wi)


def _dsa_attn_kernel(q_ref, k_ref, v_ref, bias_ref, o_ref, s0_ref, s1_ref, p_ref, m0_ref, m1_ref, acc_ref,
                     *, bq, bk, group, strip):
    qi = pl.program_id(2)
    q = jnp.concatenate([q_ref[:, r * LANES:(r + 1) * LANES] for r in range(group)], axis=0)
    rows = group * bq
    m0_ref[...] = jnp.full((rows, LANES), NEG_BIG, F32)
    acc_ref[...] = jnp.zeros((rows, 2 * LANES), F32)
    even, odd = (s0_ref, m0_ref, m1_ref), (s1_ref, m1_ref, m0_ref)

    n_kv = ((qi + 1) * bq + bk - 1) // bk

    def scores(j, bufs):
        ks = pl.multiple_of(jnp.minimum(j, n_kv - 1) * bk, bk)
        bufs[0][...] = lax.dot_general(q, k_ref[pl.ds(ks, bk), :], _NT, preferred_element_type=F32)

    def chunk(j, bufs):
        s_ref, m_in_ref, m_out_ref = bufs
        ks = pl.multiple_of(j * bk, bk)
        for b0 in range(0, bq, strip):
            bias = bias_ref[b0:b0 + strip, pl.ds(ks, bk)].astype(F32)
            for r in range(group):
                r0 = r * bq + b0
                _softmax_strip(s_ref[r0:r0 + strip, :] + bias, r0, m_in_ref, m_out_ref, p_ref)
        _accumulate(acc_ref, m_in_ref, m_out_ref, p_ref, v_ref[pl.ds(ks, bk), :])

    scores(0, even)

    def pair(jp, _):
        scores(2 * jp + 1, odd)
        chunk(2 * jp, even)
        scores(2 * jp + 2, even)
        chunk(2 * jp + 1, odd)
        return 0

    lax.fori_loop(0, n_kv // 2, pair, 0)

    @pl.when(n_kv % 2 == 1)
    def _():
        chunk(n_kv - 1, even)

    for r in range(group):
        sl = slice(r * bq, (r + 1) * bq)
        o_ref[:, r * LANES:(r + 1) * LANES] = (acc_ref[sl, :LANES] / acc_ref[sl, LANES:]).astype(o_ref.dtype)


def dsa_attention(q, k, v, bias, batch, seq):
    group = A_HEADS // A_KV_HEADS
    bq = bk = min(seq, 256)
    nq = seq // bq
    gw = group * LANES
    return pl.pallas_call(
        functools.partial(_dsa_attn_kernel, bq=bq, bk=bk, group=group, strip=min(bq, 64)),
        name="dsa_attn",
        grid=(batch, A_KV_HEADS, nq),
        in_specs=[pl.BlockSpec((bq, gw), lambda b, g, i: (b * nq + i, g)),
                  pl.BlockSpec((seq, LANES), lambda b, g, i: (b, g)),
                  pl.BlockSpec((seq, LANES), lambda b, g, i: (b, g)),
                  pl.BlockSpec((bq, seq), lambda b, g, i: (b * nq + i, 0))],
        out_specs=pl.BlockSpec((bq, gw), lambda b, g, i: (b * nq + i, g)),
        out_shape=jax.ShapeDtypeStruct((batch * seq, A_HEADS * LANES), BF16),
        scratch_shapes=[pltpu.VMEM((group * bq, bk), F32), pltpu.VMEM((group * bq, bk), F32),
                        pltpu.VMEM((group * bq, bk), BF16),
                        pltpu.VMEM((group * bq, LANES), F32), pltpu.VMEM((group * bq, LANES), F32),
                        pltpu.VMEM((group * bq, 2 * LANES), F32)],
        compiler_params=_params("parallel", "parallel", "parallel"),
    )(q, k, v, bias)


def dsa_layer(x, g_norm, w_in, w_o, batch, seq):
    hn = rmsnorm(x, g_norm, BF16)
    n_main = (A_HEADS + 2 * A_KV_HEADS) * A_HEAD_DIM + IDX_HEADS * IDX_DIM
    n_tail = w_in.shape[1] - n_main
    main = matmul(hn, w_in[:, :n_main].astype(BF16), F32)
    w_tail = jnp.pad(w_in[:, n_main:], ((0, 0), (0, LANES - n_tail))).astype(BF16)
    tail = matmul(hn, w_tail, F32)
    q, k, v, qi, ki, wi = dsa_prep(main, tail, batch, seq)
    bias = dsa_index(qi, ki, wi, batch, seq)
    o = dsa_attention(q, k, v, bias, batch, seq)
    return matmul(o, w_o.astype(BF16), F32, residual=x)


def _rwkv_mix_kernel(x_ref, g_ref, mix_ref, *rest, bs):
    outs, hbuf = rest[:6], rest[6]
    j = pl.program_id(1)

    @pl.when(j == 0)
    def _():
        hbuf[0:8, :] = jnp.zeros((8, D_MODEL), F32)

    hn = _rms(x_ref[...], g_ref[...])
    hbuf[8:8 + bs, :] = hn
    xx = hbuf[7:7 + bs, :] - hn
    hbuf[7:8, :] = hn[bs - 1:bs, :]
    for i in range(6):
        outs[i][...] = (hn + xx * mix_ref[i:i + 1, :]).astype(BF16)


def rwkv_mix(x, g_norm, mix, batch, seq):
    bs = min(seq, 256)
    nb = seq // bs
    m, d = x.shape
    mix8 = jnp.pad(mix, ((0, 2), (0, 0)))
    return pl.pallas_call(
        functools.partial(_rwkv_mix_kernel, bs=bs),
        name="rwkv_mix",
        grid=(batch, nb),
        in_specs=[pl.BlockSpec((bs, d), lambda b, j: (b * nb + j, 0)),
                  pl.BlockSpec((1, d), lambda b, j: (0, 0)),
                  pl.BlockSpec((8, d), lambda b, j: (0, 0))],
        out_specs=[pl.BlockSpec((bs, d), lambda b, j: (b * nb + j, 0))] * 6,
        out_shape=[jax.ShapeDtypeStruct((m, d), BF16)] * 6,
        scratch_shapes=[pltpu.VMEM((bs + 8, d), F32)],
        compiler_params=_params("parallel", "arbitrary"),
    )(x, g_norm.reshape(1, d), mix8)


def _head_sum_matrix():
    return ((_iota((LANES, LANES), 0) // C_HEAD_DIM) == (_iota((LANES, LANES), 1) // C_HEAD_DIM)).astype(F32)


def _head_sum(x, p):
    return jnp.dot(x, p, precision=HIGHEST, preferred_element_type=F32)


def _rwkv_pre_kernel(k_ref, wl_ref, al_ref, w0_ref, a0_ref, kk_ref_, ka_ref, lw_o, kk_o, km_o, b_o):
    p = _head_sum_matrix()
    for t in range(D_MODEL // LANES):
        sl = slice(t * LANES, (t + 1) * LANES)
        k = k_ref[:, sl]
        z = w0_ref[:, sl] + wl_ref[:, sl]
        w_log = _log_sigmoid(z) - 0.5
        lw_o[:, sl] = -jnp.exp(w_log)
        a = jax.nn.sigmoid(a0_ref[:, sl] + al_ref[:, sl])
        kr = k * kk_ref_[:, sl]
        kk = kr * lax.rsqrt(jnp.maximum(_head_sum(kr * kr, p), 1e-24))
        kk_o[:, sl] = kk
        km_o[:, sl] = k * (1.0 + (a - 1.0) * ka_ref[:, sl])
        b_o[:, sl] = kk * a


def rwkv_pre(k, wl, al, w0, a0, k_k, k_a):
    m, d = k.shape
    bm = _largest_divisor(m, (256, 128, 64))
    big = pl.BlockSpec((bm, d), lambda i: (i, 0))
    row = pl.BlockSpec((1, d), lambda i: (0, 0))
    r2 = lambda a: a.reshape(1, d).astype(F32)
    return pl.pallas_call(
        _rwkv_pre_kernel,
        name="rwkv_pre",
        grid=(m // bm,),
        in_specs=[big, big, big, row, row, row, row],
        out_specs=[big] * 4,
        out_shape=[jax.ShapeDtypeStruct((m, d), F32)] * 4,
        compiler_params=_params("parallel"),
    )(k, wl, al, r2(w0), r2(a0), r2(k_k), r2(k_a))


def _rwkv_rec_kernel(r_ref, lw_ref, k_ref, v_ref, kk_ref, b_ref, y_ref, st_ref, *, tc, npair, mm_dtype):
    c = pl.program_id(2)

    @pl.when(c == 0)
    def _():
        st_ref[...] = jnp.zeros(st_ref.shape, F32)

    def mm(a, b, dims=None):
        a = a.astype(mm_dtype)
        b = b.astype(mm_dtype)
        prec = HIGHEST if mm_dtype == F32 else None
        if dims is None:
            return jnp.dot(a, b, precision=prec, preferred_element_type=F32)
        return lax.dot_general(a, b, dims, precision=prec, preferred_element_type=F32)

    row = _iota((tc, tc), 0)
    col = _iota((tc, tc), 1)
    tri_incl = (row >= col).astype(F32)
    strict = row > col
    incl = row >= col
    eye = (row == col).astype(F32)
    lane = _iota((tc, LANES), 1)
    head_masks = [lane < C_HEAD_DIM, lane >= C_HEAD_DIM]
    blockdiag = (_iota((LANES, LANES), 0) // C_HEAD_DIM) == (_iota((LANES, LANES), 1) // C_HEAD_DIM)
    n_sq = max((tc - 1).bit_length() - 1, 0)

    pairs = range(npair)
    chains = [(p, h) for p in pairs for h in range(2)]
    tile = lambda a, p: a[:, p * LANES:(p + 1) * LANES]
    cast = lambda a: a.astype(mm_dtype)

    r, lw, k, v, kk, b = (ref[...] for ref in (r_ref, lw_ref, k_ref, v_ref, kk_ref, b_ref))
    cum = jnp.dot(tri_incl, lw, precision=HIGHEST, preferred_element_type=F32)
    g_in = jnp.exp(cum)
    g_inv = jnp.exp(-cum)
    a_t = -kk * jnp.exp(cum - lw)
    b_t = b * g_inv
    k_t = k * g_inv
    r_t = r * g_in
    g_end = g_in[tc - 1:tc, :]
    b_c, k_c, v_c, r_c = cast(b_t), cast(k_t), cast(v), cast(r_t)
    bg_x, kg_x = cast((b_t * g_end).T), cast((k_t * g_end).T)
    g_col = g_in.T[:, tc - 1:tc]
    rows = lambda a, p: a[p * LANES:(p + 1) * LANES, :]

    a_m = {(p, h): cast(jnp.where(head_masks[h], tile(a_t, p), 0.0)) for p, h in chains}
    left = {(p, h): jnp.concatenate([a_m[p, h], cast(jnp.where(head_masks[h], tile(r_t, p), 0.0))], axis=0)
            for p, h in chains}
    gb = {c_: mm(left[c_], tile(b_c, c_[0]), _NT) for c_ in chains}
    gk = {c_: mm(left[c_], tile(k_c, c_[0]), _NT) for c_ in chains}
    a_ab = {c_: jnp.where(strict, gb[c_][:tc], 0.0) for c_ in chains}
    a_ak = {c_: cast(jnp.where(strict, gk[c_][:tc], 0.0)) for c_ in chains}
    a_rb = {c_: cast(jnp.where(incl, gb[c_][tc:], 0.0)) for c_ in chains}
    a_rk = {c_: cast(jnp.where(incl, gk[c_][tc:], 0.0)) for c_ in chains}
    akv = {c_: cast(mm(a_ak[c_], tile(v_c, c_[0]))) for c_ in chains}
    y0 = {c_: mm(a_rk[c_], tile(v_c, c_[0])) for c_ in chains}
    x = {c_: eye + a_ab[c_] for c_ in chains}
    pw = {c_: cast(a_ab[c_]) for c_ in chains}
    for _ in range(n_sq):
        pw = {c_: cast(mm(pw[c_], pw[c_])) for c_ in chains}
        x = {c_: x[c_] + mm(pw[c_], cast(x[c_])) for c_ in chains}
    x = {c_: cast(x[c_]) for c_ in chains}
    w_h = {c_: mm(x[c_], a_m[c_]) for c_ in chains}
    u_h = {c_: mm(x[c_], akv[c_]) for c_ in chains}

    s0 = {p: st_ref[p] for p in pairs}
    ws = {p: mm(jnp.concatenate([cast(w_h[p, 0] + w_h[p, 1]), tile(r_c, p)], axis=0), cast(s0[p]))
          for p in pairs}
    u = {p: ws[p][:tc] + jnp.where(head_masks[0], u_h[p, 0], u_h[p, 1]) for p in pairs}
    u_c = {p: cast(u[p]) for p in pairs}
    yb = {c_: mm(a_rb[c_], u_c[c_[0]]) for c_ in chains}
    upd = {p: mm(rows(bg_x, p), u_c[p]) + mm(rows(kg_x, p), tile(v_c, p)) for p in pairs}
    for p in pairs:
        y = ws[p][tc:] + jnp.where(head_masks[0], y0[p, 0] + yb[p, 0], y0[p, 1] + yb[p, 1])
        y_ref[:, p * LANES:(p + 1) * LANES] = y
        st_ref[p] = s0[p] * rows(g_col, p) + jnp.where(blockdiag, upd[p], 0.0)


def rwkv_rec(r, lw, k, v, kk, b, batch, seq, mm_dtype=BF16):
    m, d = r.shape
    tc = min(seq, 64)
    nc = seq // tc
    npair = 16
    gw = npair * LANES
    blk = pl.BlockSpec((tc, gw), lambda bi, g, c: (bi * nc + c, g))
    return pl.pallas_call(
        functools.partial(_rwkv_rec_kernel, tc=tc, npair=npair, mm_dtype=mm_dtype),
        name="rwkv_rec",
        grid=(batch, d // gw, nc),
        in_specs=[blk] * 6,
        out_specs=blk,
        out_shape=jax.ShapeDtypeStruct((m, d), F32),
        scratch_shapes=[pltpu.VMEM((npair, LANES, LANES), F32)],
        compiler_params=_params("parallel", "parallel", "arbitrary"),
    )(r, lw, k, v, kk, b)


def _rwkv_post_kernel(y_ref, r_ref, k_ref, v_ref, g_ref, gng_ref, gnb_ref, rk_ref, o_ref):
    p = _head_sum_matrix()
    inv_n = 1.0 / C_HEAD_DIM
    for t in range(D_MODEL // LANES):
        sl = slice(t * LANES, (t + 1) * LANES)
        y = y_ref[:, sl]
        mu = _head_sum(y, p) * inv_n
        yc = y - mu
        var = _head_sum(yc * yc, p) * inv_n
        yn = yc * lax.rsqrt(var + C_GN_EPS) * gng_ref[:, sl] + gnb_ref[:, sl]
        bonus = _head_sum(r_ref[:, sl] * k_ref[:, sl] * rk_ref[:, sl], p) * v_ref[:, sl]
        o_ref[:, sl] = ((yn + bonus) * g_ref[:, sl]).astype(o_ref.dtype)


def rwkv_post(y, r, kmod, v, g, gn_g, gn_b, r_k):
    m, d = y.shape
    bm = _largest_divisor(m, (256, 128, 64))
    big = pl.BlockSpec((bm, d), lambda i: (i, 0))
    row = pl.BlockSpec((1, d), lambda i: (0, 0))
    r2 = lambda a: a.reshape(1, d).astype(F32)
    return pl.pallas_call(
        _rwkv_post_kernel,
        name="rwkv_post",
        grid=(m // bm,),
        in_specs=[big] * 5 + [row] * 3,
        out_specs=big,
        out_shape=jax.ShapeDtypeStruct((m, d), BF16),
        compiler_params=_params("parallel"),
    )(y, r, kmod, v, g, r2(gn_g), r2(gn_b), r2(r_k))


def rwkv_layer(x, g_norm, mix, w0, w1, w2, a0, a1, a2, g1, g2, k_k, k_a, r_k, w_rkv, w_o, gn_g, gn_b,
               batch, seq):
    bf = lambda a: a.astype(BF16)
    xr, xk, xv, xw, xa, xg = rwkv_mix(x, g_norm, mix, batch, seq)
    r = matmul(xr, bf(w_rkv[0]), F32)
    k = matmul(xk, bf(w_rkv[1]), F32)
    v = matmul(xv, bf(w_rkv[2]), F32)
    wl = lora(xw, bf(w1), bf(w2), "tanh")
    al = lora(xa, bf(a1), bf(a2), "none")
    g = lora(xg, bf(g1), bf(g2), "sigmoid")
    lw, kk, kmod, b = rwkv_pre(k, wl, al, w0, a0, k_k, k_a)
    y = rwkv_rec(r, lw, kmod, v, kk, b, batch, seq)
    o = rwkv_post(y, r, kmod, v, g, gn_g, gn_b, r_k)
    return matmul(o, bf(w_o), F32, residual=x)


def kernel(x, norm_mix, norm_ffn, norm_final, ffn_gate, ffn_up, ffn_down, a_w_in, a_w_o, b_w_in, b_f_bias,
           b_w_o, c_mix, c_w0, c_w1, c_w2, c_a0, c_a1, c_a2, c_g1, c_g2, c_k_k, c_k_a, c_r_k, c_w_rkv,
           c_w_o, c_gn_g, c_gn_b):
    batch, seq, d = x.shape
    depth = norm_mix.shape[0]
    h = x.reshape(batch * seq, d)
    for i in range(depth):
        kind, j = i % 3, i // 3
        if kind == 0:
            h = dsa_layer(h, norm_mix[i], a_w_in[j], a_w_o[j], batch, seq)
        elif kind == 1:
            h = fox_layer(h, norm_mix[i], b_w_in[j], b_f_bias[j], b_w_o[j], batch, seq)
        else:
            h = rwkv_layer(h, norm_mix[i], c_mix[j], c_w0[j], c_w1[j], c_w2[j], c_a0[j], c_a1[j], c_a2[j],
                           c_g1[j], c_g2[j], c_k_k[j], c_k_a[j], c_r_k[j], c_w_rkv[j], c_w_o[j],
                           c_gn_g[j], c_gn_b[j], batch, seq)
        h = swiglu_layer(h, norm_ffn[i], ffn_gate[i], ffn_up[i], ffn_down[i])
    return rmsnorm(h, norm_final, x.dtype).reshape(batch, seq, d)
```

```python
import functools

import jax
import jax.numpy as jnp
from jax import lax
from jax.experimental import pallas as pl
from jax.experimental.pallas import tpu as pltpu

F32 = jnp.float32
BF16 = jnp.bfloat16
I32 = jnp.int32
HIGHEST = lax.Precision.HIGHEST

D_MODEL = 2048
CHUNK = 64
RMS_EPS = 1e-6
ROPE_THETA = 10000.0
A_HEADS, A_KV_HEADS, A_HEAD_DIM = 16, 4, 128
IDX_HEADS, IDX_DIM, TOPK_MAX = 16, 64, 256
B_HEADS, B_HEAD_DIM = 16, 128
C_HEAD_DIM = 64
C_GN_EPS = C_HEAD_DIM * 1e-5

LANES = 128
VMEM_LIMIT_BYTES = 56 * 1024 * 1024

NEG_BIG = -1e30
LOG2E = 1.4426950408889634
_NT = (((1,), (1,)), ((), ()))
_TN = (((0,), (0,)), ((), ()))


def _params(*sem):
    return pltpu.CompilerParams(dimension_semantics=sem, vmem_limit_bytes=VMEM_LIMIT_BYTES)


def _iota(shape, dim):
    return lax.broadcasted_iota(I32, shape, dim)


def _rms(x, g):
    return x * lax.rsqrt(jnp.mean(x * x, axis=-1, keepdims=True) + RMS_EPS) * g


def _rmsnorm_kernel(x_ref, g_ref, o_ref):
    o_ref[...] = _rms(x_ref[...], g_ref[...]).astype(o_ref.dtype)


def rmsnorm(x, g, out_dtype):
    m, d = x.shape
    bm = min(m, 512)
    return pl.pallas_call(
        _rmsnorm_kernel,
        name="rmsnorm",
        grid=(m // bm,),
        in_specs=[pl.BlockSpec((bm, d), lambda i: (i, 0)), pl.BlockSpec((1, d), lambda i: (0, 0))],
        out_specs=pl.BlockSpec((bm, d), lambda i: (i, 0)),
        out_shape=jax.ShapeDtypeStruct((m, d), out_dtype),
        compiler_params=_params("parallel"),
    )(x, g.reshape(1, d))


def _largest_divisor(n, candidates):
    for c in candidates:
        if n % c == 0:
            return c
    return n


def _mm_kernel(x_ref, w_ref, o_ref):
    o_ref[...] = jnp.dot(x_ref[...], w_ref[...], preferred_element_type=F32).astype(o_ref.dtype)


def _mm_res_kernel(x_ref, w_ref, r_ref, o_ref):
    acc = jnp.dot(x_ref[...], w_ref[...], preferred_element_type=F32)
    o_ref[...] = (r_ref[...] + acc).astype(o_ref.dtype)


def matmul(x, w, out_dtype, residual=None):
    m, k = x.shape
    n = w.shape[1]
    bm = _largest_divisor(m, (1024, 512, 256, 128))
    bn = _largest_divisor(n, (1024, 512, 256, 128)) if k <= 2048 else _largest_divisor(n, (512, 256, 128))
    in_specs = [pl.BlockSpec((bm, k), lambda i, j: (i, 0)), pl.BlockSpec((k, bn), lambda i, j: (0, j))]
    args = [x, w]
    body = _mm_kernel
    if residual is not None:
        in_specs.append(pl.BlockSpec((bm, bn), lambda i, j: (i, j)))
        args.append(residual)
        body = _mm_res_kernel
    return pl.pallas_call(
        body,
        name=f"mm_{k}x{n}",
        grid=(m // bm, n // bn),
        in_specs=in_specs,
        out_specs=pl.BlockSpec((bm, bn), lambda i, j: (i, j)),
        out_shape=jax.ShapeDtypeStruct((m, n), out_dtype),
        compiler_params=_params("parallel", "parallel"),
    )(*args)


def _gateup_kernel(x_ref, wg_ref, wu_ref, o_ref):
    x = x_ref[...]
    g = jnp.dot(x, wg_ref[...], preferred_element_type=F32)
    u = jnp.dot(x, wu_ref[...], preferred_element_type=F32)
    o_ref[...] = (g * jax.nn.sigmoid(g) * u).astype(o_ref.dtype)


def gate_up(x, wg, wu):
    m, k = x.shape
    n = wg.shape[1]
    bm = _largest_divisor(m, (1024, 512, 256, 128))
    bn = _largest_divisor(n, (512, 256, 128))
    return pl.pallas_call(
        _gateup_kernel,
        name="gate_up",
        grid=(m // bm, n // bn),
        in_specs=[pl.BlockSpec((bm, k), lambda i, j: (i, 0)),
                  pl.BlockSpec((k, bn), lambda i, j: (0, j)),
                  pl.BlockSpec((k, bn), lambda i, j: (0, j))],
        out_specs=pl.BlockSpec((bm, bn), lambda i, j: (i, j)),
        out_shape=jax.ShapeDtypeStruct((m, n), BF16),
        compiler_params=_params("parallel", "parallel"),
    )(x, wg, wu)


def _lora_kernel(x_ref, w1_ref, w2_ref, o_ref, *, act):
    t = jnp.dot(x_ref[...], w1_ref[...], preferred_element_type=F32)
    if act == "tanh":
        t = jnp.tanh(t)
    elif act == "sigmoid":
        t = jax.nn.sigmoid(t)
    o_ref[...] = jnp.dot(t.astype(BF16), w2_ref[...], preferred_element_type=F32)


def lora(x, w1, w2, act):
    m, k = x.shape
    r = w1.shape[1]
    rp = -(-r // LANES) * LANES
    if act == "sigmoid":
        assert rp == r, "sigmoid(0) != 0: the rank must not be padded"
    w1 = jnp.pad(w1, ((0, 0), (0, rp - r)))
    w2 = jnp.pad(w2, ((0, rp - r), (0, 0)))
    n = w2.shape[1]
    bm = _largest_divisor(m, (1024, 512, 256, 128))
    return pl.pallas_call(
        functools.partial(_lora_kernel, act=act),
        name="lora_" + act,
        grid=(m // bm,),
        in_specs=[pl.BlockSpec((bm, k), lambda i: (i, 0)),
                  pl.BlockSpec((k, rp), lambda i: (0, 0)),
                  pl.BlockSpec((rp, n), lambda i: (0, 0))],
        out_specs=pl.BlockSpec((bm, n), lambda i: (i, 0)),
        out_shape=jax.ShapeDtypeStruct((m, n), F32),
        compiler_params=_params("parallel"),
    )(x, w1, w2)


def swiglu_layer(x, g_norm, w_gate, w_up, w_down):
    hn = rmsnorm(x, g_norm, BF16)
    h = gate_up(hn, w_gate.astype(BF16), w_up.astype(BF16))
    return matmul(h, w_down.astype(BF16), F32, residual=x)


def _log_sigmoid(x):
    return jnp.minimum(x, 0.0) - jnp.log1p(jnp.exp(-jnp.abs(x)))


def _fox_prep_kernel(fl_ref, bias_ref, c_ref, ct_ref, *, seq, ch):
    tri = (_iota((ch, ch), 0) >= _iota((ch, ch), 1)).astype(F32)
    carry = jnp.zeros((1, LANES), F32)
    for c in range(seq // ch):
        lf = _log_sigmoid(fl_ref[c * ch:(c + 1) * ch, :] + bias_ref[...])
        cs = jnp.dot(tri, lf, precision=HIGHEST, preferred_element_type=F32) + carry
        c2 = cs * LOG2E
        c_ref[c * ch:(c + 1) * ch, :] = c2
        ct_ref[:, c * ch:(c + 1) * ch] = c2.T
        carry = cs[ch - 1:ch, :]


def fox_prep(fl, bias_pad, batch, seq):
    ch = min(seq, 256)
    return pl.pallas_call(
        functools.partial(_fox_prep_kernel, seq=seq, ch=ch),
        name="fox_prep",
        grid=(batch,),
        in_specs=[pl.BlockSpec((seq, LANES), lambda b: (b, 0)), pl.BlockSpec((1, LANES), lambda b: (0, 0))],
        out_specs=[pl.BlockSpec((seq, LANES), lambda b: (b, 0)),
                   pl.BlockSpec((None, LANES, seq), lambda b: (b, 0, 0))],
        out_shape=[jax.ShapeDtypeStruct((batch * seq, LANES), F32),
                   jax.ShapeDtypeStruct((batch, LANES, seq), F32)],
        compiler_params=_params("parallel"),
    )(fl, bias_pad)


def _softmax_strip(z, r0, m_in_ref, m_out_ref, p_ref, row_term=None):
    rows, bk = z.shape
    tiles = [z[:, t * LANES:(t + 1) * LANES] for t in range(bk // LANES)]
    zmax = functools.reduce(jnp.maximum, tiles)
    zmax = jnp.broadcast_to(jnp.max(zmax, axis=1, keepdims=True), (rows, LANES))
    if row_term is not None:
        zmax = zmax + row_term
    m_new = jnp.maximum(m_in_ref[r0:r0 + rows, :], zmax)
    m_out_ref[r0:r0 + rows, :] = m_new
    shift = m_new if row_term is None else m_new - row_term
    for t, zt in enumerate(tiles):
        p_ref[r0:r0 + rows, t * LANES:(t + 1) * LANES] = jnp.exp2(zt - shift).astype(BF16)


def _accumulate(acc_ref, m_in_ref, m_out_ref, p_ref, v, row_lo=0):
    alpha = jnp.exp2(m_in_ref[row_lo:, :] - m_out_ref[row_lo:, :])
    v1 = jnp.concatenate([v, jnp.ones(v.shape, v.dtype)], axis=1)
    pv = jnp.dot(p_ref[row_lo:, :], v1, preferred_element_type=F32)
    for t in range(2):
        sl = slice(t * LANES, (t + 1) * LANES)
        acc_ref[row_lo:, sl] = alpha * acc_ref[row_lo:, sl] + pv[:, sl]


def _fox_attn_kernel(q_ref, k_ref, v_ref, c_ref, ck_ref, o_ref, s0_ref, s1_ref, p_ref, m0_ref, m1_ref, acc_ref,
                     *, bq, bk, strip):
    h = pl.program_id(1)
    qi = pl.program_id(2)
    q = q_ref[...]
    cq = jnp.sum(jnp.where(_iota((bq, LANES), 1) == h, c_ref[...], 0.0), axis=1, keepdims=True)
    cq = jnp.broadcast_to(cq, (bq, LANES))
    m0_ref[...] = jnp.full((bq, LANES), NEG_BIG, F32)
    acc_ref[...] = jnp.zeros((bq, 2 * LANES), F32)
    even, odd = (s0_ref, m0_ref, m1_ref), (s1_ref, m1_ref, m0_ref)

    def scores(t, bufs, row_lo=0):
        ks = pl.multiple_of(t * bk, bk)
        bufs[0][row_lo:, :] = lax.dot_general(q[row_lo:], k_ref[pl.ds(ks, bk), :], _NT,
                                              preferred_element_type=F32)

    def chunk(t, bufs, row_lo=0, diag_off=None):
        s_ref, m_in_ref, m_out_ref = bufs
        ks = pl.multiple_of(t * bk, bk)
        ck = ck_ref[:, pl.ds(ks, bk)]
        for r0 in range(row_lo, bq, strip):
            z = s_ref[r0:r0 + strip, :] - ck
            if diag_off is not None and diag_off + bk - 1 > r0:
                visible = diag_off + _iota((strip, bk), 1) <= r0 + _iota((strip, bk), 0)
                z = jnp.where(visible, z, NEG_BIG)
            _softmax_strip(z, r0, m_in_ref, m_out_ref, p_ref, row_term=cq[r0:r0 + strip])
        _accumulate(acc_ref, m_in_ref, m_out_ref, p_ref, v_ref[pl.ds(ks, bk), :], row_lo)

    assert bq == 2 * bk
    scores(0, even)

    def pair(jp, _):
        scores(2 * jp + 1, odd)
        chunk(2 * jp, even)
        scores(2 * jp + 2, even)
        chunk(2 * jp + 1, odd)
        return 0

    lax.fori_loop(0, qi, pair, 0)
    scores(2 * qi + 1, odd, row_lo=bk)
    chunk(2 * qi, even, diag_off=0)
    chunk(2 * qi + 1, odd, row_lo=bk, diag_off=bk)
    o_ref[...] = (acc_ref[:, :LANES] / acc_ref[:, LANES:]).astype(o_ref.dtype)


def fox_attention(qkv, c, ck4, batch, seq):
    nh = B_HEADS
    bq = min(seq, 512)
    bk = min(seq, 256)
    nq = seq // bq
    return pl.pallas_call(
        functools.partial(_fox_attn_kernel, bq=bq, bk=bk, strip=min(bq, 128)),
        name="fox_attn",
        grid=(batch, nh, nq),
        in_specs=[pl.BlockSpec((bq, LANES), lambda b, h, i: (b * nq + i, h)),
                  pl.BlockSpec((seq, LANES), lambda b, h, i: (b, nh + h)),
                  pl.BlockSpec((seq, LANES), lambda b, h, i: (b, 2 * nh + h)),
                  pl.BlockSpec((bq, LANES), lambda b, h, i: (b * nq + i, 0)),
                  pl.BlockSpec((None, None, 1, seq), lambda b, h, i: (b, h, 0, 0))],
        out_specs=pl.BlockSpec((bq, LANES), lambda b, h, i: (b * nq + i, h)),
        out_shape=jax.ShapeDtypeStruct((batch * seq, nh * LANES), BF16),
        scratch_shapes=[pltpu.VMEM((bq, bk), F32), pltpu.VMEM((bq, bk), F32), pltpu.VMEM((bq, bk), BF16),
                        pltpu.VMEM((bq, LANES), F32), pltpu.VMEM((bq, LANES), F32),
                        pltpu.VMEM((bq, 2 * LANES), F32)],
        compiler_params=_params("parallel", "parallel", "parallel"),
    )(qkv, qkv, qkv, c, ck4)


def fox_layer(x, g_norm, w_in, f_bias, w_o, batch, seq):
    nh = B_HEADS
    hn = rmsnorm(x, g_norm, BF16)
    n_q, n_qkv = nh * B_HEAD_DIM, 3 * nh * B_HEAD_DIM
    col_scale = jnp.where(jnp.arange(n_qkv) < n_q, B_HEAD_DIM ** -0.5 * LOG2E, 1.0).astype(F32)
    qkv = matmul(hn, (w_in[:, :n_qkv] * col_scale).astype(BF16), BF16)
    w_f = jnp.pad(w_in[:, n_qkv:], ((0, 0), (0, LANES - nh))).astype(BF16)
    fl = matmul(hn, w_f, F32)
    bias_pad = jnp.pad(f_bias.astype(F32), (0, LANES - nh)).reshape(1, LANES)
    c, ct = fox_prep(fl, bias_pad, batch, seq)
    ck4 = ct[:, :nh, :].reshape(batch, nh, 1, seq)
    o = fox_attention(qkv, c, ck4, batch, seq)
    return matmul(o, w_o.astype(BF16), F32, residual=x)


def _rope_tables(seq, head_dim):
    half = head_dim // 2
    inv = ROPE_THETA ** (-jnp.arange(half, dtype=F32) / half)
    ang = jnp.arange(seq, dtype=F32)[:, None] * inv[None, :]
    cos = jnp.cos(ang)
    sin = jnp.sin(ang)
    reps = LANES // head_dim
    cos_t = jnp.tile(jnp.concatenate([cos, cos], axis=1), (1, reps))
    sin_t = jnp.tile(jnp.concatenate([-sin, sin], axis=1), (1, reps))
    return cos_t, sin_t


def _rope128(x, cos, sin):
    return x * cos + pltpu.roll(x, 64, 1) * sin


def _rope64(x, cos, sin, lane):
    partner = jnp.where((lane % 64) < 32, pltpu.roll(x, 96, 1), pltpu.roll(x, 32, 1))
    return x * cos + partner * sin


def _dsa_prep_kernel(main_ref, tail_ref, c128_ref, s128_ref, c64_ref, s64_ref,
                     q_ref, k_ref, v_ref, qi_ref, ki_ref, wi_ref, *, br, q_scale, wi_scale):
    c128, s128 = c128_ref[...], s128_ref[...]
    c64, s64 = c64_ref[...], s64_ref[...]
    lane = _iota((br, LANES), 1)
    tile = lambda t: main_ref[:, t * LANES:(t + 1) * LANES]
    nq, nkv, npair = A_HEADS, A_KV_HEADS, IDX_HEADS // 2
    cq128, sq128 = c128 * q_scale, s128 * q_scale
    for t in range(nq):
        q_ref[:, t * LANES:(t + 1) * LANES] = _rope128(tile(t), cq128, sq128).astype(BF16)
    for t in range(nkv):
        k_ref[:, t * LANES:(t + 1) * LANES] = _rope128(tile(nq + t), c128, s128).astype(BF16)
        v_ref[:, t * LANES:(t + 1) * LANES] = tile(nq + nkv + t).astype(BF16)
    for t in range(npair):
        r = _rope64(tile(nq + 2 * nkv + t), c64, s64, lane)
        qi_ref[:, (2 * t) * LANES:(2 * t + 1) * LANES] = jnp.where(lane < 64, r, 0.0).astype(BF16)
        qi_ref[:, (2 * t + 1) * LANES:(2 * t + 2) * LANES] = jnp.where(lane >= 64, r, 0.0).astype(BF16)
    tl = tail_ref[...]
    kr = _rope64(tl, c64, s64, lane)
    ki_ref[...] = jnp.where(lane < 64, kr, pltpu.roll(kr, 64, 1)).astype(BF16)
    wi_ref[...] = tl * wi_scale


def dsa_prep(main, tail, batch, seq):
    br = min(seq, 256)
    nblk = seq // br
    c128, s128 = _rope_tables(seq, A_HEAD_DIM)
    c64, s64 = _rope_tables(seq, IDX_DIM)
    m = batch * seq
    tab = pl.BlockSpec((br, LANES), lambda i: (i % nblk, 0))
    row = lambda w: pl.BlockSpec((br, w), lambda i: (i, 0))
    return pl.pallas_call(
        functools.partial(_dsa_prep_kernel, br=br, q_scale=A_HEAD_DIM ** -0.5 * LOG2E,
                          wi_scale=IDX_HEADS ** -0.5 * IDX_DIM ** -0.5),
        name="dsa_prep",
        grid=(m // br,),
        in_specs=[row(main.shape[1]), row(LANES), tab, tab, tab, tab],
        out_specs=[row(A_HEADS * LANES), row(A_KV_HEADS * LANES), row(A_KV_HEADS * LANES),
                   row(IDX_HEADS * LANES), row(LANES), row(LANES)],
        out_shape=[jax.ShapeDtypeStruct((m, A_HEADS * LANES), BF16),
                   jax.ShapeDtypeStruct((m, A_KV_HEADS * LANES), BF16),
                   jax.ShapeDtypeStruct((m, A_KV_HEADS * LANES), BF16),
                   jax.ShapeDtypeStruct((m, IDX_HEADS * LANES), BF16),
                   jax.ShapeDtypeStruct((m, LANES), BF16),
                   jax.ShapeDtypeStruct((m, LANES), F32)],
        compiler_params=_params("parallel"),
    )(main, tail, c128, s128, c64, s64)


_MIN32 = -2 ** 31
_BITS_PER_CHECK = 4
_KEY_NEG_INF = (0xFF800000 - 2 ** 32) ^ 0x7FFFFFFF


def _order_key(s):
    b = pltpu.bitcast(s, I32)
    return jnp.where(b >= 0, b, b ^ 0x7FFFFFFF)


def _dsa_index_kernel(qi_ref, ki_ref, wi_ref, bias_ref, key_ref, *, bq, seq, kc, topk):
    qb = pl.program_id(1)
    n_kc = ((qb + 1) * bq + kc - 1) // kc
    g8 = kc // 8
    rb = min(bq, 256)
    lane = _iota((bq, LANES), 1)
    wi = wi_ref[...]
    wcol = [jnp.sum(jnp.where(lane == 64 + h, wi, 0.0), axis=1, keepdims=True) for h in range(IDX_HEADS)]
    k_off = _iota((rb, kc), 1)

    def kc_body(j, _):
        ks = pl.multiple_of(j * kc, kc)
        kblk = ki_ref[pl.ds(ks, kc), :]
        for r0 in range(0, bq, rb):
            acc = jnp.zeros((rb, kc), F32)
            for h in range(IDX_HEADS):
                s = lax.dot_general(qi_ref[r0:r0 + rb, h * LANES:(h + 1) * LANES], kblk, _NT,
                                    preferred_element_type=F32)
                acc = acc + wcol[h][r0:r0 + rb] * jnp.maximum(s, 0.0)
            q_chunk = (qb * bq + r0 + _iota((rb, kc), 0)) // CHUNK
            acc = jnp.where((ks + k_off) // CHUNK <= q_chunk, acc, -jnp.inf)
            key_ref[pl.ds(ks, kc), r0:r0 + rb] = _order_key(acc.T)
        return 0

    lax.fori_loop(0, n_kc, kc_body, 0)

    def count(indicator):
        def body(j, acc):
            x = key_ref[pl.ds(pl.multiple_of(j * kc, kc), kc), :].reshape(g8, 8, bq)
            return acc + jnp.sum(indicator(x, j), axis=0)
        acc = lax.fori_loop(0, n_kc, body, jnp.zeros((8, bq), F32))
        return jnp.sum(acc, axis=0, keepdims=True)

    n_adm = ((qb * bq + _iota((1, bq), 1)) // CHUNK + 1) * CHUNK
    take_all = n_adm <= topk
    settled = lambda cnt_t: jnp.logical_or(take_all, cnt_t == topk)
    pending = lambda cnt_t: jnp.max(jnp.where(settled(cnt_t), 0, 1))

    def bit_cond(c):
        return jnp.logical_and(c[0] < 32, c[3] > 0)

    def bit_body(c):
        i, tu, cnt_t, _ = c
        for b in range(_BITS_PER_CHECK):
            cand_u = tu | lax.shift_left(jnp.int32(1), 31 - (i + b))
            cand_s = jnp.broadcast_to(cand_u ^ _MIN32, (8, bq))
            cnt = count(lambda x, j: jnp.where(x >= cand_s, 1.0, 0.0))
            ok = cnt >= topk
            cnt_t = jnp.where(ok, cnt, cnt_t)
            tu = jnp.where(ok, cand_u, tu)
        return i + _BITS_PER_CHECK, tu, cnt_t, pending(cnt_t)

    cnt0 = jnp.zeros((1, bq), F32) + (n_kc * kc).astype(F32)
    _, tu, cnt_ge, n_pending = lax.while_loop(
        bit_cond, bit_body, (jnp.int32(0), jnp.zeros((1, bq), I32), cnt0, pending(cnt0)))
    ts = tu ^ _MIN32
    thr = jnp.where(take_all, _KEY_NEG_INF + 1, ts)

    def no_ties():
        return thr - 1, jnp.zeros((1, bq), I32), jnp.full((1, bq), -1, I32)

    def with_ties():
        tied = jnp.logical_not(settled(cnt_ge))
        ts8 = jnp.broadcast_to(ts, (8, bq))
        cnt_gt = count(lambda x, j: jnp.where(x > ts8, 1.0, 0.0))
        need = topk - cnt_gt
        k_idx = _iota((g8, 8, bq), 0) * 8 + _iota((g8, 8, bq), 1)

        def j_body(i, j0):
            cand = j0 | lax.shift_left(jnp.int32(1), (seq.bit_length() - 2) - i)
            cand8 = jnp.broadcast_to(cand, (8, bq))
            cnt = count(lambda x, j: jnp.where(x == ts8, jnp.where(j * kc + k_idx < cand8, 1.0, 0.0), 0.0))
            return jnp.where(cnt < need, cand, j0)
        j0 = lax.fori_loop(0, seq.bit_length() - 1, j_body, jnp.zeros((1, bq), I32))
        return jnp.where(tied, ts, thr - 1), jnp.where(tied, ts, 0), jnp.where(tied, j0, -1)

    gt_thr, eq_val, jlim = lax.cond(n_pending > 0, with_ties, no_ties)

    bias_ref[...] = jnp.full((bq, seq), NEG_BIG, BF16)
    k_row = _iota((kc, rb), 0)

    def out_body(j, _):
        ks = pl.multiple_of(j * kc, kc)
        for r0 in range(0, bq, rb):
            qs = slice(r0, r0 + rb)
            x = key_ref[pl.ds(ks, kc), qs]
            tie_ok = jnp.where(ks + k_row <= jlim[:, qs], 0.0, NEG_BIG)
            b = jnp.where(x > gt_thr[:, qs], 0.0, jnp.where(x == eq_val[:, qs], tie_ok, NEG_BIG))
            bias_ref[qs, pl.ds(ks, kc)] = b.T.astype(BF16)
        return 0

    lax.fori_loop(0, n_kc, out_body, 0)


def dsa_index(qi, ki, wi, batch, seq):
    bq = min(seq, 512)
    nq = seq // bq
    kc = min(seq, 256)
    topk = min(TOPK_MAX, seq // 4)
    return pl.pallas_call(
        functools.partial(_dsa_index_kernel, bq=bq, seq=seq, kc=kc, topk=topk),
        name="dsa_index",
        grid=(batch, nq),
        in_specs=[pl.BlockSpec((bq, IDX_HEADS * LANES), lambda b, i: (b * nq + i, 0)),
                  pl.BlockSpec((seq, LANES), lambda b, i: (b, 0)),
                  pl.BlockSpec((bq, LANES), lambda b, i: (b * nq + i, 0))],
        out_specs=pl.BlockSpec((bq, seq), lambda b, i: (b * nq + i, 0)),
        out_shape=jax.ShapeDtypeStruct((batch * seq, seq), BF16),
        scratch_shapes=[pltpu.VMEM((seq, bq), I32)],
        compiler_params=_params("parallel", "parallel"),
    )(qi, ki, wi)


def _dsa_attn_kernel(q_ref, k_ref, v_ref, bias_ref, o_ref, s0_ref, s1_ref, p_ref, m0_ref, m1_ref, acc_ref,
                     *, bq, bk, group, strip):
    qi = pl.program_id(2)
    q = jnp.concatenate([q_ref[:, r * LANES:(r + 1) * LANES] for r in range(group)], axis=0)
    rows = group * bq
    m0_ref[...] = jnp.full((rows, LANES), NEG_BIG, F32)
    acc_ref[...] = jnp.zeros((rows, 2 * LANES), F32)
    even, odd = (s0_ref, m0_ref, m1_ref), (s1_ref, m1_ref, m0_ref)

    n_kv = ((qi + 1) * bq + bk - 1) // bk

    def scores(j, bufs):
        ks = pl.multiple_of(jnp.minimum(j, n_kv - 1) * bk, bk)
        bufs[0][...] = lax.dot_general(q, k_ref[pl.ds(ks, bk), :], _NT, preferred_element_type=F32)

    def chunk(j, bufs):
        s_ref, m_in_ref, m_out_ref = bufs
        ks = pl.multiple_of(j * bk, bk)
        for b0 in range(0, bq, strip):
            bias = bias_ref[b0:b0 + strip, pl.ds(ks, bk)].astype(F32)
            for r in range(group):
                r0 = r * bq + b0
                _softmax_strip(s_ref[r0:r0 + strip, :] + bias, r0, m_in_ref, m_out_ref, p_ref)
        _accumulate(acc_ref, m_in_ref, m_out_ref, p_ref, v_ref[pl.ds(ks, bk), :])

    scores(0, even)

    def pair(jp, _):
        scores(2 * jp + 1, odd)
        chunk(2 * jp, even)
        scores(2 * jp + 2, even)
        chunk(2 * jp + 1, odd)
        return 0

    lax.fori_loop(0, n_kv // 2, pair, 0)

    @pl.when(n_kv % 2 == 1)
    def _():
        chunk(n_kv - 1, even)

    for r in range(group):
        sl = slice(r * bq, (r + 1) * bq)
        o_ref[:, r * LANES:(r + 1) * LANES] = (acc_ref[sl, :LANES] / acc_ref[sl, LANES:]).astype(o_ref.dtype)


def dsa_attention(q, k, v, bias, batch, seq):
    group = A_HEADS // A_KV_HEADS
    bq = bk = min(seq, 256)
    nq = seq // bq
    gw = group * LANES
    return pl.pallas_call(
        functools.partial(_dsa_attn_kernel, bq=bq, bk=bk, group=group, strip=min(bq, 64)),
        name="dsa_attn",
        grid=(batch, A_KV_HEADS, nq),
        in_specs=[pl.BlockSpec((bq, gw), lambda b, g, i: (b * nq + i, g)),
                  pl.BlockSpec((seq, LANES), lambda b, g, i: (b, g)),
                  pl.BlockSpec((seq, LANES), lambda b, g, i: (b, g)),
                  pl.BlockSpec((bq, seq), lambda b, g, i: (b * nq + i, 0))],
        out_specs=pl.BlockSpec((bq, gw), lambda b, g, i: (b * nq + i, g)),
        out_shape=jax.ShapeDtypeStruct((batch * seq, A_HEADS * LANES), BF16),
        scratch_shapes=[pltpu.VMEM((group * bq, bk), F32), pltpu.VMEM((group * bq, bk), F32),
                        pltpu.VMEM((group * bq, bk), BF16),
                        pltpu.VMEM((group * bq, LANES), F32), pltpu.VMEM((group * bq, LANES), F32),
                        pltpu.VMEM((group * bq, 2 * LANES), F32)],
        compiler_params=_params("parallel", "parallel", "parallel"),
    )(q, k, v, bias)


def dsa_layer(x, g_norm, w_in, w_o, batch, seq):
    hn = rmsnorm(x, g_norm, BF16)
    n_main = (A_HEADS + 2 * A_KV_HEADS) * A_HEAD_DIM + IDX_HEADS * IDX_DIM
    n_tail = w_in.shape[1] - n_main
    main = matmul(hn, w_in[:, :n_main].astype(BF16), F32)
    w_tail = jnp.pad(w_in[:, n_main:], ((0, 0), (0, LANES - n_tail))).astype(BF16)
    tail = matmul(hn, w_tail, F32)
    q, k, v, qi, ki, wi = dsa_prep(main, tail, batch, seq)
    bias = dsa_index(qi, ki, wi, batch, seq)
    o = dsa_attention(q, k, v, bias, batch, seq)
    return matmul(o, w_o.astype(BF16), F32, residual=x)


def _rwkv_mix_kernel(x_ref, g_ref, mix_ref, *rest, bs):
    outs, hbuf = rest[:6], rest[6]
    j = pl.program_id(1)

    @pl.when(j == 0)
    def _():
        hbuf[0:8, :] = jnp.zeros((8, D_MODEL), F32)

    hn = _rms(x_ref[...], g_ref[...])
    hbuf[8:8 + bs, :] = hn
    xx = hbuf[7:7 + bs, :] - hn
    hbuf[7:8, :] = hn[bs - 1:bs, :]
    for i in range(6):
        outs[i][...] = (hn + xx * mix_ref[i:i + 1, :]).astype(BF16)


def rwkv_mix(x, g_norm, mix, batch, seq):
    bs = min(seq, 256)
    nb = seq // bs
    m, d = x.shape
    mix8 = jnp.pad(mix, ((0, 2), (0, 0)))
    return pl.pallas_call(
        functools.partial(_rwkv_mix_kernel, bs=bs),
        name="rwkv_mix",
        grid=(batch, nb),
        in_specs=[pl.BlockSpec((bs, d), lambda b, j: (b * nb + j, 0)),
                  pl.BlockSpec((1, d), lambda b, j: (0, 0)),
                  pl.BlockSpec((8, d), lambda b, j: (0, 0))],
        out_specs=[pl.BlockSpec((bs, d), lambda b, j: (b * nb + j, 0))] * 6,
        out_shape=[jax.ShapeDtypeStruct((m, d), BF16)] * 6,
        scratch_shapes=[pltpu.VMEM((bs + 8, d), F32)],
        compiler_params=_params("parallel", "arbitrary"),
    )(x, g_norm.reshape(1, d), mix8)


def _head_sum_matrix():
    return ((_iota((LANES, LANES), 0) // C_HEAD_DIM) == (_iota((LANES, LANES), 1) // C_HEAD_DIM)).astype(F32)


def _head_sum(x, p):
    return jnp.dot(x, p, precision=HIGHEST, preferred_element_type=F32)


def _head_sum_2x(x, p):
    hi = x.astype(BF16)
    mid = (x - hi.astype(F32)).astype(BF16)
    pb = p.astype(BF16)
    return jnp.dot(hi, pb, preferred_element_type=F32) + jnp.dot(mid, pb, preferred_element_type=F32)


def _rwkv_pre_kernel(k_ref, wl_ref, al_ref, w0_ref, a0_ref, kk_ref_, ka_ref, lw_o, kk_o, km_o, b_o):
    p = _head_sum_matrix()
    for t in range(D_MODEL // LANES):
        sl = slice(t * LANES, (t + 1) * LANES)
        k = k_ref[:, sl]
        z = w0_ref[:, sl] + wl_ref[:, sl]
        w_log = _log_sigmoid(z) - 0.5
        lw_o[:, sl] = -jnp.exp(w_log)
        a = jax.nn.sigmoid(a0_ref[:, sl] + al_ref[:, sl])
        kr = k * kk_ref_[:, sl]
        kk = kr * lax.rsqrt(jnp.maximum(_head_sum(kr * kr, p), 1e-24))
        kk_o[:, sl] = kk
        km_o[:, sl] = k * (1.0 + (a - 1.0) * ka_ref[:, sl])
        b_o[:, sl] = kk * a


def rwkv_pre(k, wl, al, w0, a0, k_k, k_a):
    m, d = k.shape
    bm = _largest_divisor(m, (256, 128, 64))
    big = pl.BlockSpec((bm, d), lambda i: (i, 0))
    row = pl.BlockSpec((1, d), lambda i: (0, 0))
    r2 = lambda a: a.reshape(1, d).astype(F32)
    return pl.pallas_call(
        _rwkv_pre_kernel,
        name="rwkv_pre",
        grid=(m // bm,),
        in_specs=[big, big, big, row, row, row, row],
        out_specs=[big] * 4,
        out_shape=[jax.ShapeDtypeStruct((m, d), F32)] * 4,
        compiler_params=_params("parallel"),
    )(k, wl, al, r2(w0), r2(a0), r2(k_k), r2(k_a))


def _rwkv_rec_kernel(r_ref, lw_ref, k_ref, v_ref, kk_ref, b_ref, y_ref, st_ref, *, tc, npair, mm_dtype):
    c = pl.program_id(2)

    @pl.when(c == 0)
    def _():
        st_ref[...] = jnp.zeros(st_ref.shape, F32)

    def mm(a, b, dims=None):
        a = a.astype(mm_dtype)
        b = b.astype(mm_dtype)
        prec = HIGHEST if mm_dtype == F32 else None
        if dims is None:
            return jnp.dot(a, b, precision=prec, preferred_element_type=F32)
        return lax.dot_general(a, b, dims, precision=prec, preferred_element_type=F32)

    row = _iota((tc, tc), 0)
    col = _iota((tc, tc), 1)
    tri_incl = (row >= col).astype(F32)
    strict = row > col
    incl = row >= col
    eye = (row == col).astype(F32)
    lane = _iota((tc, LANES), 1)
    head_masks = [lane < C_HEAD_DIM, lane >= C_HEAD_DIM]
    blockdiag = (_iota((LANES, LANES), 0) // C_HEAD_DIM) == (_iota((LANES, LANES), 1) // C_HEAD_DIM)
    n_sq = max((tc - 1).bit_length() - 1, 0)

    pairs = range(npair)
    chains = [(p, h) for p in pairs for h in range(2)]
    tile = lambda a, p: a[:, p * LANES:(p + 1) * LANES]
    cast = lambda a: a.astype(mm_dtype)

    r, lw, k, v, kk, b = (ref[...] for ref in (r_ref, lw_ref, k_ref, v_ref, kk_ref, b_ref))
    cum = jnp.dot(tri_incl, lw, precision=HIGHEST, preferred_element_type=F32)
    g_in = jnp.exp(cum)
    g_inv = jnp.exp(-cum)
    a_t = -kk * jnp.exp(cum - lw)
    b_t = b * g_inv
    k_t = k * g_inv
    r_t = r * g_in
    g_end = g_in[tc - 1:tc, :]
    b_c, k_c, v_c, r_c = cast(b_t), cast(k_t), cast(v), cast(r_t)
    bg_x, kg_x = cast((b_t * g_end).T), cast((k_t * g_end).T)
    g_col = g_in.T[:, tc - 1:tc]
    rows = lambda a, p: a[p * LANES:(p + 1) * LANES, :]

    a_m = {(p, h): cast(jnp.where(head_masks[h], tile(a_t, p), 0.0)) for p, h in chains}
    left = {(p, h): jnp.concatenate([a_m[p, h], cast(jnp.where(head_masks[h], tile(r_t, p), 0.0))], axis=0)
            for p, h in chains}
    gb = {c_: mm(left[c_], tile(b_c, c_[0]), _NT) for c_ in chains}
    gk = {c_: mm(left[c_], tile(k_c, c_[0]), _NT) for c_ in chains}
    a_ab = {c_: jnp.where(strict, gb[c_][:tc], 0.0) for c_ in chains}
    a_ak = {c_: cast(jnp.where(strict, gk[c_][:tc], 0.0)) for c_ in chains}
    a_rb = {c_: cast(jnp.where(incl, gb[c_][tc:], 0.0)) for c_ in chains}
    a_rk = {c_: cast(jnp.where(incl, gk[c_][tc:], 0.0)) for c_ in chains}
    akv = {c_: cast(mm(a_ak[c_], tile(v_c, c_[0]))) for c_ in chains}
    y0 = {c_: mm(a_rk[c_], tile(v_c, c_[0])) for c_ in chains}
    x = {c_: eye + a_ab[c_] for c_ in chains}
    pw = {c_: cast(a_ab[c_]) for c_ in chains}
    for _ in range(n_sq):
        pw = {c_: cast(mm(pw[c_], pw[c_])) for c_ in chains}
        x = {c_: x[c_] + mm(pw[c_], cast(x[c_])) for c_ in chains}
    x = {c_: cast(x[c_]) for c_ in chains}
    w_h = {c_: mm(x[c_], a_m[c_]) for c_ in chains}
    u_h = {c_: mm(x[c_], akv[c_]) for c_ in chains}

    s0 = {p: st_ref[p] for p in pairs}
    ws = {p: mm(jnp.concatenate([cast(w_h[p, 0] + w_h[p, 1]), tile(r_c, p)], axis=0), cast(s0[p]))
          for p in pairs}
    u = {p: ws[p][:tc] + jnp.where(head_masks[0], u_h[p, 0], u_h[p, 1]) for p in pairs}
    u_c = {p: cast(u[p]) for p in pairs}
    yb = {c_: mm(a_rb[c_], u_c[c_[0]]) for c_ in chains}
    upd = {p: mm(rows(bg_x, p), u_c[p]) + mm(rows(kg_x, p), tile(v_c, p)) for p in pairs}
    for p in pairs:
        y = ws[p][tc:] + jnp.where(head_masks[0], y0[p, 0] + yb[p, 0], y0[p, 1] + yb[p, 1])
        y_ref[:, p * LANES:(p + 1) * LANES] = y
        st_ref[p] = s0[p] * rows(g_col, p) + jnp.where(blockdiag, upd[p], 0.0)


def rwkv_rec(r, lw, k, v, kk, b, batch, seq, mm_dtype=BF16):
    m, d = r.shape
    tc = min(seq, 64)
    nc = seq // tc
    npair = 16
    gw = npair * LANES
    blk = pl.BlockSpec((tc, gw), lambda bi, g, c: (bi * nc + c, g))
    return pl.pallas_call(
        functools.partial(_rwkv_rec_kernel, tc=tc, npair=npair, mm_dtype=mm_dtype),
        name="rwkv_rec",
        grid=(batch, d // gw, nc),
        in_specs=[blk] * 6,
        out_specs=blk,
        out_shape=jax.ShapeDtypeStruct((m, d), F32),
        scratch_shapes=[pltpu.VMEM((npair, LANES, LANES), F32)],
        compiler_params=_params("parallel", "parallel", "arbitrary"),
    )(r, lw, k, v, kk, b)


def _rwkv_post_kernel(y_ref, r_ref, k_ref, v_ref, g_ref, gng_ref, gnb_ref, rk_ref, o_ref):
    p = _head_sum_matrix()
    inv_n = 1.0 / C_HEAD_DIM
    for t in range(D_MODEL // LANES):
        sl = slice(t * LANES, (t + 1) * LANES)
        y = y_ref[:, sl]
        mu = _head_sum_2x(y, p) * inv_n
        yc = y - mu
        var = _head_sum_2x(yc * yc, p) * inv_n
        yn = yc * lax.rsqrt(var + C_GN_EPS) * gng_ref[:, sl] + gnb_ref[:, sl]
        bonus = _head_sum_2x(r_ref[:, sl] * k_ref[:, sl] * rk_ref[:, sl], p) * v_ref[:, sl]
        o_ref[:, sl] = ((yn + bonus) * g_ref[:, sl]).astype(o_ref.dtype)


def rwkv_post(y, r, kmod, v, g, gn_g, gn_b, r_k):
    m, d = y.shape
    bm = _largest_divisor(m, (256, 128, 64))
    big = pl.BlockSpec((bm, d), lambda i: (i, 0))
    row = pl.BlockSpec((1, d), lambda i: (0, 0))
    r2 = lambda a: a.reshape(1, d).astype(F32)
    return pl.pallas_call(
        _rwkv_post_kernel,
        name="rwkv_post",
        grid=(m // bm,),
        in_specs=[big] * 5 + [row] * 3,
        out_specs=big,
        out_shape=jax.ShapeDtypeStruct((m, d), BF16),
        compiler_params=_params("parallel"),
    )(y, r, kmod, v, g, r2(gn_g), r2(gn_b), r2(r_k))


def rwkv_layer(x, g_norm, mix, w0, w1, w2, a0, a1, a2, g1, g2, k_k, k_a, r_k, w_rkv, w_o, gn_g, gn_b,
               batch, seq):
    bf = lambda a: a.astype(BF16)
    xr, xk, xv, xw, xa, xg = rwkv_mix(x, g_norm, mix, batch, seq)
    r = matmul(xr, bf(w_rkv[0]), F32)
    k = matmul(xk, bf(w_rkv[1]), F32)
    v = matmul(xv, bf(w_rkv[2]), F32)
    wl = lora(xw, bf(w1), bf(w2), "tanh")
    al = lora(xa, bf(a1), bf(a2), "none")
    g = lora(xg, bf(g1), bf(g2), "sigmoid")
    lw, kk, kmod, b = rwkv_pre(k, wl, al, w0, a0, k_k, k_a)
    y = rwkv_rec(r, lw, kmod, v, kk, b, batch, seq)
    o = rwkv_post(y, r, kmod, v, g, gn_g, gn_b, r_k)
    return matmul(o, bf(w_o), F32, residual=x)


def kernel(x, norm_mix, norm_ffn, norm_final, ffn_gate, ffn_up, ffn_down, a_w_in, a_w_o, b_w_in, b_f_bias,
           b_w_o, c_mix, c_w0, c_w1, c_w2, c_a0, c_a1, c_a2, c_g1, c_g2, c_k_k, c_k_a, c_r_k, c_w_rkv,
           c_w_o, c_gn_g, c_gn_b):
    batch, seq, d = x.shape
    depth = norm_mix.shape[0]
    h = x.reshape(batch * seq, d)
    for i in range(depth):
        kind, j = i % 3, i // 3
        if kind == 0:
            h = dsa_layer(h, norm_mix[i], a_w_in[j], a_w_o[j], batch, seq)
        elif kind == 1:
            h = fox_layer(h, norm_mix[i], b_w_in[j], b_f_bias[j], b_w_o[j], batch, seq)
        else:
            h = rwkv_layer(h, norm_mix[i], c_mix[j], c_w0[j], c_w1[j], c_w2[j], c_a0[j], c_a1[j], c_a2[j],
                           c_g1[j], c_g2[j], c_k_k[j], c_k_a[j], c_r_k[j], c_w_rkv[j], c_w_o[j],
                           c_gn_g[j], c_gn_b[j], batch, seq)
        h = swiglu_layer(h, norm_ffn[i], ffn_gate[i], ffn_up[i], ffn_down[i])
    return rmsnorm(h, norm_final, x.dtype).reshape(batch, seq, d)
```

```python
import functools

import jax
import jax.numpy as jnp
from jax import lax
from jax.experimental import pallas as pl
from jax.experimental.pallas import tpu as pltpu

F32 = jnp.float32
BF16 = jnp.bfloat16
I32 = jnp.int32
HIGHEST = lax.Precision.HIGHEST

D_MODEL = 2048
CHUNK = 64
RMS_EPS = 1e-6
ROPE_THETA = 10000.0
A_HEADS, A_KV_HEADS, A_HEAD_DIM = 16, 4, 128
IDX_HEADS, IDX_DIM, TOPK_MAX = 16, 64, 256
B_HEADS, B_HEAD_DIM = 16, 128
C_HEAD_DIM = 64
C_GN_EPS = C_HEAD_DIM * 1e-5

LANES = 128
VMEM_LIMIT_BYTES = 56 * 1024 * 1024

NEG_BIG = -1e30
LOG2E = 1.4426950408889634
_NT = (((1,), (1,)), ((), ()))
_TN = (((0,), (0,)), ((), ()))


def _params(*sem):
    return pltpu.CompilerParams(dimension_semantics=sem, vmem_limit_bytes=VMEM_LIMIT_BYTES)


def _iota(shape, dim):
    return lax.broadcasted_iota(I32, shape, dim)


def _rms(x, g):
    return x * lax.rsqrt(jnp.mean(x * x, axis=-1, keepdims=True) + RMS_EPS) * g


def _rmsnorm_kernel(x_ref, g_ref, o_ref):
    o_ref[...] = _rms(x_ref[...], g_ref[...]).astype(o_ref.dtype)


def rmsnorm(x, g, out_dtype):
    m, d = x.shape
    bm = min(m, 512)
    return pl.pallas_call(
        _rmsnorm_kernel,
        name="rmsnorm",
        grid=(m // bm,),
        in_specs=[pl.BlockSpec((bm, d), lambda i: (i, 0)), pl.BlockSpec((1, d), lambda i: (0, 0))],
        out_specs=pl.BlockSpec((bm, d), lambda i: (i, 0)),
        out_shape=jax.ShapeDtypeStruct((m, d), out_dtype),
        compiler_params=_params("parallel"),
    )(x, g.reshape(1, d))


def _largest_divisor(n, candidates):
    for c in candidates:
        if n % c == 0:
            return c
    return n


def _mm_kernel(x_ref, w_ref, o_ref):
    o_ref[...] = jnp.dot(x_ref[...], w_ref[...], preferred_element_type=F32).astype(o_ref.dtype)


def _mm_res_kernel(x_ref, w_ref, r_ref, o_ref):
    acc = jnp.dot(x_ref[...], w_ref[...], preferred_element_type=F32)
    o_ref[...] = (r_ref[...] + acc).astype(o_ref.dtype)


def matmul(x, w, out_dtype, residual=None):
    m, k = x.shape
    n = w.shape[1]
    bm = _largest_divisor(m, (1024, 512, 256, 128))
    bn = _largest_divisor(n, (1024, 512, 256, 128)) if k <= 2048 else _largest_divisor(n, (512, 256, 128))
    in_specs = [pl.BlockSpec((bm, k), lambda i, j: (i, 0)), pl.BlockSpec((k, bn), lambda i, j: (0, j))]
    args = [x, w]
    body = _mm_kernel
    if residual is not None:
        in_specs.append(pl.BlockSpec((bm, bn), lambda i, j: (i, j)))
        args.append(residual)
        body = _mm_res_kernel
    return pl.pallas_call(
        body,
        name=f"mm_{k}x{n}",
        grid=(m // bm, n // bn),
        in_specs=in_specs,
        out_specs=pl.BlockSpec((bm, bn), lambda i, j: (i, j)),
        out_shape=jax.ShapeDtypeStruct((m, n), out_dtype),
        compiler_params=_params("parallel", "parallel"),
    )(*args)


def _gateup_kernel(x_ref, wg_ref, wu_ref, o_ref):
    x = x_ref[...]
    g = jnp.dot(x, wg_ref[...], preferred_element_type=F32)
    u = jnp.dot(x, wu_ref[...], preferred_element_type=F32)
    o_ref[...] = (g * jax.nn.sigmoid(g) * u).astype(o_ref.dtype)


def gate_up(x, wg, wu):
    m, k = x.shape
    n = wg.shape[1]
    bm = _largest_divisor(m, (1024, 512, 256, 128))
    bn = _largest_divisor(n, (512, 256, 128))
    return pl.pallas_call(
        _gateup_kernel,
        name="gate_up",
        grid=(m // bm, n // bn),
        in_specs=[pl.BlockSpec((bm, k), lambda i, j: (i, 0)),
                  pl.BlockSpec((k, bn), lambda i, j: (0, j)),
                  pl.BlockSpec((k, bn), lambda i, j: (0, j))],
        out_specs=pl.BlockSpec((bm, bn), lambda i, j: (i, j)),
        out_shape=jax.ShapeDtypeStruct((m, n), BF16),
        compiler_params=_params("parallel", "parallel"),
    )(x, wg, wu)


def _lora_kernel(x_ref, w1_ref, w2_ref, o_ref, *, act):
    t = jnp.dot(x_ref[...], w1_ref[...], preferred_element_type=F32)
    if act == "tanh":
        t = jnp.tanh(t)
    elif act == "sigmoid":
        t = jax.nn.sigmoid(t)
    o_ref[...] = jnp.dot(t.astype(BF16), w2_ref[...], preferred_element_type=F32)


def lora(x, w1, w2, act):
    m, k = x.shape
    r = w1.shape[1]
    rp = -(-r // LANES) * LANES
    if act == "sigmoid":
        assert rp == r, "sigmoid(0) != 0: the rank must not be padded"
    w1 = jnp.pad(w1, ((0, 0), (0, rp - r)))
    w2 = jnp.pad(w2, ((0, rp - r), (0, 0)))
    n = w2.shape[1]
    bm = _largest_divisor(m, (1024, 512, 256, 128))
    return pl.pallas_call(
        functools.partial(_lora_kernel, act=act),
        name="lora_" + act,
        grid=(m // bm,),
        in_specs=[pl.BlockSpec((bm, k), lambda i: (i, 0)),
                  pl.BlockSpec((k, rp), lambda i: (0, 0)),
                  pl.BlockSpec((rp, n), lambda i: (0, 0))],
        out_specs=pl.BlockSpec((bm, n), lambda i: (i, 0)),
        out_shape=jax.ShapeDtypeStruct((m, n), F32),
        compiler_params=_params("parallel"),
    )(x, w1, w2)


def swiglu_layer(x, g_norm, w_gate, w_up, w_down):
    hn = rmsnorm(x, g_norm, BF16)
    h = gate_up(hn, w_gate.astype(BF16), w_up.astype(BF16))
    return matmul(h, w_down.astype(BF16), F32, residual=x)


def _log_sigmoid(x):
    return jnp.minimum(x, 0.0) - jnp.log1p(jnp.exp(-jnp.abs(x)))


def _fox_prep_kernel(fl_ref, bias_ref, c_ref, ct_ref, *, seq, ch):
    tri = (_iota((ch, ch), 0) >= _iota((ch, ch), 1)).astype(F32)
    carry = jnp.zeros((1, LANES), F32)
    for c in range(seq // ch):
        lf = _log_sigmoid(fl_ref[c * ch:(c + 1) * ch, :] + bias_ref[...])
        cs = jnp.dot(tri, lf, precision=HIGHEST, preferred_element_type=F32) + carry
        c2 = cs * LOG2E
        c_ref[c * ch:(c + 1) * ch, :] = c2
        ct_ref[:, c * ch:(c + 1) * ch] = c2.T
        carry = cs[ch - 1:ch, :]


def fox_prep(fl, bias_pad, batch, seq):
    ch = min(seq, 256)
    return pl.pallas_call(
        functools.partial(_fox_prep_kernel, seq=seq, ch=ch),
        name="fox_prep",
        grid=(batch,),
        in_specs=[pl.BlockSpec((seq, LANES), lambda b: (b, 0)), pl.BlockSpec((1, LANES), lambda b: (0, 0))],
        out_specs=[pl.BlockSpec((seq, LANES), lambda b: (b, 0)),
                   pl.BlockSpec((None, LANES, seq), lambda b: (b, 0, 0))],
        out_shape=[jax.ShapeDtypeStruct((batch * seq, LANES), F32),
                   jax.ShapeDtypeStruct((batch, LANES, seq), F32)],
        compiler_params=_params("parallel"),
    )(fl, bias_pad)


def _softmax_strip(z, r0, m_in_ref, m_out_ref, p_ref, row_term=None):
    rows, bk = z.shape
    tiles = [z[:, t * LANES:(t + 1) * LANES] for t in range(bk // LANES)]
    zmax = functools.reduce(jnp.maximum, tiles)
    zmax = jnp.broadcast_to(jnp.max(zmax, axis=1, keepdims=True), (rows, LANES))
    if row_term is not None:
        zmax = zmax + row_term
    m_new = jnp.maximum(m_in_ref[r0:r0 + rows, :], zmax)
    m_out_ref[r0:r0 + rows, :] = m_new
    shift = m_new if row_term is None else m_new - row_term
    for t, zt in enumerate(tiles):
        p_ref[r0:r0 + rows, t * LANES:(t + 1) * LANES] = jnp.exp2(zt - shift).astype(BF16)


def _accumulate(acc_ref, m_in_ref, m_out_ref, p_ref, v, row_lo=0):
    alpha = jnp.exp2(m_in_ref[row_lo:, :] - m_out_ref[row_lo:, :])
    v1 = jnp.concatenate([v, jnp.ones(v.shape, v.dtype)], axis=1)
    pv = jnp.dot(p_ref[row_lo:, :], v1, preferred_element_type=F32)
    for t in range(2):
        sl = slice(t * LANES, (t + 1) * LANES)
        acc_ref[row_lo:, sl] = alpha * acc_ref[row_lo:, sl] + pv[:, sl]


def _fox_attn_kernel(q_ref, k_ref, v_ref, c_ref, ck_ref, o_ref, s0_ref, s1_ref, p_ref, m0_ref, m1_ref, acc_ref,
                     *, bq, bk, strip):
    h = pl.program_id(1)
    qi = pl.program_id(2)
    q = q_ref[...]
    cq = jnp.sum(jnp.where(_iota((bq, LANES), 1) == h, c_ref[...], 0.0), axis=1, keepdims=True)
    cq = jnp.broadcast_to(cq, (bq, LANES))
    m0_ref[...] = jnp.full((bq, LANES), NEG_BIG, F32)
    acc_ref[...] = jnp.zeros((bq, 2 * LANES), F32)
    even, odd = (s0_ref, m0_ref, m1_ref), (s1_ref, m1_ref, m0_ref)

    def scores(t, bufs, row_lo=0):
        ks = pl.multiple_of(t * bk, bk)
        bufs[0][row_lo:, :] = lax.dot_general(q[row_lo:], k_ref[pl.ds(ks, bk), :], _NT,
                                              preferred_element_type=F32)

    def chunk(t, bufs, row_lo=0, diag_off=None):
        s_ref, m_in_ref, m_out_ref = bufs
        ks = pl.multiple_of(t * bk, bk)
        ck = ck_ref[:, pl.ds(ks, bk)]
        for r0 in range(row_lo, bq, strip):
            z = s_ref[r0:r0 + strip, :] - ck
            if diag_off is not None and diag_off + bk - 1 > r0:
                visible = diag_off + _iota((strip, bk), 1) <= r0 + _iota((strip, bk), 0)
                z = jnp.where(visible, z, NEG_BIG)
            _softmax_strip(z, r0, m_in_ref, m_out_ref, p_ref, row_term=cq[r0:r0 + strip])
        _accumulate(acc_ref, m_in_ref, m_out_ref, p_ref, v_ref[pl.ds(ks, bk), :], row_lo)

    n_diag = bq // bk
    assert n_diag % 2 == 0
    scores(0, even)

    def pair(jp, _):
        scores(2 * jp + 1, odd)
        chunk(2 * jp, even)
        scores(2 * jp + 2, even)
        chunk(2 * jp + 1, odd)
        return 0

    lax.fori_loop(0, qi * (n_diag // 2), pair, 0)
    bufs = (even, odd)
    for d in range(n_diag):
        if d + 1 < n_diag:
            scores(qi * n_diag + d + 1, bufs[(d + 1) % 2], row_lo=(d + 1) * bk)
        chunk(qi * n_diag + d, bufs[d % 2], row_lo=d * bk, diag_off=d * bk)
    o_ref[...] = (acc_ref[:, :LANES] / acc_ref[:, LANES:]).astype(o_ref.dtype)


def fox_attention(qkv, c, ck4, batch, seq):
    nh = B_HEADS
    bq = min(seq, 1024)
    bk = min(seq // 2, 256)
    nq = seq // bq
    return pl.pallas_call(
        functools.partial(_fox_attn_kernel, bq=bq, bk=bk, strip=min(bq, 128)),
        name="fox_attn",
        grid=(batch, nh, nq),
        in_specs=[pl.BlockSpec((bq, LANES), lambda b, h, i: (b * nq + i, h)),
                  pl.BlockSpec((seq, LANES), lambda b, h, i: (b, nh + h)),
                  pl.BlockSpec((seq, LANES), lambda b, h, i: (b, 2 * nh + h)),
                  pl.BlockSpec((bq, LANES), lambda b, h, i: (b * nq + i, 0)),
                  pl.BlockSpec((None, None, 1, seq), lambda b, h, i: (b, h, 0, 0))],
        out_specs=pl.BlockSpec((bq, LANES), lambda b, h, i: (b * nq + i, h)),
        out_shape=jax.ShapeDtypeStruct((batch * seq, nh * LANES), BF16),
        scratch_shapes=[pltpu.VMEM((bq, bk), F32), pltpu.VMEM((bq, bk), F32), pltpu.VMEM((bq, bk), BF16),
                        pltpu.VMEM((bq, LANES), F32), pltpu.VMEM((bq, LANES), F32),
                        pltpu.VMEM((bq, 2 * LANES), F32)],
        compiler_params=_params("parallel", "parallel", "parallel"),
    )(qkv, qkv, qkv, c, ck4)


def fox_layer(x, g_norm, w_in, f_bias, w_o, batch, seq):
    nh = B_HEADS
    hn = rmsnorm(x, g_norm, BF16)
    n_q, n_qkv = nh * B_HEAD_DIM, 3 * nh * B_HEAD_DIM
    col_scale = jnp.where(jnp.arange(n_qkv) < n_q, B_HEAD_DIM ** -0.5 * LOG2E, 1.0).astype(F32)
    qkv = matmul(hn, (w_in[:, :n_qkv] * col_scale).astype(BF16), BF16)
    w_f = jnp.pad(w_in[:, n_qkv:], ((0, 0), (0, LANES - nh))).astype(BF16)
    fl = matmul(hn, w_f, F32)
    bias_pad = jnp.pad(f_bias.astype(F32), (0, LANES - nh)).reshape(1, LANES)
    c, ct = fox_prep(fl, bias_pad, batch, seq)
    ck4 = ct[:, :nh, :].reshape(batch, nh, 1, seq)
    o = fox_attention(qkv, c, ck4, batch, seq)
    return matmul(o, w_o.astype(BF16), F32, residual=x)


def _rope_tables(seq, head_dim):
    half = head_dim // 2
    inv = ROPE_THETA ** (-jnp.arange(half, dtype=F32) / half)
    ang = jnp.arange(seq, dtype=F32)[:, None] * inv[None, :]
    cos = jnp.cos(ang)
    sin = jnp.sin(ang)
    reps = LANES // head_dim
    cos_t = jnp.tile(jnp.concatenate([cos, cos], axis=1), (1, reps))
    sin_t = jnp.tile(jnp.concatenate([-sin, sin], axis=1), (1, reps))
    return cos_t, sin_t


def _rope128(x, cos, sin):
    return x * cos + pltpu.roll(x, 64, 1) * sin


def _rope64(x, cos, sin, lane):
    partner = jnp.where((lane % 64) < 32, pltpu.roll(x, 96, 1), pltpu.roll(x, 32, 1))
    return x * cos + partner * sin


def _dsa_prep_kernel(main_ref, tail_ref, c128_ref, s128_ref, c64_ref, s64_ref,
                     q_ref, k_ref, v_ref, qi_ref, ki_ref, wi_ref, *, br, q_scale, wi_scale):
    c128, s128 = c128_ref[...], s128_ref[...]
    c64, s64 = c64_ref[...], s64_ref[...]
    lane = _iota((br, LANES), 1)
    tile = lambda t: main_ref[:, t * LANES:(t + 1) * LANES]
    nq, nkv, npair = A_HEADS, A_KV_HEADS, IDX_HEADS // 2
    cq128, sq128 = c128 * q_scale, s128 * q_scale
    for t in range(nq):
        q_ref[:, t * LANES:(t + 1) * LANES] = _rope128(tile(t), cq128, sq128).astype(BF16)
    for t in range(nkv):
        k_ref[:, t * LANES:(t + 1) * LANES] = _rope128(tile(nq + t), c128, s128).astype(BF16)
        v_ref[:, t * LANES:(t + 1) * LANES] = tile(nq + nkv + t).astype(BF16)
    for t in range(npair):
        r = _rope64(tile(nq + 2 * nkv + t), c64, s64, lane)
        qi_ref[:, (2 * t) * LANES:(2 * t + 1) * LANES] = jnp.where(lane < 64, r, 0.0).astype(BF16)
        qi_ref[:, (2 * t + 1) * LANES:(2 * t + 2) * LANES] = jnp.where(lane >= 64, r, 0.0).astype(BF16)
    tl = tail_ref[...]
    kr = _rope64(tl, c64, s64, lane)
    ki_ref[...] = jnp.where(lane < 64, kr, pltpu.roll(kr, 64, 1)).astype(BF16)
    wi_ref[...] = tl * wi_scale


def dsa_prep(main, tail, batch, seq):
    br = min(seq, 256)
    nblk = seq // br
    c128, s128 = _rope_tables(seq, A_HEAD_DIM)
    c64, s64 = _rope_tables(seq, IDX_DIM)
    m = batch * seq
    tab = pl.BlockSpec((br, LANES), lambda i: (i % nblk, 0))
    row = lambda w: pl.BlockSpec((br, w), lambda i: (i, 0))
    return pl.pallas_call(
        functools.partial(_dsa_prep_kernel, br=br, q_scale=A_HEAD_DIM ** -0.5 * LOG2E,
                          wi_scale=IDX_HEADS ** -0.5 * IDX_DIM ** -0.5),
        name="dsa_prep",
        grid=(m // br,),
        in_specs=[row(main.shape[1]), row(LANES), tab, tab, tab, tab],
        out_specs=[row(A_HEADS * LANES), row(A_KV_HEADS * LANES), row(A_KV_HEADS * LANES),
                   row(IDX_HEADS * LANES), row(LANES), row(LANES)],
        out_shape=[jax.ShapeDtypeStruct((m, A_HEADS * LANES), BF16),
                   jax.ShapeDtypeStruct((m, A_KV_HEADS * LANES), BF16),
                   jax.ShapeDtypeStruct((m, A_KV_HEADS * LANES), BF16),
                   jax.ShapeDtypeStruct((m, IDX_HEADS * LANES), BF16),
                   jax.ShapeDtypeStruct((m, LANES), BF16),
                   jax.ShapeDtypeStruct((m, LANES), F32)],
        compiler_params=_params("parallel"),
    )(main, tail, c128, s128, c64, s64)


_MIN32 = -2 ** 31
_BITS_PER_CHECK = 4
_KEY_NEG_INF = (0xFF800000 - 2 ** 32) ^ 0x7FFFFFFF


def _order_key(s):
    b = pltpu.bitcast(s, I32)
    return jnp.where(b >= 0, b, b ^ 0x7FFFFFFF)


def _dsa_index_kernel(qi_ref, ki_ref, wi_ref, bias_ref, key_ref, *, bq, seq, kc, topk):
    qb = pl.program_id(1)
    n_kc = ((qb + 1) * bq + kc - 1) // kc
    g8 = kc // 8
    rb = min(bq, 256)
    lane = _iota((bq, LANES), 1)
    wi = wi_ref[...]
    wcol = [jnp.sum(jnp.where(lane == 64 + h, wi, 0.0), axis=1, keepdims=True) for h in range(IDX_HEADS)]
    k_off = _iota((rb, kc), 1)

    def kc_body(j, _):
        ks = pl.multiple_of(j * kc, kc)
        kblk = ki_ref[pl.ds(ks, kc), :]
        for r0 in range(0, bq, rb):
            acc = jnp.zeros((rb, kc), F32)
            for h in range(IDX_HEADS):
                s = lax.dot_general(qi_ref[r0:r0 + rb, h * LANES:(h + 1) * LANES], kblk, _NT,
                                    preferred_element_type=F32)
                acc = acc + wcol[h][r0:r0 + rb] * jnp.maximum(s, 0.0)
            q_chunk = (qb * bq + r0 + _iota((rb, kc), 0)) // CHUNK
            acc = jnp.where((ks + k_off) // CHUNK <= q_chunk, acc, -jnp.inf)
            key_ref[pl.ds(ks, kc), r0:r0 + rb] = _order_key(acc.T)
        return 0

    lax.fori_loop(0, n_kc, kc_body, 0)

    def count(indicator):
        def body(j, acc):
            x = key_ref[pl.ds(pl.multiple_of(j * kc, kc), kc), :].reshape(g8, 8, bq)
            return acc + jnp.sum(indicator(x, j), axis=0)
        acc = lax.fori_loop(0, n_kc, body, jnp.zeros((8, bq), F32))
        return jnp.sum(acc, axis=0, keepdims=True)

    n_adm = ((qb * bq + _iota((1, bq), 1)) // CHUNK + 1) * CHUNK
    take_all = n_adm <= topk
    settled = lambda cnt_t: jnp.logical_or(take_all, cnt_t == topk)
    pending = lambda cnt_t: jnp.max(jnp.where(settled(cnt_t), 0, 1))

    def bit_cond(c):
        return jnp.logical_and(c[0] < 32, c[3] > 0)

    def bit_body(c):
        i, tu, cnt_t, _ = c
        for b in range(_BITS_PER_CHECK):
            cand_u = tu | lax.shift_left(jnp.int32(1), 31 - (i + b))
            cand_s = jnp.broadcast_to(cand_u ^ _MIN32, (8, bq))
            cnt = count(lambda x, j: jnp.where(x >= cand_s, 1.0, 0.0))
            ok = cnt >= topk
            cnt_t = jnp.where(ok, cnt, cnt_t)
            tu = jnp.where(ok, cand_u, tu)
        return i + _BITS_PER_CHECK, tu, cnt_t, pending(cnt_t)

    cnt0 = jnp.zeros((1, bq), F32) + (n_kc * kc).astype(F32)
    _, tu, cnt_ge, n_pending = lax.while_loop(
        bit_cond, bit_body, (jnp.int32(0), jnp.zeros((1, bq), I32), cnt0, pending(cnt0)))
    ts = tu ^ _MIN32
    thr = jnp.where(take_all, _KEY_NEG_INF + 1, ts)

    def no_ties():
        return thr - 1, jnp.zeros((1, bq), I32), jnp.full((1, bq), -1, I32)

    def with_ties():
        tied = jnp.logical_not(settled(cnt_ge))
        ts8 = jnp.broadcast_to(ts, (8, bq))
        cnt_gt = count(lambda x, j: jnp.where(x > ts8, 1.0, 0.0))
        need = topk - cnt_gt
        k_idx = _iota((g8, 8, bq), 0) * 8 + _iota((g8, 8, bq), 1)

        def j_body(i, j0):
            cand = j0 | lax.shift_left(jnp.int32(1), (seq.bit_length() - 2) - i)
            cand8 = jnp.broadcast_to(cand, (8, bq))
            cnt = count(lambda x, j: jnp.where(x == ts8, jnp.where(j * kc + k_idx < cand8, 1.0, 0.0), 0.0))
            return jnp.where(cnt < need, cand, j0)
        j0 = lax.fori_loop(0, seq.bit_length() - 1, j_body, jnp.zeros((1, bq), I32))
        return jnp.where(tied, ts, thr - 1), jnp.where(tied, ts, 0), jnp.where(tied, j0, -1)

    gt_thr, eq_val, jlim = lax.cond(n_pending > 0, with_ties, no_ties)

    bias_ref[...] = jnp.full((bq, seq), NEG_BIG, BF16)
    k_row = _iota((kc, rb), 0)

    def out_body(j, _):
        ks = pl.multiple_of(j * kc, kc)
        for r0 in range(0, bq, rb):
            qs = slice(r0, r0 + rb)
            x = key_ref[pl.ds(ks, kc), qs]
            tie_ok = jnp.where(ks + k_row <= jlim[:, qs], 0.0, NEG_BIG)
            b = jnp.where(x > gt_thr[:, qs], 0.0, jnp.where(x == eq_val[:, qs], tie_ok, NEG_BIG))
            bias_ref[qs, pl.ds(ks, kc)] = b.T.astype(BF16)
        return 0

    lax.fori_loop(0, n_kc, out_body, 0)


def dsa_index(qi, ki, wi, batch, seq):
    bq = min(seq, 512)
    nq = seq // bq
    kc = min(seq, 256)
    topk = min(TOPK_MAX, seq // 4)
    return pl.pallas_call(
        functools.partial(_dsa_index_kernel, bq=bq, seq=seq, kc=kc, topk=topk),
        name="dsa_index",
        grid=(batch, nq),
        in_specs=[pl.BlockSpec((bq, IDX_HEADS * LANES), lambda b, i: (b * nq + i, 0)),
                  pl.BlockSpec((seq, LANES), lambda b, i: (b, 0)),
                  pl.BlockSpec((bq, LANES), lambda b, i: (b * nq + i, 0))],
        out_specs=pl.BlockSpec((bq, seq), lambda b, i: (b * nq + i, 0)),
        out_shape=jax.ShapeDtypeStruct((batch * seq, seq), BF16),
        scratch_shapes=[pltpu.VMEM((seq, bq), I32)],
        compiler_params=_params("parallel", "parallel"),
    )(qi, ki, wi)


def _dsa_attn_kernel(q_ref, k_ref, v_ref, bias_ref, o_ref, s0_ref, s1_ref, p_ref, m0_ref, m1_ref, acc_ref,
                     *, bq, bk, group, strip):
    qi = pl.program_id(2)
    q = jnp.concatenate([q_ref[:, r * LANES:(r + 1) * LANES] for r in range(group)], axis=0)
    rows = group * bq
    m0_ref[...] = jnp.full((rows, LANES), NEG_BIG, F32)
    acc_ref[...] = jnp.zeros((rows, 2 * LANES), F32)
    even, odd = (s0_ref, m0_ref, m1_ref), (s1_ref, m1_ref, m0_ref)

    n_kv = ((qi + 1) * bq + bk - 1) // bk

    def scores(j, bufs):
        ks = pl.multiple_of(jnp.minimum(j, n_kv - 1) * bk, bk)
        bufs[0][...] = lax.dot_general(q, k_ref[pl.ds(ks, bk), :], _NT, preferred_element_type=F32)

    def chunk(j, bufs):
        s_ref, m_in_ref, m_out_ref = bufs
        ks = pl.multiple_of(j * bk, bk)
        for b0 in range(0, bq, strip):
            bias = bias_ref[b0:b0 + strip, pl.ds(ks, bk)].astype(F32)
            for r in range(group):
                r0 = r * bq + b0
                _softmax_strip(s_ref[r0:r0 + strip, :] + bias, r0, m_in_ref, m_out_ref, p_ref)
        _accumulate(acc_ref, m_in_ref, m_out_ref, p_ref, v_ref[pl.ds(ks, bk), :])

    scores(0, even)

    def pair(jp, _):
        scores(2 * jp + 1, odd)
        chunk(2 * jp, even)
        scores(2 * jp + 2, even)
        chunk(2 * jp + 1, odd)
        return 0

    lax.fori_loop(0, n_kv // 2, pair, 0)

    @pl.when(n_kv % 2 == 1)
    def _():
        chunk(n_kv - 1, even)

    for r in range(group):
        sl = slice(r * bq, (r + 1) * bq)
        o_ref[:, r * LANES:(r + 1) * LANES] = (acc_ref[sl, :LANES] / acc_ref[sl, LANES:]).astype(o_ref.dtype)


def dsa_attention(q, k, v, bias, batch, seq):
    group = A_HEADS // A_KV_HEADS
    bq = bk = min(seq, 256)
    nq = seq // bq
    gw = group * LANES
    return pl.pallas_call(
        functools.partial(_dsa_attn_kernel, bq=bq, bk=bk, group=group, strip=min(bq, 64)),
        name="dsa_attn",
        grid=(batch, A_KV_HEADS, nq),
        in_specs=[pl.BlockSpec((bq, gw), lambda b, g, i: (b * nq + i, g)),
                  pl.BlockSpec((seq, LANES), lambda b, g, i: (b, g)),
                  pl.BlockSpec((seq, LANES), lambda b, g, i: (b, g)),
                  pl.BlockSpec((bq, seq), lambda b, g, i: (b * nq + i, 0))],
        out_specs=pl.BlockSpec((bq, gw), lambda b, g, i: (b * nq + i, g)),
        out_shape=jax.ShapeDtypeStruct((batch * seq, A_HEADS * LANES), BF16),
        scratch_shapes=[pltpu.VMEM((group * bq, bk), F32), pltpu.VMEM((group * bq, bk), F32),
                        pltpu.VMEM((group * bq, bk), BF16),
                        pltpu.VMEM((group * bq, LANES), F32), pltpu.VMEM((group * bq, LANES), F32),
                        pltpu.VMEM((group * bq, 2 * LANES), F32)],
        compiler_params=_params("parallel", "parallel", "parallel"),
    )(q, k, v, bias)


def dsa_layer(x, g_norm, w_in, w_o, batch, seq):
    hn = rmsnorm(x, g_norm, BF16)
    n_main = (A_HEADS + 2 * A_KV_HEADS) * A_HEAD_DIM + IDX_HEADS * IDX_DIM
    n_tail = w_in.shape[1] - n_main
    main = matmul(hn, w_in[:, :n_main].astype(BF16), F32)
    w_tail = jnp.pad(w_in[:, n_main:], ((0, 0), (0, LANES - n_tail))).astype(BF16)
    tail = matmul(hn, w_tail, F32)
    q, k, v, qi, ki, wi = dsa_prep(main, tail, batch, seq)
    bias = dsa_index(qi, ki, wi, batch, seq)
    o = dsa_attention(q, k, v, bias, batch, seq)
    return matmul(o, w_o.astype(BF16), F32, residual=x)


def _rwkv_mix_kernel(x_ref, g_ref, mix_ref, *rest, bs):
    outs, hbuf = rest[:6], rest[6]
    j = pl.program_id(1)

    @pl.when(j == 0)
    def _():
        hbuf[0:8, :] = jnp.zeros((8, D_MODEL), F32)

    hn = _rms(x_ref[...], g_ref[...])
    hbuf[8:8 + bs, :] = hn
    xx = hbuf[7:7 + bs, :] - hn
    hbuf[7:8, :] = hn[bs - 1:bs, :]
    for i in range(6):
        outs[i][...] = (hn + xx * mix_ref[i:i + 1, :]).astype(BF16)


def rwkv_mix(x, g_norm, mix, batch, seq):
    bs = min(seq, 256)
    nb = seq // bs
    m, d = x.shape
    mix8 = jnp.pad(mix, ((0, 2), (0, 0)))
    return pl.pallas_call(
        functools.partial(_rwkv_mix_kernel, bs=bs),
        name="rwkv_mix",
        grid=(batch, nb),
        in_specs=[pl.BlockSpec((bs, d), lambda b, j: (b * nb + j, 0)),
                  pl.BlockSpec((1, d), lambda b, j: (0, 0)),
                  pl.BlockSpec((8, d), lambda b, j: (0, 0))],
        out_specs=[pl.BlockSpec((bs, d), lambda b, j: (b * nb + j, 0))] * 6,
        out_shape=[jax.ShapeDtypeStruct((m, d), BF16)] * 6,
        scratch_shapes=[pltpu.VMEM((bs + 8, d), F32)],
        compiler_params=_params("parallel", "arbitrary"),
    )(x, g_norm.reshape(1, d), mix8)


def _head_sum_matrix():
    return ((_iota((LANES, LANES), 0) // C_HEAD_DIM) == (_iota((LANES, LANES), 1) // C_HEAD_DIM)).astype(F32)


def _head_sum(x, p):
    return jnp.dot(x, p, precision=HIGHEST, preferred_element_type=F32)


def _head_sum_2x(x, p):
    hi = x.astype(BF16)
    mid = (x - hi.astype(F32)).astype(BF16)
    pb = p.astype(BF16)
    return jnp.dot(hi, pb, preferred_element_type=F32) + jnp.dot(mid, pb, preferred_element_type=F32)


def _rwkv_pre_kernel(k_ref, wl_ref, al_ref, w0_ref, a0_ref, kk_ref_, ka_ref, lw_o, kk_o, km_o, b_o):
    p = _head_sum_matrix()
    for t in range(D_MODEL // LANES):
        sl = slice(t * LANES, (t + 1) * LANES)
        k = k_ref[:, sl]
        z = w0_ref[:, sl] + wl_ref[:, sl]
        w_log = _log_sigmoid(z) - 0.5
        lw_o[:, sl] = -jnp.exp(w_log)
        a = jax.nn.sigmoid(a0_ref[:, sl] + al_ref[:, sl])
        kr = k * kk_ref_[:, sl]
        kk = kr * lax.rsqrt(jnp.maximum(_head_sum(kr * kr, p), 1e-24))
        kk_o[:, sl] = kk
        km_o[:, sl] = k * (1.0 + (a - 1.0) * ka_ref[:, sl])
        b_o[:, sl] = kk * a


def rwkv_pre(k, wl, al, w0, a0, k_k, k_a):
    m, d = k.shape
    bm = _largest_divisor(m, (256, 128, 64))
    big = pl.BlockSpec((bm, d), lambda i: (i, 0))
    row = pl.BlockSpec((1, d), lambda i: (0, 0))
    r2 = lambda a: a.reshape(1, d).astype(F32)
    return pl.pallas_call(
        _rwkv_pre_kernel,
        name="rwkv_pre",
        grid=(m // bm,),
        in_specs=[big, big, big, row, row, row, row],
        out_specs=[big] * 4,
        out_shape=[jax.ShapeDtypeStruct((m, d), F32)] * 4,
        compiler_params=_params("parallel"),
    )(k, wl, al, r2(w0), r2(a0), r2(k_k), r2(k_a))


def _rwkv_rec_kernel(r_ref, lw_ref, k_ref, v_ref, kk_ref, b_ref, y_ref, st_ref, *, tc, npair, mm_dtype):
    c = pl.program_id(2)

    @pl.when(c == 0)
    def _():
        st_ref[...] = jnp.zeros(st_ref.shape, F32)

    def mm(a, b, dims=None):
        a = a.astype(mm_dtype)
        b = b.astype(mm_dtype)
        prec = HIGHEST if mm_dtype == F32 else None
        if dims is None:
            return jnp.dot(a, b, precision=prec, preferred_element_type=F32)
        return lax.dot_general(a, b, dims, precision=prec, preferred_element_type=F32)

    row = _iota((tc, tc), 0)
    col = _iota((tc, tc), 1)
    tri_incl = (row >= col).astype(F32)
    strict = row > col
    incl = row >= col
    eye = (row == col).astype(F32)
    lane = _iota((tc, LANES), 1)
    head_masks = [lane < C_HEAD_DIM, lane >= C_HEAD_DIM]
    blockdiag = (_iota((LANES, LANES), 0) // C_HEAD_DIM) == (_iota((LANES, LANES), 1) // C_HEAD_DIM)
    n_sq = max((tc - 1).bit_length() - 1, 0)

    pairs = range(npair)
    chains = [(p, h) for p in pairs for h in range(2)]
    tile = lambda a, p: a[:, p * LANES:(p + 1) * LANES]
    cast = lambda a: a.astype(mm_dtype)

    r, lw, k, v, kk, b = (ref[...] for ref in (r_ref, lw_ref, k_ref, v_ref, kk_ref, b_ref))
    cum = jnp.dot(tri_incl, lw, precision=HIGHEST, preferred_element_type=F32)
    g_in = jnp.exp(cum)
    g_inv = jnp.exp(-cum)
    a_t = -kk * jnp.exp(cum - lw)
    b_t = b * g_inv
    k_t = k * g_inv
    r_t = r * g_in
    g_end = g_in[tc - 1:tc, :]
    b_c, k_c, v_c, r_c = cast(b_t), cast(k_t), cast(v), cast(r_t)
    bk_x = cast(jnp.concatenate([b_t * g_end, k_t * g_end], axis=0).T)
    g_col = g_in.T[:, tc - 1:tc]
    rows = lambda a, p: a[p * LANES:(p + 1) * LANES, :]

    a_m = {(p, h): cast(jnp.where(head_masks[h], tile(a_t, p), 0.0)) for p, h in chains}
    left = {(p, h): jnp.concatenate([a_m[p, h], cast(jnp.where(head_masks[h], tile(r_t, p), 0.0))], axis=0)
            for p, h in chains}
    gb = {c_: mm(left[c_], tile(b_c, c_[0]), _NT) for c_ in chains}
    gk = {c_: mm(left[c_], tile(k_c, c_[0]), _NT) for c_ in chains}
    a_ab = {c_: jnp.where(strict, gb[c_][:tc], 0.0) for c_ in chains}
    a_rb = {c_: cast(jnp.where(incl, gb[c_][tc:], 0.0)) for c_ in chains}
    low2 = jnp.concatenate([strict, incl], axis=0)
    kv = {c_: mm(cast(jnp.where(low2, gk[c_], 0.0)), tile(v_c, c_[0])) for c_ in chains}
    akv = {c_: cast(kv[c_][:tc]) for c_ in chains}
    y0 = {c_: kv[c_][tc:] for c_ in chains}
    x = {c_: eye + a_ab[c_] for c_ in chains}
    pw = {c_: cast(a_ab[c_]) for c_ in chains}
    for _ in range(n_sq):
        pw = {c_: cast(mm(pw[c_], pw[c_])) for c_ in chains}
        x = {c_: x[c_] + mm(pw[c_], cast(x[c_])) for c_ in chains}
    x = {c_: cast(x[c_]) for c_ in chains}
    xw = {c_: mm(x[c_], jnp.concatenate([a_m[c_], akv[c_]], axis=1)) for c_ in chains}
    w_h = {c_: xw[c_][:, :LANES] for c_ in chains}
    u_h = {c_: xw[c_][:, LANES:] for c_ in chains}

    s0 = {p: st_ref[p] for p in pairs}
    ws = {p: mm(jnp.concatenate([cast(w_h[p, 0] + w_h[p, 1]), tile(r_c, p)], axis=0), cast(s0[p]))
          for p in pairs}
    u = {p: ws[p][:tc] + jnp.where(head_masks[0], u_h[p, 0], u_h[p, 1]) for p in pairs}
    u_c = {p: cast(u[p]) for p in pairs}
    yb = {c_: mm(a_rb[c_], u_c[c_[0]]) for c_ in chains}
    upd = {p: mm(rows(bk_x, p), jnp.concatenate([u_c[p], tile(v_c, p)], axis=0)) for p in pairs}
    for p in pairs:
        y = ws[p][tc:] + jnp.where(head_masks[0], y0[p, 0] + yb[p, 0], y0[p, 1] + yb[p, 1])
        y_ref[:, p * LANES:(p + 1) * LANES] = y
        st_ref[p] = s0[p] * rows(g_col, p) + jnp.where(blockdiag, upd[p], 0.0)


def rwkv_rec(r, lw, k, v, kk, b, batch, seq, mm_dtype=BF16):
    m, d = r.shape
    tc = min(seq, 64)
    nc = seq // tc
    npair = 16
    gw = npair * LANES
    blk = pl.BlockSpec((tc, gw), lambda bi, g, c: (bi * nc + c, g))
    return pl.pallas_call(
        functools.partial(_rwkv_rec_kernel, tc=tc, npair=npair, mm_dtype=mm_dtype),
        name="rwkv_rec",
        grid=(batch, d // gw, nc),
        in_specs=[blk] * 6,
        out_specs=blk,
        out_shape=jax.ShapeDtypeStruct((m, d), F32),
        scratch_shapes=[pltpu.VMEM((npair, LANES, LANES), F32)],
        compiler_params=_params("parallel", "parallel", "arbitrary"),
    )(r, lw, k, v, kk, b)


def _rwkv_post_kernel(y_ref, r_ref, k_ref, v_ref, g_ref, gng_ref, gnb_ref, rk_ref, o_ref):
    p = _head_sum_matrix()
    inv_n = 1.0 / C_HEAD_DIM
    for t in range(D_MODEL // LANES):
        sl = slice(t * LANES, (t + 1) * LANES)
        y = y_ref[:, sl]
        mu = _head_sum_2x(y, p) * inv_n
        yc = y - mu
        var = _head_sum_2x(yc * yc, p) * inv_n
        yn = yc * lax.rsqrt(var + C_GN_EPS) * gng_ref[:, sl] + gnb_ref[:, sl]
        bonus = _head_sum_2x(r_ref[:, sl] * k_ref[:, sl] * rk_ref[:, sl], p) * v_ref[:, sl]
        o_ref[:, sl] = ((yn + bonus) * g_ref[:, sl]).astype(o_ref.dtype)


def rwkv_post(y, r, kmod, v, g, gn_g, gn_b, r_k):
    m, d = y.shape
    bm = _largest_divisor(m, (256, 128, 64))
    big = pl.BlockSpec((bm, d), lambda i: (i, 0))
    row = pl.BlockSpec((1, d), lambda i: (0, 0))
    r2 = lambda a: a.reshape(1, d).astype(F32)
    return pl.pallas_call(
        _rwkv_post_kernel,
        name="rwkv_post",
        grid=(m // bm,),
        in_specs=[big] * 5 + [row] * 3,
        out_specs=big,
        out_shape=jax.ShapeDtypeStruct((m, d), BF16),
        compiler_params=_params("parallel"),
    )(y, r, kmod, v, g, r2(gn_g), r2(gn_b), r2(r_k))


def rwkv_layer(x, g_norm, mix, w0, w1, w2, a0, a1, a2, g1, g2, k_k, k_a, r_k, w_rkv, w_o, gn_g, gn_b,
               batch, seq):
    bf = lambda a: a.astype(BF16)
    xr, xk, xv, xw, xa, xg = rwkv_mix(x, g_norm, mix, batch, seq)
    r = matmul(xr, bf(w_rkv[0]), F32)
    k = matmul(xk, bf(w_rkv[1]), F32)
    v = matmul(xv, bf(w_rkv[2]), F32)
    wl = lora(xw, bf(w1), bf(w2), "tanh")
    al = lora(xa, bf(a1), bf(a2), "none")
    g = lora(xg, bf(g1), bf(g2), "sigmoid")
    lw, kk, kmod, b = rwkv_pre(k, wl, al, w0, a0, k_k, k_a)
    y = rwkv_rec(r, lw, kmod, v, kk, b, batch, seq)
    o = rwkv_post(y, r, kmod, v, g, gn_g, gn_b, r_k)
    return matmul(o, bf(w_o), F32, residual=x)


def kernel(x, norm_mix, norm_ffn, norm_final, ffn_gate, ffn_up, ffn_down, a_w_in, a_w_o, b_w_in, b_f_bias,
           b_w_o, c_mix, c_w0, c_w1, c_w2, c_a0, c_a1, c_a2, c_g1, c_g2, c_k_k, c_k_a, c_r_k, c_w_rkv,
           c_w_o, c_gn_g, c_gn_b):
    batch, seq, d = x.shape
    depth = norm_mix.shape[0]
    h = x.reshape(batch * seq, d)
    for i in range(depth):
        kind, j = i % 3, i // 3
        if kind == 0:
            h = dsa_layer(h, norm_mix[i], a_w_in[j], a_w_o[j], batch, seq)
        elif kind == 1:
            h = fox_layer(h, norm_mix[i], b_w_in[j], b_f_bias[j], b_w_o[j], batch, seq)
        else:
            h = rwkv_layer(h, norm_mix[i], c_mix[j], c_w0[j], c_w1[j], c_w2[j], c_a0[j], c_a1[j], c_a2[j],
                           c_g1[j], c_g2[j], c_k_k[j], c_k_a[j], c_r_k[j], c_w_rkv[j], c_w_o[j],
                           c_gn_g[j], c_gn_b[j], batch, seq)
        h = swiglu_layer(h, norm_ffn[i], ffn_gate[i], ffn_up[i], ffn_down[i])
    return rmsnorm(h, norm_final, x.dtype).reshape(batch, seq, d)
```

```python
import functools

import jax
import jax.numpy as jnp
from jax import lax
from jax.experimental import pallas as pl
from jax.experimental.pallas import tpu as pltpu

F32 = jnp.float32
BF16 = jnp.bfloat16
I32 = jnp.int32
HIGHEST = lax.Precision.HIGHEST

D_MODEL = 2048
CHUNK = 64
RMS_EPS = 1e-6
ROPE_THETA = 10000.0
A_HEADS, A_KV_HEADS, A_HEAD_DIM = 16, 4, 128
IDX_HEADS, IDX_DIM, TOPK_MAX = 16, 64, 256
B_HEADS, B_HEAD_DIM = 16, 128
C_HEAD_DIM = 64
C_GN_EPS = C_HEAD_DIM * 1e-5

LANES = 128
VMEM_LIMIT_BYTES = 56 * 1024 * 1024

NEG_BIG = -1e30
LOG2E = 1.4426950408889634
_NT = (((1,), (1,)), ((), ()))
_TN = (((0,), (0,)), ((), ()))


def _params(*sem):
    return pltpu.CompilerParams(dimension_semantics=sem, vmem_limit_bytes=VMEM_LIMIT_BYTES)


def _iota(shape, dim):
    return lax.broadcasted_iota(I32, shape, dim)


def _rms(x, g):
    return x * lax.rsqrt(jnp.mean(x * x, axis=-1, keepdims=True) + RMS_EPS) * g


def _rmsnorm_kernel(x_ref, g_ref, o_ref):
    o_ref[...] = _rms(x_ref[...], g_ref[...]).astype(o_ref.dtype)


def rmsnorm(x, g, out_dtype):
    m, d = x.shape
    bm = min(m, 512)
    return pl.pallas_call(
        _rmsnorm_kernel,
        name="rmsnorm",
        grid=(m // bm,),
        in_specs=[pl.BlockSpec((bm, d), lambda i: (i, 0)), pl.BlockSpec((1, d), lambda i: (0, 0))],
        out_specs=pl.BlockSpec((bm, d), lambda i: (i, 0)),
        out_shape=jax.ShapeDtypeStruct((m, d), out_dtype),
        compiler_params=_params("parallel"),
    )(x, g.reshape(1, d))


def _largest_divisor(n, candidates):
    for c in candidates:
        if n % c == 0:
            return c
    return n


def _mm_kernel(x_ref, w_ref, o_ref):
    o_ref[...] = jnp.dot(x_ref[...], w_ref[...], preferred_element_type=F32).astype(o_ref.dtype)


def _mm_res_kernel(x_ref, w_ref, r_ref, o_ref):
    acc = jnp.dot(x_ref[...], w_ref[...], preferred_element_type=F32)
    o_ref[...] = (r_ref[...] + acc).astype(o_ref.dtype)


def matmul(x, w, out_dtype, residual=None):
    m, k = x.shape
    n = w.shape[1]
    bm = _largest_divisor(m, (1024, 512, 256, 128))
    bn = _largest_divisor(n, (1024, 512, 256, 128)) if k <= 2048 else _largest_divisor(n, (512, 256, 128))
    in_specs = [pl.BlockSpec((bm, k), lambda i, j: (i, 0)), pl.BlockSpec((k, bn), lambda i, j: (0, j))]
    args = [x, w]
    body = _mm_kernel
    if residual is not None:
        in_specs.append(pl.BlockSpec((bm, bn), lambda i, j: (i, j)))
        args.append(residual)
        body = _mm_res_kernel
    return pl.pallas_call(
        body,
        name=f"mm_{k}x{n}",
        grid=(m // bm, n // bn),
        in_specs=in_specs,
        out_specs=pl.BlockSpec((bm, bn), lambda i, j: (i, j)),
        out_shape=jax.ShapeDtypeStruct((m, n), out_dtype),
        compiler_params=_params("parallel", "parallel"),
    )(*args)


def _gateup_kernel(x_ref, wg_ref, wu_ref, o_ref):
    x = x_ref[...]
    g = jnp.dot(x, wg_ref[...], preferred_element_type=F32)
    u = jnp.dot(x, wu_ref[...], preferred_element_type=F32)
    o_ref[...] = (g * jax.nn.sigmoid(g) * u).astype(o_ref.dtype)


def gate_up(x, wg, wu):
    m, k = x.shape
    n = wg.shape[1]
    bm = _largest_divisor(m, (1024, 512, 256, 128))
    bn = _largest_divisor(n, (512, 256, 128))
    return pl.pallas_call(
        _gateup_kernel,
        name="gate_up",
        grid=(m // bm, n // bn),
        in_specs=[pl.BlockSpec((bm, k), lambda i, j: (i, 0)),
                  pl.BlockSpec((k, bn), lambda i, j: (0, j)),
                  pl.BlockSpec((k, bn), lambda i, j: (0, j))],
        out_specs=pl.BlockSpec((bm, bn), lambda i, j: (i, j)),
        out_shape=jax.ShapeDtypeStruct((m, n), BF16),
        compiler_params=_params("parallel", "parallel"),
    )(x, wg, wu)


def _lora_kernel(x_ref, w1_ref, w2_ref, o_ref, *, act):
    t = jnp.dot(x_ref[...], w1_ref[...], preferred_element_type=F32)
    if act == "tanh":
        t = jnp.tanh(t)
    elif act == "sigmoid":
        t = jax.nn.sigmoid(t)
    o_ref[...] = jnp.dot(t.astype(BF16), w2_ref[...], preferred_element_type=F32)


def lora(x, w1, w2, act):
    m, k = x.shape
    r = w1.shape[1]
    rp = -(-r // LANES) * LANES
    if act == "sigmoid":
        assert rp == r, "sigmoid(0) != 0: the rank must not be padded"
    w1 = jnp.pad(w1, ((0, 0), (0, rp - r)))
    w2 = jnp.pad(w2, ((0, rp - r), (0, 0)))
    n = w2.shape[1]
    bm = _largest_divisor(m, (1024, 512, 256, 128))
    return pl.pallas_call(
        functools.partial(_lora_kernel, act=act),
        name="lora_" + act,
        grid=(m // bm,),
        in_specs=[pl.BlockSpec((bm, k), lambda i: (i, 0)),
                  pl.BlockSpec((k, rp), lambda i: (0, 0)),
                  pl.BlockSpec((rp, n), lambda i: (0, 0))],
        out_specs=pl.BlockSpec((bm, n), lambda i: (i, 0)),
        out_shape=jax.ShapeDtypeStruct((m, n), F32),
        compiler_params=_params("parallel"),
    )(x, w1, w2)


def swiglu_layer(x, g_norm, w_gate, w_up, w_down):
    hn = rmsnorm(x, g_norm, BF16)
    h = gate_up(hn, w_gate.astype(BF16), w_up.astype(BF16))
    return matmul(h, w_down.astype(BF16), F32, residual=x)


def _log_sigmoid(x):
    return jnp.minimum(x, 0.0) - jnp.log1p(jnp.exp(-jnp.abs(x)))


def _fox_prep_kernel(fl_ref, bias_ref, c_ref, ct_ref, *, seq, ch):
    tri = (_iota((ch, ch), 0) >= _iota((ch, ch), 1)).astype(F32)
    carry = jnp.zeros((1, LANES), F32)
    for c in range(seq // ch):
        lf = _log_sigmoid(fl_ref[c * ch:(c + 1) * ch, :] + bias_ref[...])
        cs = jnp.dot(tri, lf, precision=HIGHEST, preferred_element_type=F32) + carry
        c2 = cs * LOG2E
        c_ref[c * ch:(c + 1) * ch, :] = c2
        ct_ref[:, c * ch:(c + 1) * ch] = c2.T
        carry = cs[ch - 1:ch, :]


def fox_prep(fl, bias_pad, batch, seq):
    ch = min(seq, 256)
    return pl.pallas_call(
        functools.partial(_fox_prep_kernel, seq=seq, ch=ch),
        name="fox_prep",
        grid=(batch,),
        in_specs=[pl.BlockSpec((seq, LANES), lambda b: (b, 0)), pl.BlockSpec((1, LANES), lambda b: (0, 0))],
        out_specs=[pl.BlockSpec((seq, LANES), lambda b: (b, 0)),
                   pl.BlockSpec((None, LANES, seq), lambda b: (b, 0, 0))],
        out_shape=[jax.ShapeDtypeStruct((batch * seq, LANES), F32),
                   jax.ShapeDtypeStruct((batch, LANES, seq), F32)],
        compiler_params=_params("parallel"),
    )(fl, bias_pad)


def _softmax_strip(z, r0, m_in_ref, m_out_ref, p_ref, row_term=None):
    rows, bk = z.shape
    tiles = [z[:, t * LANES:(t + 1) * LANES] for t in range(bk // LANES)]
    zmax = functools.reduce(jnp.maximum, tiles)
    zmax = jnp.broadcast_to(jnp.max(zmax, axis=1, keepdims=True), (rows, LANES))
    if row_term is not None:
        zmax = zmax + row_term
    m_new = jnp.maximum(m_in_ref[r0:r0 + rows, :], zmax)
    m_out_ref[r0:r0 + rows, :] = m_new
    shift = m_new if row_term is None else m_new - row_term
    for t, zt in enumerate(tiles):
        p_ref[r0:r0 + rows, t * LANES:(t + 1) * LANES] = jnp.exp2(zt - shift).astype(BF16)


def _accumulate(acc_ref, m_in_ref, m_out_ref, p_ref, v, row_lo=0):
    alpha = jnp.exp2(m_in_ref[row_lo:, :] - m_out_ref[row_lo:, :])
    v1 = jnp.concatenate([v, jnp.ones(v.shape, v.dtype)], axis=1)
    pv = jnp.dot(p_ref[row_lo:, :], v1, preferred_element_type=F32)
    for t in range(2):
        sl = slice(t * LANES, (t + 1) * LANES)
        acc_ref[row_lo:, sl] = alpha * acc_ref[row_lo:, sl] + pv[:, sl]


def _fox_attn_kernel(q_ref, k_ref, v_ref, c_ref, ck_ref, o_ref, s0_ref, s1_ref, p_ref, m0_ref, m1_ref, acc_ref,
                     *, bq, bk, strip):
    h = pl.program_id(1)
    qi = pl.program_id(2)
    q = q_ref[...]
    cq = jnp.sum(jnp.where(_iota((bq, LANES), 1) == h, c_ref[...], 0.0), axis=1, keepdims=True)
    cq = jnp.broadcast_to(cq, (bq, LANES))
    m0_ref[...] = jnp.full((bq, LANES), NEG_BIG, F32)
    acc_ref[...] = jnp.zeros((bq, 2 * LANES), F32)
    even, odd = (s0_ref, m0_ref, m1_ref), (s1_ref, m1_ref, m0_ref)

    def scores(t, bufs, row_lo=0):
        ks = pl.multiple_of(t * bk, bk)
        bufs[0][row_lo:, :] = lax.dot_general(q[row_lo:], k_ref[pl.ds(ks, bk), :], _NT,
                                              preferred_element_type=F32)

    def chunk(t, bufs, row_lo=0, diag_off=None):
        s_ref, m_in_ref, m_out_ref = bufs
        ks = pl.multiple_of(t * bk, bk)
        ck = ck_ref[:, pl.ds(ks, bk)]
        for r0 in range(row_lo, bq, strip):
            z = s_ref[r0:r0 + strip, :] - ck
            if diag_off is not None and diag_off + bk - 1 > r0:
                visible = diag_off + _iota((strip, bk), 1) <= r0 + _iota((strip, bk), 0)
                z = jnp.where(visible, z, NEG_BIG)
            _softmax_strip(z, r0, m_in_ref, m_out_ref, p_ref, row_term=cq[r0:r0 + strip])
        _accumulate(acc_ref, m_in_ref, m_out_ref, p_ref, v_ref[pl.ds(ks, bk), :], row_lo)

    n_diag = bq // bk
    assert n_diag % 2 == 0
    scores(0, even)

    def pair(jp, _):
        scores(2 * jp + 1, odd)
        chunk(2 * jp, even)
        scores(2 * jp + 2, even)
        chunk(2 * jp + 1, odd)
        return 0

    lax.fori_loop(0, qi * (n_diag // 2), pair, 0)
    bufs = (even, odd)
    for d in range(n_diag):
        if d + 1 < n_diag:
            scores(qi * n_diag + d + 1, bufs[(d + 1) % 2], row_lo=(d + 1) * bk)
        chunk(qi * n_diag + d, bufs[d % 2], row_lo=d * bk, diag_off=d * bk)
    o_ref[...] = (acc_ref[:, :LANES] / acc_ref[:, LANES:]).astype(o_ref.dtype)


def fox_attention(qkv, c, ck4, batch, seq):
    nh = B_HEADS
    bq = min(seq, 1024)
    bk = min(seq // 2, 256)
    nq = seq // bq
    return pl.pallas_call(
        functools.partial(_fox_attn_kernel, bq=bq, bk=bk, strip=min(bq, 128)),
        name="fox_attn",
        grid=(batch, nh, nq),
        in_specs=[pl.BlockSpec((bq, LANES), lambda b, h, i: (b * nq + i, h)),
                  pl.BlockSpec((seq, LANES), lambda b, h, i: (b, nh + h)),
                  pl.BlockSpec((seq, LANES), lambda b, h, i: (b, 2 * nh + h)),
                  pl.BlockSpec((bq, LANES), lambda b, h, i: (b * nq + i, 0)),
                  pl.BlockSpec((None, None, 1, seq), lambda b, h, i: (b, h, 0, 0))],
        out_specs=pl.BlockSpec((bq, LANES), lambda b, h, i: (b * nq + i, h)),
        out_shape=jax.ShapeDtypeStruct((batch * seq, nh * LANES), BF16),
        scratch_shapes=[pltpu.VMEM((bq, bk), F32), pltpu.VMEM((bq, bk), F32), pltpu.VMEM((bq, bk), BF16),
                        pltpu.VMEM((bq, LANES), F32), pltpu.VMEM((bq, LANES), F32),
                        pltpu.VMEM((bq, 2 * LANES), F32)],
        compiler_params=_params("parallel", "parallel", "parallel"),
    )(qkv, qkv, qkv, c, ck4)


def fox_layer(x, g_norm, w_in, f_bias, w_o, batch, seq):
    nh = B_HEADS
    hn = rmsnorm(x, g_norm, BF16)
    n_q, n_qkv = nh * B_HEAD_DIM, 3 * nh * B_HEAD_DIM
    col_scale = jnp.where(jnp.arange(n_qkv) < n_q, B_HEAD_DIM ** -0.5 * LOG2E, 1.0).astype(F32)
    qkv = matmul(hn, (w_in[:, :n_qkv] * col_scale).astype(BF16), BF16)
    w_f = jnp.pad(w_in[:, n_qkv:], ((0, 0), (0, LANES - nh))).astype(BF16)
    fl = matmul(hn, w_f, F32)
    bias_pad = jnp.pad(f_bias.astype(F32), (0, LANES - nh)).reshape(1, LANES)
    c, ct = fox_prep(fl, bias_pad, batch, seq)
    ck4 = ct[:, :nh, :].reshape(batch, nh, 1, seq)
    o = fox_attention(qkv, c, ck4, batch, seq)
    return matmul(o, w_o.astype(BF16), F32, residual=x)


def _rope_tables(seq, head_dim):
    half = head_dim // 2
    inv = ROPE_THETA ** (-jnp.arange(half, dtype=F32) / half)
    ang = jnp.arange(seq, dtype=F32)[:, None] * inv[None, :]
    cos = jnp.cos(ang)
    sin = jnp.sin(ang)
    reps = LANES // head_dim
    cos_t = jnp.tile(jnp.concatenate([cos, cos], axis=1), (1, reps))
    sin_t = jnp.tile(jnp.concatenate([-sin, sin], axis=1), (1, reps))
    return cos_t, sin_t


def _rope128(x, cos, sin):
    return x * cos + pltpu.roll(x, 64, 1) * sin


def _rope64(x, cos, sin, lane):
    partner = jnp.where((lane % 64) < 32, pltpu.roll(x, 96, 1), pltpu.roll(x, 32, 1))
    return x * cos + partner * sin


_MIN32 = -2 ** 31
_BITS_PER_CHECK = 4
_KEY_NEG_INF = (0xFF800000 - 2 ** 32) ^ 0x7FFFFFFF


def _order_key(s):
    b = pltpu.bitcast(s, I32)
    return jnp.where(b >= 0, b, b ^ 0x7FFFFFFF)


def _dsa_index_kernel(qf_ref, tq_ref, tk_ref, cq_ref, sq_ref, ck_ref, sk_ref, bias_ref, key_ref, qi_ref, ki_ref,
                      *, bq, seq, kc, topk, wi_scale):
    qb = pl.program_id(1)
    n_kc = ((qb + 1) * bq + kc - 1) // kc
    g8 = kc // 8
    rb = min(bq, 256)
    lane = _iota((bq, LANES), 1)

    @pl.when(qb == 0)
    def _():
        for r0 in range(0, seq, rb):
            rs = slice(r0, r0 + rb)
            lane_k = _iota((rb, LANES), 1)
            kr = _rope64(tk_ref[rs, :], ck_ref[rs, :], sk_ref[rs, :], lane_k)
            ki_ref[rs, :] = jnp.where(lane_k < 64, kr, pltpu.roll(kr, 64, 1)).astype(BF16)

    c64, s64 = cq_ref[...], sq_ref[...]
    for t in range(IDX_HEADS // 2):
        r = _rope64(qf_ref[:, t * LANES:(t + 1) * LANES], c64, s64, lane)
        qi_ref[:, (2 * t) * LANES:(2 * t + 1) * LANES] = jnp.where(lane < 64, r, 0.0).astype(BF16)
        qi_ref[:, (2 * t + 1) * LANES:(2 * t + 2) * LANES] = jnp.where(lane >= 64, r, 0.0).astype(BF16)
    wi = tq_ref[...] * wi_scale
    wcol = [jnp.sum(jnp.where(lane == 64 + h, wi, 0.0), axis=1, keepdims=True) for h in range(IDX_HEADS)]
    k_off = _iota((rb, kc), 1)

    def kc_body(j, _):
        ks = pl.multiple_of(j * kc, kc)
        kblk = ki_ref[pl.ds(ks, kc), :]
        for r0 in range(0, bq, rb):
            acc = jnp.zeros((rb, kc), F32)
            for h in range(IDX_HEADS):
                s = lax.dot_general(qi_ref[r0:r0 + rb, h * LANES:(h + 1) * LANES], kblk, _NT,
                                    preferred_element_type=F32)
                acc = acc + wcol[h][r0:r0 + rb] * jnp.maximum(s, 0.0)
            q_chunk = (qb * bq + r0 + _iota((rb, kc), 0)) // CHUNK
            acc = jnp.where((ks + k_off) // CHUNK <= q_chunk, acc, -jnp.inf)
            key_ref[pl.ds(ks, kc), r0:r0 + rb] = _order_key(acc.T)
        return 0

    lax.fori_loop(0, n_kc, kc_body, 0)

    def count(indicator):
        def body(j, acc):
            x = key_ref[pl.ds(pl.multiple_of(j * kc, kc), kc), :].reshape(g8, 8, bq)
            return acc + jnp.sum(indicator(x, j), axis=0)
        acc = lax.fori_loop(0, n_kc, body, jnp.zeros((8, bq), F32))
        return jnp.sum(acc, axis=0, keepdims=True)

    n_adm = ((qb * bq + _iota((1, bq), 1)) // CHUNK + 1) * CHUNK
    take_all = n_adm <= topk
    settled = lambda cnt_t: jnp.logical_or(take_all, cnt_t == topk)
    pending = lambda cnt_t: jnp.max(jnp.where(settled(cnt_t), 0, 1))

    def bit_cond(c):
        return jnp.logical_and(c[0] < 32, c[3] > 0)

    def bit_body(c):
        i, tu, cnt_t, _ = c
        for b in range(_BITS_PER_CHECK):
            cand_u = tu | lax.shift_left(jnp.int32(1), 31 - (i + b))
            cand_s = jnp.broadcast_to(cand_u ^ _MIN32, (8, bq))
            cnt = count(lambda x, j: jnp.where(x >= cand_s, 1.0, 0.0))
            ok = cnt >= topk
            cnt_t = jnp.where(ok, cnt, cnt_t)
            tu = jnp.where(ok, cand_u, tu)
        return i + _BITS_PER_CHECK, tu, cnt_t, pending(cnt_t)

    cnt0 = jnp.zeros((1, bq), F32) + (n_kc * kc).astype(F32)
    _, tu, cnt_ge, n_pending = lax.while_loop(
        bit_cond, bit_body, (jnp.int32(0), jnp.zeros((1, bq), I32), cnt0, pending(cnt0)))
    ts = tu ^ _MIN32
    thr = jnp.where(take_all, _KEY_NEG_INF + 1, ts)

    def no_ties():
        return thr - 1, jnp.zeros((1, bq), I32), jnp.full((1, bq), -1, I32)

    def with_ties():
        tied = jnp.logical_not(settled(cnt_ge))
        ts8 = jnp.broadcast_to(ts, (8, bq))
        cnt_gt = count(lambda x, j: jnp.where(x > ts8, 1.0, 0.0))
        need = topk - cnt_gt
        k_idx = _iota((g8, 8, bq), 0) * 8 + _iota((g8, 8, bq), 1)

        def j_body(i, j0):
            cand = j0 | lax.shift_left(jnp.int32(1), (seq.bit_length() - 2) - i)
            cand8 = jnp.broadcast_to(cand, (8, bq))
            cnt = count(lambda x, j: jnp.where(x == ts8, jnp.where(j * kc + k_idx < cand8, 1.0, 0.0), 0.0))
            return jnp.where(cnt < need, cand, j0)
        j0 = lax.fori_loop(0, seq.bit_length() - 1, j_body, jnp.zeros((1, bq), I32))
        return jnp.where(tied, ts, thr - 1), jnp.where(tied, ts, 0), jnp.where(tied, j0, -1)

    gt_thr, eq_val, jlim = lax.cond(n_pending > 0, with_ties, no_ties)

    bias_ref[...] = jnp.full((bq, seq), NEG_BIG, BF16)
    k_row = _iota((kc, rb), 0)

    def out_body(j, _):
        ks = pl.multiple_of(j * kc, kc)
        for r0 in range(0, bq, rb):
            qs = slice(r0, r0 + rb)
            x = key_ref[pl.ds(ks, kc), qs]
            tie_ok = jnp.where(ks + k_row <= jlim[:, qs], 0.0, NEG_BIG)
            b = jnp.where(x > gt_thr[:, qs], 0.0, jnp.where(x == eq_val[:, qs], tie_ok, NEG_BIG))
            bias_ref[qs, pl.ds(ks, kc)] = b.T.astype(BF16)
        return 0

    lax.fori_loop(0, n_kc, out_body, 0)


def dsa_index(main, tail, batch, seq):
    bq = min(seq, 512)
    nq = seq // bq
    kc = min(seq, 256)
    topk = min(TOPK_MAX, seq // 4)
    n_qi = IDX_HEADS * IDX_DIM
    qi_block = ((A_HEADS + 2 * A_KV_HEADS) * A_HEAD_DIM) // n_qi
    c64, s64 = _rope_tables(seq, IDX_DIM)
    tab_q = pl.BlockSpec((bq, LANES), lambda b, i: (i, 0))
    tab_k = pl.BlockSpec((seq, LANES), lambda b, i: (0, 0))
    return pl.pallas_call(
        functools.partial(_dsa_index_kernel, bq=bq, seq=seq, kc=kc, topk=topk,
                          wi_scale=IDX_HEADS ** -0.5 * IDX_DIM ** -0.5),
        name="dsa_index",
        grid=(batch, nq),
        in_specs=[pl.BlockSpec((bq, n_qi), lambda b, i: (b * nq + i, qi_block)),
                  pl.BlockSpec((bq, LANES), lambda b, i: (b * nq + i, 0)),
                  pl.BlockSpec((seq, LANES), lambda b, i: (b, 0)),
                  tab_q, tab_q, tab_k, tab_k],
        out_specs=pl.BlockSpec((bq, seq), lambda b, i: (b * nq + i, 0)),
        out_shape=jax.ShapeDtypeStruct((batch * seq, seq), BF16),
        scratch_shapes=[pltpu.VMEM((seq, bq), I32), pltpu.VMEM((bq, IDX_HEADS * LANES), BF16),
                        pltpu.VMEM((seq, LANES), BF16)],
        compiler_params=_params("parallel", "arbitrary"),
    )(main, tail, tail, c64, s64, c64, s64)


def _dsa_attn_kernel(qf_ref, kf_ref, vf_ref, cq_ref, sq_ref, ck_ref, sk_ref, bias_ref, o_ref,
                     q_ref, k_ref, v_ref, s0_ref, s1_ref, p_ref, m0_ref, m1_ref, acc_ref,
                     *, bq, bk, seq, group, strip, q_scale):
    qi = pl.program_id(2)
    rows = group * bq

    @pl.when(qi == 0)
    def _():
        for r0 in range(0, seq, bq):
            rs = slice(r0, r0 + bq)
            k_ref[rs, :] = _rope128(kf_ref[rs, :], ck_ref[rs, :], sk_ref[rs, :]).astype(BF16)
            v_ref[rs, :] = vf_ref[rs, :].astype(BF16)

    cq, sq = cq_ref[...] * q_scale, sq_ref[...] * q_scale
    for r in range(group):
        q_ref[r * bq:(r + 1) * bq, :] = _rope128(qf_ref[:, r * LANES:(r + 1) * LANES], cq, sq).astype(BF16)
    q = q_ref[...]
    m0_ref[...] = jnp.full((rows, LANES), NEG_BIG, F32)
    acc_ref[...] = jnp.zeros((rows, 2 * LANES), F32)
    even, odd = (s0_ref, m0_ref, m1_ref), (s1_ref, m1_ref, m0_ref)

    n_kv = ((qi + 1) * bq + bk - 1) // bk

    def scores(j, bufs):
        ks = pl.multiple_of(jnp.minimum(j, n_kv - 1) * bk, bk)
        bufs[0][...] = lax.dot_general(q, k_ref[pl.ds(ks, bk), :], _NT, preferred_element_type=F32)

    def chunk(j, bufs):
        s_ref, m_in_ref, m_out_ref = bufs
        ks = pl.multiple_of(j * bk, bk)
        for b0 in range(0, bq, strip):
            bias = bias_ref[b0:b0 + strip, pl.ds(ks, bk)].astype(F32)
            for r in range(group):
                r0 = r * bq + b0
                _softmax_strip(s_ref[r0:r0 + strip, :] + bias, r0, m_in_ref, m_out_ref, p_ref)
        _accumulate(acc_ref, m_in_ref, m_out_ref, p_ref, v_ref[pl.ds(ks, bk), :])

    scores(0, even)

    def pair(jp, _):
        scores(2 * jp + 1, odd)
        chunk(2 * jp, even)
        scores(2 * jp + 2, even)
        chunk(2 * jp + 1, odd)
        return 0

    lax.fori_loop(0, n_kv // 2, pair, 0)

    @pl.when(n_kv % 2 == 1)
    def _():
        chunk(n_kv - 1, even)

    for r in range(group):
        sl = slice(r * bq, (r + 1) * bq)
        o_ref[:, r * LANES:(r + 1) * LANES] = (acc_ref[sl, :LANES] / acc_ref[sl, LANES:]).astype(o_ref.dtype)


def dsa_attention(main, bias, batch, seq):
    group = A_HEADS // A_KV_HEADS
    bq = bk = min(seq, 256)
    nq = seq // bq
    gw = group * LANES
    c128, s128 = _rope_tables(seq, A_HEAD_DIM)
    tab_q = pl.BlockSpec((bq, LANES), lambda b, g, i: (i, 0))
    tab_k = pl.BlockSpec((seq, LANES), lambda b, g, i: (0, 0))
    return pl.pallas_call(
        functools.partial(_dsa_attn_kernel, bq=bq, bk=bk, seq=seq, group=group, strip=min(bq, 64),
                          q_scale=A_HEAD_DIM ** -0.5 * LOG2E),
        name="dsa_attn",
        grid=(batch, A_KV_HEADS, nq),
        in_specs=[pl.BlockSpec((bq, gw), lambda b, g, i: (b * nq + i, g)),
                  pl.BlockSpec((seq, LANES), lambda b, g, i: (b, A_HEADS + g)),
                  pl.BlockSpec((seq, LANES), lambda b, g, i: (b, A_HEADS + A_KV_HEADS + g)),
                  tab_q, tab_q, tab_k, tab_k,
                  pl.BlockSpec((bq, seq), lambda b, g, i: (b * nq + i, 0))],
        out_specs=pl.BlockSpec((bq, gw), lambda b, g, i: (b * nq + i, g)),
        out_shape=jax.ShapeDtypeStruct((batch * seq, A_HEADS * LANES), BF16),
        scratch_shapes=[pltpu.VMEM((group * bq, LANES), BF16), pltpu.VMEM((seq, LANES), BF16),
                        pltpu.VMEM((seq, LANES), BF16),
                        pltpu.VMEM((group * bq, bk), F32), pltpu.VMEM((group * bq, bk), F32),
                        pltpu.VMEM((group * bq, bk), BF16),
                        pltpu.VMEM((group * bq, LANES), F32), pltpu.VMEM((group * bq, LANES), F32),
                        pltpu.VMEM((group * bq, 2 * LANES), F32)],
        compiler_params=_params("parallel", "parallel", "arbitrary"),
    )(main, main, main, c128, s128, c128, s128, bias)


def dsa_layer(x, g_norm, w_in, w_o, batch, seq):
    hn = rmsnorm(x, g_norm, BF16)
    n_main = (A_HEADS + 2 * A_KV_HEADS) * A_HEAD_DIM + IDX_HEADS * IDX_DIM
    n_tail = w_in.shape[1] - n_main
    main = matmul(hn, w_in[:, :n_main].astype(BF16), F32)
    w_tail = jnp.pad(w_in[:, n_main:], ((0, 0), (0, LANES - n_tail))).astype(BF16)
    tail = matmul(hn, w_tail, F32)
    bias = dsa_index(main, tail, batch, seq)
    o = dsa_attention(main, bias, batch, seq)
    return matmul(o, w_o.astype(BF16), F32, residual=x)


def _rwkv_mix_kernel(x_ref, g_ref, mix_ref, *rest, bs):
    outs, hbuf = rest[:6], rest[6]
    j = pl.program_id(1)

    @pl.when(j == 0)
    def _():
        hbuf[0:8, :] = jnp.zeros((8, D_MODEL), F32)

    hn = _rms(x_ref[...], g_ref[...])
    hbuf[8:8 + bs, :] = hn
    xx = hbuf[7:7 + bs, :] - hn
    hbuf[7:8, :] = hn[bs - 1:bs, :]
    for i in range(6):
        outs[i][...] = (hn + xx * mix_ref[i:i + 1, :]).astype(BF16)


def rwkv_mix(x, g_norm, mix, batch, seq):
    bs = min(seq, 256)
    nb = seq // bs
    m, d = x.shape
    mix8 = jnp.pad(mix, ((0, 2), (0, 0)))
    return pl.pallas_call(
        functools.partial(_rwkv_mix_kernel, bs=bs),
        name="rwkv_mix",
        grid=(batch, nb),
        in_specs=[pl.BlockSpec((bs, d), lambda b, j: (b * nb + j, 0)),
                  pl.BlockSpec((1, d), lambda b, j: (0, 0)),
                  pl.BlockSpec((8, d), lambda b, j: (0, 0))],
        out_specs=[pl.BlockSpec((bs, d), lambda b, j: (b * nb + j, 0))] * 6,
        out_shape=[jax.ShapeDtypeStruct((m, d), BF16)] * 6,
        scratch_shapes=[pltpu.VMEM((bs + 8, d), F32)],
        compiler_params=_params("parallel", "arbitrary"),
    )(x, g_norm.reshape(1, d), mix8)


def _head_sum_matrix():
    return ((_iota((LANES, LANES), 0) // C_HEAD_DIM) == (_iota((LANES, LANES), 1) // C_HEAD_DIM)).astype(F32)


def _head_sum_2x(x, p):
    hi = x.astype(BF16)
    mid = (x - hi.astype(F32)).astype(BF16)
    pb = p.astype(BF16)
    return jnp.dot(hi, pb, preferred_element_type=F32) + jnp.dot(mid, pb, preferred_element_type=F32)


def _rwkv_rec_kernel(r_ref, k_ref, v_ref, wl_ref, al_ref, w0_ref, a0_ref, kkp_ref, ka_ref, y_ref, km_ref, st_ref,
                     *, tc, npair, mm_dtype):
    c = pl.program_id(2)

    @pl.when(c == 0)
    def _():
        st_ref[...] = jnp.zeros(st_ref.shape, F32)

    def mm(a, b, dims=None):
        a = a.astype(mm_dtype)
        b = b.astype(mm_dtype)
        prec = HIGHEST if mm_dtype == F32 else None
        if dims is None:
            return jnp.dot(a, b, precision=prec, preferred_element_type=F32)
        return lax.dot_general(a, b, dims, precision=prec, preferred_element_type=F32)

    row = _iota((tc, tc), 0)
    col = _iota((tc, tc), 1)
    tri_incl = (row >= col).astype(F32)
    strict = row > col
    incl = row >= col
    eye = (row == col).astype(F32)
    lane = _iota((tc, LANES), 1)
    head_masks = [lane < C_HEAD_DIM, lane >= C_HEAD_DIM]
    blockdiag = (_iota((LANES, LANES), 0) // C_HEAD_DIM) == (_iota((LANES, LANES), 1) // C_HEAD_DIM)
    n_sq = max((tc - 1).bit_length() - 1, 0)

    pairs = range(npair)
    chains = [(p, h) for p in pairs for h in range(2)]
    tile = lambda a, p: a[:, p * LANES:(p + 1) * LANES]
    cast = lambda a: a.astype(mm_dtype)

    r, k_raw, v = r_ref[...], k_ref[...], v_ref[...]
    lw = -jnp.exp(_log_sigmoid(w0_ref[...] + wl_ref[...]) - 0.5)
    a = jax.nn.sigmoid(a0_ref[...] + al_ref[...])
    kr = k_raw * kkp_ref[...]
    hsum = _head_sum_matrix()
    ss = jnp.concatenate([_head_sum_2x(tile(kr, p) * tile(kr, p), hsum) for p in pairs], axis=1)
    kk = kr * lax.rsqrt(jnp.maximum(ss, 1e-24))
    k = k_raw * (1.0 + (a - 1.0) * ka_ref[...])
    b = kk * a
    km_ref[...] = k
    cum = jnp.dot(tri_incl, lw, precision=HIGHEST, preferred_element_type=F32)
    g_in = jnp.exp(cum)
    g_inv = jnp.exp(-cum)
    a_t = -kk * jnp.exp(cum - lw)
    b_t = b * g_inv
    k_t = k * g_inv
    r_t = r * g_in
    g_end = g_in[tc - 1:tc, :]
    b_c, k_c, v_c, r_c = cast(b_t), cast(k_t), cast(v), cast(r_t)
    bk_x = cast(jnp.concatenate([b_t * g_end, k_t * g_end], axis=0).T)
    g_col = g_in.T[:, tc - 1:tc]
    rows = lambda a, p: a[p * LANES:(p + 1) * LANES, :]

    a_m = {(p, h): cast(jnp.where(head_masks[h], tile(a_t, p), 0.0)) for p, h in chains}
    left = {(p, h): jnp.concatenate([a_m[p, h], cast(jnp.where(head_masks[h], tile(r_t, p), 0.0))], axis=0)
            for p, h in chains}
    gb = {c_: mm(left[c_], tile(b_c, c_[0]), _NT) for c_ in chains}
    gk = {c_: mm(left[c_], tile(k_c, c_[0]), _NT) for c_ in chains}
    a_ab = {c_: jnp.where(strict, gb[c_][:tc], 0.0) for c_ in chains}
    a_rb = {c_: cast(jnp.where(incl, gb[c_][tc:], 0.0)) for c_ in chains}
    low2 = jnp.concatenate([strict, incl], axis=0)
    kv = {c_: mm(cast(jnp.where(low2, gk[c_], 0.0)), tile(v_c, c_[0])) for c_ in chains}
    akv = {c_: cast(kv[c_][:tc]) for c_ in chains}
    y0 = {c_: kv[c_][tc:] for c_ in chains}
    x = {c_: eye + a_ab[c_] for c_ in chains}
    pw = {c_: cast(a_ab[c_]) for c_ in chains}
    for _ in range(n_sq):
        pw = {c_: cast(mm(pw[c_], pw[c_])) for c_ in chains}
        x = {c_: x[c_] + mm(pw[c_], cast(x[c_])) for c_ in chains}
    x = {c_: cast(x[c_]) for c_ in chains}
    xw = {c_: mm(x[c_], jnp.concatenate([a_m[c_], akv[c_]], axis=1)) for c_ in chains}
    w_h = {c_: xw[c_][:, :LANES] for c_ in chains}
    u_h = {c_: xw[c_][:, LANES:] for c_ in chains}

    s0 = {p: st_ref[p] for p in pairs}
    ws = {p: mm(jnp.concatenate([cast(w_h[p, 0] + w_h[p, 1]), tile(r_c, p)], axis=0), cast(s0[p]))
          for p in pairs}
    u = {p: ws[p][:tc] + jnp.where(head_masks[0], u_h[p, 0], u_h[p, 1]) for p in pairs}
    u_c = {p: cast(u[p]) for p in pairs}
    yb = {c_: mm(a_rb[c_], u_c[c_[0]]) for c_ in chains}
    upd = {p: mm(rows(bk_x, p), jnp.concatenate([u_c[p], tile(v_c, p)], axis=0)) for p in pairs}
    for p in pairs:
        y = ws[p][tc:] + jnp.where(head_masks[0], y0[p, 0] + yb[p, 0], y0[p, 1] + yb[p, 1])
        y_ref[:, p * LANES:(p + 1) * LANES] = y
        st_ref[p] = s0[p] * rows(g_col, p) + jnp.where(blockdiag, upd[p], 0.0)


def rwkv_rec(r, k, v, wl, al, w0, a0, k_k, k_a, batch, seq, mm_dtype=BF16):
    m, d = r.shape
    tc = min(seq, 64)
    nc = seq // tc
    npair = 16
    gw = npair * LANES
    blk = pl.BlockSpec((tc, gw), lambda bi, g, c: (bi * nc + c, g))
    row = pl.BlockSpec((1, gw), lambda bi, g, c: (0, g))
    r2 = lambda a: a.reshape(1, d).astype(F32)
    return pl.pallas_call(
        functools.partial(_rwkv_rec_kernel, tc=tc, npair=npair, mm_dtype=mm_dtype),
        name="rwkv_rec",
        grid=(batch, d // gw, nc),
        in_specs=[blk] * 5 + [row] * 4,
        out_specs=[blk, blk],
        out_shape=[jax.ShapeDtypeStruct((m, d), F32)] * 2,
        scratch_shapes=[pltpu.VMEM((npair, LANES, LANES), F32)],
        compiler_params=_params("parallel", "parallel", "arbitrary"),
    )(r, k, v, wl, al, r2(w0), r2(a0), r2(k_k), r2(k_a))


def _rwkv_post_kernel(y_ref, r_ref, k_ref, v_ref, g_ref, gng_ref, gnb_ref, rk_ref, o_ref):
    p = _head_sum_matrix()
    inv_n = 1.0 / C_HEAD_DIM
    for t in range(D_MODEL // LANES):
        sl = slice(t * LANES, (t + 1) * LANES)
        y = y_ref[:, sl]
        mu = _head_sum_2x(y, p) * inv_n
        yc = y - mu
        var = _head_sum_2x(yc * yc, p) * inv_n
        yn = yc * lax.rsqrt(var + C_GN_EPS) * gng_ref[:, sl] + gnb_ref[:, sl]
        bonus = _head_sum_2x(r_ref[:, sl] * k_ref[:, sl] * rk_ref[:, sl], p) * v_ref[:, sl]
        o_ref[:, sl] = ((yn + bonus) * g_ref[:, sl]).astype(o_ref.dtype)


def rwkv_post(y, r, kmod, v, g, gn_g, gn_b, r_k):
    m, d = y.shape
    bm = _largest_divisor(m, (256, 128, 64))
    big = pl.BlockSpec((bm, d), lambda i: (i, 0))
    row = pl.BlockSpec((1, d), lambda i: (0, 0))
    r2 = lambda a: a.reshape(1, d).astype(F32)
    return pl.pallas_call(
        _rwkv_post_kernel,
        name="rwkv_post",
        grid=(m // bm,),
        in_specs=[big] * 5 + [row] * 3,
        out_specs=big,
        out_shape=jax.ShapeDtypeStruct((m, d), BF16),
        compiler_params=_params("parallel"),
    )(y, r, kmod, v, g, r2(gn_g), r2(gn_b), r2(r_k))


def rwkv_layer(x, g_norm, mix, w0, w1, w2, a0, a1, a2, g1, g2, k_k, k_a, r_k, w_rkv, w_o, gn_g, gn_b,
               batch, seq):
    bf = lambda a: a.astype(BF16)
    xr, xk, xv, xw, xa, xg = rwkv_mix(x, g_norm, mix, batch, seq)
    r = matmul(xr, bf(w_rkv[0]), F32)
    k = matmul(xk, bf(w_rkv[1]), F32)
    v = matmul(xv, bf(w_rkv[2]), F32)
    wl = lora(xw, bf(w1), bf(w2), "tanh")
    al = lora(xa, bf(a1), bf(a2), "none")
    g = lora(xg, bf(g1), bf(g2), "sigmoid")
    y, kmod = rwkv_rec(r, k, v, wl, al, w0, a0, k_k, k_a, batch, seq)
    o = rwkv_post(y, r, kmod, v, g, gn_g, gn_b, r_k)
    return matmul(o, bf(w_o), F32, residual=x)


def kernel(x, norm_mix, norm_ffn, norm_final, ffn_gate, ffn_up, ffn_down, a_w_in, a_w_o, b_w_in, b_f_bias,
           b_w_o, c_mix, c_w0, c_w1, c_w2, c_a0, c_a1, c_a2, c_g1, c_g2, c_k_k, c_k_a, c_r_k, c_w_rkv,
           c_w_o, c_gn_g, c_gn_b):
    batch, seq, d = x.shape
    depth = norm_mix.shape[0]
    h = x.reshape(batch * seq, d)
    for i in range(depth):
        kind, j = i % 3, i // 3
        if kind == 0:
            h = dsa_layer(h, norm_mix[i], a_w_in[j], a_w_o[j], batch, seq)
        elif kind == 1:
            h = fox_layer(h, norm_mix[i], b_w_in[j], b_f_bias[j], b_w_o[j], batch, seq)
        else:
            h = rwkv_layer(h, norm_mix[i], c_mix[j], c_w0[j], c_w1[j], c_w2[j], c_a0[j], c_a1[j], c_a2[j],
                           c_g1[j], c_g2[j], c_k_k[j], c_k_a[j], c_r_k[j], c_w_rkv[j], c_w_o[j],
                           c_gn_g[j], c_gn_b[j], batch, seq)
        h = swiglu_layer(h, norm_ffn[i], ffn_gate[i], ffn_up[i], ffn_down[i])
    return rmsnorm(h, norm_final, x.dtype).reshape(batch, seq, d)
```

```python
import functools

import jax
import jax.numpy as jnp
from jax import lax
from jax.experimental import pallas as pl
from jax.experimental.pallas import tpu as pltpu

F32 = jnp.float32
BF16 = jnp.bfloat16
I32 = jnp.int32
HIGHEST = lax.Precision.HIGHEST

D_MODEL = 2048
CHUNK = 64
RMS_EPS = 1e-6
ROPE_THETA = 10000.0
A_HEADS, A_KV_HEADS, A_HEAD_DIM = 16, 4, 128
IDX_HEADS, IDX_DIM, TOPK_MAX = 16, 64, 256
B_HEADS, B_HEAD_DIM = 16, 128
C_HEAD_DIM = 64
C_GN_EPS = C_HEAD_DIM * 1e-5

LANES = 128
VMEM_LIMIT_BYTES = 56 * 1024 * 1024

NEG_BIG = -1e30
LOG2E = 1.4426950408889634
_NT = (((1,), (1,)), ((), ()))
_TN = (((0,), (0,)), ((), ()))


def _params(*sem):
    return pltpu.CompilerParams(dimension_semantics=sem, vmem_limit_bytes=VMEM_LIMIT_BYTES)


def _iota(shape, dim):
    return lax.broadcasted_iota(I32, shape, dim)


def _rms(x, g):
    return x * lax.rsqrt(jnp.mean(x * x, axis=-1, keepdims=True) + RMS_EPS) * g


def _rmsnorm_kernel(x_ref, g_ref, o_ref):
    o_ref[...] = _rms(x_ref[...], g_ref[...]).astype(o_ref.dtype)


def rmsnorm(x, g, out_dtype):
    m, d = x.shape
    bm = min(m, 512)
    return pl.pallas_call(
        _rmsnorm_kernel,
        name="rmsnorm",
        grid=(m // bm,),
        in_specs=[pl.BlockSpec((bm, d), lambda i: (i, 0)), pl.BlockSpec((1, d), lambda i: (0, 0))],
        out_specs=pl.BlockSpec((bm, d), lambda i: (i, 0)),
        out_shape=jax.ShapeDtypeStruct((m, d), out_dtype),
        compiler_params=_params("parallel"),
    )(x, g.reshape(1, d))


def _largest_divisor(n, candidates):
    for c in candidates:
        if n % c == 0:
            return c
    return n


def _mm_kernel(x_ref, w_ref, o_ref):
    o_ref[...] = jnp.dot(x_ref[...], w_ref[...], preferred_element_type=F32).astype(o_ref.dtype)


def _mm_res_kernel(x_ref, w_ref, r_ref, o_ref):
    acc = jnp.dot(x_ref[...], w_ref[...], preferred_element_type=F32)
    o_ref[...] = (r_ref[...] + acc).astype(o_ref.dtype)


def matmul(x, w, out_dtype, residual=None):
    m, k = x.shape
    n = w.shape[1]
    bm = _largest_divisor(m, (1024, 512, 256, 128))
    bn = _largest_divisor(n, (1024, 512, 256, 128)) if k <= 2048 else _largest_divisor(n, (512, 256, 128))
    in_specs = [pl.BlockSpec((bm, k), lambda i, j: (i, 0)), pl.BlockSpec((k, bn), lambda i, j: (0, j))]
    args = [x, w]
    body = _mm_kernel
    if residual is not None:
        in_specs.append(pl.BlockSpec((bm, bn), lambda i, j: (i, j)))
        args.append(residual)
        body = _mm_res_kernel
    return pl.pallas_call(
        body,
        name=f"mm_{k}x{n}",
        grid=(m // bm, n // bn),
        in_specs=in_specs,
        out_specs=pl.BlockSpec((bm, bn), lambda i, j: (i, j)),
        out_shape=jax.ShapeDtypeStruct((m, n), out_dtype),
        compiler_params=_params("parallel", "parallel"),
    )(*args)


def _gateup_kernel(x_ref, wg_ref, wu_ref, o_ref, wgb_ref, wub_ref):
    @pl.when(pl.program_id(1) == 0)
    def _():
        wgb_ref[...] = wg_ref[...].astype(BF16)
        wub_ref[...] = wu_ref[...].astype(BF16)

    x = x_ref[...]
    g = jnp.dot(x, wgb_ref[...], preferred_element_type=F32)
    u = jnp.dot(x, wub_ref[...], preferred_element_type=F32)
    o_ref[...] = (g * jax.nn.sigmoid(g) * u).astype(o_ref.dtype)


def gate_up(x, wg, wu, layer):
    m, k = x.shape
    n = wg.shape[2]
    bm = _largest_divisor(m, (1024, 512, 256, 128))
    bn = _largest_divisor(n, (512, 256, 128))
    return pl.pallas_call(
        _gateup_kernel,
        name="gate_up",
        grid=(n // bn, m // bm),
        in_specs=[pl.BlockSpec((bm, k), lambda j, i: (i, 0)),
                  pl.BlockSpec((None, k, bn), lambda j, i: (layer, 0, j)),
                  pl.BlockSpec((None, k, bn), lambda j, i: (layer, 0, j))],
        out_specs=pl.BlockSpec((bm, bn), lambda j, i: (i, j)),
        out_shape=jax.ShapeDtypeStruct((m, n), BF16),
        scratch_shapes=[pltpu.VMEM((k, bn), BF16), pltpu.VMEM((k, bn), BF16)],
        compiler_params=_params("parallel", "arbitrary"),
    )(x, wg, wu)


def _lora_kernel(x_ref, w1_ref, w2_ref, o_ref, *, act):
    t = jnp.dot(x_ref[...], w1_ref[...], preferred_element_type=F32)
    if act == "tanh":
        t = jnp.tanh(t)
    elif act == "sigmoid":
        t = jax.nn.sigmoid(t)
    o_ref[...] = jnp.dot(t.astype(BF16), w2_ref[...], preferred_element_type=F32)


def lora(x, w1, w2, act):
    m, k = x.shape
    r = w1.shape[1]
    rp = -(-r // LANES) * LANES
    if act == "sigmoid":
        assert rp == r, "sigmoid(0) != 0: the rank must not be padded"
    w1 = jnp.pad(w1, ((0, 0), (0, rp - r)))
    w2 = jnp.pad(w2, ((0, rp - r), (0, 0)))
    n = w2.shape[1]
    bm = _largest_divisor(m, (1024, 512, 256, 128))
    return pl.pallas_call(
        functools.partial(_lora_kernel, act=act),
        name="lora_" + act,
        grid=(m // bm,),
        in_specs=[pl.BlockSpec((bm, k), lambda i: (i, 0)),
                  pl.BlockSpec((k, rp), lambda i: (0, 0)),
                  pl.BlockSpec((rp, n), lambda i: (0, 0))],
        out_specs=pl.BlockSpec((bm, n), lambda i: (i, 0)),
        out_shape=jax.ShapeDtypeStruct((m, n), F32),
        compiler_params=_params("parallel"),
    )(x, w1, w2)


def swiglu_layer(x, g_norm, w_gate_all, w_up_all, layer, w_down):
    hn = rmsnorm(x, g_norm, BF16)
    h = gate_up(hn, w_gate_all, w_up_all, layer)
    return matmul(h, w_down.astype(BF16), F32, residual=x)


def _log_sigmoid(x):
    return jnp.minimum(x, 0.0) - jnp.log1p(jnp.exp(-jnp.abs(x)))


def _fox_prep_kernel(fl_ref, bias_ref, c_ref, ct_ref, *, seq, ch):
    tri = (_iota((ch, ch), 0) >= _iota((ch, ch), 1)).astype(F32)
    carry = jnp.zeros((1, LANES), F32)
    for c in range(seq // ch):
        lf = _log_sigmoid(fl_ref[c * ch:(c + 1) * ch, :] + bias_ref[...])
        cs = jnp.dot(tri, lf, precision=HIGHEST, preferred_element_type=F32) + carry
        c2 = cs * LOG2E
        c_ref[c * ch:(c + 1) * ch, :] = c2
        ct_ref[:, c * ch:(c + 1) * ch] = c2.T
        carry = cs[ch - 1:ch, :]


def fox_prep(fl, bias_pad, batch, seq):
    ch = min(seq, 256)
    return pl.pallas_call(
        functools.partial(_fox_prep_kernel, seq=seq, ch=ch),
        name="fox_prep",
        grid=(batch,),
        in_specs=[pl.BlockSpec((seq, LANES), lambda b: (b, 0)), pl.BlockSpec((1, LANES), lambda b: (0, 0))],
        out_specs=[pl.BlockSpec((seq, LANES), lambda b: (b, 0)),
                   pl.BlockSpec((None, LANES, seq), lambda b: (b, 0, 0))],
        out_shape=[jax.ShapeDtypeStruct((batch * seq, LANES), F32),
                   jax.ShapeDtypeStruct((batch, LANES, seq), F32)],
        compiler_params=_params("parallel"),
    )(fl, bias_pad)


def _softmax_strip(z, r0, m_in_ref, m_out_ref, p_ref, row_term=None):
    rows, bk = z.shape
    tiles = [z[:, t * LANES:(t + 1) * LANES] for t in range(bk // LANES)]
    zmax = functools.reduce(jnp.maximum, tiles)
    zmax = jnp.broadcast_to(jnp.max(zmax, axis=1, keepdims=True), (rows, LANES))
    if row_term is not None:
        zmax = zmax + row_term
    m_new = jnp.maximum(m_in_ref[r0:r0 + rows, :], zmax)
    m_out_ref[r0:r0 + rows, :] = m_new
    shift = m_new if row_term is None else m_new - row_term
    for t, zt in enumerate(tiles):
        p_ref[r0:r0 + rows, t * LANES:(t + 1) * LANES] = jnp.exp2(zt - shift).astype(BF16)


def _accumulate(acc_ref, m_in_ref, m_out_ref, p_ref, v, row_lo=0):
    alpha = jnp.exp2(m_in_ref[row_lo:, :] - m_out_ref[row_lo:, :])
    v1 = jnp.concatenate([v, jnp.ones(v.shape, v.dtype)], axis=1)
    pv = jnp.dot(p_ref[row_lo:, :], v1, preferred_element_type=F32)
    for t in range(2):
        sl = slice(t * LANES, (t + 1) * LANES)
        acc_ref[row_lo:, sl] = alpha * acc_ref[row_lo:, sl] + pv[:, sl]


def _fox_attn_kernel(q_ref, k_ref, v_ref, c_ref, ck_ref, o_ref, s0_ref, s1_ref, p_ref, m0_ref, m1_ref, acc_ref,
                     *, bq, bk, strip):
    h = pl.program_id(1)
    qi = pl.program_id(2)
    q = q_ref[...]
    cq = jnp.sum(jnp.where(_iota((bq, LANES), 1) == h, c_ref[...], 0.0), axis=1, keepdims=True)
    cq = jnp.broadcast_to(cq, (bq, LANES))
    m0_ref[...] = jnp.full((bq, LANES), NEG_BIG, F32)
    acc_ref[...] = jnp.zeros((bq, 2 * LANES), F32)
    even, odd = (s0_ref, m0_ref, m1_ref), (s1_ref, m1_ref, m0_ref)

    def scores(t, bufs, row_lo=0):
        ks = pl.multiple_of(t * bk, bk)
        bufs[0][row_lo:, :] = lax.dot_general(q[row_lo:], k_ref[pl.ds(ks, bk), :], _NT,
                                              preferred_element_type=F32)

    def chunk(t, bufs, row_lo=0, diag_off=None):
        s_ref, m_in_ref, m_out_ref = bufs
        ks = pl.multiple_of(t * bk, bk)
        ck = ck_ref[:, pl.ds(ks, bk)]
        for r0 in range(row_lo, bq, strip):
            z = s_ref[r0:r0 + strip, :] - ck
            if diag_off is not None and diag_off + bk - 1 > r0:
                visible = diag_off + _iota((strip, bk), 1) <= r0 + _iota((strip, bk), 0)
                z = jnp.where(visible, z, NEG_BIG)
            _softmax_strip(z, r0, m_in_ref, m_out_ref, p_ref, row_term=cq[r0:r0 + strip])
        _accumulate(acc_ref, m_in_ref, m_out_ref, p_ref, v_ref[pl.ds(ks, bk), :], row_lo)

    n_diag = bq // bk
    assert n_diag % 2 == 0
    scores(0, even)

    def pair(jp, _):
        scores(2 * jp + 1, odd)
        chunk(2 * jp, even)
        scores(2 * jp + 2, even)
        chunk(2 * jp + 1, odd)
        return 0

    lax.fori_loop(0, qi * (n_diag // 2), pair, 0)
    bufs = (even, odd)
    for d in range(n_diag):
        if d + 1 < n_diag:
            scores(qi * n_diag + d + 1, bufs[(d + 1) % 2], row_lo=(d + 1) * bk)
        chunk(qi * n_diag + d, bufs[d % 2], row_lo=d * bk, diag_off=d * bk)
    o_ref[...] = (acc_ref[:, :LANES] / acc_ref[:, LANES:]).astype(o_ref.dtype)


def fox_attention(qkv, c, ck4, batch, seq):
    nh = B_HEADS
    bq = min(seq, 1024)
    bk = min(seq // 2, 256)
    nq = seq // bq
    return pl.pallas_call(
        functools.partial(_fox_attn_kernel, bq=bq, bk=bk, strip=min(bq, 128)),
        name="fox_attn",
        grid=(batch, nh, nq),
        in_specs=[pl.BlockSpec((bq, LANES), lambda b, h, i: (b * nq + i, h)),
                  pl.BlockSpec((seq, LANES), lambda b, h, i: (b, nh + h)),
                  pl.BlockSpec((seq, LANES), lambda b, h, i: (b, 2 * nh + h)),
                  pl.BlockSpec((bq, LANES), lambda b, h, i: (b * nq + i, 0)),
                  pl.BlockSpec((None, None, 1, seq), lambda b, h, i: (b, h, 0, 0))],
        out_specs=pl.BlockSpec((bq, LANES), lambda b, h, i: (b * nq + i, h)),
        out_shape=jax.ShapeDtypeStruct((batch * seq, nh * LANES), BF16),
        scratch_shapes=[pltpu.VMEM((bq, bk), F32), pltpu.VMEM((bq, bk), F32), pltpu.VMEM((bq, bk), BF16),
                        pltpu.VMEM((bq, LANES), F32), pltpu.VMEM((bq, LANES), F32),
                        pltpu.VMEM((bq, 2 * LANES), F32)],
        compiler_params=_params("parallel", "parallel", "parallel"),
    )(qkv, qkv, qkv, c, ck4)


def fox_layer(x, g_norm, w_in, f_bias, w_o, batch, seq):
    nh = B_HEADS
    hn = rmsnorm(x, g_norm, BF16)
    n_q, n_qkv = nh * B_HEAD_DIM, 3 * nh * B_HEAD_DIM
    col_scale = jnp.where(jnp.arange(n_qkv) < n_q, B_HEAD_DIM ** -0.5 * LOG2E, 1.0).astype(F32)
    qkv = matmul(hn, (w_in[:, :n_qkv] * col_scale).astype(BF16), BF16)
    w_f = jnp.pad(w_in[:, n_qkv:], ((0, 0), (0, LANES - nh))).astype(BF16)
    fl = matmul(hn, w_f, F32)
    bias_pad = jnp.pad(f_bias.astype(F32), (0, LANES - nh)).reshape(1, LANES)
    c, ct = fox_prep(fl, bias_pad, batch, seq)
    ck4 = ct[:, :nh, :].reshape(batch, nh, 1, seq)
    o = fox_attention(qkv, c, ck4, batch, seq)
    return matmul(o, w_o.astype(BF16), F32, residual=x)


def _rope_tables(seq, head_dim):
    half = head_dim // 2
    inv = ROPE_THETA ** (-jnp.arange(half, dtype=F32) / half)
    ang = jnp.arange(seq, dtype=F32)[:, None] * inv[None, :]
    cos = jnp.cos(ang)
    sin = jnp.sin(ang)
    reps = LANES // head_dim
    cos_t = jnp.tile(jnp.concatenate([cos, cos], axis=1), (1, reps))
    sin_t = jnp.tile(jnp.concatenate([-sin, sin], axis=1), (1, reps))
    return cos_t, sin_t


def _rope128(x, cos, sin):
    return x * cos + pltpu.roll(x, 64, 1) * sin


def _rope64(x, cos, sin, lane):
    partner = jnp.where((lane % 64) < 32, pltpu.roll(x, 96, 1), pltpu.roll(x, 32, 1))
    return x * cos + partner * sin


_MIN32 = -2 ** 31
_BITS_PER_CHECK = 4
_KEY_NEG_INF = (0xFF800000 - 2 ** 32) ^ 0x7FFFFFFF


def _order_key(s):
    b = pltpu.bitcast(s, I32)
    return jnp.where(b >= 0, b, b ^ 0x7FFFFFFF)


def _dsa_index_kernel(qf_ref, tq_ref, tk_ref, cq_ref, sq_ref, ck_ref, sk_ref, bias_ref, key_ref, qi_ref, ki_ref,
                      *, bq, seq, kc, topk, wi_scale):
    qb = pl.program_id(1)
    n_kc = ((qb + 1) * bq + kc - 1) // kc
    g8 = kc // 8
    rb = min(bq, 256)
    lane = _iota((bq, LANES), 1)

    @pl.when(qb == 0)
    def _():
        for r0 in range(0, seq, rb):
            rs = slice(r0, r0 + rb)
            lane_k = _iota((rb, LANES), 1)
            kr = _rope64(tk_ref[rs, :], ck_ref[rs, :], sk_ref[rs, :], lane_k)
            ki_ref[rs, :] = jnp.where(lane_k < 64, kr, pltpu.roll(kr, 64, 1)).astype(BF16)

    c64, s64 = cq_ref[...], sq_ref[...]
    for t in range(IDX_HEADS // 2):
        r = _rope64(qf_ref[:, t * LANES:(t + 1) * LANES], c64, s64, lane)
        qi_ref[:, (2 * t) * LANES:(2 * t + 1) * LANES] = jnp.where(lane < 64, r, 0.0).astype(BF16)
        qi_ref[:, (2 * t + 1) * LANES:(2 * t + 2) * LANES] = jnp.where(lane >= 64, r, 0.0).astype(BF16)
    wi = tq_ref[...] * wi_scale
    wcol = [jnp.sum(jnp.where(lane == 64 + h, wi, 0.0), axis=1, keepdims=True) for h in range(IDX_HEADS)]
    k_off = _iota((rb, kc), 1)

    def kc_body(j, _):
        ks = pl.multiple_of(j * kc, kc)
        kblk = ki_ref[pl.ds(ks, kc), :]
        for r0 in range(0, bq, rb):
            acc = jnp.zeros((rb, kc), F32)
            for h in range(IDX_HEADS):
                s = lax.dot_general(qi_ref[r0:r0 + rb, h * LANES:(h + 1) * LANES], kblk, _NT,
                                    preferred_element_type=F32)
                acc = acc + wcol[h][r0:r0 + rb] * jnp.maximum(s, 0.0)
            q_chunk = (qb * bq + r0 + _iota((rb, kc), 0)) // CHUNK
            acc = jnp.where((ks + k_off) // CHUNK <= q_chunk, acc, -jnp.inf)
            key_ref[pl.ds(ks, kc), r0:r0 + rb] = _order_key(acc.T)
        return 0

    lax.fori_loop(0, n_kc, kc_body, 0)

    def count(indicator):
        def body(j, acc):
            x = key_ref[pl.ds(pl.multiple_of(j * kc, kc), kc), :].reshape(g8, 8, bq)
            return acc + jnp.sum(indicator(x, j), axis=0)
        acc = lax.fori_loop(0, n_kc, body, jnp.zeros((8, bq), F32))
        return jnp.sum(acc, axis=0, keepdims=True)

    n_adm = ((qb * bq + _iota((1, bq), 1)) // CHUNK + 1) * CHUNK
    take_all = n_adm <= topk
    settled = lambda cnt_t: jnp.logical_or(take_all, cnt_t == topk)
    pending = lambda cnt_t: jnp.max(jnp.where(settled(cnt_t), 0, 1))

    def bit_cond(c):
        return jnp.logical_and(c[0] < 32, c[3] > 0)

    def bit_body(c):
        i, tu, cnt_t, _ = c
        for b in range(_BITS_PER_CHECK):
            cand_u = tu | lax.shift_left(jnp.int32(1), 31 - (i + b))
            cand_s = jnp.broadcast_to(cand_u ^ _MIN32, (8, bq))
            cnt = count(lambda x, j: jnp.where(x >= cand_s, 1.0, 0.0))
            ok = cnt >= topk
            cnt_t = jnp.where(ok, cnt, cnt_t)
            tu = jnp.where(ok, cand_u, tu)
        return i + _BITS_PER_CHECK, tu, cnt_t, pending(cnt_t)

    cnt0 = jnp.zeros((1, bq), F32) + (n_kc * kc).astype(F32)
    _, tu, cnt_ge, n_pending = lax.while_loop(
        bit_cond, bit_body, (jnp.int32(0), jnp.zeros((1, bq), I32), cnt0, pending(cnt0)))
    ts = tu ^ _MIN32
    thr = jnp.where(take_all, _KEY_NEG_INF + 1, ts)

    def no_ties():
        return thr - 1, jnp.zeros((1, bq), I32), jnp.full((1, bq), -1, I32)

    def with_ties():
        tied = jnp.logical_not(settled(cnt_ge))
        ts8 = jnp.broadcast_to(ts, (8, bq))
        cnt_gt = count(lambda x, j: jnp.where(x > ts8, 1.0, 0.0))
        need = topk - cnt_gt
        k_idx = _iota((g8, 8, bq), 0) * 8 + _iota((g8, 8, bq), 1)

        def j_body(i, j0):
            cand = j0 | lax.shift_left(jnp.int32(1), (seq.bit_length() - 2) - i)
            cand8 = jnp.broadcast_to(cand, (8, bq))
            cnt = count(lambda x, j: jnp.where(x == ts8, jnp.where(j * kc + k_idx < cand8, 1.0, 0.0), 0.0))
            return jnp.where(cnt < need, cand, j0)
        j0 = lax.fori_loop(0, seq.bit_length() - 1, j_body, jnp.zeros((1, bq), I32))
        return jnp.where(tied, ts, thr - 1), jnp.where(tied, ts, 0), jnp.where(tied, j0, -1)

    gt_thr, eq_val, jlim = lax.cond(n_pending > 0, with_ties, no_ties)

    bias_ref[...] = jnp.full((bq, seq), NEG_BIG, BF16)
    k_row = _iota((kc, rb), 0)

    def out_body(j, _):
        ks = pl.multiple_of(j * kc, kc)
        for r0 in range(0, bq, rb):
            qs = slice(r0, r0 + rb)
            x = key_ref[pl.ds(ks, kc), qs]
            tie_ok = jnp.where(ks + k_row <= jlim[:, qs], 0.0, NEG_BIG)
            b = jnp.where(x > gt_thr[:, qs], 0.0, jnp.where(x == eq_val[:, qs], tie_ok, NEG_BIG))
            bias_ref[qs, pl.ds(ks, kc)] = b.T.astype(BF16)
        return 0

    lax.fori_loop(0, n_kc, out_body, 0)


def dsa_index(main, tail, batch, seq):
    bq = min(seq, 512)
    nq = seq // bq
    kc = min(seq, 256)
    topk = min(TOPK_MAX, seq // 4)
    n_qi = IDX_HEADS * IDX_DIM
    qi_block = ((A_HEADS + 2 * A_KV_HEADS) * A_HEAD_DIM) // n_qi
    c64, s64 = _rope_tables(seq, IDX_DIM)
    tab_q = pl.BlockSpec((bq, LANES), lambda b, i: (i, 0))
    tab_k = pl.BlockSpec((seq, LANES), lambda b, i: (0, 0))
    return pl.pallas_call(
        functools.partial(_dsa_index_kernel, bq=bq, seq=seq, kc=kc, topk=topk,
                          wi_scale=IDX_HEADS ** -0.5 * IDX_DIM ** -0.5),
        name="dsa_index",
        grid=(batch, nq),
        in_specs=[pl.BlockSpec((bq, n_qi), lambda b, i: (b * nq + i, qi_block)),
                  pl.BlockSpec((bq, LANES), lambda b, i: (b * nq + i, 0)),
                  pl.BlockSpec((seq, LANES), lambda b, i: (b, 0)),
                  tab_q, tab_q, tab_k, tab_k],
        out_specs=pl.BlockSpec((bq, seq), lambda b, i: (b * nq + i, 0)),
        out_shape=jax.ShapeDtypeStruct((batch * seq, seq), BF16),
        scratch_shapes=[pltpu.VMEM((seq, bq), I32), pltpu.VMEM((bq, IDX_HEADS * LANES), BF16),
                        pltpu.VMEM((seq, LANES), BF16)],
        compiler_params=_params("parallel", "arbitrary"),
    )(main, tail, tail, c64, s64, c64, s64)


def _dsa_attn_kernel(qf_ref, kf_ref, vf_ref, cq_ref, sq_ref, ck_ref, sk_ref, bias_ref, o_ref,
                     q_ref, k_ref, v_ref, s0_ref, s1_ref, p_ref, m0_ref, m1_ref, acc_ref,
                     *, bq, bk, seq, group, strip, q_scale):
    qi = pl.program_id(2)
    rows = group * bq

    @pl.when(qi == 0)
    def _():
        for r0 in range(0, seq, bq):
            rs = slice(r0, r0 + bq)
            k_ref[rs, :] = _rope128(kf_ref[rs, :], ck_ref[rs, :], sk_ref[rs, :]).astype(BF16)
            v_ref[rs, :] = vf_ref[rs, :].astype(BF16)

    cq, sq = cq_ref[...] * q_scale, sq_ref[...] * q_scale
    for r in range(group):
        q_ref[r * bq:(r + 1) * bq, :] = _rope128(qf_ref[:, r * LANES:(r + 1) * LANES], cq, sq).astype(BF16)
    q = q_ref[...]
    m0_ref[...] = jnp.full((rows, LANES), NEG_BIG, F32)
    acc_ref[...] = jnp.zeros((rows, 2 * LANES), F32)
    even, odd = (s0_ref, m0_ref, m1_ref), (s1_ref, m1_ref, m0_ref)

    n_kv = ((qi + 1) * bq + bk - 1) // bk

    def scores(j, bufs):
        ks = pl.multiple_of(jnp.minimum(j, n_kv - 1) * bk, bk)
        bufs[0][...] = lax.dot_general(q, k_ref[pl.ds(ks, bk), :], _NT, preferred_element_type=F32)

    def chunk(j, bufs):
        s_ref, m_in_ref, m_out_ref = bufs
        ks = pl.multiple_of(j * bk, bk)
        for b0 in range(0, bq, strip):
            bias = bias_ref[b0:b0 + strip, pl.ds(ks, bk)].astype(F32)
            for r in range(group):
                r0 = r * bq + b0
                _softmax_strip(s_ref[r0:r0 + strip, :] + bias, r0, m_in_ref, m_out_ref, p_ref)
        _accumulate(acc_ref, m_in_ref, m_out_ref, p_ref, v_ref[pl.ds(ks, bk), :])

    scores(0, even)

    def pair(jp, _):
        scores(2 * jp + 1, odd)
        chunk(2 * jp, even)
        scores(2 * jp + 2, even)
        chunk(2 * jp + 1, odd)
        return 0

    lax.fori_loop(0, n_kv // 2, pair, 0)

    @pl.when(n_kv % 2 == 1)
    def _():
        chunk(n_kv - 1, even)

    for r in range(group):
        sl = slice(r * bq, (r + 1) * bq)
        o_ref[:, r * LANES:(r + 1) * LANES] = (acc_ref[sl, :LANES] / acc_ref[sl, LANES:]).astype(o_ref.dtype)


def dsa_attention(main, bias, batch, seq):
    group = A_HEADS // A_KV_HEADS
    bq = bk = min(seq, 256)
    nq = seq // bq
    gw = group * LANES
    c128, s128 = _rope_tables(seq, A_HEAD_DIM)
    tab_q = pl.BlockSpec((bq, LANES), lambda b, g, i: (i, 0))
    tab_k = pl.BlockSpec((seq, LANES), lambda b, g, i: (0, 0))
    return pl.pallas_call(
        functools.partial(_dsa_attn_kernel, bq=bq, bk=bk, seq=seq, group=group, strip=min(bq, 64),
                          q_scale=A_HEAD_DIM ** -0.5 * LOG2E),
        name="dsa_attn",
        grid=(batch, A_KV_HEADS, nq),
        in_specs=[pl.BlockSpec((bq, gw), lambda b, g, i: (b * nq + i, g)),
                  pl.BlockSpec((seq, LANES), lambda b, g, i: (b, A_HEADS + g)),
                  pl.BlockSpec((seq, LANES), lambda b, g, i: (b, A_HEADS + A_KV_HEADS + g)),
                  tab_q, tab_q, tab_k, tab_k,
                  pl.BlockSpec((bq, seq), lambda b, g, i: (b * nq + i, 0))],
        out_specs=pl.BlockSpec((bq, gw), lambda b, g, i: (b * nq + i, g)),
        out_shape=jax.ShapeDtypeStruct((batch * seq, A_HEADS * LANES), BF16),
        scratch_shapes=[pltpu.VMEM((group * bq, LANES), BF16), pltpu.VMEM((seq, LANES), BF16),
                        pltpu.VMEM((seq, LANES), BF16),
                        pltpu.VMEM((group * bq, bk), F32), pltpu.VMEM((group * bq, bk), F32),
                        pltpu.VMEM((group * bq, bk), BF16),
                        pltpu.VMEM((group * bq, LANES), F32), pltpu.VMEM((group * bq, LANES), F32),
                        pltpu.VMEM((group * bq, 2 * LANES), F32)],
        compiler_params=_params("parallel", "parallel", "arbitrary"),
    )(main, main, main, c128, s128, c128, s128, bias)


def dsa_layer(x, g_norm, w_in, w_o, batch, seq):
    hn = rmsnorm(x, g_norm, BF16)
    n_main = (A_HEADS + 2 * A_KV_HEADS) * A_HEAD_DIM + IDX_HEADS * IDX_DIM
    n_tail = w_in.shape[1] - n_main
    main = matmul(hn, w_in[:, :n_main].astype(BF16), F32)
    w_tail = jnp.pad(w_in[:, n_main:], ((0, 0), (0, LANES - n_tail))).astype(BF16)
    tail = matmul(hn, w_tail, F32)
    bias = dsa_index(main, tail, batch, seq)
    o = dsa_attention(main, bias, batch, seq)
    return matmul(o, w_o.astype(BF16), F32, residual=x)


def _rwkv_mix_kernel(x_ref, g_ref, mix_ref, *rest, bs, rs):
    outs, hbuf = rest[:6], rest[6]
    j = pl.program_id(1)

    @pl.when(j == 0)
    def _():
        hbuf[0:8, :] = jnp.zeros((8, D_MODEL), F32)

    g = g_ref[...]
    for r0 in range(0, bs, rs):
        hn = _rms(x_ref[r0:r0 + rs, :], g)
        hbuf[8 + r0:8 + r0 + rs, :] = hn
        xx = hbuf[7 + r0:7 + r0 + rs, :] - hn
        for i in range(6):
            outs[i][r0:r0 + rs, :] = (hn + xx * mix_ref[i:i + 1, :]).astype(BF16)
    hbuf[7:8, :] = hbuf[7 + bs:8 + bs, :]


def rwkv_mix(x, g_norm, mix, batch, seq):
    bs = min(seq, 256)
    nb = seq // bs
    m, d = x.shape
    mix8 = jnp.pad(mix, ((0, 2), (0, 0)))
    return pl.pallas_call(
        functools.partial(_rwkv_mix_kernel, bs=bs, rs=min(bs, 32)),
        name="rwkv_mix",
        grid=(batch, nb),
        in_specs=[pl.BlockSpec((bs, d), lambda b, j: (b * nb + j, 0)),
                  pl.BlockSpec((1, d), lambda b, j: (0, 0)),
                  pl.BlockSpec((8, d), lambda b, j: (0, 0))],
        out_specs=[pl.BlockSpec((bs, d), lambda b, j: (b * nb + j, 0))] * 6,
        out_shape=[jax.ShapeDtypeStruct((m, d), BF16)] * 6,
        scratch_shapes=[pltpu.VMEM((bs + 8, d), F32)],
        compiler_params=_params("parallel", "arbitrary"),
    )(x, g_norm.reshape(1, d), mix8)


def _head_sum_matrix():
    return ((_iota((LANES, LANES), 0) // C_HEAD_DIM) == (_iota((LANES, LANES), 1) // C_HEAD_DIM)).astype(F32)


def _head_sum_2x(x, p):
    hi = x.astype(BF16)
    mid = (x - hi.astype(F32)).astype(BF16)
    pb = p.astype(BF16)
    return jnp.dot(hi, pb, preferred_element_type=F32) + jnp.dot(mid, pb, preferred_element_type=F32)


def _rwkv_rec_kernel(r_ref, k_ref, v_ref, wl_ref, al_ref, w0_ref, a0_ref, kkp_ref, ka_ref, y_ref, km_ref, st_ref,
                     *, tc, npair, mm_dtype):
    c = pl.program_id(2)

    @pl.when(c == 0)
    def _():
        st_ref[...] = jnp.zeros(st_ref.shape, F32)

    def mm(a, b, dims=None):
        a = a.astype(mm_dtype)
        b = b.astype(mm_dtype)
        prec = HIGHEST if mm_dtype == F32 else None
        if dims is None:
            return jnp.dot(a, b, precision=prec, preferred_element_type=F32)
        return lax.dot_general(a, b, dims, precision=prec, preferred_element_type=F32)

    row = _iota((tc, tc), 0)
    col = _iota((tc, tc), 1)
    tri_incl = (row >= col).astype(F32)
    strict = row > col
    incl = row >= col
    eye = (row == col).astype(F32)
    lane = _iota((tc, LANES), 1)
    head_masks = [lane < C_HEAD_DIM, lane >= C_HEAD_DIM]
    blockdiag = (_iota((LANES, LANES), 0) // C_HEAD_DIM) == (_iota((LANES, LANES), 1) // C_HEAD_DIM)
    n_sq = max((tc - 1).bit_length() - 1, 0)

    pairs = range(npair)
    chains = [(p, h) for p in pairs for h in range(2)]
    tile = lambda a, p: a[:, p * LANES:(p + 1) * LANES]
    cast = lambda a: a.astype(mm_dtype)

    r, k_raw, v = r_ref[...], k_ref[...], v_ref[...]
    lw = -jnp.exp(_log_sigmoid(w0_ref[...] + wl_ref[...]) - 0.5)
    a = jax.nn.sigmoid(a0_ref[...] + al_ref[...])
    kr = k_raw * kkp_ref[...]
    hsum = _head_sum_matrix()
    ss = jnp.concatenate([_head_sum_2x(tile(kr, p) * tile(kr, p), hsum) for p in pairs], axis=1)
    kk = kr * lax.rsqrt(jnp.maximum(ss, 1e-24))
    k = k_raw * (1.0 + (a - 1.0) * ka_ref[...])
    b = kk * a
    km_ref[...] = k
    cum = jnp.dot(tri_incl, lw, precision=HIGHEST, preferred_element_type=F32)
    g_in = jnp.exp(cum)
    g_inv = jnp.exp(-cum)
    a_t = -kk * jnp.exp(cum - lw)
    b_t = b * g_inv
    k_t = k * g_inv
    r_t = r * g_in
    g_end = g_in[tc - 1:tc, :]
    b_c, k_c, v_c, r_c = cast(b_t), cast(k_t), cast(v), cast(r_t)
    bk_x = cast(jnp.concatenate([b_t * g_end, k_t * g_end], axis=0).T)
    g_col = g_in.T[:, tc - 1:tc]
    rows = lambda a, p: a[p * LANES:(p + 1) * LANES, :]

    a_m = {(p, h): cast(jnp.where(head_masks[h], tile(a_t, p), 0.0)) for p, h in chains}
    left = {(p, h): jnp.concatenate([a_m[p, h], cast(jnp.where(head_masks[h], tile(r_t, p), 0.0))], axis=0)
            for p, h in chains}
    gb = {c_: mm(left[c_], tile(b_c, c_[0]), _NT) for c_ in chains}
    gk = {c_: mm(left[c_], tile(k_c, c_[0]), _NT) for c_ in chains}
    a_ab = {c_: jnp.where(strict, gb[c_][:tc], 0.0) for c_ in chains}
    a_rb = {c_: cast(jnp.where(incl, gb[c_][tc:], 0.0)) for c_ in chains}
    low2 = jnp.concatenate([strict, incl], axis=0)
    kv = {c_: mm(cast(jnp.where(low2, gk[c_], 0.0)), tile(v_c, c_[0])) for c_ in chains}
    akv = {c_: cast(kv[c_][:tc]) for c_ in chains}
    y0 = {c_: kv[c_][tc:] for c_ in chains}
    x = {c_: eye + a_ab[c_] for c_ in chains}
    pw = {c_: cast(a_ab[c_]) for c_ in chains}
    for _ in range(n_sq):
        pw = {c_: cast(mm(pw[c_], pw[c_])) for c_ in chains}
        x = {c_: x[c_] + mm(pw[c_], cast(x[c_])) for c_ in chains}
    x = {c_: cast(x[c_]) for c_ in chains}
    xw = {c_: mm(x[c_], jnp.concatenate([a_m[c_], akv[c_]], axis=1)) for c_ in chains}
    w_h = {c_: xw[c_][:, :LANES] for c_ in chains}
    u_h = {c_: xw[c_][:, LANES:] for c_ in chains}

    s0 = {p: st_ref[p] for p in pairs}
    ws = {p: mm(jnp.concatenate([cast(w_h[p, 0] + w_h[p, 1]), tile(r_c, p)], axis=0), cast(s0[p]))
          for p in pairs}
    u = {p: ws[p][:tc] + jnp.where(head_masks[0], u_h[p, 0], u_h[p, 1]) for p in pairs}
    u_c = {p: cast(u[p]) for p in pairs}
    yb = {c_: mm(a_rb[c_], u_c[c_[0]]) for c_ in chains}
    upd = {p: mm(rows(bk_x, p), jnp.concatenate([u_c[p], tile(v_c, p)], axis=0)) for p in pairs}
    for p in pairs:
        y = ws[p][tc:] + jnp.where(head_masks[0], y0[p, 0] + yb[p, 0], y0[p, 1] + yb[p, 1])
        y_ref[:, p * LANES:(p + 1) * LANES] = y
        st_ref[p] = s0[p] * rows(g_col, p) + jnp.where(blockdiag, upd[p], 0.0)


def rwkv_rec(r, k, v, wl, al, w0, a0, k_k, k_a, batch, seq, mm_dtype=BF16):
    m, d = r.shape
    tc = min(seq, 64)
    nc = seq // tc
    npair = 16
    gw = npair * LANES
    blk = pl.BlockSpec((tc, gw), lambda bi, g, c: (bi * nc + c, g))
    row = pl.BlockSpec((1, gw), lambda bi, g, c: (0, g))
    r2 = lambda a: a.reshape(1, d).astype(F32)
    return pl.pallas_call(
        functools.partial(_rwkv_rec_kernel, tc=tc, npair=npair, mm_dtype=mm_dtype),
        name="rwkv_rec",
        grid=(batch, d // gw, nc),
        in_specs=[blk] * 5 + [row] * 4,
        out_specs=[blk, blk],
        out_shape=[jax.ShapeDtypeStruct((m, d), F32)] * 2,
        scratch_shapes=[pltpu.VMEM((npair, LANES, LANES), F32)],
        compiler_params=_params("parallel", "parallel", "arbitrary"),
    )(r, k, v, wl, al, r2(w0), r2(a0), r2(k_k), r2(k_a))


def _rwkv_post_kernel(y_ref, r_ref, k_ref, v_ref, g_ref, gng_ref, gnb_ref, rk_ref, o_ref):
    p = _head_sum_matrix()
    inv_n = 1.0 / C_HEAD_DIM
    for t in range(D_MODEL // LANES):
        sl = slice(t * LANES, (t + 1) * LANES)
        y = y_ref[:, sl]
        mu = _head_sum_2x(y, p) * inv_n
        yc = y - mu
        var = _head_sum_2x(yc * yc, p) * inv_n
        yn = yc * lax.rsqrt(var + C_GN_EPS) * gng_ref[:, sl] + gnb_ref[:, sl]
        bonus = _head_sum_2x(r_ref[:, sl] * k_ref[:, sl] * rk_ref[:, sl], p) * v_ref[:, sl]
        o_ref[:, sl] = ((yn + bonus) * g_ref[:, sl]).astype(o_ref.dtype)


def rwkv_post(y, r, kmod, v, g, gn_g, gn_b, r_k):
    m, d = y.shape
    bm = _largest_divisor(m, (256, 128, 64))
    big = pl.BlockSpec((bm, d), lambda i: (i, 0))
    row = pl.BlockSpec((1, d), lambda i: (0, 0))
    r2 = lambda a: a.reshape(1, d).astype(F32)
    return pl.pallas_call(
        _rwkv_post_kernel,
        name="rwkv_post",
        grid=(m // bm,),
        in_specs=[big] * 5 + [row] * 3,
        out_specs=big,
        out_shape=jax.ShapeDtypeStruct((m, d), BF16),
        compiler_params=_params("parallel"),
    )(y, r, kmod, v, g, r2(gn_g), r2(gn_b), r2(r_k))


def rwkv_layer(x, g_norm, mix, w0, w1, w2, a0, a1, a2, g1, g2, k_k, k_a, r_k, w_rkv, w_o, gn_g, gn_b,
               batch, seq):
    bf = lambda a: a.astype(BF16)
    xr, xk, xv, xw, xa, xg = rwkv_mix(x, g_norm, mix, batch, seq)
    r = matmul(xr, bf(w_rkv[0]), F32)
    k = matmul(xk, bf(w_rkv[1]), F32)
    v = matmul(xv, bf(w_rkv[2]), F32)
    wl = lora(xw, bf(w1), bf(w2), "tanh")
    al = lora(xa, bf(a1), bf(a2), "none")
    g = lora(xg, bf(g1), bf(g2), "sigmoid")
    y, kmod = rwkv_rec(r, k, v, wl, al, w0, a0, k_k, k_a, batch, seq)
    o = rwkv_post(y, r, kmod, v, g, gn_g, gn_b, r_k)
    return matmul(o, bf(w_o), F32, residual=x)


def kernel(x, norm_mix, norm_ffn, norm_final, ffn_gate, ffn_up, ffn_down, a_w_in, a_w_o, b_w_in, b_f_bias,
           b_w_o, c_mix, c_w0, c_w1, c_w2, c_a0, c_a1, c_a2, c_g1, c_g2, c_k_k, c_k_a, c_r_k, c_w_rkv,
           c_w_o, c_gn_g, c_gn_b):
    batch, seq, d = x.shape
    depth = norm_mix.shape[0]
    h = x.reshape(batch * seq, d)
    for i in range(depth):
        kind, j = i % 3, i // 3
        if kind == 0:
            h = dsa_layer(h, norm_mix[i], a_w_in[j], a_w_o[j], batch, seq)
        elif kind == 1:
            h = fox_layer(h, norm_mix[i], b_w_in[j], b_f_bias[j], b_w_o[j], batch, seq)
        else:
            h = rwkv_layer(h, norm_mix[i], c_mix[j], c_w0[j], c_w1[j], c_w2[j], c_a0[j], c_a1[j], c_a2[j],
                           c_g1[j], c_g2[j], c_k_k[j], c_k_a[j], c_r_k[j], c_w_rkv[j], c_w_o[j],
                           c_gn_g[j], c_gn_b[j], batch, seq)
        h = swiglu_layer(h, norm_ffn[i], ffn_gate, ffn_up, i, ffn_down[i])
    return rmsnorm(h, norm_final, x.dtype).reshape(batch, seq, d)
```

```python
import functools

import jax
import jax.numpy as jnp
from jax import lax
from jax.experimental import pallas as pl
from jax.experimental.pallas import tpu as pltpu

F32 = jnp.float32
BF16 = jnp.bfloat16
I32 = jnp.int32
HIGHEST = lax.Precision.HIGHEST

D_MODEL = 2048
CHUNK = 64
RMS_EPS = 1e-6
ROPE_THETA = 10000.0
A_HEADS, A_KV_HEADS, A_HEAD_DIM = 16, 4, 128
IDX_HEADS, IDX_DIM, TOPK_MAX = 16, 64, 256
B_HEADS, B_HEAD_DIM = 16, 128
C_HEAD_DIM = 64
C_GN_EPS = C_HEAD_DIM * 1e-5

LANES = 128
VMEM_LIMIT_BYTES = 56 * 1024 * 1024

NEG_BIG = -1e30
LOG2E = 1.4426950408889634
_NT = (((1,), (1,)), ((), ()))
_TN = (((0,), (0,)), ((), ()))


def _params(*sem):
    return pltpu.CompilerParams(dimension_semantics=sem, vmem_limit_bytes=VMEM_LIMIT_BYTES)


def _iota(shape, dim):
    return lax.broadcasted_iota(I32, shape, dim)


def _rms(x, g):
    return x * lax.rsqrt(jnp.mean(x * x, axis=-1, keepdims=True) + RMS_EPS) * g


def _rmsnorm_kernel(x_ref, g_ref, o_ref):
    o_ref[...] = _rms(x_ref[...], g_ref[...]).astype(o_ref.dtype)


def rmsnorm(x, g, out_dtype):
    m, d = x.shape
    bm = min(m, 512)
    return pl.pallas_call(
        _rmsnorm_kernel,
        name="rmsnorm",
        grid=(m // bm,),
        in_specs=[pl.BlockSpec((bm, d), lambda i: (i, 0)), pl.BlockSpec((1, d), lambda i: (0, 0))],
        out_specs=pl.BlockSpec((bm, d), lambda i: (i, 0)),
        out_shape=jax.ShapeDtypeStruct((m, d), out_dtype),
        compiler_params=_params("parallel"),
    )(x, g.reshape(1, d))


def _largest_divisor(n, candidates):
    for c in candidates:
        if n % c == 0:
            return c
    return n


def _mm_kernel(x_ref, w_ref, o_ref):
    o_ref[...] = jnp.dot(x_ref[...], w_ref[...], preferred_element_type=F32).astype(o_ref.dtype)


def _mm_res_kernel(x_ref, w_ref, r_ref, o_ref):
    acc = jnp.dot(x_ref[...], w_ref[...], preferred_element_type=F32)
    o_ref[...] = (r_ref[...] + acc).astype(o_ref.dtype)


def matmul(x, w, out_dtype, residual=None):
    m, k = x.shape
    n = w.shape[1]
    bm = _largest_divisor(m, (1024, 512, 256, 128))
    bn = _largest_divisor(n, (1024, 512, 256, 128)) if k <= 2048 else _largest_divisor(n, (512, 256, 128))
    in_specs = [pl.BlockSpec((bm, k), lambda i, j: (i, 0)), pl.BlockSpec((k, bn), lambda i, j: (0, j))]
    args = [x, w]
    body = _mm_kernel
    if residual is not None:
        in_specs.append(pl.BlockSpec((bm, bn), lambda i, j: (i, j)))
        args.append(residual)
        body = _mm_res_kernel
    return pl.pallas_call(
        body,
        name=f"mm_{k}x{n}",
        grid=(m // bm, n // bn),
        in_specs=in_specs,
        out_specs=pl.BlockSpec((bm, bn), lambda i, j: (i, j)),
        out_shape=jax.ShapeDtypeStruct((m, n), out_dtype),
        compiler_params=_params("parallel", "parallel"),
    )(*args)


def _gateup_kernel(x_ref, wg_ref, wu_ref, o_ref, wgb_ref, wub_ref):
    @pl.when(pl.program_id(1) == 0)
    def _():
        wgb_ref[...] = wg_ref[...].astype(BF16)
        wub_ref[...] = wu_ref[...].astype(BF16)

    x = x_ref[...]
    g = jnp.dot(x, wgb_ref[...], preferred_element_type=F32)
    u = jnp.dot(x, wub_ref[...], preferred_element_type=F32)
    o_ref[...] = (g * jax.nn.sigmoid(g) * u).astype(o_ref.dtype)


def gate_up(x, wg, wu, layer):
    m, k = x.shape
    n = wg.shape[2]
    bm = _largest_divisor(m, (1024, 512, 256, 128))
    bn = _largest_divisor(n, (512, 256, 128))
    return pl.pallas_call(
        _gateup_kernel,
        name="gate_up",
        grid=(n // bn, m // bm),
        in_specs=[pl.BlockSpec((bm, k), lambda j, i: (i, 0)),
                  pl.BlockSpec((None, k, bn), lambda j, i: (layer, 0, j)),
                  pl.BlockSpec((None, k, bn), lambda j, i: (layer, 0, j))],
        out_specs=pl.BlockSpec((bm, bn), lambda j, i: (i, j)),
        out_shape=jax.ShapeDtypeStruct((m, n), BF16),
        scratch_shapes=[pltpu.VMEM((k, bn), BF16), pltpu.VMEM((k, bn), BF16)],
        compiler_params=_params("parallel", "arbitrary"),
    )(x, wg, wu)


def _lora_kernel(x_ref, w1_ref, w2_ref, o_ref, *, act):
    t = jnp.dot(x_ref[...], w1_ref[...], preferred_element_type=F32)
    if act == "tanh":
        t = jnp.tanh(t)
    elif act == "sigmoid":
        t = jax.nn.sigmoid(t)
    o_ref[...] = jnp.dot(t.astype(BF16), w2_ref[...], preferred_element_type=F32)


def lora(x, w1, w2, act):
    m, k = x.shape
    r = w1.shape[1]
    rp = -(-r // LANES) * LANES
    if act == "sigmoid":
        assert rp == r, "sigmoid(0) != 0: the rank must not be padded"
    w1 = jnp.pad(w1, ((0, 0), (0, rp - r)))
    w2 = jnp.pad(w2, ((0, rp - r), (0, 0)))
    n = w2.shape[1]
    bm = _largest_divisor(m, (1024, 512, 256, 128))
    return pl.pallas_call(
        functools.partial(_lora_kernel, act=act),
        name="lora_" + act,
        grid=(m // bm,),
        in_specs=[pl.BlockSpec((bm, k), lambda i: (i, 0)),
                  pl.BlockSpec((k, rp), lambda i: (0, 0)),
                  pl.BlockSpec((rp, n), lambda i: (0, 0))],
        out_specs=pl.BlockSpec((bm, n), lambda i: (i, 0)),
        out_shape=jax.ShapeDtypeStruct((m, n), F32),
        compiler_params=_params("parallel"),
    )(x, w1, w2)


def swiglu_layer(x, g_norm, w_gate_all, w_up_all, layer, w_down):
    hn = rmsnorm(x, g_norm, BF16)
    h = gate_up(hn, w_gate_all, w_up_all, layer)
    return matmul(h, w_down.astype(BF16), F32, residual=x)


def _log_sigmoid(x):
    return jnp.minimum(x, 0.0) - jnp.log1p(jnp.exp(-jnp.abs(x)))


def _fox_prep_kernel(fl_ref, bias_ref, c_ref, ct_ref, *, seq, ch):
    tri = (_iota((ch, ch), 0) >= _iota((ch, ch), 1)).astype(F32)
    carry = jnp.zeros((1, LANES), F32)
    for c in range(seq // ch):
        lf = _log_sigmoid(fl_ref[c * ch:(c + 1) * ch, :] + bias_ref[...])
        cs = jnp.dot(tri, lf, precision=HIGHEST, preferred_element_type=F32) + carry
        c2 = cs * LOG2E
        c_ref[c * ch:(c + 1) * ch, :] = c2
        ct_ref[:, c * ch:(c + 1) * ch] = c2.T
        carry = cs[ch - 1:ch, :]


def fox_prep(fl, bias_pad, batch, seq):
    ch = min(seq, 256)
    return pl.pallas_call(
        functools.partial(_fox_prep_kernel, seq=seq, ch=ch),
        name="fox_prep",
        grid=(batch,),
        in_specs=[pl.BlockSpec((seq, LANES), lambda b: (b, 0)), pl.BlockSpec((1, LANES), lambda b: (0, 0))],
        out_specs=[pl.BlockSpec((seq, LANES), lambda b: (b, 0)),
                   pl.BlockSpec((None, LANES, seq), lambda b: (b, 0, 0))],
        out_shape=[jax.ShapeDtypeStruct((batch * seq, LANES), F32),
                   jax.ShapeDtypeStruct((batch, LANES, seq), F32)],
        compiler_params=_params("parallel"),
    )(fl, bias_pad)


def _softmax_strip(z, r0, m_in_ref, m_out_ref, p_ref, row_term=None):
    rows, bk = z.shape
    tiles = [z[:, t * LANES:(t + 1) * LANES] for t in range(bk // LANES)]
    zmax = functools.reduce(jnp.maximum, tiles)
    zmax = jnp.broadcast_to(jnp.max(zmax, axis=1, keepdims=True), (rows, LANES))
    if row_term is not None:
        zmax = zmax + row_term
    m_new = jnp.maximum(m_in_ref[r0:r0 + rows, :], zmax)
    m_out_ref[r0:r0 + rows, :] = m_new
    shift = m_new if row_term is None else m_new - row_term
    for t, zt in enumerate(tiles):
        p_ref[r0:r0 + rows, t * LANES:(t + 1) * LANES] = jnp.exp2(zt - shift).astype(BF16)


def _accumulate(acc_ref, m_in_ref, m_out_ref, p_ref, v, row_lo=0):
    alpha = jnp.exp2(m_in_ref[row_lo:, :] - m_out_ref[row_lo:, :])
    v1 = jnp.concatenate([v, jnp.ones(v.shape, v.dtype)], axis=1)
    pv = jnp.dot(p_ref[row_lo:, :], v1, preferred_element_type=F32)
    for t in range(2):
        sl = slice(t * LANES, (t + 1) * LANES)
        acc_ref[row_lo:, sl] = alpha * acc_ref[row_lo:, sl] + pv[:, sl]


def _fox_attn_kernel(q_ref, k_ref, v_ref, c_ref, ck_ref, o_ref, s0_ref, s1_ref, p_ref, m0_ref, m1_ref, acc_ref,
                     *, bq, bk, strip):
    h = pl.program_id(1)
    qi = pl.program_id(2)
    q = q_ref[...]
    cq = jnp.sum(jnp.where(_iota((bq, LANES), 1) == h, c_ref[...], 0.0), axis=1, keepdims=True)
    cq = jnp.broadcast_to(cq, (bq, LANES))
    m0_ref[...] = jnp.full((bq, LANES), NEG_BIG, F32)
    acc_ref[...] = jnp.zeros((bq, 2 * LANES), F32)
    even, odd = (s0_ref, m0_ref, m1_ref), (s1_ref, m1_ref, m0_ref)

    def scores(t, bufs, row_lo=0):
        ks = pl.multiple_of(t * bk, bk)
        bufs[0][row_lo:, :] = lax.dot_general(q[row_lo:], k_ref[pl.ds(ks, bk), :], _NT,
                                              preferred_element_type=F32)

    def chunk(t, bufs, row_lo=0, diag_off=None):
        s_ref, m_in_ref, m_out_ref = bufs
        ks = pl.multiple_of(t * bk, bk)
        ck = ck_ref[:, pl.ds(ks, bk)]
        for r0 in range(row_lo, bq, strip):
            z = s_ref[r0:r0 + strip, :] - ck
            if diag_off is not None and diag_off + bk - 1 > r0:
                visible = diag_off + _iota((strip, bk), 1) <= r0 + _iota((strip, bk), 0)
                z = jnp.where(visible, z, NEG_BIG)
            _softmax_strip(z, r0, m_in_ref, m_out_ref, p_ref, row_term=cq[r0:r0 + strip])
        _accumulate(acc_ref, m_in_ref, m_out_ref, p_ref, v_ref[pl.ds(ks, bk), :], row_lo)

    n_diag = bq // bk
    assert n_diag % 2 == 0
    scores(0, even)

    def pair(jp, _):
        scores(2 * jp + 1, odd)
        chunk(2 * jp, even)
        scores(2 * jp + 2, even)
        chunk(2 * jp + 1, odd)
        return 0

    lax.fori_loop(0, qi * (n_diag // 2), pair, 0)
    bufs = (even, odd)
    for d in range(n_diag):
        if d + 1 < n_diag:
            scores(qi * n_diag + d + 1, bufs[(d + 1) % 2], row_lo=(d + 1) * bk)
        chunk(qi * n_diag + d, bufs[d % 2], row_lo=d * bk, diag_off=d * bk)
    o_ref[...] = (acc_ref[:, :LANES] / acc_ref[:, LANES:]).astype(o_ref.dtype)


def fox_attention(qkv, c, ck4, batch, seq):
    nh = B_HEADS
    bq = min(seq, 1024)
    bk = min(seq // 2, 512)
    nq = seq // bq
    return pl.pallas_call(
        functools.partial(_fox_attn_kernel, bq=bq, bk=bk, strip=min(bq, 64)),
        name="fox_attn",
        grid=(batch, nh, nq),
        in_specs=[pl.BlockSpec((bq, LANES), lambda b, h, i: (b * nq + i, h)),
                  pl.BlockSpec((seq, LANES), lambda b, h, i: (b, nh + h)),
                  pl.BlockSpec((seq, LANES), lambda b, h, i: (b, 2 * nh + h)),
                  pl.BlockSpec((bq, LANES), lambda b, h, i: (b * nq + i, 0)),
                  pl.BlockSpec((None, None, 1, seq), lambda b, h, i: (b, h, 0, 0))],
        out_specs=pl.BlockSpec((bq, LANES), lambda b, h, i: (b * nq + i, h)),
        out_shape=jax.ShapeDtypeStruct((batch * seq, nh * LANES), BF16),
        scratch_shapes=[pltpu.VMEM((bq, bk), F32), pltpu.VMEM((bq, bk), F32), pltpu.VMEM((bq, bk), BF16),
                        pltpu.VMEM((bq, LANES), F32), pltpu.VMEM((bq, LANES), F32),
                        pltpu.VMEM((bq, 2 * LANES), F32)],
        compiler_params=_params("parallel", "parallel", "parallel"),
    )(qkv, qkv, qkv, c, ck4)


def fox_layer(x, g_norm, w_in, f_bias, w_o, batch, seq):
    nh = B_HEADS
    hn = rmsnorm(x, g_norm, BF16)
    n_q, n_qkv = nh * B_HEAD_DIM, 3 * nh * B_HEAD_DIM
    col_scale = jnp.where(jnp.arange(n_qkv) < n_q, B_HEAD_DIM ** -0.5 * LOG2E, 1.0).astype(F32)
    qkv = matmul(hn, (w_in[:, :n_qkv] * col_scale).astype(BF16), BF16)
    w_f = jnp.pad(w_in[:, n_qkv:], ((0, 0), (0, LANES - nh))).astype(BF16)
    fl = matmul(hn, w_f, F32)
    bias_pad = jnp.pad(f_bias.astype(F32), (0, LANES - nh)).reshape(1, LANES)
    c, ct = fox_prep(fl, bias_pad, batch, seq)
    ck4 = ct[:, :nh, :].reshape(batch, nh, 1, seq)
    o = fox_attention(qkv, c, ck4, batch, seq)
    return matmul(o, w_o.astype(BF16), F32, residual=x)


def _rope_tables(seq, head_dim):
    half = head_dim // 2
    inv = ROPE_THETA ** (-jnp.arange(half, dtype=F32) / half)
    ang = jnp.arange(seq, dtype=F32)[:, None] * inv[None, :]
    cos = jnp.cos(ang)
    sin = jnp.sin(ang)
    reps = LANES // head_dim
    cos_t = jnp.tile(jnp.concatenate([cos, cos], axis=1), (1, reps))
    sin_t = jnp.tile(jnp.concatenate([-sin, sin], axis=1), (1, reps))
    return cos_t, sin_t


def _rope128(x, cos, sin):
    return x * cos + pltpu.roll(x, 64, 1) * sin


def _rope64(x, cos, sin, lane):
    partner = jnp.where((lane % 64) < 32, pltpu.roll(x, 96, 1), pltpu.roll(x, 32, 1))
    return x * cos + partner * sin


_MIN32 = -2 ** 31
_BITS_PER_CHECK = 4
_KEY_NEG_INF = (0xFF800000 - 2 ** 32) ^ 0x7FFFFFFF


def _order_key(s):
    b = pltpu.bitcast(s, I32)
    return jnp.where(b >= 0, b, b ^ 0x7FFFFFFF)


def _dsa_index_kernel(qf_ref, tq_ref, tk_ref, cq_ref, sq_ref, ck_ref, sk_ref, bias_ref, key_ref, qi_ref, ki_ref,
                      *, bq, seq, kc, topk, wi_scale):
    qb = pl.program_id(1)
    n_kc = ((qb + 1) * bq + kc - 1) // kc
    g8 = kc // 8
    rb = min(bq, 256)
    lane = _iota((bq, LANES), 1)

    @pl.when(qb == 0)
    def _():
        for r0 in range(0, seq, rb):
            rs = slice(r0, r0 + rb)
            lane_k = _iota((rb, LANES), 1)
            kr = _rope64(tk_ref[rs, :], ck_ref[rs, :], sk_ref[rs, :], lane_k)
            ki_ref[rs, :] = jnp.where(lane_k < 64, kr, pltpu.roll(kr, 64, 1)).astype(BF16)

    c64, s64 = cq_ref[...], sq_ref[...]
    for t in range(IDX_HEADS // 2):
        r = _rope64(qf_ref[:, t * LANES:(t + 1) * LANES], c64, s64, lane)
        qi_ref[:, (2 * t) * LANES:(2 * t + 1) * LANES] = jnp.where(lane < 64, r, 0.0).astype(BF16)
        qi_ref[:, (2 * t + 1) * LANES:(2 * t + 2) * LANES] = jnp.where(lane >= 64, r, 0.0).astype(BF16)
    wi = tq_ref[...] * wi_scale
    wcol = [jnp.sum(jnp.where(lane == 64 + h, wi, 0.0), axis=1, keepdims=True) for h in range(IDX_HEADS)]
    k_off = _iota((rb, kc), 1)

    def kc_body(j, _):
        ks = pl.multiple_of(j * kc, kc)
        kblk = ki_ref[pl.ds(ks, kc), :]
        for r0 in range(0, bq, rb):
            acc = jnp.zeros((rb, kc), F32)
            for h in range(IDX_HEADS):
                s = lax.dot_general(qi_ref[r0:r0 + rb, h * LANES:(h + 1) * LANES], kblk, _NT,
                                    preferred_element_type=F32)
                acc = acc + wcol[h][r0:r0 + rb] * jnp.maximum(s, 0.0)
            q_chunk = (qb * bq + r0 + _iota((rb, kc), 0)) // CHUNK
            acc = jnp.where((ks + k_off) // CHUNK <= q_chunk, acc, -jnp.inf)
            key_ref[pl.ds(ks, kc), r0:r0 + rb] = _order_key(acc.T)
        return 0

    lax.fori_loop(0, n_kc, kc_body, 0)

    def count(indicator):
        def body(j, acc):
            x = key_ref[pl.ds(pl.multiple_of(j * kc, kc), kc), :].reshape(g8, 8, bq)
            return acc + jnp.sum(indicator(x, j), axis=0)
        acc = lax.fori_loop(0, n_kc, body, jnp.zeros((8, bq), F32))
        return jnp.sum(acc, axis=0, keepdims=True)

    n_adm = ((qb * bq + _iota((1, bq), 1)) // CHUNK + 1) * CHUNK
    take_all = n_adm <= topk
    settled = lambda cnt_t: jnp.logical_or(take_all, cnt_t == topk)
    pending = lambda cnt_t: jnp.max(jnp.where(settled(cnt_t), 0, 1))

    def bit_cond(c):
        return jnp.logical_and(c[0] < 32, c[3] > 0)

    def bit_body(c):
        i, tu, cnt_t, _ = c
        for b in range(_BITS_PER_CHECK):
            cand_u = tu | lax.shift_left(jnp.int32(1), 31 - (i + b))
            cand_s = jnp.broadcast_to(cand_u ^ _MIN32, (8, bq))
            cnt = count(lambda x, j: jnp.where(x >= cand_s, 1.0, 0.0))
            ok = cnt >= topk
            cnt_t = jnp.where(ok, cnt, cnt_t)
            tu = jnp.where(ok, cand_u, tu)
        return i + _BITS_PER_CHECK, tu, cnt_t, pending(cnt_t)

    cnt0 = jnp.zeros((1, bq), F32) + (n_kc * kc).astype(F32)
    _, tu, cnt_ge, n_pending = lax.while_loop(
        bit_cond, bit_body, (jnp.int32(0), jnp.zeros((1, bq), I32), cnt0, pending(cnt0)))
    ts = tu ^ _MIN32
    thr = jnp.where(take_all, _KEY_NEG_INF + 1, ts)

    def no_ties():
        return thr - 1, jnp.zeros((1, bq), I32), jnp.full((1, bq), -1, I32)

    def with_ties():
        tied = jnp.logical_not(settled(cnt_ge))
        ts8 = jnp.broadcast_to(ts, (8, bq))
        cnt_gt = count(lambda x, j: jnp.where(x > ts8, 1.0, 0.0))
        need = topk - cnt_gt
        k_idx = _iota((g8, 8, bq), 0) * 8 + _iota((g8, 8, bq), 1)

        def j_body(i, j0):
            cand = j0 | lax.shift_left(jnp.int32(1), (seq.bit_length() - 2) - i)
            cand8 = jnp.broadcast_to(cand, (8, bq))
            cnt = count(lambda x, j: jnp.where(x == ts8, jnp.where(j * kc + k_idx < cand8, 1.0, 0.0), 0.0))
            return jnp.where(cnt < need, cand, j0)
        j0 = lax.fori_loop(0, seq.bit_length() - 1, j_body, jnp.zeros((1, bq), I32))
        return jnp.where(tied, ts, thr - 1), jnp.where(tied, ts, 0), jnp.where(tied, j0, -1)

    gt_thr, eq_val, jlim = lax.cond(n_pending > 0, with_ties, no_ties)

    bias_ref[...] = jnp.full((bq, seq), NEG_BIG, BF16)
    k_row = _iota((kc, rb), 0)

    def out_body(j, _):
        ks = pl.multiple_of(j * kc, kc)
        for r0 in range(0, bq, rb):
            qs = slice(r0, r0 + rb)
            x = key_ref[pl.ds(ks, kc), qs]
            tie_ok = jnp.where(ks + k_row <= jlim[:, qs], 0.0, NEG_BIG)
            b = jnp.where(x > gt_thr[:, qs], 0.0, jnp.where(x == eq_val[:, qs], tie_ok, NEG_BIG))
            bias_ref[qs, pl.ds(ks, kc)] = b.T.astype(BF16)
        return 0

    lax.fori_loop(0, n_kc, out_body, 0)


def dsa_index(main, tail, batch, seq):
    bq = min(seq, 512)
    nq = seq // bq
    kc = min(seq, 256)
    topk = min(TOPK_MAX, seq // 4)
    n_qi = IDX_HEADS * IDX_DIM
    qi_block = ((A_HEADS + 2 * A_KV_HEADS) * A_HEAD_DIM) // n_qi
    c64, s64 = _rope_tables(seq, IDX_DIM)
    tab_q = pl.BlockSpec((bq, LANES), lambda b, i: (i, 0))
    tab_k = pl.BlockSpec((seq, LANES), lambda b, i: (0, 0))
    return pl.pallas_call(
        functools.partial(_dsa_index_kernel, bq=bq, seq=seq, kc=kc, topk=topk,
                          wi_scale=IDX_HEADS ** -0.5 * IDX_DIM ** -0.5),
        name="dsa_index",
        grid=(batch, nq),
        in_specs=[pl.BlockSpec((bq, n_qi), lambda b, i: (b * nq + i, qi_block)),
                  pl.BlockSpec((bq, LANES), lambda b, i: (b * nq + i, 0)),
                  pl.BlockSpec((seq, LANES), lambda b, i: (b, 0)),
                  tab_q, tab_q, tab_k, tab_k],
        out_specs=pl.BlockSpec((bq, seq), lambda b, i: (b * nq + i, 0)),
        out_shape=jax.ShapeDtypeStruct((batch * seq, seq), BF16),
        scratch_shapes=[pltpu.VMEM((seq, bq), I32), pltpu.VMEM((bq, IDX_HEADS * LANES), BF16),
                        pltpu.VMEM((seq, LANES), BF16)],
        compiler_params=_params("parallel", "arbitrary"),
    )(main, tail, tail, c64, s64, c64, s64)


def _dsa_attn_kernel(qf_ref, kf_ref, vf_ref, cq_ref, sq_ref, ck_ref, sk_ref, bias_ref, o_ref,
                     q_ref, k_ref, v_ref, s0_ref, s1_ref, p_ref, m0_ref, m1_ref, acc_ref,
                     *, bq, bk, seq, group, strip, q_scale):
    qi = pl.program_id(2)
    rows = group * bq

    @pl.when(qi == 0)
    def _():
        for r0 in range(0, seq, bq):
            rs = slice(r0, r0 + bq)
            k_ref[rs, :] = _rope128(kf_ref[rs, :], ck_ref[rs, :], sk_ref[rs, :]).astype(BF16)
            v_ref[rs, :] = vf_ref[rs, :].astype(BF16)

    cq, sq = cq_ref[...] * q_scale, sq_ref[...] * q_scale
    for r in range(group):
        q_ref[r * bq:(r + 1) * bq, :] = _rope128(qf_ref[:, r * LANES:(r + 1) * LANES], cq, sq).astype(BF16)
    q = q_ref[...]
    m0_ref[...] = jnp.full((rows, LANES), NEG_BIG, F32)
    acc_ref[...] = jnp.zeros((rows, 2 * LANES), F32)
    even, odd = (s0_ref, m0_ref, m1_ref), (s1_ref, m1_ref, m0_ref)

    n_kv = ((qi + 1) * bq + bk - 1) // bk

    def scores(j, bufs):
        ks = pl.multiple_of(jnp.minimum(j, n_kv - 1) * bk, bk)
        bufs[0][...] = lax.dot_general(q, k_ref[pl.ds(ks, bk), :], _NT, preferred_element_type=F32)

    def chunk(j, bufs):
        s_ref, m_in_ref, m_out_ref = bufs
        ks = pl.multiple_of(j * bk, bk)
        for b0 in range(0, bq, strip):
            bias = bias_ref[b0:b0 + strip, pl.ds(ks, bk)].astype(F32)
            for r in range(group):
                r0 = r * bq + b0
                _softmax_strip(s_ref[r0:r0 + strip, :] + bias, r0, m_in_ref, m_out_ref, p_ref)
        _accumulate(acc_ref, m_in_ref, m_out_ref, p_ref, v_ref[pl.ds(ks, bk), :])

    scores(0, even)

    def pair(jp, _):
        scores(2 * jp + 1, odd)
        chunk(2 * jp, even)
        scores(2 * jp + 2, even)
        chunk(2 * jp + 1, odd)
        return 0

    lax.fori_loop(0, n_kv // 2, pair, 0)

    @pl.when(n_kv % 2 == 1)
    def _():
        chunk(n_kv - 1, even)

    for r in range(group):
        sl = slice(r * bq, (r + 1) * bq)
        o_ref[:, r * LANES:(r + 1) * LANES] = (acc_ref[sl, :LANES] / acc_ref[sl, LANES:]).astype(o_ref.dtype)


def dsa_attention(main, bias, batch, seq):
    group = A_HEADS // A_KV_HEADS
    bq = bk = min(seq, 256)
    nq = seq // bq
    gw = group * LANES
    c128, s128 = _rope_tables(seq, A_HEAD_DIM)
    tab_q = pl.BlockSpec((bq, LANES), lambda b, g, i: (i, 0))
    tab_k = pl.BlockSpec((seq, LANES), lambda b, g, i: (0, 0))
    return pl.pallas_call(
        functools.partial(_dsa_attn_kernel, bq=bq, bk=bk, seq=seq, group=group, strip=min(bq, 64),
                          q_scale=A_HEAD_DIM ** -0.5 * LOG2E),
        name="dsa_attn",
        grid=(batch, A_KV_HEADS, nq),
        in_specs=[pl.BlockSpec((bq, gw), lambda b, g, i: (b * nq + i, g)),
                  pl.BlockSpec((seq, LANES), lambda b, g, i: (b, A_HEADS + g)),
                  pl.BlockSpec((seq, LANES), lambda b, g, i: (b, A_HEADS + A_KV_HEADS + g)),
                  tab_q, tab_q, tab_k, tab_k,
                  pl.BlockSpec((bq, seq), lambda b, g, i: (b * nq + i, 0))],
        out_specs=pl.BlockSpec((bq, gw), lambda b, g, i: (b * nq + i, g)),
        out_shape=jax.ShapeDtypeStruct((batch * seq, A_HEADS * LANES), BF16),
        scratch_shapes=[pltpu.VMEM((group * bq, LANES), BF16), pltpu.VMEM((seq, LANES), BF16),
                        pltpu.VMEM((seq, LANES), BF16),
                        pltpu.VMEM((group * bq, bk), F32), pltpu.VMEM((group * bq, bk), F32),
                        pltpu.VMEM((group * bq, bk), BF16),
                        pltpu.VMEM((group * bq, LANES), F32), pltpu.VMEM((group * bq, LANES), F32),
                        pltpu.VMEM((group * bq, 2 * LANES), F32)],
        compiler_params=_params("parallel", "parallel", "arbitrary"),
    )(main, main, main, c128, s128, c128, s128, bias)


def dsa_layer(x, g_norm, w_in, w_o, batch, seq):
    hn = rmsnorm(x, g_norm, BF16)
    n_main = (A_HEADS + 2 * A_KV_HEADS) * A_HEAD_DIM + IDX_HEADS * IDX_DIM
    n_tail = w_in.shape[1] - n_main
    main = matmul(hn, w_in[:, :n_main].astype(BF16), F32)
    w_tail = jnp.pad(w_in[:, n_main:], ((0, 0), (0, LANES - n_tail))).astype(BF16)
    tail = matmul(hn, w_tail, F32)
    bias = dsa_index(main, tail, batch, seq)
    o = dsa_attention(main, bias, batch, seq)
    return matmul(o, w_o.astype(BF16), F32, residual=x)


def _rwkv_mix_kernel(x_ref, g_ref, mix_ref, *rest, bs, rs):
    outs, hbuf = rest[:6], rest[6]
    j = pl.program_id(1)

    @pl.when(j == 0)
    def _():
        hbuf[0:8, :] = jnp.zeros((8, D_MODEL), F32)

    g = g_ref[...]
    for r0 in range(0, bs, rs):
        hn = _rms(x_ref[r0:r0 + rs, :], g)
        hbuf[8 + r0:8 + r0 + rs, :] = hn
        xx = hbuf[7 + r0:7 + r0 + rs, :] - hn
        for i in range(6):
            outs[i][r0:r0 + rs, :] = (hn + xx * mix_ref[i:i + 1, :]).astype(BF16)
    hbuf[7:8, :] = hbuf[7 + bs:8 + bs, :]


def rwkv_mix(x, g_norm, mix, batch, seq):
    bs = min(seq, 256)
    nb = seq // bs
    m, d = x.shape
    mix8 = jnp.pad(mix, ((0, 2), (0, 0)))
    return pl.pallas_call(
        functools.partial(_rwkv_mix_kernel, bs=bs, rs=min(bs, 32)),
        name="rwkv_mix",
        grid=(batch, nb),
        in_specs=[pl.BlockSpec((bs, d), lambda b, j: (b * nb + j, 0)),
                  pl.BlockSpec((1, d), lambda b, j: (0, 0)),
                  pl.BlockSpec((8, d), lambda b, j: (0, 0))],
        out_specs=[pl.BlockSpec((bs, d), lambda b, j: (b * nb + j, 0))] * 6,
        out_shape=[jax.ShapeDtypeStruct((m, d), BF16)] * 6,
        scratch_shapes=[pltpu.VMEM((bs + 8, d), F32)],
        compiler_params=_params("parallel", "arbitrary"),
    )(x, g_norm.reshape(1, d), mix8)


def _head_sum_matrix():
    return ((_iota((LANES, LANES), 0) // C_HEAD_DIM) == (_iota((LANES, LANES), 1) // C_HEAD_DIM)).astype(F32)


def _head_sum_2x(x, p):
    hi = x.astype(BF16)
    mid = (x - hi.astype(F32)).astype(BF16)
    pb = p.astype(BF16)
    return jnp.dot(hi, pb, preferred_element_type=F32) + jnp.dot(mid, pb, preferred_element_type=F32)


def _rwkv_rec_kernel(r_ref, k_ref, v_ref, wl_ref, al_ref, w0_ref, a0_ref, kkp_ref, ka_ref, y_ref, km_ref, st_ref,
                     *, tc, npair, mm_dtype):
    c = pl.program_id(2)

    @pl.when(c == 0)
    def _():
        st_ref[...] = jnp.zeros(st_ref.shape, F32)

    def mm(a, b, dims=None):
        a = a.astype(mm_dtype)
        b = b.astype(mm_dtype)
        prec = HIGHEST if mm_dtype == F32 else None
        if dims is None:
            return jnp.dot(a, b, precision=prec, preferred_element_type=F32)
        return lax.dot_general(a, b, dims, precision=prec, preferred_element_type=F32)

    row = _iota((tc, tc), 0)
    col = _iota((tc, tc), 1)
    tri_incl = (row >= col).astype(F32)
    strict = row > col
    incl = row >= col
    eye = (row == col).astype(F32)
    lane = _iota((tc, LANES), 1)
    head_masks = [lane < C_HEAD_DIM, lane >= C_HEAD_DIM]
    blockdiag = (_iota((LANES, LANES), 0) // C_HEAD_DIM) == (_iota((LANES, LANES), 1) // C_HEAD_DIM)
    n_sq = max((tc - 1).bit_length() - 1, 0)

    pairs = range(npair)
    chains = [(p, h) for p in pairs for h in range(2)]
    tile = lambda a, p: a[:, p * LANES:(p + 1) * LANES]
    cast = lambda a: a.astype(mm_dtype)

    r, k_raw, v = r_ref[...], k_ref[...], v_ref[...]
    lw = -jnp.exp(_log_sigmoid(w0_ref[...] + wl_ref[...]) - 0.5)
    a = jax.nn.sigmoid(a0_ref[...] + al_ref[...])
    kr = k_raw * kkp_ref[...]
    hsum = _head_sum_matrix()
    ss = jnp.concatenate([_head_sum_2x(tile(kr, p) * tile(kr, p), hsum) for p in pairs], axis=1)
    kk = kr * lax.rsqrt(jnp.maximum(ss, 1e-24))
    k = k_raw * (1.0 + (a - 1.0) * ka_ref[...])
    b = kk * a
    km_ref[...] = k
    cum = jnp.dot(tri_incl, lw, precision=HIGHEST, preferred_element_type=F32)
    g_in = jnp.exp(cum)
    g_inv = jnp.exp(-cum)
    a_t = -kk * jnp.exp(cum - lw)
    b_t = b * g_inv
    k_t = k * g_inv
    r_t = r * g_in
    g_end = g_in[tc - 1:tc, :]
    b_c, k_c, v_c, r_c = cast(b_t), cast(k_t), cast(v), cast(r_t)
    bk_x = cast(jnp.concatenate([b_t * g_end, k_t * g_end], axis=0).T)
    g_col = g_in.T[:, tc - 1:tc]
    rows = lambda a, p: a[p * LANES:(p + 1) * LANES, :]

    a_m = {(p, h): cast(jnp.where(head_masks[h], tile(a_t, p), 0.0)) for p, h in chains}
    left = {(p, h): jnp.concatenate([a_m[p, h], cast(jnp.where(head_masks[h], tile(r_t, p), 0.0))], axis=0)
            for p, h in chains}
    gb = {c_: mm(left[c_], tile(b_c, c_[0]), _NT) for c_ in chains}
    gk = {c_: mm(left[c_], tile(k_c, c_[0]), _NT) for c_ in chains}
    a_ab = {c_: jnp.where(strict, gb[c_][:tc], 0.0) for c_ in chains}
    a_rb = {c_: cast(jnp.where(incl, gb[c_][tc:], 0.0)) for c_ in chains}
    low2 = jnp.concatenate([strict, incl], axis=0)
    kv = {c_: mm(cast(jnp.where(low2, gk[c_], 0.0)), tile(v_c, c_[0])) for c_ in chains}
    akv = {c_: cast(kv[c_][:tc]) for c_ in chains}
    y0 = {c_: kv[c_][tc:] for c_ in chains}
    x = {c_: eye + a_ab[c_] for c_ in chains}
    pw = {c_: cast(a_ab[c_]) for c_ in chains}
    for _ in range(n_sq):
        pw = {c_: cast(mm(pw[c_], pw[c_])) for c_ in chains}
        x = {c_: x[c_] + mm(pw[c_], cast(x[c_])) for c_ in chains}
    x = {c_: cast(x[c_]) for c_ in chains}
    xw = {c_: mm(x[c_], jnp.concatenate([a_m[c_], akv[c_]], axis=1)) for c_ in chains}
    w_h = {c_: xw[c_][:, :LANES] for c_ in chains}
    u_h = {c_: xw[c_][:, LANES:] for c_ in chains}

    s0 = {p: st_ref[p] for p in pairs}
    ws = {p: mm(jnp.concatenate([cast(w_h[p, 0] + w_h[p, 1]), tile(r_c, p)], axis=0), cast(s0[p]))
          for p in pairs}
    u = {p: ws[p][:tc] + jnp.where(head_masks[0], u_h[p, 0], u_h[p, 1]) for p in pairs}
    u_c = {p: cast(u[p]) for p in pairs}
    yb = {c_: mm(a_rb[c_], u_c[c_[0]]) for c_ in chains}
    upd = {p: mm(rows(bk_x, p), jnp.concatenate([u_c[p], tile(v_c, p)], axis=0)) for p in pairs}
    for p in pairs:
        y = ws[p][tc:] + jnp.where(head_masks[0], y0[p, 0] + yb[p, 0], y0[p, 1] + yb[p, 1])
        y_ref[:, p * LANES:(p + 1) * LANES] = y
        st_ref[p] = s0[p] * rows(g_col, p) + jnp.where(blockdiag, upd[p], 0.0)


def rwkv_rec(r, k, v, wl, al, w0, a0, k_k, k_a, batch, seq, mm_dtype=BF16):
    m, d = r.shape
    tc = min(seq, 64)
    nc = seq // tc
    npair = 16
    gw = npair * LANES
    blk = pl.BlockSpec((tc, gw), lambda bi, g, c: (bi * nc + c, g))
    row = pl.BlockSpec((1, gw), lambda bi, g, c: (0, g))
    r2 = lambda a: a.reshape(1, d).astype(F32)
    return pl.pallas_call(
        functools.partial(_rwkv_rec_kernel, tc=tc, npair=npair, mm_dtype=mm_dtype),
        name="rwkv_rec",
        grid=(batch, d // gw, nc),
        in_specs=[blk] * 5 + [row] * 4,
        out_specs=[blk, blk],
        out_shape=[jax.ShapeDtypeStruct((m, d), F32)] * 2,
        scratch_shapes=[pltpu.VMEM((npair, LANES, LANES), F32)],
        compiler_params=_params("parallel", "parallel", "arbitrary"),
    )(r, k, v, wl, al, r2(w0), r2(a0), r2(k_k), r2(k_a))


def _rwkv_post_kernel(y_ref, r_ref, k_ref, v_ref, g_ref, gng_ref, gnb_ref, rk_ref, o_ref):
    p = _head_sum_matrix()
    inv_n = 1.0 / C_HEAD_DIM
    for t in range(D_MODEL // LANES):
        sl = slice(t * LANES, (t + 1) * LANES)
        y = y_ref[:, sl]
        mu = _head_sum_2x(y, p) * inv_n
        yc = y - mu
        var = _head_sum_2x(yc * yc, p) * inv_n
        yn = yc * lax.rsqrt(var + C_GN_EPS) * gng_ref[:, sl] + gnb_ref[:, sl]
        bonus = _head_sum_2x(r_ref[:, sl] * k_ref[:, sl] * rk_ref[:, sl], p) * v_ref[:, sl]
        o_ref[:, sl] = ((yn + bonus) * g_ref[:, sl]).astype(o_ref.dtype)


def rwkv_post(y, r, kmod, v, g, gn_g, gn_b, r_k):
    m, d = y.shape
    bm = _largest_divisor(m, (256, 128, 64))
    big = pl.BlockSpec((bm, d), lambda i: (i, 0))
    row = pl.BlockSpec((1, d), lambda i: (0, 0))
    r2 = lambda a: a.reshape(1, d).astype(F32)
    return pl.pallas_call(
        _rwkv_post_kernel,
        name="rwkv_post",
        grid=(m // bm,),
        in_specs=[big] * 5 + [row] * 3,
        out_specs=big,
        out_shape=jax.ShapeDtypeStruct((m, d), BF16),
        compiler_params=_params("parallel"),
    )(y, r, kmod, v, g, r2(gn_g), r2(gn_b), r2(r_k))


def rwkv_layer(x, g_norm, mix, w0, w1, w2, a0, a1, a2, g1, g2, k_k, k_a, r_k, w_rkv, w_o, gn_g, gn_b,
               batch, seq):
    bf = lambda a: a.astype(BF16)
    xr, xk, xv, xw, xa, xg = rwkv_mix(x, g_norm, mix, batch, seq)
    r = matmul(xr, bf(w_rkv[0]), F32)
    k = matmul(xk, bf(w_rkv[1]), F32)
    v = matmul(xv, bf(w_rkv[2]), F32)
    wl = lora(xw, bf(w1), bf(w2), "tanh")
    al = lora(xa, bf(a1), bf(a2), "none")
    g = lora(xg, bf(g1), bf(g2), "sigmoid")
    y, kmod = rwkv_rec(r, k, v, wl, al, w0, a0, k_k, k_a, batch, seq)
    o = rwkv_post(y, r, kmod, v, g, gn_g, gn_b, r_k)
    return matmul(o, bf(w_o), F32, residual=x)


def kernel(x, norm_mix, norm_ffn, norm_final, ffn_gate, ffn_up, ffn_down, a_w_in, a_w_o, b_w_in, b_f_bias,
           b_w_o, c_mix, c_w0, c_w1, c_w2, c_a0, c_a1, c_a2, c_g1, c_g2, c_k_k, c_k_a, c_r_k, c_w_rkv,
           c_w_o, c_gn_g, c_gn_b):
    batch, seq, d = x.shape
    depth = norm_mix.shape[0]
    h = x.reshape(batch * seq, d)
    for i in range(depth):
        kind, j = i % 3, i // 3
        if kind == 0:
            h = dsa_layer(h, norm_mix[i], a_w_in[j], a_w_o[j], batch, seq)
        elif kind == 1:
            h = fox_layer(h, norm_mix[i], b_w_in[j], b_f_bias[j], b_w_o[j], batch, seq)
        else:
            h = rwkv_layer(h, norm_mix[i], c_mix[j], c_w0[j], c_w1[j], c_w2[j], c_a0[j], c_a1[j], c_a2[j],
                           c_g1[j], c_g2[j], c_k_k[j], c_k_a[j], c_r_k[j], c_w_rkv[j], c_w_o[j],
                           c_gn_g[j], c_gn_b[j], batch, seq)
        h = swiglu_layer(h, norm_ffn[i], ffn_gate, ffn_up, i, ffn_down[i])
    return rmsnorm(h, norm_final, x.dtype).reshape(batch, seq, d)
```

```python
import functools

import jax
import jax.numpy as jnp
from jax import lax
from jax.experimental import pallas as pl
from jax.experimental.pallas import tpu as pltpu

F32 = jnp.float32
BF16 = jnp.bfloat16
I32 = jnp.int32
HIGHEST = lax.Precision.HIGHEST

D_MODEL = 2048
CHUNK = 64
RMS_EPS = 1e-6
ROPE_THETA = 10000.0
A_HEADS, A_KV_HEADS, A_HEAD_DIM = 16, 4, 128
IDX_HEADS, IDX_DIM, TOPK_MAX = 16, 64, 256
B_HEADS, B_HEAD_DIM = 16, 128
C_HEAD_DIM = 64
C_GN_EPS = C_HEAD_DIM * 1e-5

LANES = 128
VMEM_LIMIT_BYTES = 56 * 1024 * 1024

NEG_BIG = -1e30
LOG2E = 1.4426950408889634
_NT = (((1,), (1,)), ((), ()))
_TN = (((0,), (0,)), ((), ()))


def _params(*sem):
    return pltpu.CompilerParams(dimension_semantics=sem, vmem_limit_bytes=VMEM_LIMIT_BYTES)


def _iota(shape, dim):
    return lax.broadcasted_iota(I32, shape, dim)


def _rms(x, g):
    return x * lax.rsqrt(jnp.mean(x * x, axis=-1, keepdims=True) + RMS_EPS) * g


def _rmsnorm_kernel(x_ref, g_ref, o_ref):
    o_ref[...] = _rms(x_ref[...], g_ref[...]).astype(o_ref.dtype)


def rmsnorm(x, g, out_dtype):
    m, d = x.shape
    bm = min(m, 512)
    return pl.pallas_call(
        _rmsnorm_kernel,
        name="rmsnorm",
        grid=(m // bm,),
        in_specs=[pl.BlockSpec((bm, d), lambda i: (i, 0)), pl.BlockSpec((1, d), lambda i: (0, 0))],
        out_specs=pl.BlockSpec((bm, d), lambda i: (i, 0)),
        out_shape=jax.ShapeDtypeStruct((m, d), out_dtype),
        compiler_params=_params("parallel"),
    )(x, g.reshape(1, d))


def _largest_divisor(n, candidates):
    for c in candidates:
        if n % c == 0:
            return c
    return n


def _mm_kernel(x_ref, w_ref, o_ref):
    o_ref[...] = jnp.dot(x_ref[...], w_ref[...], preferred_element_type=F32).astype(o_ref.dtype)


def _mm_res_kernel(x_ref, w_ref, r_ref, o_ref):
    acc = jnp.dot(x_ref[...], w_ref[...], preferred_element_type=F32)
    o_ref[...] = (r_ref[...] + acc).astype(o_ref.dtype)


def matmul(x, w, out_dtype, residual=None):
    m, k = x.shape
    n = w.shape[1]
    bm = _largest_divisor(m, (1024, 512, 256, 128))
    bn = _largest_divisor(n, (1024, 512, 256, 128)) if k <= 2048 else _largest_divisor(n, (512, 256, 128))
    in_specs = [pl.BlockSpec((bm, k), lambda i, j: (i, 0)), pl.BlockSpec((k, bn), lambda i, j: (0, j))]
    args = [x, w]
    body = _mm_kernel
    if residual is not None:
        in_specs.append(pl.BlockSpec((bm, bn), lambda i, j: (i, j)))
        args.append(residual)
        body = _mm_res_kernel
    return pl.pallas_call(
        body,
        name=f"mm_{k}x{n}",
        grid=(m // bm, n // bn),
        in_specs=in_specs,
        out_specs=pl.BlockSpec((bm, bn), lambda i, j: (i, j)),
        out_shape=jax.ShapeDtypeStruct((m, n), out_dtype),
        compiler_params=_params("parallel", "parallel"),
    )(*args)


def _gateup_kernel(x_ref, wg_ref, wu_ref, o_ref, wgb_ref, wub_ref):
    @pl.when(pl.program_id(1) == 0)
    def _():
        wgb_ref[...] = wg_ref[...].astype(BF16)
        wub_ref[...] = wu_ref[...].astype(BF16)

    x = x_ref[...]
    g = jnp.dot(x, wgb_ref[...], preferred_element_type=F32)
    u = jnp.dot(x, wub_ref[...], preferred_element_type=F32)
    o_ref[...] = (g * jax.nn.sigmoid(g) * u).astype(o_ref.dtype)


def gate_up(x, wg, wu, layer):
    m, k = x.shape
    n = wg.shape[2]
    bm = _largest_divisor(m, (2048, 1024, 512, 256, 128))
    bn = _largest_divisor(n, (512, 256, 128))
    return pl.pallas_call(
        _gateup_kernel,
        name="gate_up",
        grid=(n // bn, m // bm),
        in_specs=[pl.BlockSpec((bm, k), lambda j, i: (i, 0)),
                  pl.BlockSpec((None, k, bn), lambda j, i: (layer, 0, j)),
                  pl.BlockSpec((None, k, bn), lambda j, i: (layer, 0, j))],
        out_specs=pl.BlockSpec((bm, bn), lambda j, i: (i, j)),
        out_shape=jax.ShapeDtypeStruct((m, n), BF16),
        scratch_shapes=[pltpu.VMEM((k, bn), BF16), pltpu.VMEM((k, bn), BF16)],
        compiler_params=_params("parallel", "arbitrary"),
    )(x, wg, wu)


def _lora_kernel(x_ref, w1_ref, w2_ref, o_ref, *, act):
    t = jnp.dot(x_ref[...], w1_ref[...], preferred_element_type=F32)
    if act == "tanh":
        t = jnp.tanh(t)
    elif act == "sigmoid":
        t = jax.nn.sigmoid(t)
    o_ref[...] = jnp.dot(t.astype(BF16), w2_ref[...], preferred_element_type=F32)


def lora(x, w1, w2, act):
    m, k = x.shape
    r = w1.shape[1]
    rp = -(-r // LANES) * LANES
    if act == "sigmoid":
        assert rp == r, "sigmoid(0) != 0: the rank must not be padded"
    w1 = jnp.pad(w1, ((0, 0), (0, rp - r)))
    w2 = jnp.pad(w2, ((0, rp - r), (0, 0)))
    n = w2.shape[1]
    bm = _largest_divisor(m, (1024, 512, 256, 128))
    return pl.pallas_call(
        functools.partial(_lora_kernel, act=act),
        name="lora_" + act,
        grid=(m // bm,),
        in_specs=[pl.BlockSpec((bm, k), lambda i: (i, 0)),
                  pl.BlockSpec((k, rp), lambda i: (0, 0)),
                  pl.BlockSpec((rp, n), lambda i: (0, 0))],
        out_specs=pl.BlockSpec((bm, n), lambda i: (i, 0)),
        out_shape=jax.ShapeDtypeStruct((m, n), F32),
        compiler_params=_params("parallel"),
    )(x, w1, w2)


def swiglu_layer(x, g_norm, w_gate_all, w_up_all, layer, w_down):
    hn = rmsnorm(x, g_norm, BF16)
    h = gate_up(hn, w_gate_all, w_up_all, layer)
    return matmul(h, w_down.astype(BF16), F32, residual=x)


def _log_sigmoid(x):
    return jnp.minimum(x, 0.0) - jnp.log1p(jnp.exp(-jnp.abs(x)))


def _fox_prep_kernel(fl_ref, bias_ref, c_ref, ct_ref, *, seq, ch):
    tri = (_iota((ch, ch), 0) >= _iota((ch, ch), 1)).astype(F32)
    carry = jnp.zeros((1, LANES), F32)
    for c in range(seq // ch):
        lf = _log_sigmoid(fl_ref[c * ch:(c + 1) * ch, :] + bias_ref[...])
        cs = jnp.dot(tri, lf, precision=HIGHEST, preferred_element_type=F32) + carry
        c2 = cs * LOG2E
        c_ref[c * ch:(c + 1) * ch, :] = c2
        ct_ref[:, c * ch:(c + 1) * ch] = c2.T
        carry = cs[ch - 1:ch, :]


def fox_prep(fl, bias_pad, batch, seq):
    ch = min(seq, 256)
    return pl.pallas_call(
        functools.partial(_fox_prep_kernel, seq=seq, ch=ch),
        name="fox_prep",
        grid=(batch,),
        in_specs=[pl.BlockSpec((seq, LANES), lambda b: (b, 0)), pl.BlockSpec((1, LANES), lambda b: (0, 0))],
        out_specs=[pl.BlockSpec((seq, LANES), lambda b: (b, 0)),
                   pl.BlockSpec((None, LANES, seq), lambda b: (b, 0, 0))],
        out_shape=[jax.ShapeDtypeStruct((batch * seq, LANES), F32),
                   jax.ShapeDtypeStruct((batch, LANES, seq), F32)],
        compiler_params=_params("parallel"),
    )(fl, bias_pad)


def _softmax_strip(z, r0, m_in_ref, m_out_ref, p_ref, row_term=None):
    rows, bk = z.shape
    tiles = [z[:, t * LANES:(t + 1) * LANES] for t in range(bk // LANES)]
    zmax = functools.reduce(jnp.maximum, tiles)
    zmax = jnp.broadcast_to(jnp.max(zmax, axis=1, keepdims=True), (rows, LANES))
    if row_term is not None:
        zmax = zmax + row_term
    m_new = jnp.maximum(m_in_ref[r0:r0 + rows, :], zmax)
    m_out_ref[r0:r0 + rows, :] = m_new
    shift = m_new if row_term is None else m_new - row_term
    for t, zt in enumerate(tiles):
        p_ref[r0:r0 + rows, t * LANES:(t + 1) * LANES] = jnp.exp2(zt - shift).astype(BF16)


def _accumulate(acc_ref, m_in_ref, m_out_ref, p_ref, v, row_lo=0):
    alpha = jnp.exp2(m_in_ref[row_lo:, :] - m_out_ref[row_lo:, :])
    v1 = jnp.concatenate([v, jnp.ones(v.shape, v.dtype)], axis=1)
    pv = jnp.dot(p_ref[row_lo:, :], v1, preferred_element_type=F32)
    for t in range(2):
        sl = slice(t * LANES, (t + 1) * LANES)
        acc_ref[row_lo:, sl] = alpha * acc_ref[row_lo:, sl] + pv[:, sl]


def _fox_attn_kernel(q_ref, k_ref, v_ref, c_ref, ck_ref, o_ref, s0_ref, s1_ref, p_ref, m0_ref, m1_ref, acc_ref,
                     *, bq, bk, strip):
    h = pl.program_id(1)
    qi = pl.program_id(2)
    q = q_ref[...]
    cq = jnp.sum(jnp.where(_iota((bq, LANES), 1) == h, c_ref[...], 0.0), axis=1, keepdims=True)
    cq = jnp.broadcast_to(cq, (bq, LANES))
    m0_ref[...] = jnp.full((bq, LANES), NEG_BIG, F32)
    acc_ref[...] = jnp.zeros((bq, 2 * LANES), F32)
    even, odd = (s0_ref, m0_ref, m1_ref), (s1_ref, m1_ref, m0_ref)

    def scores(t, bufs, row_lo=0):
        ks = pl.multiple_of(t * bk, bk)
        bufs[0][row_lo:, :] = lax.dot_general(q[row_lo:], k_ref[pl.ds(ks, bk), :], _NT,
                                              preferred_element_type=F32)

    def chunk(t, bufs, row_lo=0, diag_off=None):
        s_ref, m_in_ref, m_out_ref = bufs
        ks = pl.multiple_of(t * bk, bk)
        ck = ck_ref[:, pl.ds(ks, bk)]
        for r0 in range(row_lo, bq, strip):
            z = s_ref[r0:r0 + strip, :] - ck
            if diag_off is not None and diag_off + bk - 1 > r0:
                visible = diag_off + _iota((strip, bk), 1) <= r0 + _iota((strip, bk), 0)
                z = jnp.where(visible, z, NEG_BIG)
            _softmax_strip(z, r0, m_in_ref, m_out_ref, p_ref, row_term=cq[r0:r0 + strip])
        _accumulate(acc_ref, m_in_ref, m_out_ref, p_ref, v_ref[pl.ds(ks, bk), :], row_lo)

    n_diag = bq // bk
    assert n_diag % 2 == 0
    scores(0, even)

    def pair(jp, _):
        scores(2 * jp + 1, odd)
        chunk(2 * jp, even)
        scores(2 * jp + 2, even)
        chunk(2 * jp + 1, odd)
        return 0

    lax.fori_loop(0, qi * (n_diag // 2), pair, 0)
    bufs = (even, odd)
    for d in range(n_diag):
        if d + 1 < n_diag:
            scores(qi * n_diag + d + 1, bufs[(d + 1) % 2], row_lo=(d + 1) * bk)
        chunk(qi * n_diag + d, bufs[d % 2], row_lo=d * bk, diag_off=d * bk)
    o_ref[...] = (acc_ref[:, :LANES] / acc_ref[:, LANES:]).astype(o_ref.dtype)


def fox_attention(qkv, c, ck4, batch, seq):
    nh = B_HEADS
    bq = min(seq, 1024)
    bk = min(seq // 2, 512)
    nq = seq // bq
    return pl.pallas_call(
        functools.partial(_fox_attn_kernel, bq=bq, bk=bk, strip=min(bq, 64)),
        name="fox_attn",
        grid=(batch, nh, nq),
        in_specs=[pl.BlockSpec((bq, LANES), lambda b, h, i: (b * nq + i, h)),
                  pl.BlockSpec((seq, LANES), lambda b, h, i: (b, nh + h)),
                  pl.BlockSpec((seq, LANES), lambda b, h, i: (b, 2 * nh + h)),
                  pl.BlockSpec((bq, LANES), lambda b, h, i: (b * nq + i, 0)),
                  pl.BlockSpec((None, None, 1, seq), lambda b, h, i: (b, h, 0, 0))],
        out_specs=pl.BlockSpec((bq, LANES), lambda b, h, i: (b * nq + i, h)),
        out_shape=jax.ShapeDtypeStruct((batch * seq, nh * LANES), BF16),
        scratch_shapes=[pltpu.VMEM((bq, bk), F32), pltpu.VMEM((bq, bk), F32), pltpu.VMEM((bq, bk), BF16),
                        pltpu.VMEM((bq, LANES), F32), pltpu.VMEM((bq, LANES), F32),
                        pltpu.VMEM((bq, 2 * LANES), F32)],
        compiler_params=_params("parallel", "parallel", "parallel"),
    )(qkv, qkv, qkv, c, ck4)


def fox_layer(x, g_norm, w_in, f_bias, w_o, batch, seq):
    nh = B_HEADS
    hn = rmsnorm(x, g_norm, BF16)
    n_q, n_qkv = nh * B_HEAD_DIM, 3 * nh * B_HEAD_DIM
    col_scale = jnp.where(jnp.arange(n_qkv) < n_q, B_HEAD_DIM ** -0.5 * LOG2E, 1.0).astype(F32)
    qkv = matmul(hn, (w_in[:, :n_qkv] * col_scale).astype(BF16), BF16)
    w_f = jnp.pad(w_in[:, n_qkv:], ((0, 0), (0, LANES - nh))).astype(BF16)
    fl = matmul(hn, w_f, F32)
    bias_pad = jnp.pad(f_bias.astype(F32), (0, LANES - nh)).reshape(1, LANES)
    c, ct = fox_prep(fl, bias_pad, batch, seq)
    ck4 = ct[:, :nh, :].reshape(batch, nh, 1, seq)
    o = fox_attention(qkv, c, ck4, batch, seq)
    return matmul(o, w_o.astype(BF16), F32, residual=x)


def _rope_tables(seq, head_dim):
    half = head_dim // 2
    inv = ROPE_THETA ** (-jnp.arange(half, dtype=F32) / half)
    ang = jnp.arange(seq, dtype=F32)[:, None] * inv[None, :]
    cos = jnp.cos(ang)
    sin = jnp.sin(ang)
    reps = LANES // head_dim
    cos_t = jnp.tile(jnp.concatenate([cos, cos], axis=1), (1, reps))
    sin_t = jnp.tile(jnp.concatenate([-sin, sin], axis=1), (1, reps))
    return cos_t, sin_t


def _rope128(x, cos, sin):
    return x * cos + pltpu.roll(x, 64, 1) * sin


def _rope64(x, cos, sin, lane):
    partner = jnp.where((lane % 64) < 32, pltpu.roll(x, 96, 1), pltpu.roll(x, 32, 1))
    return x * cos + partner * sin


_MIN32 = -2 ** 31
_BITS_PER_CHECK = 4
_KEY_NEG_INF = (0xFF800000 - 2 ** 32) ^ 0x7FFFFFFF


def _order_key(s):
    b = pltpu.bitcast(s, I32)
    return jnp.where(b >= 0, b, b ^ 0x7FFFFFFF)


def _dsa_index_kernel(qf_ref, tq_ref, tk_ref, cq_ref, sq_ref, ck_ref, sk_ref, bias_ref, key_ref, qi_ref, ki_ref,
                      *, bq, seq, kc, topk, wi_scale):
    qb = pl.program_id(1)
    n_kc = ((qb + 1) * bq + kc - 1) // kc
    g8 = kc // 8
    rb = min(bq, 256)
    lane = _iota((bq, LANES), 1)

    @pl.when(qb == 0)
    def _():
        for r0 in range(0, seq, rb):
            rs = slice(r0, r0 + rb)
            lane_k = _iota((rb, LANES), 1)
            kr = _rope64(tk_ref[rs, :], ck_ref[rs, :], sk_ref[rs, :], lane_k)
            ki_ref[rs, :] = jnp.where(lane_k < 64, kr, pltpu.roll(kr, 64, 1)).astype(BF16)

    c64, s64 = cq_ref[...], sq_ref[...]
    for t in range(IDX_HEADS // 2):
        r = _rope64(qf_ref[:, t * LANES:(t + 1) * LANES], c64, s64, lane)
        qi_ref[:, (2 * t) * LANES:(2 * t + 1) * LANES] = jnp.where(lane < 64, r, 0.0).astype(BF16)
        qi_ref[:, (2 * t + 1) * LANES:(2 * t + 2) * LANES] = jnp.where(lane >= 64, r, 0.0).astype(BF16)
    wi = tq_ref[...] * wi_scale
    wcol = [jnp.sum(jnp.where(lane == 64 + h, wi, 0.0), axis=1, keepdims=True) for h in range(IDX_HEADS)]
    k_off = _iota((rb, kc), 1)

    def kc_body(j, _):
        ks = pl.multiple_of(j * kc, kc)
        kblk = ki_ref[pl.ds(ks, kc), :]
        for r0 in range(0, bq, rb):
            acc = jnp.zeros((rb, kc), F32)
            for h in range(IDX_HEADS):
                s = lax.dot_general(qi_ref[r0:r0 + rb, h * LANES:(h + 1) * LANES], kblk, _NT,
                                    preferred_element_type=F32)
                acc = acc + wcol[h][r0:r0 + rb] * jnp.maximum(s, 0.0)
            q_chunk = (qb * bq + r0 + _iota((rb, kc), 0)) // CHUNK
            acc = jnp.where((ks + k_off) // CHUNK <= q_chunk, acc, -jnp.inf)
            key_ref[pl.ds(ks, kc), r0:r0 + rb] = _order_key(acc.T)
        return 0

    lax.fori_loop(0, n_kc, kc_body, 0)

    def count(indicator):
        def body(j, acc):
            x = key_ref[pl.ds(pl.multiple_of(j * kc, kc), kc), :].reshape(g8, 8, bq)
            return acc + jnp.sum(indicator(x, j), axis=0)
        acc = lax.fori_loop(0, n_kc, body, jnp.zeros((8, bq), F32))
        return jnp.sum(acc, axis=0, keepdims=True)

    n_adm = ((qb * bq + _iota((1, bq), 1)) // CHUNK + 1) * CHUNK
    take_all = n_adm <= topk
    settled = lambda cnt_t: jnp.logical_or(take_all, cnt_t == topk)
    pending = lambda cnt_t: jnp.max(jnp.where(settled(cnt_t), 0, 1))

    def bit_cond(c):
        return jnp.logical_and(c[0] < 32, c[3] > 0)

    def bit_body(c):
        i, tu, cnt_t, _ = c
        for b in range(_BITS_PER_CHECK):
            cand_u = tu | lax.shift_left(jnp.int32(1), 31 - (i + b))
            cand_s = jnp.broadcast_to(cand_u ^ _MIN32, (8, bq))
            cnt = count(lambda x, j: jnp.where(x >= cand_s, 1.0, 0.0))
            ok = cnt >= topk
            cnt_t = jnp.where(ok, cnt, cnt_t)
            tu = jnp.where(ok, cand_u, tu)
        return i + _BITS_PER_CHECK, tu, cnt_t, pending(cnt_t)

    cnt0 = jnp.zeros((1, bq), F32) + (n_kc * kc).astype(F32)
    _, tu, cnt_ge, n_pending = lax.while_loop(
        bit_cond, bit_body, (jnp.int32(0), jnp.zeros((1, bq), I32), cnt0, pending(cnt0)))
    ts = tu ^ _MIN32
    thr = jnp.where(take_all, _KEY_NEG_INF + 1, ts)

    def no_ties():
        return thr - 1, jnp.zeros((1, bq), I32), jnp.full((1, bq), -1, I32)

    def with_ties():
        tied = jnp.logical_not(settled(cnt_ge))
        ts8 = jnp.broadcast_to(ts, (8, bq))
        cnt_gt = count(lambda x, j: jnp.where(x > ts8, 1.0, 0.0))
        need = topk - cnt_gt
        k_idx = _iota((g8, 8, bq), 0) * 8 + _iota((g8, 8, bq), 1)

        def j_body(i, j0):
            cand = j0 | lax.shift_left(jnp.int32(1), (seq.bit_length() - 2) - i)
            cand8 = jnp.broadcast_to(cand, (8, bq))
            cnt = count(lambda x, j: jnp.where(x == ts8, jnp.where(j * kc + k_idx < cand8, 1.0, 0.0), 0.0))
            return jnp.where(cnt < need, cand, j0)
        j0 = lax.fori_loop(0, seq.bit_length() - 1, j_body, jnp.zeros((1, bq), I32))
        return jnp.where(tied, ts, thr - 1), jnp.where(tied, ts, 0), jnp.where(tied, j0, -1)

    gt_thr, eq_val, jlim = lax.cond(n_pending > 0, with_ties, no_ties)

    def masked_body(j, _):
        bias_ref[:, pl.ds(pl.multiple_of(j * kc, kc), kc)] = jnp.full((bq, kc), NEG_BIG, BF16)
        return 0

    lax.fori_loop(n_kc, seq // kc, masked_body, 0)
    k_row = _iota((kc, rb), 0)

    def out_body(j, _):
        ks = pl.multiple_of(j * kc, kc)
        for r0 in range(0, bq, rb):
            qs = slice(r0, r0 + rb)
            x = key_ref[pl.ds(ks, kc), qs]
            tie_ok = jnp.where(ks + k_row <= jlim[:, qs], 0.0, NEG_BIG)
            b = jnp.where(x > gt_thr[:, qs], 0.0, jnp.where(x == eq_val[:, qs], tie_ok, NEG_BIG))
            bias_ref[qs, pl.ds(ks, kc)] = b.T.astype(BF16)
        return 0

    lax.fori_loop(0, n_kc, out_body, 0)


def dsa_index(main, tail, batch, seq):
    bq = min(seq, 512)
    nq = seq // bq
    kc = min(seq, 256)
    topk = min(TOPK_MAX, seq // 4)
    n_qi = IDX_HEADS * IDX_DIM
    qi_block = ((A_HEADS + 2 * A_KV_HEADS) * A_HEAD_DIM) // n_qi
    c64, s64 = _rope_tables(seq, IDX_DIM)
    tab_q = pl.BlockSpec((bq, LANES), lambda b, i: (i, 0))
    tab_k = pl.BlockSpec((seq, LANES), lambda b, i: (0, 0))
    return pl.pallas_call(
        functools.partial(_dsa_index_kernel, bq=bq, seq=seq, kc=kc, topk=topk,
                          wi_scale=IDX_HEADS ** -0.5 * IDX_DIM ** -0.5),
        name="dsa_index",
        grid=(batch, nq),
        in_specs=[pl.BlockSpec((bq, n_qi), lambda b, i: (b * nq + i, qi_block)),
                  pl.BlockSpec((bq, LANES), lambda b, i: (b * nq + i, 0)),
                  pl.BlockSpec((seq, LANES), lambda b, i: (b, 0)),
                  tab_q, tab_q, tab_k, tab_k],
        out_specs=pl.BlockSpec((bq, seq), lambda b, i: (b * nq + i, 0)),
        out_shape=jax.ShapeDtypeStruct((batch * seq, seq), BF16),
        scratch_shapes=[pltpu.VMEM((seq, bq), I32), pltpu.VMEM((bq, IDX_HEADS * LANES), BF16),
                        pltpu.VMEM((seq, LANES), BF16)],
        compiler_params=_params("parallel", "arbitrary"),
    )(main, tail, tail, c64, s64, c64, s64)


def _dsa_attn_kernel(qf_ref, kf_ref, vf_ref, cq_ref, sq_ref, ck_ref, sk_ref, bias_ref, o_ref,
                     q_ref, k_ref, v_ref, s0_ref, s1_ref, p_ref, m0_ref, m1_ref, acc_ref,
                     *, bq, bk, seq, group, strip, q_scale):
    qi = pl.program_id(2)
    rows = group * bq

    @pl.when(qi == 0)
    def _():
        for r0 in range(0, seq, bq):
            rs = slice(r0, r0 + bq)
            k_ref[rs, :] = _rope128(kf_ref[rs, :], ck_ref[rs, :], sk_ref[rs, :]).astype(BF16)
            v_ref[rs, :] = vf_ref[rs, :].astype(BF16)

    cq, sq = cq_ref[...] * q_scale, sq_ref[...] * q_scale
    for r in range(group):
        q_ref[r * bq:(r + 1) * bq, :] = _rope128(qf_ref[:, r * LANES:(r + 1) * LANES], cq, sq).astype(BF16)
    q = q_ref[...]
    m0_ref[...] = jnp.full((rows, LANES), NEG_BIG, F32)
    acc_ref[...] = jnp.zeros((rows, 2 * LANES), F32)
    even, odd = (s0_ref, m0_ref, m1_ref), (s1_ref, m1_ref, m0_ref)

    n_kv = ((qi + 1) * bq + bk - 1) // bk

    def scores(j, bufs):
        ks = pl.multiple_of(jnp.minimum(j, n_kv - 1) * bk, bk)
        bufs[0][...] = lax.dot_general(q, k_ref[pl.ds(ks, bk), :], _NT, preferred_element_type=F32)

    def chunk(j, bufs):
        s_ref, m_in_ref, m_out_ref = bufs
        ks = pl.multiple_of(j * bk, bk)
        for b0 in range(0, bq, strip):
            bias = bias_ref[b0:b0 + strip, pl.ds(ks, bk)].astype(F32)
            for r in range(group):
                r0 = r * bq + b0
                _softmax_strip(s_ref[r0:r0 + strip, :] + bias, r0, m_in_ref, m_out_ref, p_ref)
        _accumulate(acc_ref, m_in_ref, m_out_ref, p_ref, v_ref[pl.ds(ks, bk), :])

    scores(0, even)

    def pair(jp, _):
        scores(2 * jp + 1, odd)
        chunk(2 * jp, even)
        scores(2 * jp + 2, even)
        chunk(2 * jp + 1, odd)
        return 0

    lax.fori_loop(0, n_kv // 2, pair, 0)

    @pl.when(n_kv % 2 == 1)
    def _():
        chunk(n_kv - 1, even)

    for r in range(group):
        sl = slice(r * bq, (r + 1) * bq)
        o_ref[:, r * LANES:(r + 1) * LANES] = (acc_ref[sl, :LANES] / acc_ref[sl, LANES:]).astype(o_ref.dtype)


def dsa_attention(main, bias, batch, seq):
    group = A_HEADS // A_KV_HEADS
    bq = bk = min(seq, 256)
    nq = seq // bq
    gw = group * LANES
    c128, s128 = _rope_tables(seq, A_HEAD_DIM)
    tab_q = pl.BlockSpec((bq, LANES), lambda b, g, i: (i, 0))
    tab_k = pl.BlockSpec((seq, LANES), lambda b, g, i: (0, 0))
    return pl.pallas_call(
        functools.partial(_dsa_attn_kernel, bq=bq, bk=bk, seq=seq, group=group, strip=min(bq, 64),
                          q_scale=A_HEAD_DIM ** -0.5 * LOG2E),
        name="dsa_attn",
        grid=(batch, A_KV_HEADS, nq),
        in_specs=[pl.BlockSpec((bq, gw), lambda b, g, i: (b * nq + i, g)),
                  pl.BlockSpec((seq, LANES), lambda b, g, i: (b, A_HEADS + g)),
                  pl.BlockSpec((seq, LANES), lambda b, g, i: (b, A_HEADS + A_KV_HEADS + g)),
                  tab_q, tab_q, tab_k, tab_k,
                  pl.BlockSpec((bq, seq), lambda b, g, i: (b * nq + i, 0))],
        out_specs=pl.BlockSpec((bq, gw), lambda b, g, i: (b * nq + i, g)),
        out_shape=jax.ShapeDtypeStruct((batch * seq, A_HEADS * LANES), BF16),
        scratch_shapes=[pltpu.VMEM((group * bq, LANES), BF16), pltpu.VMEM((seq, LANES), BF16),
                        pltpu.VMEM((seq, LANES), BF16),
                        pltpu.VMEM((group * bq, bk), F32), pltpu.VMEM((group * bq, bk), F32),
                        pltpu.VMEM((group * bq, bk), BF16),
                        pltpu.VMEM((group * bq, LANES), F32), pltpu.VMEM((group * bq, LANES), F32),
                        pltpu.VMEM((group * bq, 2 * LANES), F32)],
        compiler_params=_params("parallel", "parallel", "arbitrary"),
    )(main, main, main, c128, s128, c128, s128, bias)


def dsa_layer(x, g_norm, w_in, w_o, batch, seq):
    hn = rmsnorm(x, g_norm, BF16)
    n_main = (A_HEADS + 2 * A_KV_HEADS) * A_HEAD_DIM + IDX_HEADS * IDX_DIM
    n_tail = w_in.shape[1] - n_main
    main = matmul(hn, w_in[:, :n_main].astype(BF16), F32)
    w_tail = jnp.pad(w_in[:, n_main:], ((0, 0), (0, LANES - n_tail))).astype(BF16)
    tail = matmul(hn, w_tail, F32)
    bias = dsa_index(main, tail, batch, seq)
    o = dsa_attention(main, bias, batch, seq)
    return matmul(o, w_o.astype(BF16), F32, residual=x)


def _rwkv_mix_kernel(x_ref, g_ref, mix_ref, *rest, bs, rs):
    outs, hbuf = rest[:6], rest[6]
    j = pl.program_id(1)

    @pl.when(j == 0)
    def _():
        hbuf[0:8, :] = jnp.zeros((8, D_MODEL), F32)

    g = g_ref[...]
    for r0 in range(0, bs, rs):
        hn = _rms(x_ref[r0:r0 + rs, :], g)
        hbuf[8 + r0:8 + r0 + rs, :] = hn
        xx = hbuf[7 + r0:7 + r0 + rs, :] - hn
        for i in range(6):
            outs[i][r0:r0 + rs, :] = (hn + xx * mix_ref[i:i + 1, :]).astype(BF16)
    hbuf[7:8, :] = hbuf[7 + bs:8 + bs, :]


def rwkv_mix(x, g_norm, mix, batch, seq):
    bs = min(seq, 256)
    nb = seq // bs
    m, d = x.shape
    mix8 = jnp.pad(mix, ((0, 2), (0, 0)))
    return pl.pallas_call(
        functools.partial(_rwkv_mix_kernel, bs=bs, rs=min(bs, 32)),
        name="rwkv_mix",
        grid=(batch, nb),
        in_specs=[pl.BlockSpec((bs, d), lambda b, j: (b * nb + j, 0)),
                  pl.BlockSpec((1, d), lambda b, j: (0, 0)),
                  pl.BlockSpec((8, d), lambda b, j: (0, 0))],
        out_specs=[pl.BlockSpec((bs, d), lambda b, j: (b * nb + j, 0))] * 6,
        out_shape=[jax.ShapeDtypeStruct((m, d), BF16)] * 6,
        scratch_shapes=[pltpu.VMEM((bs + 8, d), F32)],
        compiler_params=_params("parallel", "arbitrary"),
    )(x, g_norm.reshape(1, d), mix8)


def _head_sum_matrix():
    return ((_iota((LANES, LANES), 0) // C_HEAD_DIM) == (_iota((LANES, LANES), 1) // C_HEAD_DIM)).astype(F32)


def _head_sum_2x(x, p):
    hi = x.astype(BF16)
    mid = (x - hi.astype(F32)).astype(BF16)
    pb = p.astype(BF16)
    return jnp.dot(hi, pb, preferred_element_type=F32) + jnp.dot(mid, pb, preferred_element_type=F32)


def _rwkv_rec_kernel(r_ref, k_ref, v_ref, wl_ref, al_ref, w0_ref, a0_ref, kkp_ref, ka_ref, y_ref, km_ref, st_ref,
                     *, tc, npair, mm_dtype):
    c = pl.program_id(2)

    @pl.when(c == 0)
    def _():
        st_ref[...] = jnp.zeros(st_ref.shape, F32)

    def mm(a, b, dims=None):
        a = a.astype(mm_dtype)
        b = b.astype(mm_dtype)
        prec = HIGHEST if mm_dtype == F32 else None
        if dims is None:
            return jnp.dot(a, b, precision=prec, preferred_element_type=F32)
        return lax.dot_general(a, b, dims, precision=prec, preferred_element_type=F32)

    row = _iota((tc, tc), 0)
    col = _iota((tc, tc), 1)
    tri_incl = (row >= col).astype(F32)
    strict = row > col
    incl = row >= col
    eye = (row == col).astype(F32)
    lane = _iota((tc, LANES), 1)
    head_masks = [lane < C_HEAD_DIM, lane >= C_HEAD_DIM]
    blockdiag = (_iota((LANES, LANES), 0) // C_HEAD_DIM) == (_iota((LANES, LANES), 1) // C_HEAD_DIM)
    n_sq = max((tc - 1).bit_length() - 1, 0)

    pairs = range(npair)
    chains = [(p, h) for p in pairs for h in range(2)]
    tile = lambda a, p: a[:, p * LANES:(p + 1) * LANES]
    cast = lambda a: a.astype(mm_dtype)

    r, k_raw, v = r_ref[...].astype(F32), k_ref[...].astype(F32), v_ref[...].astype(F32)
    lw = (-0.6065306597126334) * jax.nn.sigmoid(w0_ref[...] + wl_ref[...])
    a = jax.nn.sigmoid(a0_ref[...] + al_ref[...])
    kr = k_raw * kkp_ref[...]
    hsum = _head_sum_matrix()
    ss = jnp.concatenate([_head_sum_2x(tile(kr, p) * tile(kr, p), hsum) for p in pairs], axis=1)
    kk = kr * lax.rsqrt(jnp.maximum(ss, 1e-24))
    k = k_raw * (1.0 + (a - 1.0) * ka_ref[...])
    b = kk * a
    km_ref[...] = k
    cum = jnp.dot(tri_incl, lw, precision=HIGHEST, preferred_element_type=F32)
    g_in = jnp.exp(cum)
    g_inv = jnp.exp(-cum)
    a_t = -kk * jnp.exp(cum - lw)
    b_t = b * g_inv
    k_t = k * g_inv
    r_t = r * g_in
    g_end = g_in[tc - 1:tc, :]
    b_c, k_c, v_c, r_c = cast(b_t), cast(k_t), cast(v), cast(r_t)
    bk_x = cast(jnp.concatenate([b_t * g_end, k_t * g_end], axis=0).T)
    g_col = g_in.T[:, tc - 1:tc]
    rows = lambda a, p: a[p * LANES:(p + 1) * LANES, :]

    a_m = {(p, h): cast(jnp.where(head_masks[h], tile(a_t, p), 0.0)) for p, h in chains}
    left = {(p, h): jnp.concatenate([a_m[p, h], cast(jnp.where(head_masks[h], tile(r_t, p), 0.0))], axis=0)
            for p, h in chains}
    gb = {c_: mm(left[c_], tile(b_c, c_[0]), _NT) for c_ in chains}
    gk = {c_: mm(left[c_], tile(k_c, c_[0]), _NT) for c_ in chains}
    a_ab = {c_: jnp.where(strict, gb[c_][:tc], 0.0) for c_ in chains}
    a_rb = {c_: cast(jnp.where(incl, gb[c_][tc:], 0.0)) for c_ in chains}
    low2 = jnp.concatenate([strict, incl], axis=0)
    kv = {c_: mm(cast(jnp.where(low2, gk[c_], 0.0)), tile(v_c, c_[0])) for c_ in chains}
    akv = {c_: cast(kv[c_][:tc]) for c_ in chains}
    y0 = {c_: kv[c_][tc:] for c_ in chains}
    x = {c_: eye + a_ab[c_] for c_ in chains}
    pw = {c_: cast(a_ab[c_]) for c_ in chains}
    for _ in range(n_sq):
        pw = {c_: cast(mm(pw[c_], pw[c_])) for c_ in chains}
        x = {c_: x[c_] + mm(pw[c_], cast(x[c_])) for c_ in chains}
    x = {c_: cast(x[c_]) for c_ in chains}
    xw = {c_: mm(x[c_], jnp.concatenate([a_m[c_], akv[c_]], axis=1)) for c_ in chains}
    w_h = {c_: xw[c_][:, :LANES] for c_ in chains}
    u_h = {c_: xw[c_][:, LANES:] for c_ in chains}

    s0 = {p: st_ref[p] for p in pairs}
    ws = {p: mm(jnp.concatenate([cast(w_h[p, 0] + w_h[p, 1]), tile(r_c, p)], axis=0), cast(s0[p]))
          for p in pairs}
    u = {p: ws[p][:tc] + jnp.where(head_masks[0], u_h[p, 0], u_h[p, 1]) for p in pairs}
    u_c = {p: cast(u[p]) for p in pairs}
    yb = {c_: mm(a_rb[c_], u_c[c_[0]]) for c_ in chains}
    upd = {p: mm(rows(bk_x, p), jnp.concatenate([u_c[p], tile(v_c, p)], axis=0)) for p in pairs}
    for p in pairs:
        y = ws[p][tc:] + jnp.where(head_masks[0], y0[p, 0] + yb[p, 0], y0[p, 1] + yb[p, 1])
        y_ref[:, p * LANES:(p + 1) * LANES] = y
        st_ref[p] = s0[p] * rows(g_col, p) + jnp.where(blockdiag, upd[p], 0.0)


def rwkv_rec(r, k, v, wl, al, w0, a0, k_k, k_a, batch, seq, mm_dtype=BF16):
    m, d = r.shape
    tc = min(seq, 64)
    nc = seq // tc
    npair = 16
    gw = npair * LANES
    blk = pl.BlockSpec((tc, gw), lambda bi, g, c: (bi * nc + c, g))
    row = pl.BlockSpec((1, gw), lambda bi, g, c: (0, g))
    r2 = lambda a: a.reshape(1, d).astype(F32)
    return pl.pallas_call(
        functools.partial(_rwkv_rec_kernel, tc=tc, npair=npair, mm_dtype=mm_dtype),
        name="rwkv_rec",
        grid=(batch, d // gw, nc),
        in_specs=[blk] * 5 + [row] * 4,
        out_specs=[blk, blk],
        out_shape=[jax.ShapeDtypeStruct((m, d), F32)] * 2,
        scratch_shapes=[pltpu.VMEM((npair, LANES, LANES), F32)],
        compiler_params=_params("parallel", "parallel", "arbitrary"),
    )(r, k, v, wl, al, r2(w0), r2(a0), r2(k_k), r2(k_a))


def _rwkv_post_kernel(y_ref, r_ref, k_ref, v_ref, g_ref, gng_ref, gnb_ref, rk_ref, o_ref):
    p = _head_sum_matrix()
    inv_n = 1.0 / C_HEAD_DIM
    for t in range(D_MODEL // LANES):
        sl = slice(t * LANES, (t + 1) * LANES)
        y = y_ref[:, sl]
        mu = _head_sum_2x(y, p) * inv_n
        yc = y - mu
        var = _head_sum_2x(yc * yc, p) * inv_n
        yn = yc * lax.rsqrt(var + C_GN_EPS) * gng_ref[:, sl] + gnb_ref[:, sl]
        bonus = (_head_sum_2x(r_ref[:, sl].astype(F32) * k_ref[:, sl] * rk_ref[:, sl], p)
                 * v_ref[:, sl].astype(F32))
        o_ref[:, sl] = ((yn + bonus) * g_ref[:, sl]).astype(o_ref.dtype)


def rwkv_post(y, r, kmod, v, g, gn_g, gn_b, r_k):
    m, d = y.shape
    bm = _largest_divisor(m, (256, 128, 64))
    big = pl.BlockSpec((bm, d), lambda i: (i, 0))
    row = pl.BlockSpec((1, d), lambda i: (0, 0))
    r2 = lambda a: a.reshape(1, d).astype(F32)
    return pl.pallas_call(
        _rwkv_post_kernel,
        name="rwkv_post",
        grid=(m // bm,),
        in_specs=[big] * 5 + [row] * 3,
        out_specs=big,
        out_shape=jax.ShapeDtypeStruct((m, d), BF16),
        compiler_params=_params("parallel"),
    )(y, r, kmod, v, g, r2(gn_g), r2(gn_b), r2(r_k))


def rwkv_layer(x, g_norm, mix, w0, w1, w2, a0, a1, a2, g1, g2, k_k, k_a, r_k, w_rkv, w_o, gn_g, gn_b,
               batch, seq):
    bf = lambda a: a.astype(BF16)
    xr, xk, xv, xw, xa, xg = rwkv_mix(x, g_norm, mix, batch, seq)
    r = matmul(xr, bf(w_rkv[0]), BF16)
    k = matmul(xk, bf(w_rkv[1]), BF16)
    v = matmul(xv, bf(w_rkv[2]), BF16)
    wl = lora(xw, bf(w1), bf(w2), "tanh")
    al = lora(xa, bf(a1), bf(a2), "none")
    g = lora(xg, bf(g1), bf(g2), "sigmoid")
    y, kmod = rwkv_rec(r, k, v, wl, al, w0, a0, k_k, k_a, batch, seq)
    o = rwkv_post(y, r, kmod, v, g, gn_g, gn_b, r_k)
    return matmul(o, bf(w_o), F32, residual=x)


def kernel(x, norm_mix, norm_ffn, norm_final, ffn_gate, ffn_up, ffn_down, a_w_in, a_w_o, b_w_in, b_f_bias,
           b_w_o, c_mix, c_w0, c_w1, c_w2, c_a0, c_a1, c_a2, c_g1, c_g2, c_k_k, c_k_a, c_r_k, c_w_rkv,
           c_w_o, c_gn_g, c_gn_b):
    batch, seq, d = x.shape
    depth = norm_mix.shape[0]
    h = x.reshape(batch * seq, d)
    for i in range(depth):
        kind, j = i % 3, i // 3
        if kind == 0:
            h = dsa_layer(h, norm_mix[i], a_w_in[j], a_w_o[j], batch, seq)
        elif kind == 1:
            h = fox_layer(h, norm_mix[i], b_w_in[j], b_f_bias[j], b_w_o[j], batch, seq)
        else:
            h = rwkv_layer(h, norm_mix[i], c_mix[j], c_w0[j], c_w1[j], c_w2[j], c_a0[j], c_a1[j], c_a2[j],
                           c_g1[j], c_g2[j], c_k_k[j], c_k_a[j], c_r_k[j], c_w_rkv[j], c_w_o[j],
                           c_gn_g[j], c_gn_b[j], batch, seq)
        h = swiglu_layer(h, norm_ffn[i], ffn_gate, ffn_up, i, ffn_down[i])
    return rmsnorm(h, norm_final, x.dtype).reshape(batch, seq, d)
```

```python
import functools

import jax
import jax.numpy as jnp
from jax import lax
from jax.experimental import pallas as pl
from jax.experimental.pallas import tpu as pltpu

F32 = jnp.float32
BF16 = jnp.bfloat16
I32 = jnp.int32
HIGHEST = lax.Precision.HIGHEST

D_MODEL = 2048
CHUNK = 64
RMS_EPS = 1e-6
ROPE_THETA = 10000.0
A_HEADS, A_KV_HEADS, A_HEAD_DIM = 16, 4, 128
IDX_HEADS, IDX_DIM, TOPK_MAX = 16, 64, 256
B_HEADS, B_HEAD_DIM = 16, 128
C_HEAD_DIM = 64
C_GN_EPS = C_HEAD_DIM * 1e-5

LANES = 128
VMEM_LIMIT_BYTES = 56 * 1024 * 1024

NEG_BIG = -1e30
LOG2E = 1.4426950408889634
_NT = (((1,), (1,)), ((), ()))


def _params(*sem):
    return pltpu.CompilerParams(dimension_semantics=sem, vmem_limit_bytes=VMEM_LIMIT_BYTES)


def _iota(shape, dim):
    return lax.broadcasted_iota(I32, shape, dim)


def _rms(x, g):
    return x * lax.rsqrt(jnp.mean(x * x, axis=-1, keepdims=True) + RMS_EPS) * g


def _rmsnorm_kernel(x_ref, g_ref, o_ref):
    o_ref[...] = _rms(x_ref[...], g_ref[...]).astype(o_ref.dtype)


def rmsnorm(x, g, out_dtype):
    m, d = x.shape
    bm = min(m, 512)
    return pl.pallas_call(
        _rmsnorm_kernel,
        name="rmsnorm",
        grid=(m // bm,),
        in_specs=[pl.BlockSpec((bm, d), lambda i: (i, 0)), pl.BlockSpec((1, d), lambda i: (0, 0))],
        out_specs=pl.BlockSpec((bm, d), lambda i: (i, 0)),
        out_shape=jax.ShapeDtypeStruct((m, d), out_dtype),
        compiler_params=_params("parallel"),
    )(x, g.reshape(1, d))


def _largest_divisor(n, candidates):
    for c in candidates:
        if n % c == 0:
            return c
    return n


def _mm_kernel(x_ref, w_ref, o_ref):
    o_ref[...] = jnp.dot(x_ref[...], w_ref[...], preferred_element_type=F32).astype(o_ref.dtype)


def _mm_res_kernel(x_ref, w_ref, r_ref, o_ref):
    acc = jnp.dot(x_ref[...], w_ref[...], preferred_element_type=F32)
    o_ref[...] = (r_ref[...] + acc).astype(o_ref.dtype)


def matmul(x, w, out_dtype, residual=None):
    m, k = x.shape
    n = w.shape[1]
    bm = _largest_divisor(m, (1024, 512, 256, 128))
    bn = _largest_divisor(n, (1024, 512, 256, 128)) if k <= 2048 else _largest_divisor(n, (512, 256, 128))
    in_specs = [pl.BlockSpec((bm, k), lambda i, j: (i, 0)), pl.BlockSpec((k, bn), lambda i, j: (0, j))]
    args = [x, w]
    body = _mm_kernel
    if residual is not None:
        in_specs.append(pl.BlockSpec((bm, bn), lambda i, j: (i, j)))
        args.append(residual)
        body = _mm_res_kernel
    return pl.pallas_call(
        body,
        name=f"mm_{k}x{n}",
        grid=(m // bm, n // bn),
        in_specs=in_specs,
        out_specs=pl.BlockSpec((bm, bn), lambda i, j: (i, j)),
        out_shape=jax.ShapeDtypeStruct((m, n), out_dtype),
        compiler_params=_params("parallel", "parallel"),
    )(*args)


def _gateup_kernel(x_ref, wg_ref, wu_ref, o_ref, wgb_ref, wub_ref):
    @pl.when(pl.program_id(1) == 0)
    def _():
        wgb_ref[...] = wg_ref[...].astype(BF16)
        wub_ref[...] = wu_ref[...].astype(BF16)

    x = x_ref[...]
    g = jnp.dot(x, wgb_ref[...], preferred_element_type=F32)
    u = jnp.dot(x, wub_ref[...], preferred_element_type=F32)
    o_ref[...] = (g * jax.nn.sigmoid(g) * u).astype(o_ref.dtype)


def gate_up(x, wg, wu, layer):
    m, k = x.shape
    n = wg.shape[2]
    bm = _largest_divisor(m, (1024, 512, 256, 128))
    bn = _largest_divisor(n, (512, 256, 128))
    return pl.pallas_call(
        _gateup_kernel,
        name="gate_up",
        grid=(n // bn, m // bm),
        in_specs=[pl.BlockSpec((bm, k), lambda j, i: (i, 0)),
                  pl.BlockSpec((None, k, bn), lambda j, i: (layer, 0, j)),
                  pl.BlockSpec((None, k, bn), lambda j, i: (layer, 0, j))],
        out_specs=pl.BlockSpec((bm, bn), lambda j, i: (i, j)),
        out_shape=jax.ShapeDtypeStruct((m, n), BF16),
        scratch_shapes=[pltpu.VMEM((k, bn), BF16), pltpu.VMEM((k, bn), BF16)],
        compiler_params=_params("parallel", "arbitrary"),
    )(x, wg, wu)


def _lora_kernel(x_ref, w1_ref, w2_ref, o_ref, *, act):
    t = jnp.dot(x_ref[...], w1_ref[...], preferred_element_type=F32)
    if act == "tanh":
        t = jnp.tanh(t)
    elif act == "sigmoid":
        t = jax.nn.sigmoid(t)
    o_ref[...] = jnp.dot(t.astype(BF16), w2_ref[...], preferred_element_type=F32)


def lora(x, w1, w2, act):
    m, k = x.shape
    r = w1.shape[1]
    rp = -(-r // LANES) * LANES
    if act == "sigmoid":
        assert rp == r, "sigmoid(0) != 0: the rank must not be padded"
    w1 = jnp.pad(w1, ((0, 0), (0, rp - r)))
    w2 = jnp.pad(w2, ((0, rp - r), (0, 0)))
    n = w2.shape[1]
    bm = _largest_divisor(m, (1024, 512, 256, 128))
    return pl.pallas_call(
        functools.partial(_lora_kernel, act=act),
        name="lora_" + act,
        grid=(m // bm,),
        in_specs=[pl.BlockSpec((bm, k), lambda i: (i, 0)),
                  pl.BlockSpec((k, rp), lambda i: (0, 0)),
                  pl.BlockSpec((rp, n), lambda i: (0, 0))],
        out_specs=pl.BlockSpec((bm, n), lambda i: (i, 0)),
        out_shape=jax.ShapeDtypeStruct((m, n), F32),
        compiler_params=_params("parallel"),
    )(x, w1, w2)


def swiglu_layer(x, g_norm, w_gate_all, w_up_all, layer, w_down):
    hn = rmsnorm(x, g_norm, BF16)
    h = gate_up(hn, w_gate_all, w_up_all, layer)
    return matmul(h, w_down.astype(BF16), F32, residual=x)


def _log_sigmoid(x):
    return jnp.minimum(x, 0.0) - jnp.log1p(jnp.exp(-jnp.abs(x)))


def _fox_prep_kernel(fl_ref, bias_ref, c_ref, ct_ref, *, seq, ch):
    tri = (_iota((ch, ch), 0) >= _iota((ch, ch), 1)).astype(F32)
    carry = jnp.zeros((1, LANES), F32)
    for c in range(seq // ch):
        lf = _log_sigmoid(fl_ref[c * ch:(c + 1) * ch, :] + bias_ref[...])
        cs = jnp.dot(tri, lf, precision=HIGHEST, preferred_element_type=F32) + carry
        c2 = cs * LOG2E
        c_ref[c * ch:(c + 1) * ch, :] = c2
        ct_ref[:, c * ch:(c + 1) * ch] = c2.T
        carry = cs[ch - 1:ch, :]


def fox_prep(fl, bias_pad, batch, seq):
    ch = min(seq, 256)
    return pl.pallas_call(
        functools.partial(_fox_prep_kernel, seq=seq, ch=ch),
        name="fox_prep",
        grid=(batch,),
        in_specs=[pl.BlockSpec((seq, LANES), lambda b: (b, 0)), pl.BlockSpec((1, LANES), lambda b: (0, 0))],
        out_specs=[pl.BlockSpec((seq, LANES), lambda b: (b, 0)),
                   pl.BlockSpec((None, LANES, seq), lambda b: (b, 0, 0))],
        out_shape=[jax.ShapeDtypeStruct((batch * seq, LANES), F32),
                   jax.ShapeDtypeStruct((batch, LANES, seq), F32)],
        compiler_params=_params("parallel"),
    )(fl, bias_pad)


def _softmax_strip(z, r0, m_in_ref, m_out_ref, p_ref, row_term=None):
    rows, bk = z.shape
    tiles = [z[:, t * LANES:(t + 1) * LANES] for t in range(bk // LANES)]
    zmax = functools.reduce(jnp.maximum, tiles)
    zmax = jnp.broadcast_to(jnp.max(zmax, axis=1, keepdims=True), (rows, LANES))
    if row_term is not None:
        zmax = zmax + row_term
    m_new = jnp.maximum(m_in_ref[r0:r0 + rows, :], zmax)
    m_out_ref[r0:r0 + rows, :] = m_new
    shift = m_new if row_term is None else m_new - row_term
    for t, zt in enumerate(tiles):
        p_ref[r0:r0 + rows, t * LANES:(t + 1) * LANES] = jnp.exp2(zt - shift).astype(BF16)


def _accumulate(acc_ref, m_in_ref, m_out_ref, p_ref, v, row_lo=0):
    alpha = jnp.exp2(m_in_ref[row_lo:, :] - m_out_ref[row_lo:, :])
    v1 = jnp.concatenate([v, jnp.ones(v.shape, v.dtype)], axis=1)
    pv = jnp.dot(p_ref[row_lo:, :], v1, preferred_element_type=F32)
    for t in range(2):
        sl = slice(t * LANES, (t + 1) * LANES)
        acc_ref[row_lo:, sl] = alpha * acc_ref[row_lo:, sl] + pv[:, sl]


def _fox_attn_kernel(q_ref, k_ref, v_ref, c_ref, ck_ref, o_ref, s0_ref, s1_ref, p_ref, m0_ref, m1_ref, acc_ref,
                     *, bq, bk, strip):
    h = pl.program_id(1)
    qi = pl.program_id(2)
    q = q_ref[...]
    cq = jnp.sum(jnp.where(_iota((bq, LANES), 1) == h, c_ref[...], 0.0), axis=1, keepdims=True)
    cq = jnp.broadcast_to(cq, (bq, LANES))
    m0_ref[...] = jnp.full((bq, LANES), NEG_BIG, F32)
    acc_ref[...] = jnp.zeros((bq, 2 * LANES), F32)
    even, odd = (s0_ref, m0_ref, m1_ref), (s1_ref, m1_ref, m0_ref)

    def scores(t, bufs, row_lo=0):
        ks = pl.multiple_of(t * bk, bk)
        bufs[0][row_lo:, :] = lax.dot_general(q[row_lo:], k_ref[pl.ds(ks, bk), :], _NT,
                                              preferred_element_type=F32)

    def chunk(t, bufs, row_lo=0, diag_off=None):
        s_ref, m_in_ref, m_out_ref = bufs
        ks = pl.multiple_of(t * bk, bk)
        ck = ck_ref[:, pl.ds(ks, bk)]
        for r0 in range(row_lo, bq, strip):
            z = s_ref[r0:r0 + strip, :] - ck
            if diag_off is not None and diag_off + bk - 1 > r0:
                visible = diag_off + _iota((strip, bk), 1) <= r0 + _iota((strip, bk), 0)
                z = jnp.where(visible, z, NEG_BIG)
            _softmax_strip(z, r0, m_in_ref, m_out_ref, p_ref, row_term=cq[r0:r0 + strip])
        _accumulate(acc_ref, m_in_ref, m_out_ref, p_ref, v_ref[pl.ds(ks, bk), :], row_lo)

    n_diag = bq // bk
    assert n_diag % 2 == 0
    scores(0, even)

    def pair(jp, _):
        scores(2 * jp + 1, odd)
        chunk(2 * jp, even)
        scores(2 * jp + 2, even)
        chunk(2 * jp + 1, odd)
        return 0

    lax.fori_loop(0, qi * (n_diag // 2), pair, 0)
    bufs = (even, odd)
    for d in range(n_diag):
        if d + 1 < n_diag:
            scores(qi * n_diag + d + 1, bufs[(d + 1) % 2], row_lo=(d + 1) * bk)
        chunk(qi * n_diag + d, bufs[d % 2], row_lo=d * bk, diag_off=d * bk)
    o_ref[...] = (acc_ref[:, :LANES] / acc_ref[:, LANES:]).astype(o_ref.dtype)


def fox_attention(qkv, c, ck4, batch, seq):
    nh = B_HEADS
    bq = min(seq, 1024)
    bk = min(seq // 2, 512)
    nq = seq // bq
    return pl.pallas_call(
        functools.partial(_fox_attn_kernel, bq=bq, bk=bk, strip=min(bq, 64)),
        name="fox_attn",
        grid=(batch, nh, nq),
        in_specs=[pl.BlockSpec((bq, LANES), lambda b, h, i: (b * nq + i, h)),
                  pl.BlockSpec((seq, LANES), lambda b, h, i: (b, nh + h)),
                  pl.BlockSpec((seq, LANES), lambda b, h, i: (b, 2 * nh + h)),
                  pl.BlockSpec((bq, LANES), lambda b, h, i: (b * nq + i, 0)),
                  pl.BlockSpec((None, None, 1, seq), lambda b, h, i: (b, h, 0, 0))],
        out_specs=pl.BlockSpec((bq, LANES), lambda b, h, i: (b * nq + i, h)),
        out_shape=jax.ShapeDtypeStruct((batch * seq, nh * LANES), BF16),
        scratch_shapes=[pltpu.VMEM((bq, bk), F32), pltpu.VMEM((bq, bk), F32), pltpu.VMEM((bq, bk), BF16),
                        pltpu.VMEM((bq, LANES), F32), pltpu.VMEM((bq, LANES), F32),
                        pltpu.VMEM((bq, 2 * LANES), F32)],
        compiler_params=_params("parallel", "parallel", "parallel"),
    )(qkv, qkv, qkv, c, ck4)


def fox_layer(x, g_norm, w_in, f_bias, w_o, batch, seq):
    nh = B_HEADS
    hn = rmsnorm(x, g_norm, BF16)
    n_q, n_qkv = nh * B_HEAD_DIM, 3 * nh * B_HEAD_DIM
    col_scale = jnp.where(jnp.arange(n_qkv) < n_q, B_HEAD_DIM ** -0.5 * LOG2E, 1.0).astype(F32)
    qkv = matmul(hn, (w_in[:, :n_qkv] * col_scale).astype(BF16), BF16)
    w_f = jnp.pad(w_in[:, n_qkv:], ((0, 0), (0, LANES - nh))).astype(BF16)
    fl = matmul(hn, w_f, F32)
    bias_pad = jnp.pad(f_bias.astype(F32), (0, LANES - nh)).reshape(1, LANES)
    c, ct = fox_prep(fl, bias_pad, batch, seq)
    ck4 = ct[:, :nh, :].reshape(batch, nh, 1, seq)
    o = fox_attention(qkv, c, ck4, batch, seq)
    return matmul(o, w_o.astype(BF16), F32, residual=x)


def _rope_tables(seq, head_dim):
    half = head_dim // 2
    inv = ROPE_THETA ** (-jnp.arange(half, dtype=F32) / half)
    ang = jnp.arange(seq, dtype=F32)[:, None] * inv[None, :]
    cos = jnp.cos(ang)
    sin = jnp.sin(ang)
    reps = LANES // head_dim
    cos_t = jnp.tile(jnp.concatenate([cos, cos], axis=1), (1, reps))
    sin_t = jnp.tile(jnp.concatenate([-sin, sin], axis=1), (1, reps))
    return cos_t, sin_t


def _rope128(x, cos, sin):
    return x * cos + pltpu.roll(x, 64, 1) * sin


def _rope64(x, cos, sin, lane):
    partner = jnp.where((lane % 64) < 32, pltpu.roll(x, 96, 1), pltpu.roll(x, 32, 1))
    return x * cos + partner * sin


_MIN32 = -2 ** 31
_BITS_PER_CHECK = 4
_KEY_NEG_INF = (0xFF800000 - 2 ** 32) ^ 0x7FFFFFFF


def _order_key(s):
    b = pltpu.bitcast(s, I32)
    return jnp.where(b >= 0, b, b ^ 0x7FFFFFFF)


def _dsa_index_kernel(qf_ref, tq_ref, tk_ref, cq_ref, sq_ref, ck_ref, sk_ref, bias_ref, key_ref, qi_ref, ki_ref,
                      *, bq, seq, kc, topk, wi_scale):
    qb = pl.program_id(1)
    n_kc = ((qb + 1) * bq + kc - 1) // kc
    g8 = kc // 8
    rb = min(bq, 256)
    lane = _iota((bq, LANES), 1)

    @pl.when(qb == 0)
    def _():
        for r0 in range(0, seq, rb):
            rs = slice(r0, r0 + rb)
            lane_k = _iota((rb, LANES), 1)
            kr = _rope64(tk_ref[rs, :], ck_ref[rs, :], sk_ref[rs, :], lane_k)
            ki_ref[rs, :] = jnp.where(lane_k < 64, kr, pltpu.roll(kr, 64, 1)).astype(BF16)

    c64, s64 = cq_ref[...], sq_ref[...]
    for t in range(IDX_HEADS // 2):
        r = _rope64(qf_ref[:, t * LANES:(t + 1) * LANES], c64, s64, lane)
        qi_ref[:, (2 * t) * LANES:(2 * t + 1) * LANES] = jnp.where(lane < 64, r, 0.0).astype(BF16)
        qi_ref[:, (2 * t + 1) * LANES:(2 * t + 2) * LANES] = jnp.where(lane >= 64, r, 0.0).astype(BF16)
    wi = tq_ref[...] * wi_scale
    wcol = [jnp.sum(jnp.where(lane == 64 + h, wi, 0.0), axis=1, keepdims=True) for h in range(IDX_HEADS)]
    k_off = _iota((rb, kc), 1)

    def kc_body(j, _):
        ks = pl.multiple_of(j * kc, kc)
        kblk = ki_ref[pl.ds(ks, kc), :]
        for r0 in range(0, bq, rb):
            acc = jnp.zeros((rb, kc), F32)
            for h in range(IDX_HEADS):
                s = lax.dot_general(qi_ref[r0:r0 + rb, h * LANES:(h + 1) * LANES], kblk, _NT,
                                    preferred_element_type=F32)
                acc = acc + wcol[h][r0:r0 + rb] * jnp.maximum(s, 0.0)
            q_chunk = (qb * bq + r0 + _iota((rb, kc), 0)) // CHUNK
            acc = jnp.where((ks + k_off) // CHUNK <= q_chunk, acc, -jnp.inf)
            key_ref[pl.ds(ks, kc), r0:r0 + rb] = _order_key(acc.T)
        return 0

    lax.fori_loop(0, n_kc, kc_body, 0)

    def count(indicator):
        def body(j, acc):
            x = key_ref[pl.ds(pl.multiple_of(j * kc, kc), kc), :].reshape(g8, 8, bq)
            return acc + jnp.sum(indicator(x, j), axis=0)
        acc = lax.fori_loop(0, n_kc, body, jnp.zeros((8, bq), F32))
        return jnp.sum(acc, axis=0, keepdims=True)

    n_adm = ((qb * bq + _iota((1, bq), 1)) // CHUNK + 1) * CHUNK
    take_all = n_adm <= topk
    settled = lambda cnt_t: jnp.logical_or(take_all, cnt_t == topk)
    pending = lambda cnt_t: jnp.max(jnp.where(settled(cnt_t), 0, 1))

    def bit_cond(c):
        return jnp.logical_and(c[0] < 32, c[3] > 0)

    def bit_body(c):
        i, tu, cnt_t, _ = c
        for b in range(_BITS_PER_CHECK):
            cand_u = tu | lax.shift_left(jnp.int32(1), 31 - (i + b))
            cand_s = jnp.broadcast_to(cand_u ^ _MIN32, (8, bq))
            cnt = count(lambda x, j: jnp.where(x >= cand_s, 1.0, 0.0))
            ok = cnt >= topk
            cnt_t = jnp.where(ok, cnt, cnt_t)
            tu = jnp.where(ok, cand_u, tu)
        return i + _BITS_PER_CHECK, tu, cnt_t, pending(cnt_t)

    cnt0 = jnp.zeros((1, bq), F32) + (n_kc * kc).astype(F32)
    _, tu, cnt_ge, n_pending = lax.while_loop(
        bit_cond, bit_body, (jnp.int32(0), jnp.zeros((1, bq), I32), cnt0, pending(cnt0)))
    ts = tu ^ _MIN32
    thr = jnp.where(take_all, _KEY_NEG_INF + 1, ts)

    def no_ties():
        return thr - 1, jnp.zeros((1, bq), I32), jnp.full((1, bq), -1, I32)

    def with_ties():
        tied = jnp.logical_not(settled(cnt_ge))
        ts8 = jnp.broadcast_to(ts, (8, bq))
        cnt_gt = count(lambda x, j: jnp.where(x > ts8, 1.0, 0.0))
        need = topk - cnt_gt
        k_idx = _iota((g8, 8, bq), 0) * 8 + _iota((g8, 8, bq), 1)

        def j_body(i, j0):
            cand = j0 | lax.shift_left(jnp.int32(1), (seq.bit_length() - 2) - i)
            cand8 = jnp.broadcast_to(cand, (8, bq))
            cnt = count(lambda x, j: jnp.where(x == ts8, jnp.where(j * kc + k_idx < cand8, 1.0, 0.0), 0.0))
            return jnp.where(cnt < need, cand, j0)
        j0 = lax.fori_loop(0, seq.bit_length() - 1, j_body, jnp.zeros((1, bq), I32))
        return jnp.where(tied, ts, thr - 1), jnp.where(tied, ts, 0), jnp.where(tied, j0, -1)

    gt_thr, eq_val, jlim = lax.cond(n_pending > 0, with_ties, no_ties)

    def masked_body(j, _):
        bias_ref[:, pl.ds(pl.multiple_of(j * kc, kc), kc)] = jnp.full((bq, kc), NEG_BIG, BF16)
        return 0

    lax.fori_loop(n_kc, seq // kc, masked_body, 0)
    k_row = _iota((kc, rb), 0)

    def out_body(j, _):
        ks = pl.multiple_of(j * kc, kc)
        for r0 in range(0, bq, rb):
            qs = slice(r0, r0 + rb)
            x = key_ref[pl.ds(ks, kc), qs]
            tie_ok = jnp.where(ks + k_row <= jlim[:, qs], 0.0, NEG_BIG)
            b = jnp.where(x > gt_thr[:, qs], 0.0, jnp.where(x == eq_val[:, qs], tie_ok, NEG_BIG))
            bias_ref[qs, pl.ds(ks, kc)] = b.T.astype(BF16)
        return 0

    lax.fori_loop(0, n_kc, out_body, 0)


def dsa_index(main, tail, batch, seq):
    bq = min(seq, 512)
    nq = seq // bq
    kc = min(seq, 256)
    topk = min(TOPK_MAX, seq // 4)
    n_qi = IDX_HEADS * IDX_DIM
    qi_block = ((A_HEADS + 2 * A_KV_HEADS) * A_HEAD_DIM) // n_qi
    c64, s64 = _rope_tables(seq, IDX_DIM)
    tab_q = pl.BlockSpec((bq, LANES), lambda b, i: (i, 0))
    tab_k = pl.BlockSpec((seq, LANES), lambda b, i: (0, 0))
    return pl.pallas_call(
        functools.partial(_dsa_index_kernel, bq=bq, seq=seq, kc=kc, topk=topk,
                          wi_scale=IDX_HEADS ** -0.5 * IDX_DIM ** -0.5),
        name="dsa_index",
        grid=(batch, nq),
        in_specs=[pl.BlockSpec((bq, n_qi), lambda b, i: (b * nq + i, qi_block)),
                  pl.BlockSpec((bq, LANES), lambda b, i: (b * nq + i, 0)),
                  pl.BlockSpec((seq, LANES), lambda b, i: (b, 0)),
                  tab_q, tab_q, tab_k, tab_k],
        out_specs=pl.BlockSpec((bq, seq), lambda b, i: (b * nq + i, 0)),
        out_shape=jax.ShapeDtypeStruct((batch * seq, seq), BF16),
        scratch_shapes=[pltpu.VMEM((seq, bq), I32), pltpu.VMEM((bq, IDX_HEADS * LANES), BF16),
                        pltpu.VMEM((seq, LANES), BF16)],
        compiler_params=_params("parallel", "arbitrary"),
    )(main, tail, tail, c64, s64, c64, s64)


def _dsa_attn_kernel(qf_ref, kf_ref, vf_ref, cq_ref, sq_ref, ck_ref, sk_ref, bias_ref, o_ref,
                     q_ref, k_ref, v_ref, s0_ref, s1_ref, p_ref, m0_ref, m1_ref, acc_ref,
                     *, bq, bk, seq, group, strip, q_scale):
    qi = pl.program_id(2)
    rows = group * bq

    @pl.when(qi == 0)
    def _():
        for r0 in range(0, seq, bq):
            rs = slice(r0, r0 + bq)
            k_ref[rs, :] = _rope128(kf_ref[rs, :], ck_ref[rs, :], sk_ref[rs, :]).astype(BF16)
            v_ref[rs, :] = vf_ref[rs, :].astype(BF16)

    cq, sq = cq_ref[...] * q_scale, sq_ref[...] * q_scale
    for r in range(group):
        q_ref[r * bq:(r + 1) * bq, :] = _rope128(qf_ref[:, r * LANES:(r + 1) * LANES], cq, sq).astype(BF16)
    q = q_ref[...]
    m0_ref[...] = jnp.full((rows, LANES), NEG_BIG, F32)
    acc_ref[...] = jnp.zeros((rows, 2 * LANES), F32)
    even, odd = (s0_ref, m0_ref, m1_ref), (s1_ref, m1_ref, m0_ref)

    n_kv = ((qi + 1) * bq + bk - 1) // bk

    def scores(j, bufs):
        ks = pl.multiple_of(jnp.minimum(j, n_kv - 1) * bk, bk)
        bufs[0][...] = lax.dot_general(q, k_ref[pl.ds(ks, bk), :], _NT, preferred_element_type=F32)

    def chunk(j, bufs):
        s_ref, m_in_ref, m_out_ref = bufs
        ks = pl.multiple_of(j * bk, bk)
        for b0 in range(0, bq, strip):
            bias = bias_ref[b0:b0 + strip, pl.ds(ks, bk)].astype(F32)
            for r in range(group):
                r0 = r * bq + b0
                _softmax_strip(s_ref[r0:r0 + strip, :] + bias, r0, m_in_ref, m_out_ref, p_ref)
        _accumulate(acc_ref, m_in_ref, m_out_ref, p_ref, v_ref[pl.ds(ks, bk), :])

    scores(0, even)

    def pair(jp, _):
        scores(2 * jp + 1, odd)
        chunk(2 * jp, even)
        scores(2 * jp + 2, even)
        chunk(2 * jp + 1, odd)
        return 0

    lax.fori_loop(0, n_kv // 2, pair, 0)

    @pl.when(n_kv % 2 == 1)
    def _():
        chunk(n_kv - 1, even)

    for r in range(group):
        sl = slice(r * bq, (r + 1) * bq)
        o_ref[:, r * LANES:(r + 1) * LANES] = (acc_ref[sl, :LANES] / acc_ref[sl, LANES:]).astype(o_ref.dtype)


def dsa_attention(main, bias, batch, seq):
    group = A_HEADS // A_KV_HEADS
    bq = bk = min(seq, 256)
    nq = seq // bq
    gw = group * LANES
    c128, s128 = _rope_tables(seq, A_HEAD_DIM)
    tab_q = pl.BlockSpec((bq, LANES), lambda b, g, i: (i, 0))
    tab_k = pl.BlockSpec((seq, LANES), lambda b, g, i: (0, 0))
    return pl.pallas_call(
        functools.partial(_dsa_attn_kernel, bq=bq, bk=bk, seq=seq, group=group, strip=min(bq, 64),
                          q_scale=A_HEAD_DIM ** -0.5 * LOG2E),
        name="dsa_attn",
        grid=(batch, A_KV_HEADS, nq),
        in_specs=[pl.BlockSpec((bq, gw), lambda b, g, i: (b * nq + i, g)),
                  pl.BlockSpec((seq, LANES), lambda b, g, i: (b, A_HEADS + g)),
                  pl.BlockSpec((seq, LANES), lambda b, g, i: (b, A_HEADS + A_KV_HEADS + g)),
                  tab_q, tab_q, tab_k, tab_k,
                  pl.BlockSpec((bq, seq), lambda b, g, i: (b * nq + i, 0))],
        out_specs=pl.BlockSpec((bq, gw), lambda b, g, i: (b * nq + i, g)),
        out_shape=jax.ShapeDtypeStruct((batch * seq, A_HEADS * LANES), BF16),
        scratch_shapes=[pltpu.VMEM((group * bq, LANES), BF16), pltpu.VMEM((seq, LANES), BF16),
                        pltpu.VMEM((seq, LANES), BF16),
                        pltpu.VMEM((group * bq, bk), F32), pltpu.VMEM((group * bq, bk), F32),
                        pltpu.VMEM((group * bq, bk), BF16),
                        pltpu.VMEM((group * bq, LANES), F32), pltpu.VMEM((group * bq, LANES), F32),
                        pltpu.VMEM((group * bq, 2 * LANES), F32)],
        compiler_params=_params("parallel", "parallel", "arbitrary"),
    )(main, main, main, c128, s128, c128, s128, bias)


def dsa_layer(x, g_norm, w_in, w_o, batch, seq):
    hn = rmsnorm(x, g_norm, BF16)
    n_main = (A_HEADS + 2 * A_KV_HEADS) * A_HEAD_DIM + IDX_HEADS * IDX_DIM
    n_tail = w_in.shape[1] - n_main
    main = matmul(hn, w_in[:, :n_main].astype(BF16), F32)
    w_tail = jnp.pad(w_in[:, n_main:], ((0, 0), (0, LANES - n_tail))).astype(BF16)
    tail = matmul(hn, w_tail, F32)
    bias = dsa_index(main, tail, batch, seq)
    o = dsa_attention(main, bias, batch, seq)
    return matmul(o, w_o.astype(BF16), F32, residual=x)


def _rwkv_mix_kernel(x_ref, g_ref, mix_ref, *rest, bs, rs):
    outs, hbuf = rest[:6], rest[6]
    j = pl.program_id(1)

    @pl.when(j == 0)
    def _():
        hbuf[0:8, :] = jnp.zeros((8, D_MODEL), F32)

    g = g_ref[...]
    for r0 in range(0, bs, rs):
        hn = _rms(x_ref[r0:r0 + rs, :], g)
        hbuf[8 + r0:8 + r0 + rs, :] = hn
        xx = hbuf[7 + r0:7 + r0 + rs, :] - hn
        for i in range(6):
            outs[i][r0:r0 + rs, :] = (hn + xx * mix_ref[i:i + 1, :]).astype(BF16)
    hbuf[7:8, :] = hbuf[7 + bs:8 + bs, :]


def rwkv_mix(x, g_norm, mix, batch, seq):
    bs = min(seq, 256)
    nb = seq // bs
    m, d = x.shape
    mix8 = jnp.pad(mix, ((0, 2), (0, 0)))
    return pl.pallas_call(
        functools.partial(_rwkv_mix_kernel, bs=bs, rs=min(bs, 32)),
        name="rwkv_mix",
        grid=(batch, nb),
        in_specs=[pl.BlockSpec((bs, d), lambda b, j: (b * nb + j, 0)),
                  pl.BlockSpec((1, d), lambda b, j: (0, 0)),
                  pl.BlockSpec((8, d), lambda b, j: (0, 0))],
        out_specs=[pl.BlockSpec((bs, d), lambda b, j: (b * nb + j, 0))] * 6,
        out_shape=[jax.ShapeDtypeStruct((m, d), BF16)] * 6,
        scratch_shapes=[pltpu.VMEM((bs + 8, d), F32)],
        compiler_params=_params("parallel", "arbitrary"),
    )(x, g_norm.reshape(1, d), mix8)


def _head_sum_matrix():
    return ((_iota((LANES, LANES), 0) // C_HEAD_DIM) == (_iota((LANES, LANES), 1) // C_HEAD_DIM)).astype(F32)


def _head_sum_2x(x, p):
    hi = x.astype(BF16)
    mid = (x - hi.astype(F32)).astype(BF16)
    pb = p.astype(BF16)
    return jnp.dot(hi, pb, preferred_element_type=F32) + jnp.dot(mid, pb, preferred_element_type=F32)


def _rwkv_rec_kernel(r_ref, k_ref, v_ref, wl_ref, al_ref, w0_ref, a0_ref, kkp_ref, ka_ref, y_ref, km_ref, st_ref,
                     *, tc, npair):
    c = pl.program_id(2)

    @pl.when(c == 0)
    def _():
        st_ref[...] = jnp.zeros(st_ref.shape, F32)

    def mm(a, b, dims=None):
        if dims is None:
            return jnp.dot(a, b, preferred_element_type=F32)
        return lax.dot_general(a, b, dims, preferred_element_type=F32)

    row = _iota((tc, tc), 0)
    col = _iota((tc, tc), 1)
    tri_incl = (row >= col).astype(F32)
    lane = _iota((tc, LANES), 1)
    head_masks = [lane < C_HEAD_DIM, lane >= C_HEAD_DIM]
    blockdiag = (_iota((LANES, LANES), 0) // C_HEAD_DIM) == (_iota((LANES, LANES), 1) // C_HEAD_DIM)
    n_sq = max((tc - 1).bit_length() - 1, 0)

    pairs = range(npair)
    tile = lambda a, p: a[:, p * LANES:(p + 1) * LANES]
    cast = lambda a: a.astype(BF16)

    r, k_raw, v = r_ref[...].astype(F32), k_ref[...].astype(F32), v_ref[...].astype(F32)
    lw = (-0.6065306597126334) * jax.nn.sigmoid(w0_ref[...] + wl_ref[...])
    a = jax.nn.sigmoid(a0_ref[...] + al_ref[...])
    kr = k_raw * kkp_ref[...]
    hsum = _head_sum_matrix()
    ss = jnp.concatenate([_head_sum_2x(tile(kr, p) * tile(kr, p), hsum) for p in pairs], axis=1)
    kk = kr * lax.rsqrt(jnp.maximum(ss, 1e-24))
    k = k_raw * (1.0 + (a - 1.0) * ka_ref[...])
    b = kk * a
    km_ref[...] = k
    cum = jnp.dot(tri_incl, lw, precision=HIGHEST, preferred_element_type=F32)
    g_in = jnp.exp(cum)
    g_inv = jnp.exp(-cum)
    a_t = -kk * jnp.exp(cum - lw)
    b_t = b * g_inv
    k_t = k * g_inv
    r_t = r * g_in
    g_end = g_in[tc - 1:tc, :]
    v_c, r_c = cast(v), cast(r_t)
    bk_x = cast(jnp.concatenate([b_t * g_end, k_t * g_end], axis=0).T)
    g_col = g_in.T[:, tc - 1:tc]
    rows = lambda a, p: a[p * LANES:(p + 1) * LANES, :]

    t2 = 2 * tc
    prow, pcol = _iota((t2, t2), 0), _iota((t2, t2), 1)
    strict2 = prow % tc > pcol % tc
    incl2 = prow % tc >= pcol % tc
    eye2 = (prow == pcol).astype(F32)
    low4 = jnp.concatenate([strict2, incl2], axis=0)
    by_head = lambda x_, p: cast(jnp.concatenate(
        [jnp.where(head_masks[h], tile(x_, p), 0.0) for h in range(2)], axis=0))

    a_s = {p: by_head(a_t, p) for p in pairs}
    ar = {p: jnp.concatenate([a_s[p], by_head(r_t, p)], axis=0) for p in pairs}
    gb = {p: mm(ar[p], by_head(b_t, p), _NT) for p in pairs}
    gk = {p: mm(ar[p], by_head(k_t, p), _NT) for p in pairs}
    a_ab = {p: jnp.where(strict2, gb[p][:t2], 0.0) for p in pairs}
    a_rb = {p: cast(jnp.where(incl2, gb[p][t2:], 0.0)) for p in pairs}
    v2 = {p: jnp.concatenate([tile(v_c, p)] * 2, axis=0) for p in pairs}
    kv = {p: mm(cast(jnp.where(low4, gk[p], 0.0)), v2[p]) for p in pairs}
    akv = {p: cast(kv[p][:t2]) for p in pairs}
    x = {p: eye2 + a_ab[p] for p in pairs}
    pw = {p: cast(a_ab[p]) for p in pairs}
    for _ in range(n_sq):
        pw = {p: cast(mm(pw[p], pw[p])) for p in pairs}
        x = {p: x[p] + mm(pw[p], cast(x[p])) for p in pairs}
    xw = {p: mm(cast(x[p]), jnp.concatenate([a_s[p], akv[p]], axis=1)) for p in pairs}
    w_t = {p: cast(xw[p][:tc, :LANES] + xw[p][tc:, :LANES]) for p in pairs}
    u_0 = {p: jnp.where(head_masks[0], xw[p][:tc, LANES:], xw[p][tc:, LANES:]) for p in pairs}

    s0 = {p: st_ref[p] for p in pairs}
    ws = {p: mm(jnp.concatenate([w_t[p], tile(r_c, p)], axis=0), cast(s0[p])) for p in pairs}
    u_c = {p: cast(ws[p][:tc] + u_0[p]) for p in pairs}
    yb = {p: mm(a_rb[p], jnp.concatenate([u_c[p]] * 2, axis=0)) for p in pairs}
    upd = {p: mm(rows(bk_x, p), jnp.concatenate([u_c[p], tile(v_c, p)], axis=0)) for p in pairs}
    for p in pairs:
        y_h = kv[p][t2:] + yb[p]
        y_ref[:, p * LANES:(p + 1) * LANES] = ws[p][tc:] + jnp.where(head_masks[0], y_h[:tc], y_h[tc:])
        st_ref[p] = s0[p] * rows(g_col, p) + jnp.where(blockdiag, upd[p], 0.0)


def rwkv_rec(r, k, v, wl, al, w0, a0, k_k, k_a, batch, seq):
    m, d = r.shape
    tc = min(seq, 64)
    nc = seq // tc
    npair = 16
    gw = npair * LANES
    blk = pl.BlockSpec((tc, gw), lambda bi, g, c: (bi * nc + c, g))
    row = pl.BlockSpec((1, gw), lambda bi, g, c: (0, g))
    r2 = lambda a: a.reshape(1, d).astype(F32)
    return pl.pallas_call(
        functools.partial(_rwkv_rec_kernel, tc=tc, npair=npair),
        name="rwkv_rec",
        grid=(batch, d // gw, nc),
        in_specs=[blk] * 5 + [row] * 4,
        out_specs=[blk, blk],
        out_shape=[jax.ShapeDtypeStruct((m, d), F32)] * 2,
        scratch_shapes=[pltpu.VMEM((npair, LANES, LANES), F32)],
        compiler_params=_params("parallel", "parallel", "arbitrary"),
    )(r, k, v, wl, al, r2(w0), r2(a0), r2(k_k), r2(k_a))


def _rwkv_post_kernel(y_ref, r_ref, k_ref, v_ref, g_ref, gng_ref, gnb_ref, rk_ref, o_ref):
    p = _head_sum_matrix()
    inv_n = 1.0 / C_HEAD_DIM
    for t in range(D_MODEL // LANES):
        sl = slice(t * LANES, (t + 1) * LANES)
        y = y_ref[:, sl]
        mu = _head_sum_2x(y, p) * inv_n
        yc = y - mu
        var = _head_sum_2x(yc * yc, p) * inv_n
        yn = yc * lax.rsqrt(var + C_GN_EPS) * gng_ref[:, sl] + gnb_ref[:, sl]
        bonus = (_head_sum_2x(r_ref[:, sl].astype(F32) * k_ref[:, sl] * rk_ref[:, sl], p)
                 * v_ref[:, sl].astype(F32))
        o_ref[:, sl] = ((yn + bonus) * g_ref[:, sl]).astype(o_ref.dtype)


def rwkv_post(y, r, kmod, v, g, gn_g, gn_b, r_k):
    m, d = y.shape
    bm = _largest_divisor(m, (256, 128, 64))
    big = pl.BlockSpec((bm, d), lambda i: (i, 0))
    row = pl.BlockSpec((1, d), lambda i: (0, 0))
    r2 = lambda a: a.reshape(1, d).astype(F32)
    return pl.pallas_call(
        _rwkv_post_kernel,
        name="rwkv_post",
        grid=(m // bm,),
        in_specs=[big] * 5 + [row] * 3,
        out_specs=big,
        out_shape=jax.ShapeDtypeStruct((m, d), BF16),
        compiler_params=_params("parallel"),
    )(y, r, kmod, v, g, r2(gn_g), r2(gn_b), r2(r_k))


def rwkv_layer(x, g_norm, mix, w0, w1, w2, a0, a1, a2, g1, g2, k_k, k_a, r_k, w_rkv, w_o, gn_g, gn_b,
               batch, seq):
    bf = lambda a: a.astype(BF16)
    xr, xk, xv, xw, xa, xg = rwkv_mix(x, g_norm, mix, batch, seq)
    r = matmul(xr, bf(w_rkv[0]), BF16)
    k = matmul(xk, bf(w_rkv[1]), BF16)
    v = matmul(xv, bf(w_rkv[2]), BF16)
    wl = lora(xw, bf(w1), bf(w2), "tanh")
    al = lora(xa, bf(a1), bf(a2), "none")
    g = lora(xg, bf(g1), bf(g2), "sigmoid")
    y, kmod = rwkv_rec(r, k, v, wl, al, w0, a0, k_k, k_a, batch, seq)
    o = rwkv_post(y, r, kmod, v, g, gn_g, gn_b, r_k)
    return matmul(o, bf(w_o), F32, residual=x)


def kernel(x, norm_mix, norm_ffn, norm_final, ffn_gate, ffn_up, ffn_down, a_w_in, a_w_o, b_w_in, b_f_bias,
           b_w_o, c_mix, c_w0, c_w1, c_w2, c_a0, c_a1, c_a2, c_g1, c_g2, c_k_k, c_k_a, c_r_k, c_w_rkv,
           c_w_o, c_gn_g, c_gn_b):
    batch, seq, d = x.shape
    depth = norm_mix.shape[0]
    h = x.reshape(batch * seq, d)
    for i in range(depth):
        kind, j = i % 3, i // 3
        if kind == 0:
            h = dsa_layer(h, norm_mix[i], a_w_in[j], a_w_o[j], batch, seq)
        elif kind == 1:
            h = fox_layer(h, norm_mix[i], b_w_in[j], b_f_bias[j], b_w_o[j], batch, seq)
        else:
            h = rwkv_layer(h, norm_mix[i], c_mix[j], c_w0[j], c_w1[j], c_w2[j], c_a0[j], c_a1[j], c_a2[j],
                           c_g1[j], c_g2[j], c_k_k[j], c_k_a[j], c_r_k[j], c_w_rkv[j], c_w_o[j],
                           c_gn_g[j], c_gn_b[j], batch, seq)
        h = swiglu_layer(h, norm_ffn[i], ffn_gate, ffn_up, i, ffn_down[i])
    return rmsnorm(h, norm_final, x.dtype).reshape(batch, seq, d)
```

```python
import functools

import jax
import jax.numpy as jnp
from jax import lax
from jax.experimental import pallas as pl
from jax.experimental.pallas import tpu as pltpu

F32 = jnp.float32
BF16 = jnp.bfloat16
I32 = jnp.int32
HIGHEST = lax.Precision.HIGHEST

D_MODEL = 2048
CHUNK = 64
RMS_EPS = 1e-6
ROPE_THETA = 10000.0
A_HEADS, A_KV_HEADS, A_HEAD_DIM = 16, 4, 128
IDX_HEADS, IDX_DIM, TOPK_MAX = 16, 64, 256
B_HEADS, B_HEAD_DIM = 16, 128
C_HEAD_DIM = 64
C_GN_EPS = C_HEAD_DIM * 1e-5

LANES = 128
VMEM_LIMIT_BYTES = 56 * 1024 * 1024

NEG_BIG = -1e30
LOG2E = 1.4426950408889634
_NT = (((1,), (1,)), ((), ()))


def _params(*sem):
    return pltpu.CompilerParams(dimension_semantics=sem, vmem_limit_bytes=VMEM_LIMIT_BYTES)


def _iota(shape, dim):
    return lax.broadcasted_iota(I32, shape, dim)


def _rms(x, g):
    return x * lax.rsqrt(jnp.mean(x * x, axis=-1, keepdims=True) + RMS_EPS) * g


def _rmsnorm_kernel(x_ref, g_ref, o_ref):
    o_ref[...] = _rms(x_ref[...], g_ref[...]).astype(o_ref.dtype)


def rmsnorm(x, g, out_dtype):
    m, d = x.shape
    bm = min(m, 512)
    return pl.pallas_call(
        _rmsnorm_kernel,
        name="rmsnorm",
        grid=(m // bm,),
        in_specs=[pl.BlockSpec((bm, d), lambda i: (i, 0)), pl.BlockSpec((1, d), lambda i: (0, 0))],
        out_specs=pl.BlockSpec((bm, d), lambda i: (i, 0)),
        out_shape=jax.ShapeDtypeStruct((m, d), out_dtype),
        compiler_params=_params("parallel"),
    )(x, g.reshape(1, d))


def _largest_divisor(n, candidates):
    for c in candidates:
        if n % c == 0:
            return c
    return n


def _mm_kernel(x_ref, w_ref, o_ref):
    o_ref[...] = jnp.dot(x_ref[...], w_ref[...], preferred_element_type=F32).astype(o_ref.dtype)


def _mm_res_kernel(x_ref, w_ref, r_ref, o_ref):
    acc = jnp.dot(x_ref[...], w_ref[...], preferred_element_type=F32)
    o_ref[...] = (r_ref[...] + acc).astype(o_ref.dtype)


def matmul(x, w, out_dtype, residual=None):
    m, k = x.shape
    n = w.shape[1]
    bm = _largest_divisor(m, (1024, 512, 256, 128))
    bn = _largest_divisor(n, (1024, 512, 256, 128)) if k <= 2048 else _largest_divisor(n, (512, 256, 128))
    if residual is not None and k * n <= D_MODEL * D_MODEL:
        bm, bn = _largest_divisor(m, (512, 256, 128)), n
    in_specs = [pl.BlockSpec((bm, k), lambda i, j: (i, 0)), pl.BlockSpec((k, bn), lambda i, j: (0, j))]
    args = [x, w]
    body = _mm_kernel
    if residual is not None:
        in_specs.append(pl.BlockSpec((bm, bn), lambda i, j: (i, j)))
        args.append(residual)
        body = _mm_res_kernel
    return pl.pallas_call(
        body,
        name=f"mm_{k}x{n}",
        grid=(m // bm, n // bn),
        in_specs=in_specs,
        out_specs=pl.BlockSpec((bm, bn), lambda i, j: (i, j)),
        out_shape=jax.ShapeDtypeStruct((m, n), out_dtype),
        compiler_params=_params("parallel", "parallel"),
    )(*args)


def _gateup_kernel(x_ref, wg_ref, wu_ref, o_ref, wgb_ref, wub_ref):
    @pl.when(pl.program_id(1) == 0)
    def _():
        wgb_ref[...] = wg_ref[...].astype(BF16)
        wub_ref[...] = wu_ref[...].astype(BF16)

    x = x_ref[...]
    g = jnp.dot(x, wgb_ref[...], preferred_element_type=F32)
    u = jnp.dot(x, wub_ref[...], preferred_element_type=F32)
    o_ref[...] = (g * jax.nn.sigmoid(g) * u).astype(o_ref.dtype)


def gate_up(x, wg, wu, layer):
    m, k = x.shape
    n = wg.shape[2]
    bm = _largest_divisor(m, (1024, 512, 256, 128))
    bn = _largest_divisor(n, (512, 256, 128))
    return pl.pallas_call(
        _gateup_kernel,
        name="gate_up",
        grid=(n // bn, m // bm),
        in_specs=[pl.BlockSpec((bm, k), lambda j, i: (i, 0)),
                  pl.BlockSpec((None, k, bn), lambda j, i: (layer, 0, j)),
                  pl.BlockSpec((None, k, bn), lambda j, i: (layer, 0, j))],
        out_specs=pl.BlockSpec((bm, bn), lambda j, i: (i, j)),
        out_shape=jax.ShapeDtypeStruct((m, n), BF16),
        scratch_shapes=[pltpu.VMEM((k, bn), BF16), pltpu.VMEM((k, bn), BF16)],
        compiler_params=_params("parallel", "arbitrary"),
    )(x, wg, wu)


def _lora_kernel(x_ref, w1_ref, w2_ref, o_ref, *, act):
    t = jnp.dot(x_ref[...], w1_ref[...], preferred_element_type=F32)
    if act == "tanh":
        t = jnp.tanh(t)
    elif act == "sigmoid":
        t = jax.nn.sigmoid(t)
    o_ref[...] = jnp.dot(t.astype(BF16), w2_ref[...], preferred_element_type=F32)


def lora(x, w1, w2, act):
    m, k = x.shape
    r = w1.shape[1]
    rp = -(-r // LANES) * LANES
    if act == "sigmoid":
        assert rp == r, "sigmoid(0) != 0: the rank must not be padded"
    w1 = jnp.pad(w1, ((0, 0), (0, rp - r)))
    w2 = jnp.pad(w2, ((0, rp - r), (0, 0)))
    n = w2.shape[1]
    bm = _largest_divisor(m, (1024, 512, 256, 128))
    return pl.pallas_call(
        functools.partial(_lora_kernel, act=act),
        name="lora_" + act,
        grid=(m // bm,),
        in_specs=[pl.BlockSpec((bm, k), lambda i: (i, 0)),
                  pl.BlockSpec((k, rp), lambda i: (0, 0)),
                  pl.BlockSpec((rp, n), lambda i: (0, 0))],
        out_specs=pl.BlockSpec((bm, n), lambda i: (i, 0)),
        out_shape=jax.ShapeDtypeStruct((m, n), F32),
        compiler_params=_params("parallel"),
    )(x, w1, w2)


def swiglu_layer(x, g_norm, w_gate_all, w_up_all, layer, w_down):
    hn = rmsnorm(x, g_norm, BF16)
    h = gate_up(hn, w_gate_all, w_up_all, layer)
    return matmul(h, w_down.astype(BF16), F32, residual=x)


def _log_sigmoid(x):
    return jnp.minimum(x, 0.0) - jnp.log1p(jnp.exp(-jnp.abs(x)))


def _fox_prep_kernel(fl_ref, bias_ref, c_ref, ct_ref, *, seq, ch):
    tri = (_iota((ch, ch), 0) >= _iota((ch, ch), 1)).astype(F32)
    carry = jnp.zeros((1, LANES), F32)
    for c in range(seq // ch):
        lf = _log_sigmoid(fl_ref[c * ch:(c + 1) * ch, :] + bias_ref[...])
        cs = jnp.dot(tri, lf, precision=HIGHEST, preferred_element_type=F32) + carry
        c2 = cs * LOG2E
        c_ref[c * ch:(c + 1) * ch, :] = c2
        ct_ref[:, c * ch:(c + 1) * ch] = c2.T
        carry = cs[ch - 1:ch, :]


def fox_prep(fl, bias_pad, batch, seq):
    ch = min(seq, 256)
    return pl.pallas_call(
        functools.partial(_fox_prep_kernel, seq=seq, ch=ch),
        name="fox_prep",
        grid=(batch,),
        in_specs=[pl.BlockSpec((seq, LANES), lambda b: (b, 0)), pl.BlockSpec((1, LANES), lambda b: (0, 0))],
        out_specs=[pl.BlockSpec((seq, LANES), lambda b: (b, 0)),
                   pl.BlockSpec((None, LANES, seq), lambda b: (b, 0, 0))],
        out_shape=[jax.ShapeDtypeStruct((batch * seq, LANES), F32),
                   jax.ShapeDtypeStruct((batch, LANES, seq), F32)],
        compiler_params=_params("parallel"),
    )(fl, bias_pad)


def _softmax_strip(z, r0, m_in_ref, m_out_ref, p_ref, row_term=None):
    rows, bk = z.shape
    tiles = [z[:, t * LANES:(t + 1) * LANES] for t in range(bk // LANES)]
    zmax = functools.reduce(jnp.maximum, tiles)
    zmax = jnp.broadcast_to(jnp.max(zmax, axis=1, keepdims=True), (rows, LANES))
    if row_term is not None:
        zmax = zmax + row_term
    m_new = jnp.maximum(m_in_ref[r0:r0 + rows, :], zmax)
    m_out_ref[r0:r0 + rows, :] = m_new
    shift = m_new if row_term is None else m_new - row_term
    for t, zt in enumerate(tiles):
        p_ref[r0:r0 + rows, t * LANES:(t + 1) * LANES] = jnp.exp2(zt - shift).astype(BF16)


def _accumulate(acc_ref, m_in_ref, m_out_ref, p_ref, v, row_lo=0):
    alpha = jnp.exp2(m_in_ref[row_lo:, :] - m_out_ref[row_lo:, :])
    v1 = jnp.concatenate([v, jnp.ones(v.shape, v.dtype)], axis=1)
    pv = jnp.dot(p_ref[row_lo:, :], v1, preferred_element_type=F32)
    for t in range(2):
        sl = slice(t * LANES, (t + 1) * LANES)
        acc_ref[row_lo:, sl] = alpha * acc_ref[row_lo:, sl] + pv[:, sl]


def _fox_attn_kernel(q_ref, k_ref, v_ref, c_ref, ck_ref, o_ref, s0_ref, s1_ref, p_ref, m0_ref, m1_ref, acc_ref,
                     *, bq, bk, strip):
    h = pl.program_id(1)
    qi = pl.program_id(2)
    q = q_ref[...]
    cq = jnp.sum(jnp.where(_iota((bq, LANES), 1) == h, c_ref[...], 0.0), axis=1, keepdims=True)
    cq = jnp.broadcast_to(cq, (bq, LANES))
    m0_ref[...] = jnp.full((bq, LANES), NEG_BIG, F32)
    acc_ref[...] = jnp.zeros((bq, 2 * LANES), F32)
    even, odd = (s0_ref, m0_ref, m1_ref), (s1_ref, m1_ref, m0_ref)

    def scores(t, bufs, row_lo=0):
        ks = pl.multiple_of(t * bk, bk)
        bufs[0][row_lo:, :] = lax.dot_general(q[row_lo:], k_ref[pl.ds(ks, bk), :], _NT,
                                              preferred_element_type=F32)

    def chunk(t, bufs, row_lo=0, diag_off=None):
        s_ref, m_in_ref, m_out_ref = bufs
        ks = pl.multiple_of(t * bk, bk)
        ck = ck_ref[:, pl.ds(ks, bk)]
        for r0 in range(row_lo, bq, strip):
            z = s_ref[r0:r0 + strip, :] - ck
            if diag_off is not None and diag_off + bk - 1 > r0:
                visible = diag_off + _iota((strip, bk), 1) <= r0 + _iota((strip, bk), 0)
                z = jnp.where(visible, z, NEG_BIG)
            _softmax_strip(z, r0, m_in_ref, m_out_ref, p_ref, row_term=cq[r0:r0 + strip])
        _accumulate(acc_ref, m_in_ref, m_out_ref, p_ref, v_ref[pl.ds(ks, bk), :], row_lo)

    n_diag = bq // bk
    assert n_diag % 2 == 0
    scores(0, even)

    def pair(jp, _):
        scores(2 * jp + 1, odd)
        chunk(2 * jp, even)
        scores(2 * jp + 2, even)
        chunk(2 * jp + 1, odd)
        return 0

    lax.fori_loop(0, qi * (n_diag // 2), pair, 0)
    bufs = (even, odd)
    for d in range(n_diag):
        if d + 1 < n_diag:
            scores(qi * n_diag + d + 1, bufs[(d + 1) % 2], row_lo=(d + 1) * bk)
        chunk(qi * n_diag + d, bufs[d % 2], row_lo=d * bk, diag_off=d * bk)
    o_ref[...] = (acc_ref[:, :LANES] / acc_ref[:, LANES:]).astype(o_ref.dtype)


def fox_attention(qkv, c, ck4, batch, seq):
    nh = B_HEADS
    bq = min(seq, 1024)
    bk = min(seq // 2, 512)
    nq = seq // bq
    return pl.pallas_call(
        functools.partial(_fox_attn_kernel, bq=bq, bk=bk, strip=min(bq, 64)),
        name="fox_attn",
        grid=(batch, nh, nq),
        in_specs=[pl.BlockSpec((bq, LANES), lambda b, h, i: (b * nq + i, h)),
                  pl.BlockSpec((seq, LANES), lambda b, h, i: (b, nh + h)),
                  pl.BlockSpec((seq, LANES), lambda b, h, i: (b, 2 * nh + h)),
                  pl.BlockSpec((bq, LANES), lambda b, h, i: (b * nq + i, 0)),
                  pl.BlockSpec((None, None, 1, seq), lambda b, h, i: (b, h, 0, 0))],
        out_specs=pl.BlockSpec((bq, LANES), lambda b, h, i: (b * nq + i, h)),
        out_shape=jax.ShapeDtypeStruct((batch * seq, nh * LANES), BF16),
        scratch_shapes=[pltpu.VMEM((bq, bk), F32), pltpu.VMEM((bq, bk), F32), pltpu.VMEM((bq, bk), BF16),
                        pltpu.VMEM((bq, LANES), F32), pltpu.VMEM((bq, LANES), F32),
                        pltpu.VMEM((bq, 2 * LANES), F32)],
        compiler_params=_params("parallel", "parallel", "parallel"),
    )(qkv, qkv, qkv, c, ck4)


def fox_layer(x, g_norm, w_in, f_bias, w_o, batch, seq):
    nh = B_HEADS
    hn = rmsnorm(x, g_norm, BF16)
    n_q, n_qkv = nh * B_HEAD_DIM, 3 * nh * B_HEAD_DIM
    col_scale = jnp.where(jnp.arange(n_qkv) < n_q, B_HEAD_DIM ** -0.5 * LOG2E, 1.0).astype(F32)
    qkv = matmul(hn, (w_in[:, :n_qkv] * col_scale).astype(BF16), BF16)
    w_f = jnp.pad(w_in[:, n_qkv:], ((0, 0), (0, LANES - nh))).astype(BF16)
    fl = matmul(hn, w_f, F32)
    bias_pad = jnp.pad(f_bias.astype(F32), (0, LANES - nh)).reshape(1, LANES)
    c, ct = fox_prep(fl, bias_pad, batch, seq)
    ck4 = ct[:, :nh, :].reshape(batch, nh, 1, seq)
    o = fox_attention(qkv, c, ck4, batch, seq)
    return matmul(o, w_o.astype(BF16), F32, residual=x)


def _rope_tables(seq, head_dim):
    half = head_dim // 2
    inv = ROPE_THETA ** (-jnp.arange(half, dtype=F32) / half)
    ang = jnp.arange(seq, dtype=F32)[:, None] * inv[None, :]
    cos = jnp.cos(ang)
    sin = jnp.sin(ang)
    reps = LANES // head_dim
    cos_t = jnp.tile(jnp.concatenate([cos, cos], axis=1), (1, reps))
    sin_t = jnp.tile(jnp.concatenate([-sin, sin], axis=1), (1, reps))
    return cos_t, sin_t


def _rope128(x, cos, sin):
    return x * cos + pltpu.roll(x, 64, 1) * sin


def _rope64(x, cos, sin, lane):
    partner = jnp.where((lane % 64) < 32, pltpu.roll(x, 96, 1), pltpu.roll(x, 32, 1))
    return x * cos + partner * sin


_MIN32 = -2 ** 31
_BITS_PER_CHECK = 4
_KEY_NEG_INF = (0xFF800000 - 2 ** 32) ^ 0x7FFFFFFF


def _order_key(s):
    b = pltpu.bitcast(s, I32)
    return jnp.where(b >= 0, b, b ^ 0x7FFFFFFF)


def _dsa_index_kernel(qf_ref, tq_ref, tk_ref, cq_ref, sq_ref, ck_ref, sk_ref, bias_ref, key_ref, qi_ref, ki_ref,
                      *, bq, seq, kc, topk, wi_scale):
    qb = pl.program_id(1)
    n_kc = ((qb + 1) * bq + kc - 1) // kc
    g8 = kc // 8
    rb = min(bq, 256)
    lane = _iota((bq, LANES), 1)

    @pl.when(qb == 0)
    def _():
        for r0 in range(0, seq, rb):
            rs = slice(r0, r0 + rb)
            lane_k = _iota((rb, LANES), 1)
            kr = _rope64(tk_ref[rs, :], ck_ref[rs, :], sk_ref[rs, :], lane_k)
            ki_ref[rs, :] = jnp.where(lane_k < 64, kr, pltpu.roll(kr, 64, 1)).astype(BF16)

    c64, s64 = cq_ref[...], sq_ref[...]
    for t in range(IDX_HEADS // 2):
        r = _rope64(qf_ref[:, t * LANES:(t + 1) * LANES], c64, s64, lane)
        qi_ref[:, (2 * t) * LANES:(2 * t + 1) * LANES] = jnp.where(lane < 64, r, 0.0).astype(BF16)
        qi_ref[:, (2 * t + 1) * LANES:(2 * t + 2) * LANES] = jnp.where(lane >= 64, r, 0.0).astype(BF16)
    wi = tq_ref[...] * wi_scale
    wcol = [jnp.sum(jnp.where(lane == 64 + h, wi, 0.0), axis=1, keepdims=True) for h in range(IDX_HEADS)]
    k_off = _iota((rb, kc), 1)

    def kc_body(j, _):
        ks = pl.multiple_of(j * kc, kc)
        kblk = ki_ref[pl.ds(ks, kc), :]
        for r0 in range(0, bq, rb):
            acc = jnp.zeros((rb, kc), F32)
            for h in range(IDX_HEADS):
                s = lax.dot_general(qi_ref[r0:r0 + rb, h * LANES:(h + 1) * LANES], kblk, _NT,
                                    preferred_element_type=F32)
                acc = acc + wcol[h][r0:r0 + rb] * jnp.maximum(s, 0.0)
            q_chunk = (qb * bq + r0 + _iota((rb, kc), 0)) // CHUNK
            acc = jnp.where((ks + k_off) // CHUNK <= q_chunk, acc, -jnp.inf)
            key_ref[pl.ds(ks, kc), r0:r0 + rb] = _order_key(acc.T)
        return 0

    lax.fori_loop(0, n_kc, kc_body, 0)

    def count(indicator):
        def body(j, acc):
            x = key_ref[pl.ds(pl.multiple_of(j * kc, kc), kc), :].reshape(g8, 8, bq)
            return acc + jnp.sum(indicator(x, j), axis=0)
        acc = lax.fori_loop(0, n_kc, body, jnp.zeros((8, bq), F32))
        return jnp.sum(acc, axis=0, keepdims=True)

    n_adm = ((qb * bq + _iota((1, bq), 1)) // CHUNK + 1) * CHUNK
    take_all = n_adm <= topk
    settled = lambda cnt_t: jnp.logical_or(take_all, cnt_t == topk)
    pending = lambda cnt_t: jnp.max(jnp.where(settled(cnt_t), 0, 1))

    def bit_cond(c):
        return jnp.logical_and(c[0] < 32, c[3] > 0)

    def bit_body(c):
        i, tu, cnt_t, _ = c
        for b in range(_BITS_PER_CHECK):
            cand_u = tu | lax.shift_left(jnp.int32(1), 31 - (i + b))
            cand_s = jnp.broadcast_to(cand_u ^ _MIN32, (8, bq))
            cnt = count(lambda x, j: jnp.where(x >= cand_s, 1.0, 0.0))
            ok = cnt >= topk
            cnt_t = jnp.where(ok, cnt, cnt_t)
            tu = jnp.where(ok, cand_u, tu)
        return i + _BITS_PER_CHECK, tu, cnt_t, pending(cnt_t)

    cnt0 = jnp.zeros((1, bq), F32) + (n_kc * kc).astype(F32)
    _, tu, cnt_ge, n_pending = lax.while_loop(
        bit_cond, bit_body, (jnp.int32(0), jnp.zeros((1, bq), I32), cnt0, pending(cnt0)))
    ts = tu ^ _MIN32
    thr = jnp.where(take_all, _KEY_NEG_INF + 1, ts)

    def no_ties():
        return thr - 1, jnp.zeros((1, bq), I32), jnp.full((1, bq), -1, I32)

    def with_ties():
        tied = jnp.logical_not(settled(cnt_ge))
        ts8 = jnp.broadcast_to(ts, (8, bq))
        cnt_gt = count(lambda x, j: jnp.where(x > ts8, 1.0, 0.0))
        need = topk - cnt_gt
        k_idx = _iota((g8, 8, bq), 0) * 8 + _iota((g8, 8, bq), 1)

        def j_body(i, j0):
            cand = j0 | lax.shift_left(jnp.int32(1), (seq.bit_length() - 2) - i)
            cand8 = jnp.broadcast_to(cand, (8, bq))
            cnt = count(lambda x, j: jnp.where(x == ts8, jnp.where(j * kc + k_idx < cand8, 1.0, 0.0), 0.0))
            return jnp.where(cnt < need, cand, j0)
        j0 = lax.fori_loop(0, seq.bit_length() - 1, j_body, jnp.zeros((1, bq), I32))
        return jnp.where(tied, ts, thr - 1), jnp.where(tied, ts, 0), jnp.where(tied, j0, -1)

    gt_thr, eq_val, jlim = lax.cond(n_pending > 0, with_ties, no_ties)

    def masked_body(j, _):
        bias_ref[:, pl.ds(pl.multiple_of(j * kc, kc), kc)] = jnp.full((bq, kc), NEG_BIG, BF16)
        return 0

    lax.fori_loop(n_kc, seq // kc, masked_body, 0)
    k_row = _iota((kc, rb), 0)

    def out_body(j, _):
        ks = pl.multiple_of(j * kc, kc)
        for r0 in range(0, bq, rb):
            qs = slice(r0, r0 + rb)
            x = key_ref[pl.ds(ks, kc), qs]
            tie_ok = jnp.where(ks + k_row <= jlim[:, qs], 0.0, NEG_BIG)
            b = jnp.where(x > gt_thr[:, qs], 0.0, jnp.where(x == eq_val[:, qs], tie_ok, NEG_BIG))
            bias_ref[qs, pl.ds(ks, kc)] = b.T.astype(BF16)
        return 0

    lax.fori_loop(0, n_kc, out_body, 0)


def dsa_index(main, tail, batch, seq):
    bq = min(seq, 512)
    nq = seq // bq
    kc = min(seq, 256)
    topk = min(TOPK_MAX, seq // 4)
    n_qi = IDX_HEADS * IDX_DIM
    qi_block = ((A_HEADS + 2 * A_KV_HEADS) * A_HEAD_DIM) // n_qi
    c64, s64 = _rope_tables(seq, IDX_DIM)
    tab_q = pl.BlockSpec((bq, LANES), lambda b, i: (i, 0))
    tab_k = pl.BlockSpec((seq, LANES), lambda b, i: (0, 0))
    return pl.pallas_call(
        functools.partial(_dsa_index_kernel, bq=bq, seq=seq, kc=kc, topk=topk,
                          wi_scale=IDX_HEADS ** -0.5 * IDX_DIM ** -0.5),
        name="dsa_index",
        grid=(batch, nq),
        in_specs=[pl.BlockSpec((bq, n_qi), lambda b, i: (b * nq + i, qi_block)),
                  pl.BlockSpec((bq, LANES), lambda b, i: (b * nq + i, 0)),
                  pl.BlockSpec((seq, LANES), lambda b, i: (b, 0)),
                  tab_q, tab_q, tab_k, tab_k],
        out_specs=pl.BlockSpec((bq, seq), lambda b, i: (b * nq + i, 0)),
        out_shape=jax.ShapeDtypeStruct((batch * seq, seq), BF16),
        scratch_shapes=[pltpu.VMEM((seq, bq), I32), pltpu.VMEM((bq, IDX_HEADS * LANES), BF16),
                        pltpu.VMEM((seq, LANES), BF16)],
        compiler_params=_params("parallel", "arbitrary"),
    )(main, tail, tail, c64, s64, c64, s64)


def _dsa_attn_kernel(qf_ref, kf_ref, vf_ref, cq_ref, sq_ref, ck_ref, sk_ref, bias_ref, o_ref,
                     q_ref, k_ref, v_ref, s0_ref, s1_ref, p_ref, m0_ref, m1_ref, acc_ref,
                     *, bq, bk, seq, group, strip, q_scale):
    qi = pl.program_id(2)
    rows = group * bq

    @pl.when(qi == 0)
    def _():
        for r0 in range(0, seq, bq):
            rs = slice(r0, r0 + bq)
            k_ref[rs, :] = _rope128(kf_ref[rs, :], ck_ref[rs, :], sk_ref[rs, :]).astype(BF16)
            v_ref[rs, :] = vf_ref[rs, :].astype(BF16)

    cq, sq = cq_ref[...] * q_scale, sq_ref[...] * q_scale
    for r in range(group):
        q_ref[r * bq:(r + 1) * bq, :] = _rope128(qf_ref[:, r * LANES:(r + 1) * LANES], cq, sq).astype(BF16)
    q = q_ref[...]
    m0_ref[...] = jnp.full((rows, LANES), NEG_BIG, F32)
    acc_ref[...] = jnp.zeros((rows, 2 * LANES), F32)
    even, odd = (s0_ref, m0_ref, m1_ref), (s1_ref, m1_ref, m0_ref)

    n_kv = ((qi + 1) * bq + bk - 1) // bk

    def scores(j, bufs):
        ks = pl.multiple_of(jnp.minimum(j, n_kv - 1) * bk, bk)
        bufs[0][...] = lax.dot_general(q, k_ref[pl.ds(ks, bk), :], _NT, preferred_element_type=F32)

    def chunk(j, bufs):
        s_ref, m_in_ref, m_out_ref = bufs
        ks = pl.multiple_of(j * bk, bk)
        for b0 in range(0, bq, strip):
            bias = bias_ref[b0:b0 + strip, pl.ds(ks, bk)].astype(F32)
            for r in range(group):
                r0 = r * bq + b0
                _softmax_strip(s_ref[r0:r0 + strip, :] + bias, r0, m_in_ref, m_out_ref, p_ref)
        _accumulate(acc_ref, m_in_ref, m_out_ref, p_ref, v_ref[pl.ds(ks, bk), :])

    scores(0, even)

    def pair(jp, _):
        scores(2 * jp + 1, odd)
        chunk(2 * jp, even)
        scores(2 * jp + 2, even)
        chunk(2 * jp + 1, odd)
        return 0

    lax.fori_loop(0, n_kv // 2, pair, 0)

    @pl.when(n_kv % 2 == 1)
    def _():
        chunk(n_kv - 1, even)

    for r in range(group):
        sl = slice(r * bq, (r + 1) * bq)
        o_ref[:, r * LANES:(r + 1) * LANES] = (acc_ref[sl, :LANES] / acc_ref[sl, LANES:]).astype(o_ref.dtype)


def dsa_attention(main, bias, batch, seq):
    group = A_HEADS // A_KV_HEADS
    bq = bk = min(seq, 256)
    nq = seq // bq
    gw = group * LANES
    c128, s128 = _rope_tables(seq, A_HEAD_DIM)
    tab_q = pl.BlockSpec((bq, LANES), lambda b, g, i: (i, 0))
    tab_k = pl.BlockSpec((seq, LANES), lambda b, g, i: (0, 0))
    return pl.pallas_call(
        functools.partial(_dsa_attn_kernel, bq=bq, bk=bk, seq=seq, group=group, strip=min(bq, 64),
                          q_scale=A_HEAD_DIM ** -0.5 * LOG2E),
        name="dsa_attn",
        grid=(batch, A_KV_HEADS, nq),
        in_specs=[pl.BlockSpec((bq, gw), lambda b, g, i: (b * nq + i, g)),
                  pl.BlockSpec((seq, LANES), lambda b, g, i: (b, A_HEADS + g)),
                  pl.BlockSpec((seq, LANES), lambda b, g, i: (b, A_HEADS + A_KV_HEADS + g)),
                  tab_q, tab_q, tab_k, tab_k,
                  pl.BlockSpec((bq, seq), lambda b, g, i: (b * nq + i, 0))],
        out_specs=pl.BlockSpec((bq, gw), lambda b, g, i: (b * nq + i, g)),
        out_shape=jax.ShapeDtypeStruct((batch * seq, A_HEADS * LANES), BF16),
        scratch_shapes=[pltpu.VMEM((group * bq, LANES), BF16), pltpu.VMEM((seq, LANES), BF16),
                        pltpu.VMEM((seq, LANES), BF16),
                        pltpu.VMEM((group * bq, bk), F32), pltpu.VMEM((group * bq, bk), F32),
                        pltpu.VMEM((group * bq, bk), BF16),
                        pltpu.VMEM((group * bq, LANES), F32), pltpu.VMEM((group * bq, LANES), F32),
                        pltpu.VMEM((group * bq, 2 * LANES), F32)],
        compiler_params=_params("parallel", "parallel", "arbitrary"),
    )(main, main, main, c128, s128, c128, s128, bias)


def dsa_layer(x, g_norm, w_in, w_o, batch, seq):
    hn = rmsnorm(x, g_norm, BF16)
    n_main = (A_HEADS + 2 * A_KV_HEADS) * A_HEAD_DIM + IDX_HEADS * IDX_DIM
    n_tail = w_in.shape[1] - n_main
    main = matmul(hn, w_in[:, :n_main].astype(BF16), F32)
    w_tail = jnp.pad(w_in[:, n_main:], ((0, 0), (0, LANES - n_tail))).astype(BF16)
    tail = matmul(hn, w_tail, F32)
    bias = dsa_index(main, tail, batch, seq)
    o = dsa_attention(main, bias, batch, seq)
    return matmul(o, w_o.astype(BF16), F32, residual=x)


def _rwkv_mix_kernel(x_ref, g_ref, mix_ref, *rest, bs, rs):
    outs, hbuf = rest[:6], rest[6]
    j = pl.program_id(1)

    @pl.when(j == 0)
    def _():
        hbuf[0:8, :] = jnp.zeros((8, D_MODEL), F32)

    g = g_ref[...]
    for r0 in range(0, bs, rs):
        hn = _rms(x_ref[r0:r0 + rs, :], g)
        hbuf[8 + r0:8 + r0 + rs, :] = hn
        xx = hbuf[7 + r0:7 + r0 + rs, :] - hn
        for i in range(6):
            outs[i][r0:r0 + rs, :] = (hn + xx * mix_ref[i:i + 1, :]).astype(BF16)
    hbuf[7:8, :] = hbuf[7 + bs:8 + bs, :]


def rwkv_mix(x, g_norm, mix, batch, seq):
    bs = min(seq, 256)
    nb = seq // bs
    m, d = x.shape
    mix8 = jnp.pad(mix, ((0, 2), (0, 0)))
    return pl.pallas_call(
        functools.partial(_rwkv_mix_kernel, bs=bs, rs=min(bs, 32)),
        name="rwkv_mix",
        grid=(batch, nb),
        in_specs=[pl.BlockSpec((bs, d), lambda b, j: (b * nb + j, 0)),
                  pl.BlockSpec((1, d), lambda b, j: (0, 0)),
                  pl.BlockSpec((8, d), lambda b, j: (0, 0))],
        out_specs=[pl.BlockSpec((bs, d), lambda b, j: (b * nb + j, 0))] * 6,
        out_shape=[jax.ShapeDtypeStruct((m, d), BF16)] * 6,
        scratch_shapes=[pltpu.VMEM((bs + 8, d), F32)],
        compiler_params=_params("parallel", "arbitrary"),
    )(x, g_norm.reshape(1, d), mix8)


def _head_sum_matrix():
    return ((_iota((LANES, LANES), 0) // C_HEAD_DIM) == (_iota((LANES, LANES), 1) // C_HEAD_DIM)).astype(F32)


def _head_sum_2x(x, p):
    hi = x.astype(BF16)
    mid = (x - hi.astype(F32)).astype(BF16)
    pb = p.astype(BF16)
    return jnp.dot(hi, pb, preferred_element_type=F32) + jnp.dot(mid, pb, preferred_element_type=F32)


def _rwkv_rec_kernel(r_ref, k_ref, v_ref, wl_ref, al_ref, w0_ref, a0_ref, kkp_ref, ka_ref, y_ref, km_ref, st_ref,
                     *, tc, npair):
    c = pl.program_id(2)

    @pl.when(c == 0)
    def _():
        st_ref[...] = jnp.zeros(st_ref.shape, F32)

    def mm(a, b, dims=None):
        if dims is None:
            return jnp.dot(a, b, preferred_element_type=F32)
        return lax.dot_general(a, b, dims, preferred_element_type=F32)

    row = _iota((tc, tc), 0)
    col = _iota((tc, tc), 1)
    tri_incl = (row >= col).astype(F32)
    lane = _iota((tc, LANES), 1)
    head_masks = [lane < C_HEAD_DIM, lane >= C_HEAD_DIM]
    blockdiag = (_iota((LANES, LANES), 0) // C_HEAD_DIM) == (_iota((LANES, LANES), 1) // C_HEAD_DIM)
    n_sq = max((tc - 1).bit_length() - 1, 0)

    pairs = range(npair)
    tile = lambda a, p: a[:, p * LANES:(p + 1) * LANES]
    cast = lambda a: a.astype(BF16)

    r, k_raw, v = r_ref[...].astype(F32), k_ref[...].astype(F32), v_ref[...].astype(F32)
    lw = (-0.6065306597126334) * jax.nn.sigmoid(w0_ref[...] + wl_ref[...])
    a = jax.nn.sigmoid(a0_ref[...] + al_ref[...])
    kr = k_raw * kkp_ref[...]
    hsum = _head_sum_matrix()
    ss = jnp.concatenate([_head_sum_2x(tile(kr, p) * tile(kr, p), hsum) for p in pairs], axis=1)
    kk = kr * lax.rsqrt(jnp.maximum(ss, 1e-24))
    k = k_raw * (1.0 + (a - 1.0) * ka_ref[...])
    b = kk * a
    km_ref[...] = k
    cum = jnp.dot(tri_incl, lw, precision=HIGHEST, preferred_element_type=F32)
    g_in = jnp.exp(cum)
    g_inv = jnp.exp(-cum)
    a_t = -kk * jnp.exp(cum - lw)
    b_t = b * g_inv
    k_t = k * g_inv
    r_t = r * g_in
    g_end = g_in[tc - 1:tc, :]
    v_c, r_c = cast(v), cast(r_t)
    bk_x = cast(jnp.concatenate([b_t * g_end, k_t * g_end], axis=0).T)
    g_col = g_in.T[:, tc - 1:tc]
    rows = lambda a, p: a[p * LANES:(p + 1) * LANES, :]

    t2 = 2 * tc
    prow, pcol = _iota((t2, t2), 0), _iota((t2, t2), 1)
    strict2 = prow % tc > pcol % tc
    incl2 = prow % tc >= pcol % tc
    eye2 = (prow == pcol).astype(F32)
    low4 = jnp.concatenate([strict2, incl2], axis=0)
    by_head = lambda x_, p: cast(jnp.concatenate(
        [jnp.where(head_masks[h], tile(x_, p), 0.0) for h in range(2)], axis=0))

    a_s = {p: by_head(a_t, p) for p in pairs}
    ar = {p: jnp.concatenate([a_s[p], by_head(r_t, p)], axis=0) for p in pairs}
    gb = {p: mm(ar[p], by_head(b_t, p), _NT) for p in pairs}
    gk = {p: mm(ar[p], by_head(k_t, p), _NT) for p in pairs}
    a_ab = {p: jnp.where(strict2, gb[p][:t2], 0.0) for p in pairs}
    a_rb = {p: cast(jnp.where(incl2, gb[p][t2:], 0.0)) for p in pairs}
    v2 = {p: jnp.concatenate([tile(v_c, p)] * 2, axis=0) for p in pairs}
    kv = {p: mm(cast(jnp.where(low4, gk[p], 0.0)), v2[p]) for p in pairs}
    akv = {p: cast(kv[p][:t2]) for p in pairs}
    x = {p: eye2 + a_ab[p] for p in pairs}
    pw = {p: cast(a_ab[p]) for p in pairs}
    for _ in range(n_sq):
        pw = {p: cast(mm(pw[p], pw[p])) for p in pairs}
        x = {p: x[p] + mm(pw[p], cast(x[p])) for p in pairs}
    xw = {p: mm(cast(x[p]), jnp.concatenate([a_s[p], akv[p]], axis=1)) for p in pairs}
    w_t = {p: cast(xw[p][:tc, :LANES] + xw[p][tc:, :LANES]) for p in pairs}
    u_0 = {p: jnp.where(head_masks[0], xw[p][:tc, LANES:], xw[p][tc:, LANES:]) for p in pairs}

    s0 = {p: st_ref[p] for p in pairs}
    ws = {p: mm(jnp.concatenate([w_t[p], tile(r_c, p)], axis=0), cast(s0[p])) for p in pairs}
    u_c = {p: cast(ws[p][:tc] + u_0[p]) for p in pairs}
    yb = {p: mm(a_rb[p], jnp.concatenate([u_c[p]] * 2, axis=0)) for p in pairs}
    upd = {p: mm(rows(bk_x, p), jnp.concatenate([u_c[p], tile(v_c, p)], axis=0)) for p in pairs}
    for p in pairs:
        y_h = kv[p][t2:] + yb[p]
        y_ref[:, p * LANES:(p + 1) * LANES] = ws[p][tc:] + jnp.where(head_masks[0], y_h[:tc], y_h[tc:])
        st_ref[p] = s0[p] * rows(g_col, p) + jnp.where(blockdiag, upd[p], 0.0)


def rwkv_rec(r, k, v, wl, al, w0, a0, k_k, k_a, batch, seq):
    m, d = r.shape
    tc = min(seq, 64)
    nc = seq // tc
    npair = 16
    gw = npair * LANES
    blk = pl.BlockSpec((tc, gw), lambda bi, g, c: (bi * nc + c, g))
    row = pl.BlockSpec((1, gw), lambda bi, g, c: (0, g))
    r2 = lambda a: a.reshape(1, d).astype(F32)
    return pl.pallas_call(
        functools.partial(_rwkv_rec_kernel, tc=tc, npair=npair),
        name="rwkv_rec",
        grid=(batch, d // gw, nc),
        in_specs=[blk] * 5 + [row] * 4,
        out_specs=[blk, blk],
        out_shape=[jax.ShapeDtypeStruct((m, d), F32)] * 2,
        scratch_shapes=[pltpu.VMEM((npair, LANES, LANES), F32)],
        compiler_params=_params("parallel", "parallel", "arbitrary"),
    )(r, k, v, wl, al, r2(w0), r2(a0), r2(k_k), r2(k_a))


def _rwkv_post_kernel(y_ref, r_ref, k_ref, v_ref, g_ref, gng_ref, gnb_ref, rk_ref, o_ref):
    p = _head_sum_matrix()
    inv_n = 1.0 / C_HEAD_DIM
    for t in range(D_MODEL // LANES):
        sl = slice(t * LANES, (t + 1) * LANES)
        y = y_ref[:, sl]
        mu = _head_sum_2x(y, p) * inv_n
        yc = y - mu
        var = _head_sum_2x(yc * yc, p) * inv_n
        yn = yc * lax.rsqrt(var + C_GN_EPS) * gng_ref[:, sl] + gnb_ref[:, sl]
        bonus = (_head_sum_2x(r_ref[:, sl].astype(F32) * k_ref[:, sl] * rk_ref[:, sl], p)
                 * v_ref[:, sl].astype(F32))
        o_ref[:, sl] = ((yn + bonus) * g_ref[:, sl]).astype(o_ref.dtype)


def rwkv_post(y, r, kmod, v, g, gn_g, gn_b, r_k):
    m, d = y.shape
    bm = _largest_divisor(m, (256, 128, 64))
    big = pl.BlockSpec((bm, d), lambda i: (i, 0))
    row = pl.BlockSpec((1, d), lambda i: (0, 0))
    r2 = lambda a: a.reshape(1, d).astype(F32)
    return pl.pallas_call(
        _rwkv_post_kernel,
        name="rwkv_post",
        grid=(m // bm,),
        in_specs=[big] * 5 + [row] * 3,
        out_specs=big,
        out_shape=jax.ShapeDtypeStruct((m, d), BF16),
        compiler_params=_params("parallel"),
    )(y, r, kmod, v, g, r2(gn_g), r2(gn_b), r2(r_k))


def rwkv_layer(x, g_norm, mix, w0, w1, w2, a0, a1, a2, g1, g2, k_k, k_a, r_k, w_rkv, w_o, gn_g, gn_b,
               batch, seq):
    bf = lambda a: a.astype(BF16)
    xr, xk, xv, xw, xa, xg = rwkv_mix(x, g_norm, mix, batch, seq)
    r = matmul(xr, bf(w_rkv[0]), BF16)
    k = matmul(xk, bf(w_rkv[1]), BF16)
    v = matmul(xv, bf(w_rkv[2]), BF16)
    wl = lora(xw, bf(w1), bf(w2), "tanh")
    al = lora(xa, bf(a1), bf(a2), "none")
    g = lora(xg, bf(g1), bf(g2), "sigmoid")
    y, kmod = rwkv_rec(r, k, v, wl, al, w0, a0, k_k, k_a, batch, seq)
    o = rwkv_post(y, r, kmod, v, g, gn_g, gn_b, r_k)
    return matmul(o, bf(w_o), F32, residual=x)


def kernel(x, norm_mix, norm_ffn, norm_final, ffn_gate, ffn_up, ffn_down, a_w_in, a_w_o, b_w_in, b_f_bias,
           b_w_o, c_mix, c_w0, c_w1, c_w2, c_a0, c_a1, c_a2, c_g1, c_g2, c_k_k, c_k_a, c_r_k, c_w_rkv,
           c_w_o, c_gn_g, c_gn_b):
    batch, seq, d = x.shape
    depth = norm_mix.shape[0]
    h = x.reshape(batch * seq, d)
    for i in range(depth):
        kind, j = i % 3, i // 3
        if kind == 0:
            h = dsa_layer(h, norm_mix[i], a_w_in[j], a_w_o[j], batch, seq)
        elif kind == 1:
            h = fox_layer(h, norm_mix[i], b_w_in[j], b_f_bias[j], b_w_o[j], batch, seq)
        else:
            h = rwkv_layer(h, norm_mix[i], c_mix[j], c_w0[j], c_w1[j], c_w2[j], c_a0[j], c_a1[j], c_a2[j],
                           c_g1[j], c_g2[j], c_k_k[j], c_k_a[j], c_r_k[j], c_w_rkv[j], c_w_o[j],
                           c_gn_g[j], c_gn_b[j], batch, seq)
        h = swiglu_layer(h, norm_ffn[i], ffn_gate, ffn_up, i, ffn_down[i])
    return rmsnorm(h, norm_final, x.dtype).reshape(batch, seq, d)
```

```python
import functools

import jax
import jax.numpy as jnp
from jax import lax
from jax.experimental import pallas as pl
from jax.experimental.pallas import tpu as pltpu

F32 = jnp.float32
BF16 = jnp.bfloat16
I32 = jnp.int32
HIGHEST = lax.Precision.HIGHEST

D_MODEL = 2048
CHUNK = 64
RMS_EPS = 1e-6
ROPE_THETA = 10000.0
A_HEADS, A_KV_HEADS, A_HEAD_DIM = 16, 4, 128
IDX_HEADS, IDX_DIM, TOPK_MAX = 16, 64, 256
B_HEADS, B_HEAD_DIM = 16, 128
C_HEAD_DIM = 64
C_GN_EPS = C_HEAD_DIM * 1e-5

LANES = 128
VMEM_LIMIT_BYTES = 56 * 1024 * 1024

NEG_BIG = -1e30
LOG2E = 1.4426950408889634
_NT = (((1,), (1,)), ((), ()))


def _params(*sem):
    return pltpu.CompilerParams(dimension_semantics=sem, vmem_limit_bytes=VMEM_LIMIT_BYTES)


def _iota(shape, dim):
    return lax.broadcasted_iota(I32, shape, dim)


def _rms(x, g):
    return x * lax.rsqrt(jnp.mean(x * x, axis=-1, keepdims=True) + RMS_EPS) * g


def _rmsnorm_kernel(x_ref, g_ref, o_ref):
    o_ref[...] = _rms(x_ref[...], g_ref[...]).astype(o_ref.dtype)


def rmsnorm(x, g, out_dtype):
    m, d = x.shape
    bm = min(m, 512)
    return pl.pallas_call(
        _rmsnorm_kernel,
        name="rmsnorm",
        grid=(m // bm,),
        in_specs=[pl.BlockSpec((bm, d), lambda i: (i, 0)), pl.BlockSpec((1, d), lambda i: (0, 0))],
        out_specs=pl.BlockSpec((bm, d), lambda i: (i, 0)),
        out_shape=jax.ShapeDtypeStruct((m, d), out_dtype),
        compiler_params=_params("parallel"),
    )(x, g.reshape(1, d))


def _largest_divisor(n, candidates):
    for c in candidates:
        if n % c == 0:
            return c
    return n


def _mm_kernel(x_ref, w_ref, o_ref):
    o_ref[...] = jnp.dot(x_ref[...], w_ref[...], preferred_element_type=F32).astype(o_ref.dtype)


def _mm_res_kernel(x_ref, w_ref, r_ref, o_ref):
    acc = jnp.dot(x_ref[...], w_ref[...], preferred_element_type=F32)
    o_ref[...] = (r_ref[...] + acc).astype(o_ref.dtype)


def matmul(x, w, out_dtype, residual=None, layer=None):
    m, k = x.shape
    n = w.shape[-1]
    bm = _largest_divisor(m, (1024, 512, 256, 128))
    bn = _largest_divisor(n, (1024, 512, 256, 128)) if k <= 2048 else _largest_divisor(n, (512, 256, 128))
    if k * n <= D_MODEL * D_MODEL and n >= D_MODEL:
        bm, bn = _largest_divisor(m, (512, 256, 128)), n
    if layer is None:
        w_spec = pl.BlockSpec((k, bn), lambda i, j: (0, j))
    else:
        w_spec = pl.BlockSpec((None, k, bn), lambda i, j: (layer, 0, j))
    in_specs = [pl.BlockSpec((bm, k), lambda i, j: (i, 0)), w_spec]
    args = [x, w]
    body = _mm_kernel
    if residual is not None:
        in_specs.append(pl.BlockSpec((bm, bn), lambda i, j: (i, j)))
        args.append(residual)
        body = _mm_res_kernel
    return pl.pallas_call(
        body,
        name=f"mm_{k}x{n}",
        grid=(m // bm, n // bn),
        in_specs=in_specs,
        out_specs=pl.BlockSpec((bm, bn), lambda i, j: (i, j)),
        out_shape=jax.ShapeDtypeStruct((m, n), out_dtype),
        compiler_params=_params("parallel", "parallel"),
    )(*args)


def _gateup_kernel(x_ref, wg_ref, wu_ref, o_ref, wgb_ref, wub_ref):
    @pl.when(pl.program_id(1) == 0)
    def _():
        wgb_ref[...] = wg_ref[...].astype(BF16)
        wub_ref[...] = wu_ref[...].astype(BF16)

    x = x_ref[...]
    g = jnp.dot(x, wgb_ref[...], preferred_element_type=F32)
    u = jnp.dot(x, wub_ref[...], preferred_element_type=F32)
    o_ref[...] = (g * jax.nn.sigmoid(g) * u).astype(o_ref.dtype)


def gate_up(x, wg, wu, layer):
    m, k = x.shape
    n = wg.shape[2]
    bm = _largest_divisor(m, (1024, 512, 256, 128))
    bn = _largest_divisor(n, (512, 256, 128))
    return pl.pallas_call(
        _gateup_kernel,
        name="gate_up",
        grid=(n // bn, m // bm),
        in_specs=[pl.BlockSpec((bm, k), lambda j, i: (i, 0)),
                  pl.BlockSpec((None, k, bn), lambda j, i: (layer, 0, j)),
                  pl.BlockSpec((None, k, bn), lambda j, i: (layer, 0, j))],
        out_specs=pl.BlockSpec((bm, bn), lambda j, i: (i, j)),
        out_shape=jax.ShapeDtypeStruct((m, n), BF16),
        scratch_shapes=[pltpu.VMEM((k, bn), BF16), pltpu.VMEM((k, bn), BF16)],
        compiler_params=_params("parallel", "arbitrary"),
    )(x, wg, wu)


def _lora_kernel(x_ref, w1_ref, w2_ref, o_ref, *, act):
    t = jnp.dot(x_ref[...], w1_ref[...], preferred_element_type=F32)
    if act == "tanh":
        t = jnp.tanh(t)
    elif act == "sigmoid":
        t = jax.nn.sigmoid(t)
    o_ref[...] = jnp.dot(t.astype(BF16), w2_ref[...], preferred_element_type=F32)


def lora(x, w1, w2, act):
    m, k = x.shape
    r = w1.shape[1]
    rp = -(-r // LANES) * LANES
    if act == "sigmoid":
        assert rp == r, "sigmoid(0) != 0: the rank must not be padded"
    w1 = jnp.pad(w1, ((0, 0), (0, rp - r)))
    w2 = jnp.pad(w2, ((0, rp - r), (0, 0)))
    n = w2.shape[1]
    bm = _largest_divisor(m, (1024, 512, 256, 128))
    return pl.pallas_call(
        functools.partial(_lora_kernel, act=act),
        name="lora_" + act,
        grid=(m // bm,),
        in_specs=[pl.BlockSpec((bm, k), lambda i: (i, 0)),
                  pl.BlockSpec((k, rp), lambda i: (0, 0)),
                  pl.BlockSpec((rp, n), lambda i: (0, 0))],
        out_specs=pl.BlockSpec((bm, n), lambda i: (i, 0)),
        out_shape=jax.ShapeDtypeStruct((m, n), F32),
        compiler_params=_params("parallel"),
    )(x, w1, w2)


def swiglu_layer(x, g_norm, w_gate_all, w_up_all, w_down_all, layer):
    hn = rmsnorm(x, g_norm, BF16)
    h = gate_up(hn, w_gate_all, w_up_all, layer)
    return matmul(h, w_down_all, F32, residual=x, layer=layer)


def _log_sigmoid(x):
    return jnp.minimum(x, 0.0) - jnp.log1p(jnp.exp(-jnp.abs(x)))


def _fox_prep_kernel(fl_ref, bias_ref, c_ref, ct_ref, *, seq, ch):
    tri = (_iota((ch, ch), 0) >= _iota((ch, ch), 1)).astype(F32)
    carry = jnp.zeros((1, LANES), F32)
    for c in range(seq // ch):
        lf = _log_sigmoid(fl_ref[c * ch:(c + 1) * ch, :] + bias_ref[...])
        cs = jnp.dot(tri, lf, precision=HIGHEST, preferred_element_type=F32) + carry
        c2 = cs * LOG2E
        c_ref[c * ch:(c + 1) * ch, :] = c2
        ct_ref[:, c * ch:(c + 1) * ch] = c2.T
        carry = cs[ch - 1:ch, :]


def fox_prep(fl, bias_pad, batch, seq):
    ch = min(seq, 256)
    return pl.pallas_call(
        functools.partial(_fox_prep_kernel, seq=seq, ch=ch),
        name="fox_prep",
        grid=(batch,),
        in_specs=[pl.BlockSpec((seq, LANES), lambda b: (b, 0)), pl.BlockSpec((1, LANES), lambda b: (0, 0))],
        out_specs=[pl.BlockSpec((seq, LANES), lambda b: (b, 0)),
                   pl.BlockSpec((None, LANES, seq), lambda b: (b, 0, 0))],
        out_shape=[jax.ShapeDtypeStruct((batch * seq, LANES), F32),
                   jax.ShapeDtypeStruct((batch, LANES, seq), F32)],
        compiler_params=_params("parallel"),
    )(fl, bias_pad)


def _softmax_strip(z, r0, m_in_ref, m_out_ref, p_ref, row_term=None):
    rows, bk = z.shape
    tiles = [z[:, t * LANES:(t + 1) * LANES] for t in range(bk // LANES)]
    zmax = functools.reduce(jnp.maximum, tiles)
    zmax = jnp.broadcast_to(jnp.max(zmax, axis=1, keepdims=True), (rows, LANES))
    if row_term is not None:
        zmax = zmax + row_term
    m_new = jnp.maximum(m_in_ref[r0:r0 + rows, :], zmax)
    m_out_ref[r0:r0 + rows, :] = m_new
    shift = m_new if row_term is None else m_new - row_term
    for t, zt in enumerate(tiles):
        p_ref[r0:r0 + rows, t * LANES:(t + 1) * LANES] = jnp.exp2(zt - shift).astype(BF16)


def _accumulate(acc_ref, m_in_ref, m_out_ref, p_ref, v, row_lo=0):
    alpha = jnp.exp2(m_in_ref[row_lo:, :] - m_out_ref[row_lo:, :])
    v1 = jnp.concatenate([v, jnp.ones(v.shape, v.dtype)], axis=1)
    pv = jnp.dot(p_ref[row_lo:, :], v1, preferred_element_type=F32)
    for t in range(2):
        sl = slice(t * LANES, (t + 1) * LANES)
        acc_ref[row_lo:, sl] = alpha * acc_ref[row_lo:, sl] + pv[:, sl]


def _fox_attn_kernel(q_ref, k_ref, v_ref, c_ref, ck_ref, o_ref, s0_ref, s1_ref, p_ref, m0_ref, m1_ref, acc_ref,
                     *, bq, bk, strip):
    h = pl.program_id(1)
    qi = pl.program_id(2)
    q = q_ref[...]
    cq = jnp.sum(jnp.where(_iota((bq, LANES), 1) == h, c_ref[...], 0.0), axis=1, keepdims=True)
    cq = jnp.broadcast_to(cq, (bq, LANES))
    m0_ref[...] = jnp.full((bq, LANES), NEG_BIG, F32)
    acc_ref[...] = jnp.zeros((bq, 2 * LANES), F32)
    even, odd = (s0_ref, m0_ref, m1_ref), (s1_ref, m1_ref, m0_ref)

    def scores(t, bufs, row_lo=0):
        ks = pl.multiple_of(t * bk, bk)
        bufs[0][row_lo:, :] = lax.dot_general(q[row_lo:], k_ref[pl.ds(ks, bk), :], _NT,
                                              preferred_element_type=F32)

    def chunk(t, bufs, row_lo=0, diag_off=None):
        s_ref, m_in_ref, m_out_ref = bufs
        ks = pl.multiple_of(t * bk, bk)
        ck = ck_ref[:, pl.ds(ks, bk)]
        for r0 in range(row_lo, bq, strip):
            z = s_ref[r0:r0 + strip, :] - ck
            if diag_off is not None and diag_off + bk - 1 > r0:
                visible = diag_off + _iota((strip, bk), 1) <= r0 + _iota((strip, bk), 0)
                z = jnp.where(visible, z, NEG_BIG)
            _softmax_strip(z, r0, m_in_ref, m_out_ref, p_ref, row_term=cq[r0:r0 + strip])
        _accumulate(acc_ref, m_in_ref, m_out_ref, p_ref, v_ref[pl.ds(ks, bk), :], row_lo)

    n_diag = bq // bk
    assert n_diag % 2 == 0
    scores(0, even)

    def pair(jp, _):
        scores(2 * jp + 1, odd)
        chunk(2 * jp, even)
        scores(2 * jp + 2, even)
        chunk(2 * jp + 1, odd)
        return 0

    lax.fori_loop(0, qi * (n_diag // 2), pair, 0)
    bufs = (even, odd)
    for d in range(n_diag):
        if d + 1 < n_diag:
            scores(qi * n_diag + d + 1, bufs[(d + 1) % 2], row_lo=(d + 1) * bk)
        chunk(qi * n_diag + d, bufs[d % 2], row_lo=d * bk, diag_off=d * bk)
    o_ref[...] = (acc_ref[:, :LANES] / acc_ref[:, LANES:]).astype(o_ref.dtype)


def fox_attention(qkv, c, ck4, batch, seq):
    nh = B_HEADS
    bq = min(seq, 1024)
    bk = min(seq // 2, 512)
    nq = seq // bq
    return pl.pallas_call(
        functools.partial(_fox_attn_kernel, bq=bq, bk=bk, strip=min(bq, 64)),
        name="fox_attn",
        grid=(batch, nh, nq),
        in_specs=[pl.BlockSpec((bq, LANES), lambda b, h, i: (b * nq + i, h)),
                  pl.BlockSpec((seq, LANES), lambda b, h, i: (b, nh + h)),
                  pl.BlockSpec((seq, LANES), lambda b, h, i: (b, 2 * nh + h)),
                  pl.BlockSpec((bq, LANES), lambda b, h, i: (b * nq + i, 0)),
                  pl.BlockSpec((None, None, 1, seq), lambda b, h, i: (b, h, 0, 0))],
        out_specs=pl.BlockSpec((bq, LANES), lambda b, h, i: (b * nq + i, h)),
        out_shape=jax.ShapeDtypeStruct((batch * seq, nh * LANES), BF16),
        scratch_shapes=[pltpu.VMEM((bq, bk), F32), pltpu.VMEM((bq, bk), F32), pltpu.VMEM((bq, bk), BF16),
                        pltpu.VMEM((bq, LANES), F32), pltpu.VMEM((bq, LANES), F32),
                        pltpu.VMEM((bq, 2 * LANES), F32)],
        compiler_params=_params("parallel", "parallel", "parallel"),
    )(qkv, qkv, qkv, c, ck4)


def fox_layer(x, g_norm, w_in, f_bias, w_o, batch, seq):
    nh = B_HEADS
    hn = rmsnorm(x, g_norm, BF16)
    n_q, n_qkv = nh * B_HEAD_DIM, 3 * nh * B_HEAD_DIM
    col_scale = jnp.where(jnp.arange(n_qkv) < n_q, B_HEAD_DIM ** -0.5 * LOG2E, 1.0).astype(F32)
    qkv = matmul(hn, (w_in[:, :n_qkv] * col_scale).astype(BF16), BF16)
    w_f = jnp.pad(w_in[:, n_qkv:], ((0, 0), (0, LANES - nh))).astype(BF16)
    fl = matmul(hn, w_f, F32)
    bias_pad = jnp.pad(f_bias.astype(F32), (0, LANES - nh)).reshape(1, LANES)
    c, ct = fox_prep(fl, bias_pad, batch, seq)
    ck4 = ct[:, :nh, :].reshape(batch, nh, 1, seq)
    o = fox_attention(qkv, c, ck4, batch, seq)
    return matmul(o, w_o.astype(BF16), F32, residual=x)


def _rope_tables(seq, head_dim):
    half = head_dim // 2
    inv = ROPE_THETA ** (-jnp.arange(half, dtype=F32) / half)
    ang = jnp.arange(seq, dtype=F32)[:, None] * inv[None, :]
    cos = jnp.cos(ang)
    sin = jnp.sin(ang)
    reps = LANES // head_dim
    cos_t = jnp.tile(jnp.concatenate([cos, cos], axis=1), (1, reps))
    sin_t = jnp.tile(jnp.concatenate([-sin, sin], axis=1), (1, reps))
    return cos_t, sin_t


def _rope128(x, cos, sin):
    return x * cos + pltpu.roll(x, 64, 1) * sin


def _rope64(x, cos, sin, lane):
    partner = jnp.where((lane % 64) < 32, pltpu.roll(x, 96, 1), pltpu.roll(x, 32, 1))
    return x * cos + partner * sin


_MIN32 = -2 ** 31
_BITS_PER_CHECK = 4
_KEY_NEG_INF = (0xFF800000 - 2 ** 32) ^ 0x7FFFFFFF


def _order_key(s):
    b = pltpu.bitcast(s, I32)
    return jnp.where(b >= 0, b, b ^ 0x7FFFFFFF)


def _dsa_index_kernel(qf_ref, tq_ref, tk_ref, cq_ref, sq_ref, ck_ref, sk_ref, bias_ref, key_ref, qi_ref, ki_ref,
                      *, bq, seq, kc, topk, wi_scale):
    qb = pl.program_id(1)
    n_kc = ((qb + 1) * bq + kc - 1) // kc
    g8 = kc // 8
    rb = min(bq, 256)
    lane = _iota((bq, LANES), 1)

    @pl.when(qb == 0)
    def _():
        for r0 in range(0, seq, rb):
            rs = slice(r0, r0 + rb)
            lane_k = _iota((rb, LANES), 1)
            kr = _rope64(tk_ref[rs, :], ck_ref[rs, :], sk_ref[rs, :], lane_k)
            ki_ref[rs, :] = jnp.where(lane_k < 64, kr, pltpu.roll(kr, 64, 1)).astype(BF16)

    c64, s64 = cq_ref[...], sq_ref[...]
    for t in range(IDX_HEADS // 2):
        r = _rope64(qf_ref[:, t * LANES:(t + 1) * LANES], c64, s64, lane)
        qi_ref[:, (2 * t) * LANES:(2 * t + 1) * LANES] = jnp.where(lane < 64, r, 0.0).astype(BF16)
        qi_ref[:, (2 * t + 1) * LANES:(2 * t + 2) * LANES] = jnp.where(lane >= 64, r, 0.0).astype(BF16)
    wi = tq_ref[...] * wi_scale
    wcol = [jnp.sum(jnp.where(lane == 64 + h, wi, 0.0), axis=1, keepdims=True) for h in range(IDX_HEADS)]
    k_off = _iota((rb, kc), 1)

    def kc_body(j, _):
        ks = pl.multiple_of(j * kc, kc)
        kblk = ki_ref[pl.ds(ks, kc), :]
        for r0 in range(0, bq, rb):
            acc = jnp.zeros((rb, kc), F32)
            for h in range(IDX_HEADS):
                s = lax.dot_general(qi_ref[r0:r0 + rb, h * LANES:(h + 1) * LANES], kblk, _NT,
                                    preferred_element_type=F32)
                acc = acc + wcol[h][r0:r0 + rb] * jnp.maximum(s, 0.0)
            q_chunk = (qb * bq + r0 + _iota((rb, kc), 0)) // CHUNK
            acc = jnp.where((ks + k_off) // CHUNK <= q_chunk, acc, -jnp.inf)
            key_ref[pl.ds(ks, kc), r0:r0 + rb] = _order_key(acc.T)
        return 0

    lax.fori_loop(0, n_kc, kc_body, 0)

    def count(indicator):
        def body(j, acc):
            x = key_ref[pl.ds(pl.multiple_of(j * kc, kc), kc), :].reshape(g8, 8, bq)
            return acc + jnp.sum(indicator(x, j), axis=0)
        acc = lax.fori_loop(0, n_kc, body, jnp.zeros((8, bq), F32))
        return jnp.sum(acc, axis=0, keepdims=True)

    n_adm = ((qb * bq + _iota((1, bq), 1)) // CHUNK + 1) * CHUNK
    take_all = n_adm <= topk
    settled = lambda cnt_t: jnp.logical_or(take_all, cnt_t == topk)
    pending = lambda cnt_t: jnp.max(jnp.where(settled(cnt_t), 0, 1))

    def bit_cond(c):
        return jnp.logical_and(c[0] < 32, c[3] > 0)

    def bit_body(c):
        i, tu, cnt_t, _ = c
        for b in range(_BITS_PER_CHECK):
            cand_u = tu | lax.shift_left(jnp.int32(1), 31 - (i + b))
            cand_s = jnp.broadcast_to(cand_u ^ _MIN32, (8, bq))
            cnt = count(lambda x, j: jnp.where(x >= cand_s, 1.0, 0.0))
            ok = cnt >= topk
            cnt_t = jnp.where(ok, cnt, cnt_t)
            tu = jnp.where(ok, cand_u, tu)
        return i + _BITS_PER_CHECK, tu, cnt_t, pending(cnt_t)

    cnt0 = jnp.zeros((1, bq), F32) + (n_kc * kc).astype(F32)
    _, tu, cnt_ge, n_pending = lax.while_loop(
        bit_cond, bit_body, (jnp.int32(0), jnp.zeros((1, bq), I32), cnt0, pending(cnt0)))
    ts = tu ^ _MIN32
    thr = jnp.where(take_all, _KEY_NEG_INF + 1, ts)

    def no_ties():
        return thr - 1, jnp.zeros((1, bq), I32), jnp.full((1, bq), -1, I32)

    def with_ties():
        tied = jnp.logical_not(settled(cnt_ge))
        ts8 = jnp.broadcast_to(ts, (8, bq))
        cnt_gt = count(lambda x, j: jnp.where(x > ts8, 1.0, 0.0))
        need = topk - cnt_gt
        k_idx = _iota((g8, 8, bq), 0) * 8 + _iota((g8, 8, bq), 1)

        def j_body(i, j0):
            cand = j0 | lax.shift_left(jnp.int32(1), (seq.bit_length() - 2) - i)
            cand8 = jnp.broadcast_to(cand, (8, bq))
            cnt = count(lambda x, j: jnp.where(x == ts8, jnp.where(j * kc + k_idx < cand8, 1.0, 0.0), 0.0))
            return jnp.where(cnt < need, cand, j0)
        j0 = lax.fori_loop(0, seq.bit_length() - 1, j_body, jnp.zeros((1, bq), I32))
        return jnp.where(tied, ts, thr - 1), jnp.where(tied, ts, 0), jnp.where(tied, j0, -1)

    gt_thr, eq_val, jlim = lax.cond(n_pending > 0, with_ties, no_ties)

    def masked_body(j, _):
        bias_ref[:, pl.ds(pl.multiple_of(j * kc, kc), kc)] = jnp.full((bq, kc), NEG_BIG, BF16)
        return 0

    lax.fori_loop(n_kc, seq // kc, masked_body, 0)
    k_row = _iota((kc, rb), 0)

    def out_body(j, _):
        ks = pl.multiple_of(j * kc, kc)
        for r0 in range(0, bq, rb):
            qs = slice(r0, r0 + rb)
            x = key_ref[pl.ds(ks, kc), qs]
            tie_ok = jnp.where(ks + k_row <= jlim[:, qs], 0.0, NEG_BIG)
            b = jnp.where(x > gt_thr[:, qs], 0.0, jnp.where(x == eq_val[:, qs], tie_ok, NEG_BIG))
            bias_ref[qs, pl.ds(ks, kc)] = b.T.astype(BF16)
        return 0

    lax.fori_loop(0, n_kc, out_body, 0)


def dsa_index(main, tail, batch, seq):
    bq = min(seq, 512)
    nq = seq // bq
    kc = min(seq, 256)
    topk = min(TOPK_MAX, seq // 4)
    n_qi = IDX_HEADS * IDX_DIM
    qi_block = ((A_HEADS + 2 * A_KV_HEADS) * A_HEAD_DIM) // n_qi
    c64, s64 = _rope_tables(seq, IDX_DIM)
    tab_q = pl.BlockSpec((bq, LANES), lambda b, i: (i, 0))
    tab_k = pl.BlockSpec((seq, LANES), lambda b, i: (0, 0))
    return pl.pallas_call(
        functools.partial(_dsa_index_kernel, bq=bq, seq=seq, kc=kc, topk=topk,
                          wi_scale=IDX_HEADS ** -0.5 * IDX_DIM ** -0.5),
        name="dsa_index",
        grid=(batch, nq),
        in_specs=[pl.BlockSpec((bq, n_qi), lambda b, i: (b * nq + i, qi_block)),
                  pl.BlockSpec((bq, LANES), lambda b, i: (b * nq + i, 0)),
                  pl.BlockSpec((seq, LANES), lambda b, i: (b, 0)),
                  tab_q, tab_q, tab_k, tab_k],
        out_specs=pl.BlockSpec((bq, seq), lambda b, i: (b * nq + i, 0)),
        out_shape=jax.ShapeDtypeStruct((batch * seq, seq), BF16),
        scratch_shapes=[pltpu.VMEM((seq, bq), I32), pltpu.VMEM((bq, IDX_HEADS * LANES), BF16),
                        pltpu.VMEM((seq, LANES), BF16)],
        compiler_params=_params("parallel", "arbitrary"),
    )(main, tail, tail, c64, s64, c64, s64)


def _dsa_attn_kernel(qf_ref, kf_ref, vf_ref, cq_ref, sq_ref, ck_ref, sk_ref, bias_ref, o_ref,
                     q_ref, k_ref, v_ref, s0_ref, s1_ref, p_ref, m0_ref, m1_ref, acc_ref,
                     *, bq, bk, seq, group, strip, q_scale):
    qi = pl.program_id(2)
    rows = group * bq

    @pl.when(qi == 0)
    def _():
        for r0 in range(0, seq, bq):
            rs = slice(r0, r0 + bq)
            k_ref[rs, :] = _rope128(kf_ref[rs, :], ck_ref[rs, :], sk_ref[rs, :]).astype(BF16)
            v_ref[rs, :] = vf_ref[rs, :].astype(BF16)

    cq, sq = cq_ref[...] * q_scale, sq_ref[...] * q_scale
    for r in range(group):
        q_ref[r * bq:(r + 1) * bq, :] = _rope128(qf_ref[:, r * LANES:(r + 1) * LANES], cq, sq).astype(BF16)
    q = q_ref[...]
    m0_ref[...] = jnp.full((rows, LANES), NEG_BIG, F32)
    acc_ref[...] = jnp.zeros((rows, 2 * LANES), F32)
    even, odd = (s0_ref, m0_ref, m1_ref), (s1_ref, m1_ref, m0_ref)

    n_kv = ((qi + 1) * bq + bk - 1) // bk

    def scores(j, bufs):
        ks = pl.multiple_of(jnp.minimum(j, n_kv - 1) * bk, bk)
        bufs[0][...] = lax.dot_general(q, k_ref[pl.ds(ks, bk), :], _NT, preferred_element_type=F32)

    def chunk(j, bufs):
        s_ref, m_in_ref, m_out_ref = bufs
        ks = pl.multiple_of(j * bk, bk)
        for b0 in range(0, bq, strip):
            bias = bias_ref[b0:b0 + strip, pl.ds(ks, bk)].astype(F32)
            for r in range(group):
                r0 = r * bq + b0
                _softmax_strip(s_ref[r0:r0 + strip, :] + bias, r0, m_in_ref, m_out_ref, p_ref)
        _accumulate(acc_ref, m_in_ref, m_out_ref, p_ref, v_ref[pl.ds(ks, bk), :])

    scores(0, even)

    def pair(jp, _):
        scores(2 * jp + 1, odd)
        chunk(2 * jp, even)
        scores(2 * jp + 2, even)
        chunk(2 * jp + 1, odd)
        return 0

    lax.fori_loop(0, n_kv // 2, pair, 0)

    @pl.when(n_kv % 2 == 1)
    def _():
        chunk(n_kv - 1, even)

    for r in range(group):
        sl = slice(r * bq, (r + 1) * bq)
        o_ref[:, r * LANES:(r + 1) * LANES] = (acc_ref[sl, :LANES] / acc_ref[sl, LANES:]).astype(o_ref.dtype)


def dsa_attention(main, bias, batch, seq):
    group = A_HEADS // A_KV_HEADS
    bq = bk = min(seq, 256)
    nq = seq // bq
    gw = group * LANES
    c128, s128 = _rope_tables(seq, A_HEAD_DIM)
    tab_q = pl.BlockSpec((bq, LANES), lambda b, g, i: (i, 0))
    tab_k = pl.BlockSpec((seq, LANES), lambda b, g, i: (0, 0))
    return pl.pallas_call(
        functools.partial(_dsa_attn_kernel, bq=bq, bk=bk, seq=seq, group=group, strip=min(bq, 64),
                          q_scale=A_HEAD_DIM ** -0.5 * LOG2E),
        name="dsa_attn",
        grid=(batch, A_KV_HEADS, nq),
        in_specs=[pl.BlockSpec((bq, gw), lambda b, g, i: (b * nq + i, g)),
                  pl.BlockSpec((seq, LANES), lambda b, g, i: (b, A_HEADS + g)),
                  pl.BlockSpec((seq, LANES), lambda b, g, i: (b, A_HEADS + A_KV_HEADS + g)),
                  tab_q, tab_q, tab_k, tab_k,
                  pl.BlockSpec((bq, seq), lambda b, g, i: (b * nq + i, 0))],
        out_specs=pl.BlockSpec((bq, gw), lambda b, g, i: (b * nq + i, g)),
        out_shape=jax.ShapeDtypeStruct((batch * seq, A_HEADS * LANES), BF16),
        scratch_shapes=[pltpu.VMEM((group * bq, LANES), BF16), pltpu.VMEM((seq, LANES), BF16),
                        pltpu.VMEM((seq, LANES), BF16),
                        pltpu.VMEM((group * bq, bk), F32), pltpu.VMEM((group * bq, bk), F32),
                        pltpu.VMEM((group * bq, bk), BF16),
                        pltpu.VMEM((group * bq, LANES), F32), pltpu.VMEM((group * bq, LANES), F32),
                        pltpu.VMEM((group * bq, 2 * LANES), F32)],
        compiler_params=_params("parallel", "parallel", "arbitrary"),
    )(main, main, main, c128, s128, c128, s128, bias)


def dsa_layer(x, g_norm, w_in, w_o, batch, seq):
    hn = rmsnorm(x, g_norm, BF16)
    n_main = (A_HEADS + 2 * A_KV_HEADS) * A_HEAD_DIM + IDX_HEADS * IDX_DIM
    n_tail = w_in.shape[1] - n_main
    main = matmul(hn, w_in[:, :n_main].astype(BF16), F32)
    w_tail = jnp.pad(w_in[:, n_main:], ((0, 0), (0, LANES - n_tail))).astype(BF16)
    tail = matmul(hn, w_tail, F32)
    bias = dsa_index(main, tail, batch, seq)
    o = dsa_attention(main, bias, batch, seq)
    return matmul(o, w_o.astype(BF16), F32, residual=x)


def _rwkv_mix_kernel(x_ref, g_ref, mix_ref, *rest, bs, rs):
    outs, hbuf = rest[:6], rest[6]
    j = pl.program_id(1)

    @pl.when(j == 0)
    def _():
        hbuf[0:8, :] = jnp.zeros((8, D_MODEL), F32)

    g = g_ref[...]
    for r0 in range(0, bs, rs):
        hn = _rms(x_ref[r0:r0 + rs, :], g)
        hbuf[8 + r0:8 + r0 + rs, :] = hn
        xx = hbuf[7 + r0:7 + r0 + rs, :] - hn
        for i in range(6):
            outs[i][r0:r0 + rs, :] = (hn + xx * mix_ref[i:i + 1, :]).astype(BF16)
    hbuf[7:8, :] = hbuf[7 + bs:8 + bs, :]


def rwkv_mix(x, g_norm, mix, batch, seq):
    bs = min(seq, 256)
    nb = seq // bs
    m, d = x.shape
    mix8 = jnp.pad(mix, ((0, 2), (0, 0)))
    return pl.pallas_call(
        functools.partial(_rwkv_mix_kernel, bs=bs, rs=min(bs, 32)),
        name="rwkv_mix",
        grid=(batch, nb),
        in_specs=[pl.BlockSpec((bs, d), lambda b, j: (b * nb + j, 0)),
                  pl.BlockSpec((1, d), lambda b, j: (0, 0)),
                  pl.BlockSpec((8, d), lambda b, j: (0, 0))],
        out_specs=[pl.BlockSpec((bs, d), lambda b, j: (b * nb + j, 0))] * 6,
        out_shape=[jax.ShapeDtypeStruct((m, d), BF16)] * 6,
        scratch_shapes=[pltpu.VMEM((bs + 8, d), F32)],
        compiler_params=_params("parallel", "arbitrary"),
    )(x, g_norm.reshape(1, d), mix8)


def _head_sum_matrix():
    return ((_iota((LANES, LANES), 0) // C_HEAD_DIM) == (_iota((LANES, LANES), 1) // C_HEAD_DIM)).astype(F32)


def _head_sum_2x(x, p):
    hi = x.astype(BF16)
    mid = (x - hi.astype(F32)).astype(BF16)
    pb = p.astype(BF16)
    return jnp.dot(hi, pb, preferred_element_type=F32) + jnp.dot(mid, pb, preferred_element_type=F32)


def _rwkv_rec_kernel(r_ref, k_ref, v_ref, wl_ref, al_ref, w0_ref, a0_ref, kkp_ref, ka_ref, y_ref, km_ref, st_ref,
                     *, tc, npair):
    c = pl.program_id(2)

    @pl.when(c == 0)
    def _():
        st_ref[...] = jnp.zeros(st_ref.shape, F32)

    def mm(a, b, dims=None):
        if dims is None:
            return jnp.dot(a, b, preferred_element_type=F32)
        return lax.dot_general(a, b, dims, preferred_element_type=F32)

    row = _iota((tc, tc), 0)
    col = _iota((tc, tc), 1)
    tri_incl = (row >= col).astype(F32)
    lane = _iota((tc, LANES), 1)
    head_masks = [lane < C_HEAD_DIM, lane >= C_HEAD_DIM]
    blockdiag = (_iota((LANES, LANES), 0) // C_HEAD_DIM) == (_iota((LANES, LANES), 1) // C_HEAD_DIM)
    n_sq = max((tc - 1).bit_length() - 1, 0)

    pairs = range(npair)
    tile = lambda a, p: a[:, p * LANES:(p + 1) * LANES]
    cast = lambda a: a.astype(BF16)

    r, k_raw, v = r_ref[...].astype(F32), k_ref[...].astype(F32), v_ref[...].astype(F32)
    lw = (-0.6065306597126334) * jax.nn.sigmoid(w0_ref[...] + wl_ref[...])
    a = jax.nn.sigmoid(a0_ref[...] + al_ref[...])
    kr = k_raw * kkp_ref[...]
    hsum = _head_sum_matrix()
    ss = jnp.concatenate([_head_sum_2x(tile(kr, p) * tile(kr, p), hsum) for p in pairs], axis=1)
    kk = kr * lax.rsqrt(jnp.maximum(ss, 1e-24))
    k = k_raw * (1.0 + (a - 1.0) * ka_ref[...])
    b = kk * a
    km_ref[...] = k
    cum = jnp.dot(tri_incl, lw, precision=HIGHEST, preferred_element_type=F32)
    g_in = jnp.exp(cum)
    g_inv = jnp.exp(-cum)
    a_t = -kk * jnp.exp(cum - lw)
    b_t = b * g_inv
    k_t = k * g_inv
    r_t = r * g_in
    g_end = g_in[tc - 1:tc, :]
    v_c, r_c = cast(v), cast(r_t)
    bk_x = cast(jnp.concatenate([b_t * g_end, k_t * g_end], axis=0).T)
    g_col = g_in.T[:, tc - 1:tc]
    rows = lambda a, p: a[p * LANES:(p + 1) * LANES, :]

    t2 = 2 * tc
    prow, pcol = _iota((t2, t2), 0), _iota((t2, t2), 1)
    strict2 = prow % tc > pcol % tc
    incl2 = prow % tc >= pcol % tc
    eye2 = (prow == pcol).astype(F32)
    low4 = jnp.concatenate([strict2, incl2], axis=0)
    by_head = lambda x_, p: cast(jnp.concatenate(
        [jnp.where(head_masks[h], tile(x_, p), 0.0) for h in range(2)], axis=0))

    a_s = {p: by_head(a_t, p) for p in pairs}
    ar = {p: jnp.concatenate([a_s[p], by_head(r_t, p)], axis=0) for p in pairs}
    gb = {p: mm(ar[p], by_head(b_t, p), _NT) for p in pairs}
    gk = {p: mm(ar[p], by_head(k_t, p), _NT) for p in pairs}
    a_ab = {p: jnp.where(strict2, gb[p][:t2], 0.0) for p in pairs}
    a_rb = {p: cast(jnp.where(incl2, gb[p][t2:], 0.0)) for p in pairs}
    v2 = {p: jnp.concatenate([tile(v_c, p)] * 2, axis=0) for p in pairs}
    kv = {p: mm(cast(jnp.where(low4, gk[p], 0.0)), v2[p]) for p in pairs}
    akv = {p: cast(kv[p][:t2]) for p in pairs}
    x = {p: eye2 + a_ab[p] for p in pairs}
    pw = {p: cast(a_ab[p]) for p in pairs}
    for _ in range(n_sq):
        pw = {p: cast(mm(pw[p], pw[p])) for p in pairs}
        x = {p: x[p] + mm(pw[p], cast(x[p])) for p in pairs}
    xw = {p: mm(cast(x[p]), jnp.concatenate([a_s[p], akv[p]], axis=1)) for p in pairs}
    w_t = {p: cast(xw[p][:tc, :LANES] + xw[p][tc:, :LANES]) for p in pairs}
    u_0 = {p: jnp.where(head_masks[0], xw[p][:tc, LANES:], xw[p][tc:, LANES:]) for p in pairs}

    s0 = {p: st_ref[p] for p in pairs}
    ws = {p: mm(jnp.concatenate([w_t[p], tile(r_c, p)], axis=0), cast(s0[p])) for p in pairs}
    u_c = {p: cast(ws[p][:tc] + u_0[p]) for p in pairs}
    yb = {p: mm(a_rb[p], jnp.concatenate([u_c[p]] * 2, axis=0)) for p in pairs}
    upd = {p: mm(rows(bk_x, p), jnp.concatenate([u_c[p], tile(v_c, p)], axis=0)) for p in pairs}
    for p in pairs:
        y_h = kv[p][t2:] + yb[p]
        y_ref[:, p * LANES:(p + 1) * LANES] = ws[p][tc:] + jnp.where(head_masks[0], y_h[:tc], y_h[tc:])
        st_ref[p] = s0[p] * rows(g_col, p) + jnp.where(blockdiag, upd[p], 0.0)


def rwkv_rec(r, k, v, wl, al, w0, a0, k_k, k_a, batch, seq):
    m, d = r.shape
    tc = min(seq, 64)
    nc = seq // tc
    npair = 16
    gw = npair * LANES
    blk = pl.BlockSpec((tc, gw), lambda bi, g, c: (bi * nc + c, g))
    row = pl.BlockSpec((1, gw), lambda bi, g, c: (0, g))
    r2 = lambda a: a.reshape(1, d).astype(F32)
    return pl.pallas_call(
        functools.partial(_rwkv_rec_kernel, tc=tc, npair=npair),
        name="rwkv_rec",
        grid=(batch, d // gw, nc),
        in_specs=[blk] * 5 + [row] * 4,
        out_specs=[blk, blk],
        out_shape=[jax.ShapeDtypeStruct((m, d), F32)] * 2,
        scratch_shapes=[pltpu.VMEM((npair, LANES, LANES), F32)],
        compiler_params=_params("parallel", "parallel", "arbitrary"),
    )(r, k, v, wl, al, r2(w0), r2(a0), r2(k_k), r2(k_a))


def _rwkv_post_kernel(y_ref, r_ref, k_ref, v_ref, g_ref, gng_ref, gnb_ref, rk_ref, o_ref):
    p = _head_sum_matrix()
    inv_n = 1.0 / C_HEAD_DIM
    for t in range(D_MODEL // LANES):
        sl = slice(t * LANES, (t + 1) * LANES)
        y = y_ref[:, sl]
        mu = _head_sum_2x(y, p) * inv_n
        yc = y - mu
        var = _head_sum_2x(yc * yc, p) * inv_n
        yn = yc * lax.rsqrt(var + C_GN_EPS) * gng_ref[:, sl] + gnb_ref[:, sl]
        bonus = (_head_sum_2x(r_ref[:, sl].astype(F32) * k_ref[:, sl] * rk_ref[:, sl], p)
                 * v_ref[:, sl].astype(F32))
        o_ref[:, sl] = ((yn + bonus) * g_ref[:, sl]).astype(o_ref.dtype)


def rwkv_post(y, r, kmod, v, g, gn_g, gn_b, r_k):
    m, d = y.shape
    bm = _largest_divisor(m, (256, 128, 64))
    big = pl.BlockSpec((bm, d), lambda i: (i, 0))
    row = pl.BlockSpec((1, d), lambda i: (0, 0))
    r2 = lambda a: a.reshape(1, d).astype(F32)
    return pl.pallas_call(
        _rwkv_post_kernel,
        name="rwkv_post",
        grid=(m // bm,),
        in_specs=[big] * 5 + [row] * 3,
        out_specs=big,
        out_shape=jax.ShapeDtypeStruct((m, d), BF16),
        compiler_params=_params("parallel"),
    )(y, r, kmod, v, g, r2(gn_g), r2(gn_b), r2(r_k))


def rwkv_layer(x, g_norm, mix, w0, w1, w2, a0, a1, a2, g1, g2, k_k, k_a, r_k, w_rkv, w_o, gn_g, gn_b,
               batch, seq):
    bf = lambda a: a.astype(BF16)
    xr, xk, xv, xw, xa, xg = rwkv_mix(x, g_norm, mix, batch, seq)
    r = matmul(xr, bf(w_rkv[0]), BF16)
    k = matmul(xk, bf(w_rkv[1]), BF16)
    v = matmul(xv, bf(w_rkv[2]), BF16)
    wl = lora(xw, bf(w1), bf(w2), "tanh")
    al = lora(xa, bf(a1), bf(a2), "none")
    g = lora(xg, bf(g1), bf(g2), "sigmoid")
    y, kmod = rwkv_rec(r, k, v, wl, al, w0, a0, k_k, k_a, batch, seq)
    o = rwkv_post(y, r, kmod, v, g, gn_g, gn_b, r_k)
    return matmul(o, bf(w_o), F32, residual=x)


def kernel(x, norm_mix, norm_ffn, norm_final, ffn_gate, ffn_up, ffn_down, a_w_in, a_w_o, b_w_in, b_f_bias,
           b_w_o, c_mix, c_w0, c_w1, c_w2, c_a0, c_a1, c_a2, c_g1, c_g2, c_k_k, c_k_a, c_r_k, c_w_rkv,
           c_w_o, c_gn_g, c_gn_b):
    batch, seq, d = x.shape
    depth = norm_mix.shape[0]
    h = x.reshape(batch * seq, d)
    ffn_down_bf = ffn_down.astype(BF16)
    for i in range(depth):
        kind, j = i % 3, i // 3
        if kind == 0:
            h = dsa_layer(h, norm_mix[i], a_w_in[j], a_w_o[j], batch, seq)
        elif kind == 1:
            h = fox_layer(h, norm_mix[i], b_w_in[j], b_f_bias[j], b_w_o[j], batch, seq)
        else:
            h = rwkv_layer(h, norm_mix[i], c_mix[j], c_w0[j], c_w1[j], c_w2[j], c_a0[j], c_a1[j], c_a2[j],
                           c_g1[j], c_g2[j], c_k_k[j], c_k_a[j], c_r_k[j], c_w_rkv[j], c_w_o[j],
                           c_gn_g[j], c_gn_b[j], batch, seq)
        h = swiglu_layer(h, norm_ffn[i], ffn_gate, ffn_up, ffn_down_bf, i)
    return rmsnorm(h, norm_final, x.dtype).reshape(batch, seq, d)
```

```python
import functools

import jax
import jax.numpy as jnp
from jax import lax
from jax.experimental import pallas as pl
from jax.experimental.pallas import tpu as pltpu

F32 = jnp.float32
BF16 = jnp.bfloat16
I32 = jnp.int32
HIGHEST = lax.Precision.HIGHEST

D_MODEL = 2048
CHUNK = 64
RMS_EPS = 1e-6
ROPE_THETA = 10000.0
A_HEADS, A_KV_HEADS, A_HEAD_DIM = 16, 4, 128
IDX_HEADS, IDX_DIM, TOPK_MAX = 16, 64, 256
B_HEADS, B_HEAD_DIM = 16, 128
C_HEAD_DIM = 64
C_GN_EPS = C_HEAD_DIM * 1e-5

LANES = 128
VMEM_LIMIT_BYTES = 56 * 1024 * 1024

NEG_BIG = -1e30
LOG2E = 1.4426950408889634
STREAM_BUFFERS = 3
_NT = (((1,), (1,)), ((), ()))


def _params(*sem):
    return pltpu.CompilerParams(dimension_semantics=sem, vmem_limit_bytes=VMEM_LIMIT_BYTES)


def _iota(shape, dim):
    return lax.broadcasted_iota(I32, shape, dim)


def _rms(x, g):
    return x * lax.rsqrt(jnp.mean(x * x, axis=-1, keepdims=True) + RMS_EPS) * g


def _rmsnorm_kernel(x_hbm, g_ref, o_ref, buf, sem, *, bm, n_steps):
    i = pl.program_id(0)
    ahead = STREAM_BUFFERS - 1

    def copy(step, slot):
        rows = pl.ds(pl.multiple_of(step * bm, bm), bm)
        return pltpu.make_async_copy(x_hbm.at[rows, :], buf.at[slot], sem.at[slot])

    @pl.when(i == 0)
    def _():
        for s in range(min(ahead, n_steps)):
            copy(s, s).start()

    @pl.when(i + ahead < n_steps)
    def _():
        copy(i + ahead, (i + ahead) % STREAM_BUFFERS).start()

    slot = i % STREAM_BUFFERS
    copy(i, slot).wait()
    o_ref[...] = _rms(buf[slot], g_ref[...]).astype(o_ref.dtype)


def rmsnorm(x, g, out_dtype):
    m, d = x.shape
    bm = min(m, 512)
    n_steps = m // bm
    return pl.pallas_call(
        functools.partial(_rmsnorm_kernel, bm=bm, n_steps=n_steps),
        name="rmsnorm",
        grid=(n_steps,),
        in_specs=[pl.BlockSpec(memory_space=pl.ANY), pl.BlockSpec((1, d), lambda i: (0, 0))],
        out_specs=pl.BlockSpec((bm, d), lambda i: (i, 0)),
        out_shape=jax.ShapeDtypeStruct((m, d), out_dtype),
        scratch_shapes=[pltpu.VMEM((STREAM_BUFFERS, bm, d), x.dtype), pltpu.SemaphoreType.DMA((STREAM_BUFFERS,))],
        compiler_params=_params("arbitrary"),
    )(x, g.reshape(1, d))


def _largest_divisor(n, candidates):
    for c in candidates:
        if n % c == 0:
            return c
    return n


def _mm_kernel(x_ref, w_ref, o_ref):
    o_ref[...] = jnp.dot(x_ref[...], w_ref[...], preferred_element_type=F32).astype(o_ref.dtype)


def _mm_res_kernel(x_ref, w_ref, r_ref, o_ref):
    acc = jnp.dot(x_ref[...], w_ref[...], preferred_element_type=F32)
    o_ref[...] = (r_ref[...] + acc).astype(o_ref.dtype)


def matmul(x, w, out_dtype, residual=None, layer=None):
    m, k = x.shape
    n = w.shape[-1]
    bm = _largest_divisor(m, (1024, 512, 256, 128))
    bn = _largest_divisor(n, (1024, 512, 256, 128)) if k <= 2048 else _largest_divisor(n, (512, 256, 128))
    if k * n <= D_MODEL * D_MODEL and n >= D_MODEL:
        bm, bn = _largest_divisor(m, (512, 256, 128)), n
    if layer is None:
        w_spec = pl.BlockSpec((k, bn), lambda i, j: (0, j))
    else:
        w_spec = pl.BlockSpec((None, k, bn), lambda i, j: (layer, 0, j))
    in_specs = [pl.BlockSpec((bm, k), lambda i, j: (i, 0)), w_spec]
    args = [x, w]
    body = _mm_kernel
    if residual is not None:
        in_specs.append(pl.BlockSpec((bm, bn), lambda i, j: (i, j)))
        args.append(residual)
        body = _mm_res_kernel
    return pl.pallas_call(
        body,
        name=f"mm_{k}x{n}",
        grid=(m // bm, n // bn),
        in_specs=in_specs,
        out_specs=pl.BlockSpec((bm, bn), lambda i, j: (i, j)),
        out_shape=jax.ShapeDtypeStruct((m, n), out_dtype),
        compiler_params=_params("parallel", "parallel"),
    )(*args)


def _gateup_kernel(x_ref, wg_ref, wu_ref, o_ref, wgb_ref, wub_ref):
    @pl.when(pl.program_id(1) == 0)
    def _():
        wgb_ref[...] = wg_ref[...].astype(BF16)
        wub_ref[...] = wu_ref[...].astype(BF16)

    x = x_ref[...]
    g = jnp.dot(x, wgb_ref[...], preferred_element_type=F32)
    u = jnp.dot(x, wub_ref[...], preferred_element_type=F32)
    o_ref[...] = (g * jax.nn.sigmoid(g) * u).astype(o_ref.dtype)


def gate_up(x, wg, wu, layer):
    m, k = x.shape
    n = wg.shape[2]
    bm = _largest_divisor(m, (1024, 512, 256, 128))
    bn = _largest_divisor(n, (512, 256, 128))
    return pl.pallas_call(
        _gateup_kernel,
        name="gate_up",
        grid=(n // bn, m // bm),
        in_specs=[pl.BlockSpec((bm, k), lambda j, i: (i, 0)),
                  pl.BlockSpec((None, k, bn), lambda j, i: (layer, 0, j)),
                  pl.BlockSpec((None, k, bn), lambda j, i: (layer, 0, j))],
        out_specs=pl.BlockSpec((bm, bn), lambda j, i: (i, j)),
        out_shape=jax.ShapeDtypeStruct((m, n), BF16),
        scratch_shapes=[pltpu.VMEM((k, bn), BF16), pltpu.VMEM((k, bn), BF16)],
        compiler_params=_params("parallel", "arbitrary"),
    )(x, wg, wu)


def _lora_kernel(x_ref, w1_ref, w2_ref, o_ref, *, act):
    t = jnp.dot(x_ref[...], w1_ref[...], preferred_element_type=F32)
    if act == "tanh":
        t = jnp.tanh(t)
    elif act == "sigmoid":
        t = jax.nn.sigmoid(t)
    o_ref[...] = jnp.dot(t.astype(BF16), w2_ref[...], preferred_element_type=F32)


def lora(x, w1, w2, act):
    m, k = x.shape
    r = w1.shape[1]
    rp = -(-r // LANES) * LANES
    if act == "sigmoid":
        assert rp == r, "sigmoid(0) != 0: the rank must not be padded"
    w1 = jnp.pad(w1, ((0, 0), (0, rp - r)))
    w2 = jnp.pad(w2, ((0, rp - r), (0, 0)))
    n = w2.shape[1]
    bm = _largest_divisor(m, (1024, 512, 256, 128))
    return pl.pallas_call(
        functools.partial(_lora_kernel, act=act),
        name="lora_" + act,
        grid=(m // bm,),
        in_specs=[pl.BlockSpec((bm, k), lambda i: (i, 0)),
                  pl.BlockSpec((k, rp), lambda i: (0, 0)),
                  pl.BlockSpec((rp, n), lambda i: (0, 0))],
        out_specs=pl.BlockSpec((bm, n), lambda i: (i, 0)),
        out_shape=jax.ShapeDtypeStruct((m, n), F32),
        compiler_params=_params("parallel"),
    )(x, w1, w2)


def swiglu_layer(x, g_norm, w_gate_all, w_up_all, w_down_all, layer):
    hn = rmsnorm(x, g_norm, BF16)
    h = gate_up(hn, w_gate_all, w_up_all, layer)
    return matmul(h, w_down_all, F32, residual=x, layer=layer)


def _log_sigmoid(x):
    return jnp.minimum(x, 0.0) - jnp.log1p(jnp.exp(-jnp.abs(x)))


def _fox_prep_kernel(fl_ref, bias_ref, c_ref, ct_ref, *, seq, ch):
    tri = (_iota((ch, ch), 0) >= _iota((ch, ch), 1)).astype(F32)
    carry = jnp.zeros((1, LANES), F32)
    for c in range(seq // ch):
        lf = _log_sigmoid(fl_ref[c * ch:(c + 1) * ch, :] + bias_ref[...])
        cs = jnp.dot(tri, lf, precision=HIGHEST, preferred_element_type=F32) + carry
        c2 = cs * LOG2E
        c_ref[c * ch:(c + 1) * ch, :] = c2
        ct_ref[:, c * ch:(c + 1) * ch] = c2.T
        carry = cs[ch - 1:ch, :]


def fox_prep(fl, bias_pad, batch, seq):
    ch = min(seq, 256)
    return pl.pallas_call(
        functools.partial(_fox_prep_kernel, seq=seq, ch=ch),
        name="fox_prep",
        grid=(batch,),
        in_specs=[pl.BlockSpec((seq, LANES), lambda b: (b, 0)), pl.BlockSpec((1, LANES), lambda b: (0, 0))],
        out_specs=[pl.BlockSpec((seq, LANES), lambda b: (b, 0)),
                   pl.BlockSpec((None, LANES, seq), lambda b: (b, 0, 0))],
        out_shape=[jax.ShapeDtypeStruct((batch * seq, LANES), F32),
                   jax.ShapeDtypeStruct((batch, LANES, seq), F32)],
        compiler_params=_params("parallel"),
    )(fl, bias_pad)


def _softmax_strip(z, r0, m_in_ref, m_out_ref, p_ref, row_term=None):
    rows, bk = z.shape
    tiles = [z[:, t * LANES:(t + 1) * LANES] for t in range(bk // LANES)]
    zmax = functools.reduce(jnp.maximum, tiles)
    zmax = jnp.broadcast_to(jnp.max(zmax, axis=1, keepdims=True), (rows, LANES))
    if row_term is not None:
        zmax = zmax + row_term
    m_new = jnp.maximum(m_in_ref[r0:r0 + rows, :], zmax)
    m_out_ref[r0:r0 + rows, :] = m_new
    shift = m_new if row_term is None else m_new - row_term
    for t, zt in enumerate(tiles):
        p_ref[r0:r0 + rows, t * LANES:(t + 1) * LANES] = jnp.exp2(zt - shift).astype(BF16)


def _accumulate(acc_ref, m_in_ref, m_out_ref, p_ref, v, row_lo=0):
    alpha = jnp.exp2(m_in_ref[row_lo:, :] - m_out_ref[row_lo:, :])
    v1 = jnp.concatenate([v, jnp.ones(v.shape, v.dtype)], axis=1)
    pv = jnp.dot(p_ref[row_lo:, :], v1, preferred_element_type=F32)
    for t in range(2):
        sl = slice(t * LANES, (t + 1) * LANES)
        acc_ref[row_lo:, sl] = alpha * acc_ref[row_lo:, sl] + pv[:, sl]


def _fox_attn_kernel(q_ref, k_ref, v_ref, c_ref, ck_ref, o_ref, s0_ref, s1_ref, p_ref, m0_ref, m1_ref, acc_ref,
                     *, bq, bk, strip):
    h = pl.program_id(1)
    qi = pl.program_id(2)
    q = q_ref[...]
    cq = jnp.sum(jnp.where(_iota((bq, LANES), 1) == h, c_ref[...], 0.0), axis=1, keepdims=True)
    cq = jnp.broadcast_to(cq, (bq, LANES))
    m0_ref[...] = jnp.full((bq, LANES), NEG_BIG, F32)
    acc_ref[...] = jnp.zeros((bq, 2 * LANES), F32)
    even, odd = (s0_ref, m0_ref, m1_ref), (s1_ref, m1_ref, m0_ref)

    def scores(t, bufs, row_lo=0):
        ks = pl.multiple_of(t * bk, bk)
        bufs[0][row_lo:, :] = lax.dot_general(q[row_lo:], k_ref[pl.ds(ks, bk), :], _NT,
                                              preferred_element_type=F32)

    def chunk(t, bufs, row_lo=0, diag_off=None):
        s_ref, m_in_ref, m_out_ref = bufs
        ks = pl.multiple_of(t * bk, bk)
        ck = ck_ref[:, pl.ds(ks, bk)]
        for r0 in range(row_lo, bq, strip):
            z = s_ref[r0:r0 + strip, :] - ck
            if diag_off is not None and diag_off + bk - 1 > r0:
                visible = diag_off + _iota((strip, bk), 1) <= r0 + _iota((strip, bk), 0)
                z = jnp.where(visible, z, NEG_BIG)
            _softmax_strip(z, r0, m_in_ref, m_out_ref, p_ref, row_term=cq[r0:r0 + strip])
        _accumulate(acc_ref, m_in_ref, m_out_ref, p_ref, v_ref[pl.ds(ks, bk), :], row_lo)

    n_diag = bq // bk
    assert n_diag % 2 == 0
    scores(0, even)

    def pair(jp, _):
        scores(2 * jp + 1, odd)
        chunk(2 * jp, even)
        scores(2 * jp + 2, even)
        chunk(2 * jp + 1, odd)
        return 0

    lax.fori_loop(0, qi * (n_diag // 2), pair, 0)
    bufs = (even, odd)
    for d in range(n_diag):
        if d + 1 < n_diag:
            scores(qi * n_diag + d + 1, bufs[(d + 1) % 2], row_lo=(d + 1) * bk)
        chunk(qi * n_diag + d, bufs[d % 2], row_lo=d * bk, diag_off=d * bk)
    o_ref[...] = (acc_ref[:, :LANES] / acc_ref[:, LANES:]).astype(o_ref.dtype)


def fox_attention(qkv, c, ck4, batch, seq):
    nh = B_HEADS
    bq = min(seq, 1024)
    bk = min(seq // 2, 512)
    nq = seq // bq
    return pl.pallas_call(
        functools.partial(_fox_attn_kernel, bq=bq, bk=bk, strip=min(bq, 64)),
        name="fox_attn",
        grid=(batch, nh, nq),
        in_specs=[pl.BlockSpec((bq, LANES), lambda b, h, i: (b * nq + i, h)),
                  pl.BlockSpec((seq, LANES), lambda b, h, i: (b, nh + h)),
                  pl.BlockSpec((seq, LANES), lambda b, h, i: (b, 2 * nh + h)),
                  pl.BlockSpec((bq, LANES), lambda b, h, i: (b * nq + i, 0)),
                  pl.BlockSpec((None, None, 1, seq), lambda b, h, i: (b, h, 0, 0))],
        out_specs=pl.BlockSpec((bq, LANES), lambda b, h, i: (b * nq + i, h)),
        out_shape=jax.ShapeDtypeStruct((batch * seq, nh * LANES), BF16),
        scratch_shapes=[pltpu.VMEM((bq, bk), F32), pltpu.VMEM((bq, bk), F32), pltpu.VMEM((bq, bk), BF16),
                        pltpu.VMEM((bq, LANES), F32), pltpu.VMEM((bq, LANES), F32),
                        pltpu.VMEM((bq, 2 * LANES), F32)],
        compiler_params=_params("parallel", "parallel", "parallel"),
    )(qkv, qkv, qkv, c, ck4)


def fox_layer(x, g_norm, w_in, f_bias, w_o, batch, seq):
    nh = B_HEADS
    hn = rmsnorm(x, g_norm, BF16)
    n_q, n_qkv = nh * B_HEAD_DIM, 3 * nh * B_HEAD_DIM
    col_scale = jnp.where(jnp.arange(n_qkv) < n_q, B_HEAD_DIM ** -0.5 * LOG2E, 1.0).astype(F32)
    qkv = matmul(hn, (w_in[:, :n_qkv] * col_scale).astype(BF16), BF16)
    w_f = jnp.pad(w_in[:, n_qkv:], ((0, 0), (0, LANES - nh))).astype(BF16)
    fl = matmul(hn, w_f, F32)
    bias_pad = jnp.pad(f_bias.astype(F32), (0, LANES - nh)).reshape(1, LANES)
    c, ct = fox_prep(fl, bias_pad, batch, seq)
    ck4 = ct[:, :nh, :].reshape(batch, nh, 1, seq)
    o = fox_attention(qkv, c, ck4, batch, seq)
    return matmul(o, w_o.astype(BF16), F32, residual=x)


def _rope_tables(seq, head_dim):
    half = head_dim // 2
    inv = ROPE_THETA ** (-jnp.arange(half, dtype=F32) / half)
    ang = jnp.arange(seq, dtype=F32)[:, None] * inv[None, :]
    cos = jnp.cos(ang)
    sin = jnp.sin(ang)
    reps = LANES // head_dim
    cos_t = jnp.tile(jnp.concatenate([cos, cos], axis=1), (1, reps))
    sin_t = jnp.tile(jnp.concatenate([-sin, sin], axis=1), (1, reps))
    return cos_t, sin_t


def _rope128(x, cos, sin):
    return x * cos + pltpu.roll(x, 64, 1) * sin


def _rope64(x, cos, sin, lane):
    partner = jnp.where((lane % 64) < 32, pltpu.roll(x, 96, 1), pltpu.roll(x, 32, 1))
    return x * cos + partner * sin


_MIN32 = -2 ** 31
_BITS_PER_CHECK = 4
_KEY_NEG_INF = (0xFF800000 - 2 ** 32) ^ 0x7FFFFFFF


def _order_key(s):
    b = pltpu.bitcast(s, I32)
    return jnp.where(b >= 0, b, b ^ 0x7FFFFFFF)


def _dsa_index_kernel(qf_ref, tq_ref, tk_ref, cq_ref, sq_ref, ck_ref, sk_ref, bias_ref, key_ref, qi_ref, ki_ref,
                      *, bq, seq, kc, topk, wi_scale):
    qb = pl.program_id(1)
    n_kc = ((qb + 1) * bq + kc - 1) // kc
    g8 = kc // 8
    rb = min(bq, 256)
    lane = _iota((bq, LANES), 1)

    @pl.when(qb == 0)
    def _():
        for r0 in range(0, seq, rb):
            rs = slice(r0, r0 + rb)
            lane_k = _iota((rb, LANES), 1)
            kr = _rope64(tk_ref[rs, :], ck_ref[rs, :], sk_ref[rs, :], lane_k)
            ki_ref[rs, :] = jnp.where(lane_k < 64, kr, pltpu.roll(kr, 64, 1)).astype(BF16)

    c64, s64 = cq_ref[...], sq_ref[...]
    for t in range(IDX_HEADS // 2):
        r = _rope64(qf_ref[:, t * LANES:(t + 1) * LANES], c64, s64, lane)
        qi_ref[:, (2 * t) * LANES:(2 * t + 1) * LANES] = jnp.where(lane < 64, r, 0.0).astype(BF16)
        qi_ref[:, (2 * t + 1) * LANES:(2 * t + 2) * LANES] = jnp.where(lane >= 64, r, 0.0).astype(BF16)
    wi = tq_ref[...] * wi_scale
    wcol = [jnp.sum(jnp.where(lane == 64 + h, wi, 0.0), axis=1, keepdims=True) for h in range(IDX_HEADS)]
    k_off = _iota((rb, kc), 1)

    def kc_body(j, _):
        ks = pl.multiple_of(j * kc, kc)
        kblk = ki_ref[pl.ds(ks, kc), :]
        for r0 in range(0, bq, rb):
            acc = jnp.zeros((rb, kc), F32)
            for h in range(IDX_HEADS):
                s = lax.dot_general(qi_ref[r0:r0 + rb, h * LANES:(h + 1) * LANES], kblk, _NT,
                                    preferred_element_type=F32)
                acc = acc + wcol[h][r0:r0 + rb] * jnp.maximum(s, 0.0)
            q_chunk = (qb * bq + r0 + _iota((rb, kc), 0)) // CHUNK
            acc = jnp.where((ks + k_off) // CHUNK <= q_chunk, acc, -jnp.inf)
            key_ref[pl.ds(ks, kc), r0:r0 + rb] = _order_key(acc.T)
        return 0

    lax.fori_loop(0, n_kc, kc_body, 0)

    def count(indicator):
        def body(j, acc):
            x = key_ref[pl.ds(pl.multiple_of(j * kc, kc), kc), :].reshape(g8, 8, bq)
            return acc + jnp.sum(indicator(x, j), axis=0)
        acc = lax.fori_loop(0, n_kc, body, jnp.zeros((8, bq), F32))
        return jnp.sum(acc, axis=0, keepdims=True)

    n_adm = ((qb * bq + _iota((1, bq), 1)) // CHUNK + 1) * CHUNK
    take_all = n_adm <= topk
    settled = lambda cnt_t: jnp.logical_or(take_all, cnt_t == topk)
    pending = lambda cnt_t: jnp.max(jnp.where(settled(cnt_t), 0, 1))

    def bit_cond(c):
        return jnp.logical_and(c[0] < 32, c[3] > 0)

    def bit_body(c):
        i, tu, cnt_t, _ = c
        for b in range(_BITS_PER_CHECK):
            cand_u = tu | lax.shift_left(jnp.int32(1), 31 - (i + b))
            cand_s = jnp.broadcast_to(cand_u ^ _MIN32, (8, bq))
            cnt = count(lambda x, j: jnp.where(x >= cand_s, 1.0, 0.0))
            ok = cnt >= topk
            cnt_t = jnp.where(ok, cnt, cnt_t)
            tu = jnp.where(ok, cand_u, tu)
        return i + _BITS_PER_CHECK, tu, cnt_t, pending(cnt_t)

    cnt0 = jnp.zeros((1, bq), F32) + (n_kc * kc).astype(F32)
    _, tu, cnt_ge, n_pending = lax.while_loop(
        bit_cond, bit_body, (jnp.int32(0), jnp.zeros((1, bq), I32), cnt0, pending(cnt0)))
    ts = tu ^ _MIN32
    thr = jnp.where(take_all, _KEY_NEG_INF + 1, ts)

    def no_ties():
        return thr - 1, jnp.zeros((1, bq), I32), jnp.full((1, bq), -1, I32)

    def with_ties():
        tied = jnp.logical_not(settled(cnt_ge))
        ts8 = jnp.broadcast_to(ts, (8, bq))
        cnt_gt = count(lambda x, j: jnp.where(x > ts8, 1.0, 0.0))
        need = topk - cnt_gt
        k_idx = _iota((g8, 8, bq), 0) * 8 + _iota((g8, 8, bq), 1)

        def j_body(i, j0):
            cand = j0 | lax.shift_left(jnp.int32(1), (seq.bit_length() - 2) - i)
            cand8 = jnp.broadcast_to(cand, (8, bq))
            cnt = count(lambda x, j: jnp.where(x == ts8, jnp.where(j * kc + k_idx < cand8, 1.0, 0.0), 0.0))
            return jnp.where(cnt < need, cand, j0)
        j0 = lax.fori_loop(0, seq.bit_length() - 1, j_body, jnp.zeros((1, bq), I32))
        return jnp.where(tied, ts, thr - 1), jnp.where(tied, ts, 0), jnp.where(tied, j0, -1)

    gt_thr, eq_val, jlim = lax.cond(n_pending > 0, with_ties, no_ties)

    def masked_body(j, _):
        bias_ref[:, pl.ds(pl.multiple_of(j * kc, kc), kc)] = jnp.full((bq, kc), NEG_BIG, BF16)
        return 0

    lax.fori_loop(n_kc, seq // kc, masked_body, 0)
    k_row = _iota((kc, rb), 0)

    def out_body(j, _):
        ks = pl.multiple_of(j * kc, kc)
        for r0 in range(0, bq, rb):
            qs = slice(r0, r0 + rb)
            x = key_ref[pl.ds(ks, kc), qs]
            tie_ok = jnp.where(ks + k_row <= jlim[:, qs], 0.0, NEG_BIG)
            b = jnp.where(x > gt_thr[:, qs], 0.0, jnp.where(x == eq_val[:, qs], tie_ok, NEG_BIG))
            bias_ref[qs, pl.ds(ks, kc)] = b.T.astype(BF16)
        return 0

    lax.fori_loop(0, n_kc, out_body, 0)


def dsa_index(main, tail, batch, seq):
    bq = min(seq, 512)
    nq = seq // bq
    kc = min(seq, 256)
    topk = min(TOPK_MAX, seq // 4)
    n_qi = IDX_HEADS * IDX_DIM
    qi_block = ((A_HEADS + 2 * A_KV_HEADS) * A_HEAD_DIM) // n_qi
    c64, s64 = _rope_tables(seq, IDX_DIM)
    tab_q = pl.BlockSpec((bq, LANES), lambda b, i: (i, 0))
    tab_k = pl.BlockSpec((seq, LANES), lambda b, i: (0, 0))
    return pl.pallas_call(
        functools.partial(_dsa_index_kernel, bq=bq, seq=seq, kc=kc, topk=topk,
                          wi_scale=IDX_HEADS ** -0.5 * IDX_DIM ** -0.5),
        name="dsa_index",
        grid=(batch, nq),
        in_specs=[pl.BlockSpec((bq, n_qi), lambda b, i: (b * nq + i, qi_block)),
                  pl.BlockSpec((bq, LANES), lambda b, i: (b * nq + i, 0)),
                  pl.BlockSpec((seq, LANES), lambda b, i: (b, 0)),
                  tab_q, tab_q, tab_k, tab_k],
        out_specs=pl.BlockSpec((bq, seq), lambda b, i: (b * nq + i, 0)),
        out_shape=jax.ShapeDtypeStruct((batch * seq, seq), BF16),
        scratch_shapes=[pltpu.VMEM((seq, bq), I32), pltpu.VMEM((bq, IDX_HEADS * LANES), BF16),
                        pltpu.VMEM((seq, LANES), BF16)],
        compiler_params=_params("parallel", "arbitrary"),
    )(main, tail, tail, c64, s64, c64, s64)


def _dsa_attn_kernel(qf_ref, kf_ref, vf_ref, cq_ref, sq_ref, ck_ref, sk_ref, bias_ref, o_ref,
                     q_ref, k_ref, v_ref, s0_ref, s1_ref, p_ref, m0_ref, m1_ref, acc_ref,
                     *, bq, bk, seq, group, strip, q_scale):
    qi = pl.program_id(2)
    rows = group * bq

    @pl.when(qi == 0)
    def _():
        for r0 in range(0, seq, bq):
            rs = slice(r0, r0 + bq)
            k_ref[rs, :] = _rope128(kf_ref[rs, :], ck_ref[rs, :], sk_ref[rs, :]).astype(BF16)
            v_ref[rs, :] = vf_ref[rs, :].astype(BF16)

    cq, sq = cq_ref[...] * q_scale, sq_ref[...] * q_scale
    for r in range(group):
        q_ref[r * bq:(r + 1) * bq, :] = _rope128(qf_ref[:, r * LANES:(r + 1) * LANES], cq, sq).astype(BF16)
    q = q_ref[...]
    m0_ref[...] = jnp.full((rows, LANES), NEG_BIG, F32)
    acc_ref[...] = jnp.zeros((rows, 2 * LANES), F32)
    even, odd = (s0_ref, m0_ref, m1_ref), (s1_ref, m1_ref, m0_ref)

    n_kv = ((qi + 1) * bq + bk - 1) // bk

    def scores(j, bufs):
        ks = pl.multiple_of(jnp.minimum(j, n_kv - 1) * bk, bk)
        bufs[0][...] = lax.dot_general(q, k_ref[pl.ds(ks, bk), :], _NT, preferred_element_type=F32)

    def chunk(j, bufs):
        s_ref, m_in_ref, m_out_ref = bufs
        ks = pl.multiple_of(j * bk, bk)
        for b0 in range(0, bq, strip):
            bias = bias_ref[b0:b0 + strip, pl.ds(ks, bk)].astype(F32)
            for r in range(group):
                r0 = r * bq + b0
                _softmax_strip(s_ref[r0:r0 + strip, :] + bias, r0, m_in_ref, m_out_ref, p_ref)
        _accumulate(acc_ref, m_in_ref, m_out_ref, p_ref, v_ref[pl.ds(ks, bk), :])

    scores(0, even)

    def pair(jp, _):
        scores(2 * jp + 1, odd)
        chunk(2 * jp, even)
        scores(2 * jp + 2, even)
        chunk(2 * jp + 1, odd)
        return 0

    lax.fori_loop(0, n_kv // 2, pair, 0)

    @pl.when(n_kv % 2 == 1)
    def _():
        chunk(n_kv - 1, even)

    for r in range(group):
        sl = slice(r * bq, (r + 1) * bq)
        o_ref[:, r * LANES:(r + 1) * LANES] = (acc_ref[sl, :LANES] / acc_ref[sl, LANES:]).astype(o_ref.dtype)


def dsa_attention(main, bias, batch, seq):
    group = A_HEADS // A_KV_HEADS
    bq = bk = min(seq, 256)
    nq = seq // bq
    gw = group * LANES
    c128, s128 = _rope_tables(seq, A_HEAD_DIM)
    tab_q = pl.BlockSpec((bq, LANES), lambda b, g, i: (i, 0))
    tab_k = pl.BlockSpec((seq, LANES), lambda b, g, i: (0, 0))
    return pl.pallas_call(
        functools.partial(_dsa_attn_kernel, bq=bq, bk=bk, seq=seq, group=group, strip=min(bq, 64),
                          q_scale=A_HEAD_DIM ** -0.5 * LOG2E),
        name="dsa_attn",
        grid=(batch, A_KV_HEADS, nq),
        in_specs=[pl.BlockSpec((bq, gw), lambda b, g, i: (b * nq + i, g)),
                  pl.BlockSpec((seq, LANES), lambda b, g, i: (b, A_HEADS + g)),
                  pl.BlockSpec((seq, LANES), lambda b, g, i: (b, A_HEADS + A_KV_HEADS + g)),
                  tab_q, tab_q, tab_k, tab_k,
                  pl.BlockSpec((bq, seq), lambda b, g, i: (b * nq + i, 0))],
        out_specs=pl.BlockSpec((bq, gw), lambda b, g, i: (b * nq + i, g)),
        out_shape=jax.ShapeDtypeStruct((batch * seq, A_HEADS * LANES), BF16),
        scratch_shapes=[pltpu.VMEM((group * bq, LANES), BF16), pltpu.VMEM((seq, LANES), BF16),
                        pltpu.VMEM((seq, LANES), BF16),
                        pltpu.VMEM((group * bq, bk), F32), pltpu.VMEM((group * bq, bk), F32),
                        pltpu.VMEM((group * bq, bk), BF16),
                        pltpu.VMEM((group * bq, LANES), F32), pltpu.VMEM((group * bq, LANES), F32),
                        pltpu.VMEM((group * bq, 2 * LANES), F32)],
        compiler_params=_params("parallel", "parallel", "arbitrary"),
    )(main, main, main, c128, s128, c128, s128, bias)


def dsa_layer(x, g_norm, w_in, w_o, batch, seq):
    hn = rmsnorm(x, g_norm, BF16)
    n_main = (A_HEADS + 2 * A_KV_HEADS) * A_HEAD_DIM + IDX_HEADS * IDX_DIM
    n_tail = w_in.shape[1] - n_main
    main = matmul(hn, w_in[:, :n_main].astype(BF16), F32)
    w_tail = jnp.pad(w_in[:, n_main:], ((0, 0), (0, LANES - n_tail))).astype(BF16)
    tail = matmul(hn, w_tail, F32)
    bias = dsa_index(main, tail, batch, seq)
    o = dsa_attention(main, bias, batch, seq)
    return matmul(o, w_o.astype(BF16), F32, residual=x)


def _rwkv_mix_kernel(x_ref, g_ref, mix_ref, *rest, bs, rs):
    outs, hbuf = rest[:6], rest[6]
    j = pl.program_id(1)

    @pl.when(j == 0)
    def _():
        hbuf[0:8, :] = jnp.zeros((8, D_MODEL), F32)

    g = g_ref[...]
    for r0 in range(0, bs, rs):
        hn = _rms(x_ref[r0:r0 + rs, :], g)
        hbuf[8 + r0:8 + r0 + rs, :] = hn
        xx = hbuf[7 + r0:7 + r0 + rs, :] - hn
        for i in range(6):
            outs[i][r0:r0 + rs, :] = (hn + xx * mix_ref[i:i + 1, :]).astype(BF16)
    hbuf[7:8, :] = hbuf[7 + bs:8 + bs, :]


def rwkv_mix(x, g_norm, mix, batch, seq):
    bs = min(seq, 256)
    nb = seq // bs
    m, d = x.shape
    mix8 = jnp.pad(mix, ((0, 2), (0, 0)))
    return pl.pallas_call(
        functools.partial(_rwkv_mix_kernel, bs=bs, rs=min(bs, 32)),
        name="rwkv_mix",
        grid=(batch, nb),
        in_specs=[pl.BlockSpec((bs, d), lambda b, j: (b * nb + j, 0)),
                  pl.BlockSpec((1, d), lambda b, j: (0, 0)),
                  pl.BlockSpec((8, d), lambda b, j: (0, 0))],
        out_specs=[pl.BlockSpec((bs, d), lambda b, j: (b * nb + j, 0))] * 6,
        out_shape=[jax.ShapeDtypeStruct((m, d), BF16)] * 6,
        scratch_shapes=[pltpu.VMEM((bs + 8, d), F32)],
        compiler_params=_params("parallel", "arbitrary"),
    )(x, g_norm.reshape(1, d), mix8)


def _head_sum_matrix():
    return ((_iota((LANES, LANES), 0) // C_HEAD_DIM) == (_iota((LANES, LANES), 1) // C_HEAD_DIM)).astype(F32)


def _head_sum_2x(x, p):
    hi = x.astype(BF16)
    mid = (x - hi.astype(F32)).astype(BF16)
    pb = p.astype(BF16)
    return jnp.dot(hi, pb, preferred_element_type=F32) + jnp.dot(mid, pb, preferred_element_type=F32)


def _rwkv_rec_kernel(r_ref, k_ref, v_ref, wl_ref, al_ref, w0_ref, a0_ref, kkp_ref, ka_ref, y_ref, km_ref, st_ref,
                     *, tc, npair):
    c = pl.program_id(2)

    @pl.when(c == 0)
    def _():
        st_ref[...] = jnp.zeros(st_ref.shape, F32)

    def mm(a, b, dims=None):
        if dims is None:
            return jnp.dot(a, b, preferred_element_type=F32)
        return lax.dot_general(a, b, dims, preferred_element_type=F32)

    row = _iota((tc, tc), 0)
    col = _iota((tc, tc), 1)
    tri_incl = (row >= col).astype(F32)
    lane = _iota((tc, LANES), 1)
    head_masks = [lane < C_HEAD_DIM, lane >= C_HEAD_DIM]
    blockdiag = (_iota((LANES, LANES), 0) // C_HEAD_DIM) == (_iota((LANES, LANES), 1) // C_HEAD_DIM)
    n_sq = max((tc - 1).bit_length() - 1, 0)

    pairs = range(npair)
    tile = lambda a, p: a[:, p * LANES:(p + 1) * LANES]
    cast = lambda a: a.astype(BF16)

    r, k_raw, v = r_ref[...].astype(F32), k_ref[...].astype(F32), v_ref[...].astype(F32)
    lw = (-0.6065306597126334) * jax.nn.sigmoid(w0_ref[...] + wl_ref[...])
    a = jax.nn.sigmoid(a0_ref[...] + al_ref[...])
    kr = k_raw * kkp_ref[...]
    hsum = _head_sum_matrix()
    ss = jnp.concatenate([_head_sum_2x(tile(kr, p) * tile(kr, p), hsum) for p in pairs], axis=1)
    kk = kr * lax.rsqrt(jnp.maximum(ss, 1e-24))
    k = k_raw * (1.0 + (a - 1.0) * ka_ref[...])
    b = kk * a
    km_ref[...] = k
    cum = jnp.dot(tri_incl, lw, precision=HIGHEST, preferred_element_type=F32)
    g_in = jnp.exp(cum)
    g_inv = jnp.exp(-cum)
    a_t = -kk * jnp.exp(cum - lw)
    b_t = b * g_inv
    k_t = k * g_inv
    r_t = r * g_in
    g_end = g_in[tc - 1:tc, :]
    v_c, r_c = cast(v), cast(r_t)
    bk_x = cast(jnp.concatenate([b_t * g_end, k_t * g_end], axis=0).T)
    g_col = g_in.T[:, tc - 1:tc]
    rows = lambda a, p: a[p * LANES:(p + 1) * LANES, :]

    t2 = 2 * tc
    prow, pcol = _iota((t2, t2), 0), _iota((t2, t2), 1)
    strict2 = prow % tc > pcol % tc
    incl2 = prow % tc >= pcol % tc
    eye2 = (prow == pcol).astype(F32)
    low4 = jnp.concatenate([strict2, incl2], axis=0)
    by_head = lambda x_, p: cast(jnp.concatenate(
        [jnp.where(head_masks[h], tile(x_, p), 0.0) for h in range(2)], axis=0))

    a_s = {p: by_head(a_t, p) for p in pairs}
    ar = {p: jnp.concatenate([a_s[p], by_head(r_t, p)], axis=0) for p in pairs}
    gb = {p: mm(ar[p], by_head(b_t, p), _NT) for p in pairs}
    gk = {p: mm(ar[p], by_head(k_t, p), _NT) for p in pairs}
    a_ab = {p: jnp.where(strict2, gb[p][:t2], 0.0) for p in pairs}
    a_rb = {p: cast(jnp.where(incl2, gb[p][t2:], 0.0)) for p in pairs}
    v2 = {p: jnp.concatenate([tile(v_c, p)] * 2, axis=0) for p in pairs}
    kv = {p: mm(cast(jnp.where(low4, gk[p], 0.0)), v2[p]) for p in pairs}
    akv = {p: cast(kv[p][:t2]) for p in pairs}
    x = {p: eye2 + a_ab[p] for p in pairs}
    pw = {p: cast(a_ab[p]) for p in pairs}
    for _ in range(n_sq):
        pw = {p: cast(mm(pw[p], pw[p])) for p in pairs}
        x = {p: x[p] + mm(pw[p], cast(x[p])) for p in pairs}
    xw = {p: mm(cast(x[p]), jnp.concatenate([a_s[p], akv[p]], axis=1)) for p in pairs}
    w_t = {p: cast(xw[p][:tc, :LANES] + xw[p][tc:, :LANES]) for p in pairs}
    u_0 = {p: jnp.where(head_masks[0], xw[p][:tc, LANES:], xw[p][tc:, LANES:]) for p in pairs}

    s0 = {p: st_ref[p] for p in pairs}
    ws = {p: mm(jnp.concatenate([w_t[p], tile(r_c, p)], axis=0), cast(s0[p])) for p in pairs}
    u_c = {p: cast(ws[p][:tc] + u_0[p]) for p in pairs}
    yb = {p: mm(a_rb[p], jnp.concatenate([u_c[p]] * 2, axis=0)) for p in pairs}
    upd = {p: mm(rows(bk_x, p), jnp.concatenate([u_c[p], tile(v_c, p)], axis=0)) for p in pairs}
    for p in pairs:
        y_h = kv[p][t2:] + yb[p]
        y_ref[:, p * LANES:(p + 1) * LANES] = ws[p][tc:] + jnp.where(head_masks[0], y_h[:tc], y_h[tc:])
        st_ref[p] = s0[p] * rows(g_col, p) + jnp.where(blockdiag, upd[p], 0.0)


def rwkv_rec(r, k, v, wl, al, w0, a0, k_k, k_a, batch, seq):
    m, d = r.shape
    tc = min(seq, 64)
    nc = seq // tc
    npair = 16
    gw = npair * LANES
    blk = pl.BlockSpec((tc, gw), lambda bi, g, c: (bi * nc + c, g))
    row = pl.BlockSpec((1, gw), lambda bi, g, c: (0, g))
    r2 = lambda a: a.reshape(1, d).astype(F32)
    return pl.pallas_call(
        functools.partial(_rwkv_rec_kernel, tc=tc, npair=npair),
        name="rwkv_rec",
        grid=(batch, d // gw, nc),
        in_specs=[blk] * 5 + [row] * 4,
        out_specs=[blk, blk],
        out_shape=[jax.ShapeDtypeStruct((m, d), F32)] * 2,
        scratch_shapes=[pltpu.VMEM((npair, LANES, LANES), F32)],
        compiler_params=_params("parallel", "parallel", "arbitrary"),
    )(r, k, v, wl, al, r2(w0), r2(a0), r2(k_k), r2(k_a))


def _rwkv_post_kernel(y_ref, r_ref, k_ref, v_ref, g_ref, gng_ref, gnb_ref, rk_ref, o_ref):
    p = _head_sum_matrix()
    inv_n = 1.0 / C_HEAD_DIM
    for t in range(D_MODEL // LANES):
        sl = slice(t * LANES, (t + 1) * LANES)
        y = y_ref[:, sl]
        mu = _head_sum_2x(y, p) * inv_n
        yc = y - mu
        var = _head_sum_2x(yc * yc, p) * inv_n
        yn = yc * lax.rsqrt(var + C_GN_EPS) * gng_ref[:, sl] + gnb_ref[:, sl]
        bonus = (_head_sum_2x(r_ref[:, sl].astype(F32) * k_ref[:, sl] * rk_ref[:, sl], p)
                 * v_ref[:, sl].astype(F32))
        o_ref[:, sl] = ((yn + bonus) * g_ref[:, sl]).astype(o_ref.dtype)


def rwkv_post(y, r, kmod, v, g, gn_g, gn_b, r_k):
    m, d = y.shape
    bm = _largest_divisor(m, (256, 128, 64))
    big = pl.BlockSpec((bm, d), lambda i: (i, 0))
    row = pl.BlockSpec((1, d), lambda i: (0, 0))
    r2 = lambda a: a.reshape(1, d).astype(F32)
    return pl.pallas_call(
        _rwkv_post_kernel,
        name="rwkv_post",
        grid=(m // bm,),
        in_specs=[big] * 5 + [row] * 3,
        out_specs=big,
        out_shape=jax.ShapeDtypeStruct((m, d), BF16),
        compiler_params=_params("parallel"),
    )(y, r, kmod, v, g, r2(gn_g), r2(gn_b), r2(r_k))


def rwkv_layer(x, g_norm, mix, w0, w1, w2, a0, a1, a2, g1, g2, k_k, k_a, r_k, w_rkv, w_o, gn_g, gn_b,
               batch, seq):
    bf = lambda a: a.astype(BF16)
    xr, xk, xv, xw, xa, xg = rwkv_mix(x, g_norm, mix, batch, seq)
    r = matmul(xr, bf(w_rkv[0]), BF16)
    k = matmul(xk, bf(w_rkv[1]), BF16)
    v = matmul(xv, bf(w_rkv[2]), BF16)
    wl = lora(xw, bf(w1), bf(w2), "tanh")
    al = lora(xa, bf(a1), bf(a2), "none")
    g = lora(xg, bf(g1), bf(g2), "sigmoid")
    y, kmod = rwkv_rec(r, k, v, wl, al, w0, a0, k_k, k_a, batch, seq)
    o = rwkv_post(y, r, kmod, v, g, gn_g, gn_b, r_k)
    return matmul(o, bf(w_o), F32, residual=x)


def kernel(x, norm_mix, norm_ffn, norm_final, ffn_gate, ffn_up, ffn_down, a_w_in, a_w_o, b_w_in, b_f_bias,
           b_w_o, c_mix, c_w0, c_w1, c_w2, c_a0, c_a1, c_a2, c_g1, c_g2, c_k_k, c_k_a, c_r_k, c_w_rkv,
           c_w_o, c_gn_g, c_gn_b):
    batch, seq, d = x.shape
    depth = norm_mix.shape[0]
    h = x.reshape(batch * seq, d)
    ffn_down_bf = ffn_down.astype(BF16)
    for i in range(depth):
        kind, j = i % 3, i // 3
        if kind == 0:
            h = dsa_layer(h, norm_mix[i], a_w_in[j], a_w_o[j], batch, seq)
        elif kind == 1:
            h = fox_layer(h, norm_mix[i], b_w_in[j], b_f_bias[j], b_w_o[j], batch, seq)
        else:
            h = rwkv_layer(h, norm_mix[i], c_mix[j], c_w0[j], c_w1[j], c_w2[j], c_a0[j], c_a1[j], c_a2[j],
                           c_g1[j], c_g2[j], c_k_k[j], c_k_a[j], c_r_k[j], c_w_rkv[j], c_w_o[j],
                           c_gn_g[j], c_gn_b[j], batch, seq)
        h = swiglu_layer(h, norm_ffn[i], ffn_gate, ffn_up, ffn_down_bf, i)
    return rmsnorm(h, norm_final, x.dtype).reshape(batch, seq, d)
```
